```python
import math
import jax
import jax.numpy as jnp
from jax import lax
import numpy as np

D_MODEL = 1024
BATCH = 1
SEQ = 16384
DEPTH = 2
DEC_BATCH = 128
DEC_SEQ = 4
PAST_LEN = 16384
PAGE_SIZE = 128

HEAD_DIM = 64
ROT_DIM = HEAD_DIM // 4
ROPE_THETA = 500000.0
NORM_EPS = 1e-6
A_Q_HEADS = 4
A_KV_HEADS = 2
A_WINDOW = 128
C_Q_HEADS = 4
C_KV_HEADS = 2
C_PATTERNS = ((128, 1), (512, 4), (2048, 16))
C_MAX_SPAN = 2048
B_HEADS = 8
B_HEADDIM = 64
B_INNER = B_HEADS * B_HEADDIM
B_GROUPS = 2
B_STATE = 128
CONV_K = 4
CONV_DIM = B_INNER + 2 * B_GROUPS * B_STATE
SSD_CHUNK = 128
A_WIDTH = A_Q_HEADS * HEAD_DIM
A_KV_WIDTH = A_KV_HEADS * HEAD_DIM
C_WIDTH = C_Q_HEADS * HEAD_DIM
C_KV_WIDTH = C_KV_HEADS * HEAD_DIM
MIX_WIDTH = A_WIDTH + B_INNER + C_WIDTH
IN_SIZES = (A_WIDTH, A_KV_WIDTH, A_KV_WIDTH, C_WIDTH, C_KV_WIDTH, C_KV_WIDTH, B_INNER, CONV_DIM, B_HEADS)
N_IN = sum(IN_SIZES)
D_FF = -(-(8 * D_MODEL) // (3 * 256)) * 256
BAND_BLOCK = 128

kernel_name = 'hymba_swa_ssd_dilated_decoder'


def _rmsnorm(x, g):
    xf = x.astype(jnp.float32)
    y = xf * lax.rsqrt(jnp.mean(xf * xf, axis=-1, keepdims=True) + NORM_EPS)
    return (y * g.astype(jnp.float32)).astype(x.dtype)


def _rope(x, pos):
    half = ROT_DIM // 2
    inv = ROPE_THETA ** (-(jnp.arange(half, dtype=jnp.float32) * 2.0 / ROT_DIM))
    ang = pos.astype(jnp.float32)[:, None] * inv[None, :]
    cos = jnp.cos(ang)[None, :, None, :]
    sin = jnp.sin(ang)[None, :, None, :]
    xf = x.astype(jnp.float32)
    x1, x2 = xf[..., :half], xf[..., half:ROT_DIM]
    return jnp.concatenate([x1 * cos - x2 * sin, x2 * cos + x1 * sin, xf[..., ROT_DIM:]], axis=-1).astype(x.dtype)


def _masked_softmax(s, valid, sinks=None):
    s = jnp.where(valid, s, -jnp.inf)
    m = jnp.max(s, axis=-1, keepdims=True)
    if sinks is not None:
        m = jnp.maximum(m, sinks)
    e = jnp.exp(s - m)
    den = jnp.sum(e, axis=-1, keepdims=True)
    if sinks is not None:
        den = den + jnp.exp(sinks - m)
    return e / den, (m + jnp.log(den))[..., 0]


def _banded_attention(q, k, v, max_dist, sinks=None):
    b, L, hq, hd = q.shape
    hkv = k.shape[2]
    rep = hq // hkv
    blk = BAND_BLOCK
    nb = -(-L // blk)
    pad = nb * blk - L
    qp = jnp.pad(q, ((0, 0), (0, pad), (0, 0), (0, 0))).reshape(b, nb, blk, hkv, rep, hd)
    kp = jnp.pad(k, ((0, 0), (blk, pad), (0, 0), (0, 0))).reshape(b, nb + 1, blk, hkv, hd)
    vp = jnp.pad(v, ((0, 0), (blk, pad), (0, 0), (0, 0))).reshape(b, nb + 1, blk, hkv, hd)
    kw = jnp.concatenate([kp[:, :-1], kp[:, 1:]], axis=2)
    vw = jnp.concatenate([vp[:, :-1], vp[:, 1:]], axis=2)
    s = jnp.einsum('bnqgrd,bnkgd->bngrqk', qp, kw, preferred_element_type=jnp.float32) * (hd ** -0.5)
    qi = jnp.arange(blk)[:, None] + blk
    ki = jnp.arange(2 * blk)[None, :]
    dist = qi - ki
    kpos = jnp.arange(nb)[:, None, None] * blk + ki[None] - blk
    valid = (dist >= 0) & (dist <= max_dist) & (kpos >= 0)
    sk = None if sinks is None else sinks.astype(jnp.float32).reshape(hkv, rep, 1, 1)
    p, lse = _masked_softmax(s, valid[:, None, None], sk)
    o = jnp.einsum('bngrqk,bnkgd->bnqgrd', p.astype(v.dtype), vw)
    o = o.reshape(b, nb * blk, hq, hd)[:, :L]
    lse = lse.transpose(0, 1, 4, 2, 3).reshape(b, nb * blk, hq)[:, :L]
    return o, lse


def _merge_by_denominator(outs, lses):
    w = jax.nn.softmax(jnp.stack(lses, axis=0), axis=0)
    return jnp.einsum('pblh,pblhd->blhd', w.astype(outs[0].dtype), jnp.stack(outs, axis=0))


def _dilated_prompt(q, k, v):
    b, L, hq, hd = q.shape
    outs, lses = [], []
    for w, d in C_PATTERNS:
        def fold(t):
            h = t.shape[2]
            return t.reshape(b, L // d, d, h, hd).transpose(0, 2, 1, 3, 4).reshape(b * d, L // d, h, hd)
        o, lse = _banded_attention(fold(q), fold(k), fold(v), w // d)
        outs.append(o.reshape(b, d, L // d, hq, hd).transpose(0, 2, 1, 3, 4).reshape(b, L, hq, hd))
        lses.append(lse.reshape(b, d, L // d, hq).transpose(0, 2, 1, 3).reshape(b, L, hq))
    return _merge_by_denominator(outs, lses)


def _dilated_sample(q, kcat, vcat):
    b, T, hq, hd = q.shape
    hkv = kcat.shape[2]
    rep = hq // hkv
    buf = kcat.shape[1] - T
    qg = q.reshape(b, T, hkv, rep, hd)
    t = jnp.arange(T)
    outs, lses = [], []
    for w, d in C_PATTERNS:
        idx = buf + t[:, None] - d * jnp.arange(w // d + 1)[None, :]
        valid = idx >= 0
        idx = jnp.maximum(idx, 0)
        kg = kcat[:, idx]
        vg = vcat[:, idx]
        s = jnp.einsum('btgrd,btkgd->bgrtk', qg, kg, preferred_element_type=jnp.float32) * (hd ** -0.5)
        p, lse = _masked_softmax(s, valid, None)
        o = jnp.einsum('bgrtk,btkgd->btgrd', p.astype(vg.dtype), vg)
        outs.append(o.reshape(b, T, hq, hd))
        lses.append(lse.transpose(0, 3, 1, 2).reshape(b, T, hq))
    return _merge_by_denominator(outs, lses)


def _window_sample(q, kcat, vcat, sinks):
    b, T, hq, hd = q.shape
    hkv = kcat.shape[2]
    rep = hq // hkv
    buf = kcat.shape[1] - T
    qg = q.reshape(b, T, hkv, rep, hd)
    s = jnp.einsum('btgrd,bkgd->bgrtk', qg, kcat, preferred_element_type=jnp.float32) * (hd ** -0.5)
    dist = (buf + jnp.arange(T))[:, None] - jnp.arange(buf + T)[None, :]
    valid = (dist >= 0) & (dist < A_WINDOW)
    p, _ = _masked_softmax(s, valid, sinks.astype(jnp.float32).reshape(hkv, rep, 1, 1))
    o = jnp.einsum('bgrtk,bkgd->btgrd', p.astype(vcat.dtype), vcat)
    return o.reshape(b, T, hq, hd)


def _causal_conv(xbc, prefix, w, bias):
    L = xbc.shape[1]
    xp = jnp.concatenate([prefix.astype(xbc.dtype), xbc], axis=1)
    out = bias + sum(xp[:, j:j + L] * w[j] for j in range(CONV_K))
    return jax.nn.silu(out), xp[:, -(CONV_K - 1):]


def _ssd(x, a, bm, cm, h0):
    b, L, H, P = x.shape
    G, N = bm.shape[2], bm.shape[3]
    rep = H // G
    T = math.gcd(L, SSD_CHUNK)
    nc = L // T
    f32 = jnp.float32
    xc = x.astype(f32).reshape(b, nc, T, G, rep, P)
    ac = a.astype(f32).reshape(b, nc, T, G, rep)
    bc = bm.astype(f32).reshape(b, nc, T, G, N)
    cc = cm.astype(f32).reshape(b, nc, T, G, N)
    acum = jnp.cumsum(ac, axis=2)
    seg = acum[:, :, :, None] - acum[:, :, None, :]
    causal = jnp.tril(jnp.ones((T, T), dtype=bool))[:, :, None, None]
    lmat = jnp.exp(jnp.where(causal, seg, -jnp.inf))
    cb = jnp.einsum('bclgn,bcsgn->bclsg', cc, bc)
    y_diag = jnp.einsum('bclsgr,bcsgrp->bclgrp', cb[..., None] * lmat, xc)
    decay_to_end = jnp.exp(acum[:, :, -1:] - acum)
    chunk_states = jnp.einsum('bcsgn,bcsgrp->bcgrpn', bc, xc * decay_to_end[..., None])
    chunk_decay = jnp.exp(acum[:, :, -1])

    def step(h, inp):
        s_c, dec = inp
        return h * dec[..., None, None] + s_c, h

    h_final, h_in = lax.scan(step, h0.astype(f32).reshape(b, G, rep, P, N),
                             (jnp.moveaxis(chunk_states, 1, 0), jnp.moveaxis(chunk_decay, 1, 0)))
    h_in = jnp.moveaxis(h_in, 0, 1)
    y_off = jnp.einsum('bclgn,bcgrpn->bclgrp', cc, h_in) * jnp.exp(acum)[..., None]
    return (y_diag + y_off).reshape(b, L, H, P), h_final.reshape(b, H, P, N)


def _mamba(xbc_act, z, dt_raw, h0, lp):
    b, L, _ = xbc_act.shape
    gn = B_GROUPS * B_STATE
    xs = xbc_act[..., :B_INNER].reshape(b, L, B_HEADS, B_HEADDIM)
    bm = xbc_act[..., B_INNER:B_INNER + gn].reshape(b, L, B_GROUPS, B_STATE)
    cm = xbc_act[..., B_INNER + gn:].reshape(b, L, B_GROUPS, B_STATE)
    dt = jax.nn.softplus(dt_raw.astype(jnp.float32) + lp['dt_bias'].astype(jnp.float32))
    a = -jnp.exp(lp['a_log'].astype(jnp.float32)) * dt
    y, h = _ssd(xs.astype(jnp.float32) * dt[..., None], a, bm, cm, h0)
    y = y + xs.astype(jnp.float32) * lp['d_skip'].astype(jnp.float32)[:, None]
    y = y.reshape(b, L, B_INNER) * jax.nn.silu(z.astype(jnp.float32))
    return _rmsnorm(y, lp['ssm_norm']), h


def _in_proj(h, pos, lp):
    b, L, _ = h.shape
    proj = _rmsnorm(h, lp['norm1']) @ lp['w_in']
    parts = []
    off = 0
    for sz in IN_SIZES:
        parts.append(proj[..., off:off + sz])
        off += sz
    aq, ak, av, cq, ck, cv, z, xbc, dt = parts

    def heads(t):
        return t.reshape(b, L, -1, HEAD_DIM)

    aq = _rope(_rmsnorm(heads(aq), lp['a_qn']), pos)
    ak = _rope(_rmsnorm(heads(ak), lp['a_kn']), pos)
    cq = _rope(_rmsnorm(heads(cq), lp['c_qn']), pos)
    ck = _rope(_rmsnorm(heads(ck), lp['c_kn']), pos)
    return aq, ak, heads(av), cq, ck, heads(cv), z, xbc, dt


def _out_ffn(h, a_o, m_o, c_o, lp):
    b, L, _ = h.shape
    mix = jnp.concatenate([a_o.reshape(b, L, A_WIDTH).astype(h.dtype), m_o.astype(h.dtype),
                           c_o.reshape(b, L, C_WIDTH).astype(h.dtype)], axis=-1)
    h = h + mix @ lp['w_out']
    u = _rmsnorm(h, lp['norm2'])
    return h + (jax.nn.silu(u @ lp['w_gate']) * (u @ lp['w_up'])) @ lp['w_down']


def _layer_prompt(h, lp):
    b, L, _ = h.shape
    pos = jnp.arange(L)
    aq, ak, av, cq, ck, cv, z, xbc, dt = _in_proj(h, pos, lp)
    a_o, _ = _banded_attention(aq, ak, av, A_WINDOW - 1, lp['a_sinks'])
    c_o = _dilated_prompt(cq, ck, cv)
    xbc_act, conv_state = _causal_conv(xbc, jnp.zeros((b, CONV_K - 1, CONV_DIM), xbc.dtype), lp['conv_w'], lp['conv_b'])
    m_o, ssm_state = _mamba(xbc_act, z, dt, jnp.zeros((b, B_HEADS, B_HEADDIM, B_STATE), jnp.float32), lp)
    h = _out_ffn(h, a_o, m_o, c_o, lp)
    wa = min(A_WINDOW, L)
    wc = min(C_MAX_SPAN, L)
    return h, (ak[:, L - wa:], av[:, L - wa:], ck[:, L - wc:], cv[:, L - wc:], ssm_state, conv_state)


def _layer_sample(h, c_ak, c_av, c_ck, c_cv, s_ssm, s_conv, lp):
    b, T, _ = h.shape
    pos = PAST_LEN + jnp.arange(T)
    aq, ak, av, cq, ck, cv, z, xbc, dt = _in_proj(h, pos, lp)
    buf_a = c_ak.shape[1]
    buf_c = c_ck.shape[1]
    ka = jnp.concatenate([c_ak.astype(ak.dtype), ak], axis=1)
    va = jnp.concatenate([c_av.astype(av.dtype), av], axis=1)
    kc = jnp.concatenate([c_ck.astype(ck.dtype), ck], axis=1)
    vc = jnp.concatenate([c_cv.astype(cv.dtype), cv], axis=1)
    a_o = _window_sample(aq, ka, va, lp['a_sinks'])
    c_o = _dilated_sample(cq, kc, vc)
    xbc_act, conv_state = _causal_conv(xbc, s_conv, lp['conv_w'], lp['conv_b'])
    m_o, ssm_state = _mamba(xbc_act, z, dt, s_ssm, lp)
    h = _out_ffn(h, a_o, m_o, c_o, lp)
    return h, (ka[:, -buf_a:], va[:, -buf_a:], kc[:, -buf_c:], vc[:, -buf_c:], ssm_state, conv_state)


def setup_inputs(seed: int = 0) -> dict:
    key = jax.random.key(seed)
    ks = iter(jax.random.split(key, 40))
    f32 = jnp.float32

    def nrm(shape, scale):
        return jax.random.normal(next(ks), shape, f32) * scale

    def gain(shape):
        return 1.0 + nrm(shape, 0.02)

    a_buf = min(A_WINDOW, PAST_LEN)
    c_buf = min(C_MAX_SPAN, PAST_LEN)
    x_prompt = nrm((BATCH, SEQ, D_MODEL), 1.0)
    x_sample = nrm((DEC_BATCH, DEC_SEQ, D_MODEL), 1.0)
    cache_a_k = nrm((DEPTH, DEC_BATCH, a_buf, A_KV_HEADS, HEAD_DIM), 1.0)
    cache_a_v = nrm((DEPTH, DEC_BATCH, a_buf, A_KV_HEADS, HEAD_DIM), 1.0)
    cache_c_k = nrm((DEPTH, DEC_BATCH, c_buf, C_KV_HEADS, HEAD_DIM), 1.0)
    cache_c_v = nrm((DEPTH, DEC_BATCH, c_buf, C_KV_HEADS, HEAD_DIM), 1.0)
    state_ssm = nrm((DEPTH, DEC_BATCH, B_HEADS, B_HEADDIM, B_STATE), 0.1)
    state_conv = nrm((DEPTH, DEC_BATCH, CONV_K - 1, CONV_DIM), 1.0)
    norm1 = gain((DEPTH, D_MODEL))
    w_in = nrm((DEPTH, D_MODEL, N_IN), D_MODEL ** -0.5)
    a_qn = gain((DEPTH, HEAD_DIM))
    a_kn = gain((DEPTH, HEAD_DIM))
    a_sinks = nrm((DEPTH, A_Q_HEADS), 0.5)
    c_qn = gain((DEPTH, HEAD_DIM))
    c_kn = gain((DEPTH, HEAD_DIM))
    conv_w = nrm((DEPTH, CONV_K, CONV_DIM), CONV_K ** -0.5)
    conv_b = nrm((DEPTH, CONV_DIM), 0.02)
    u = jax.random.uniform(next(ks), (DEPTH, B_HEADS), f32)
    dt0 = jnp.exp(u * (math.log(0.1) - math.log(0.001)) + math.log(0.001))
    dt_bias = dt0 + jnp.log(-jnp.expm1(-dt0))
    a_log = jnp.log(jax.random.uniform(next(ks), (DEPTH, B_HEADS), f32, 1.0, 16.0))
    d_skip = 1.0 + nrm((DEPTH, B_HEADS), 0.1)
    ssm_norm = gain((DEPTH, B_INNER))
    w_out = nrm((DEPTH, MIX_WIDTH, D_MODEL), MIX_WIDTH ** -0.5)
    norm2 = gain((DEPTH, D_MODEL))
    w_gate = nrm((DEPTH, D_MODEL, D_FF), D_MODEL ** -0.5)
    w_up = nrm((DEPTH, D_MODEL, D_FF), D_MODEL ** -0.5)
    w_down = nrm((DEPTH, D_FF, D_MODEL), D_FF ** -0.5)
    return {'x_prompt': x_prompt, 'x_sample': x_sample,
            'cache_a_k': cache_a_k, 'cache_a_v': cache_a_v, 'cache_c_k': cache_c_k, 'cache_c_v': cache_c_v,
            'state_ssm': state_ssm, 'state_conv': state_conv,
            'norm1': norm1, 'w_in': w_in, 'a_qn': a_qn, 'a_kn': a_kn, 'a_sinks': a_sinks,
            'c_qn': c_qn, 'c_kn': c_kn, 'conv_w': conv_w, 'conv_b': conv_b, 'dt_bias': dt_bias,
            'a_log': a_log, 'd_skip': d_skip, 'ssm_norm': ssm_norm, 'w_out': w_out, 'norm2': norm2,
            'w_gate': w_gate, 'w_up': w_up, 'w_down': w_down}


def reference(x_prompt, x_sample, cache_a_k, cache_a_v, cache_c_k, cache_c_v, state_ssm, state_conv,
              norm1, w_in, a_qn, a_kn, a_sinks, c_qn, c_kn, conv_w, conv_b, dt_bias, a_log, d_skip,
              ssm_norm, w_out, norm2, w_gate, w_up, w_down):
    hp = x_prompt
    hs = x_sample
    p_out = [[] for _ in range(6)]
    s_out = [[] for _ in range(6)]
    for l in range(DEPTH):
        lp = {'norm1': norm1[l], 'w_in': w_in[l], 'a_qn': a_qn[l], 'a_kn': a_kn[l], 'a_sinks': a_sinks[l],
              'c_qn': c_qn[l], 'c_kn': c_kn[l], 'conv_w': conv_w[l], 'conv_b': conv_b[l],
              'dt_bias': dt_bias[l], 'a_log': a_log[l], 'd_skip': d_skip[l], 'ssm_norm': ssm_norm[l],
              'w_out': w_out[l], 'norm2': norm2[l], 'w_gate': w_gate[l], 'w_up': w_up[l], 'w_down': w_down[l]}
        hp, pst = _layer_prompt(hp, lp)
        hs, sst = _layer_sample(hs, cache_a_k[l], cache_a_v[l], cache_c_k[l], cache_c_v[l],
                                state_ssm[l], state_conv[l], lp)
        for i in range(6):
            p_out[i].append(pst[i])
            s_out[i].append(sst[i])
    p_a_k, p_a_v, p_c_k, p_c_v, p_ssm, p_conv = [jnp.stack(t, axis=0) for t in p_out]
    s_a_k, s_a_v, s_c_k, s_c_v, s_ssm, s_conv = [jnp.stack(t, axis=0) for t in s_out]
    return (hp, hs, p_a_k, p_a_v, p_c_k, p_c_v, p_ssm, p_conv, s_a_k, s_a_v, s_c_k, s_c_v, s_ssm, s_conv)
```

```python
import functools
import math

import numpy as np
import jax
import jax.numpy as jnp
from jax import lax
from jax.experimental import pallas as pl
from jax.experimental.pallas import tpu as pltpu

F32 = jnp.float32
BF16 = jnp.bfloat16

D_MODEL = 1024
HEAD_DIM = 64
ROT_DIM = 16
ROPE_THETA = 500000.0
NORM_EPS = 1e-6
Q_WIDTH = 256
KV_WIDTH = 128
PAST_LEN = 16384
A_WINDOW = 128
C_PATTERNS = ((128, 1), (512, 4), (2048, 16))
C_SPAN = 2048
B_HEADS = 8
B_INNER = 512
B_STATE = 128
CONV_K = 4
CONV_DIM = 1024
D_FF = 2816
N_IN = 2568
N_IN_PAD = 2688
BLK = 128
TOK0 = 4
TILE = 8
FF_CHUNK = 256
VMEM_LIMIT = 56 * 1024 * 1024

_OFF = dict(aq=0, ak=256, av=384, cq=512, ck=768, cv=896, z=1024, xbc=1536, dt=2560, end=N_IN_PAD)
_HEAD_PERM = np.concatenate([np.arange(0, 64), np.arange(128, 192), np.arange(64, 128), np.arange(192, 256)])


def _const_spec(shape):
    nd = len(shape)
    return pl.BlockSpec(shape, lambda *_: (0,) * nd, pipeline_mode=pl.Buffered(1))


def _params(sem):
    return pltpu.CompilerParams(dimension_semantics=sem, vmem_limit_bytes=VMEM_LIMIT)


def _split3(v):
    hi = v.astype(BF16)
    r1 = v - hi.astype(F32)
    mid = r1.astype(BF16)
    lo = (r1 - mid.astype(F32)).astype(BF16)
    return hi, mid, lo


def _sel_dot_rhs(sel, v):
    hi, mid, lo = _split3(v)
    d = functools.partial(jnp.dot, preferred_element_type=F32)
    return d(sel, hi) + d(sel, mid) + d(sel, lo)


def _sel_dot_lhs(v, sel):
    hi, mid, lo = _split3(v)
    d = functools.partial(jnp.dot, preferred_element_type=F32)
    return d(hi, sel) + d(mid, sel) + d(lo, sel)


def _dot_nt(a, b):
    return lax.dot_general(a, b, (((1,), (1,)), ((), ())), preferred_element_type=F32)


def _silu(x):
    return x * jax.nn.sigmoid(x)


def _softplus(x):
    return jnp.maximum(x, 0.0) + jnp.log(1.0 + jnp.exp(-jnp.abs(x)))


def _head_norm(x, gain, bd):
    x2 = x * x
    hi = x2.astype(BF16)
    lo = (x2 - hi.astype(F32)).astype(BF16)
    ms = jnp.dot(hi, bd, preferred_element_type=F32) + jnp.dot(lo, bd, preferred_element_type=F32)
    return x * lax.rsqrt(ms + NORM_EPS) * gain


def _rope(x, cos_t, sin_t):
    w = x.shape[1]
    lane = lax.broadcasted_iota(jnp.int32, x.shape, 1) & (HEAD_DIM - 1)
    partner = jnp.where(lane < ROT_DIM // 2, pltpu.roll(x, w - ROT_DIM // 2, 1), pltpu.roll(x, ROT_DIM // 2, 1))
    return x * cos_t + partner * sin_t


def _in_proj_body(h_ref, g1_ref, w_ref, tab_ref, gqa_ref, gka_ref, gqc_ref, gkc_ref, bd_ref,
                  qa_ref, ka_ref, va_ref, qc_ref, kc_ref, vc_ref, z_ref, xbc_ref, dt_ref):
    x = h_ref[...]
    ms = jnp.mean(x * x, axis=-1, keepdims=True)
    u = (x * lax.rsqrt(ms + NORM_EPS) * g1_ref[...]).astype(BF16)
    tab = tab_ref[...]
    c1, s1 = tab[:, :128], tab[:, 128:]
    c2 = jnp.concatenate([c1, c1], axis=1)
    s2 = jnp.concatenate([s1, s1], axis=1)
    bd2 = bd_ref[...]
    bd1 = bd2[:128, :128]

    def proj(name, nxt):
        return jnp.dot(u, w_ref[:, _OFF[name]:_OFF[nxt]], preferred_element_type=F32)

    qa_ref[...] = (_rope(_head_norm(proj('aq', 'ak'), gqa_ref[...], bd2), c2, s2) * 0.125).astype(BF16)
    ka_ref[...] = _rope(_head_norm(proj('ak', 'av'), gka_ref[...], bd1), c1, s1)
    va_ref[...] = proj('av', 'cq')
    qc_ref[...] = (_rope(_head_norm(proj('cq', 'ck'), gqc_ref[...], bd2), c2, s2) * 0.125).astype(BF16)
    kc_ref[...] = _rope(_head_norm(proj('ck', 'cv'), gkc_ref[...], bd1), c1, s1)
    vc_ref[...] = proj('cv', 'z')
    z_ref[...] = proj('z', 'xbc')
    xbc_ref[...] = proj('xbc', 'dt')
    dt_ref[...] = proj('dt', 'end')


def _in_proj(h, lw, tab, tm):
    rows = h.shape[0]
    grid = (rows // tm,)
    row_spec = lambda w: pl.BlockSpec((tm, w), lambda i: (i, 0))
    widths = (Q_WIDTH, KV_WIDTH, KV_WIDTH, Q_WIDTH, KV_WIDTH, KV_WIDTH, B_INNER, CONV_DIM, 128)
    dtypes = (BF16, F32, F32, BF16, F32, F32, F32, F32, F32)
    return pl.pallas_call(
        _in_proj_body,
        grid=grid,
        in_specs=[row_spec(D_MODEL), _const_spec((1, D_MODEL)), _const_spec((D_MODEL, N_IN_PAD)),
                  row_spec(256), _const_spec((1, 256)), _const_spec((1, 128)), _const_spec((1, 256)),
                  _const_spec((1, 128)), _const_spec((256, 256))],
        out_specs=[row_spec(w) for w in widths],
        out_shape=[jax.ShapeDtypeStruct((rows, w), dt) for w, dt in zip(widths, dtypes)],
        compiler_params=_params(("parallel",)),
        name="in_proj",
    )(h, lw['norm1'], lw['w_in'], tab, lw['a_qn'], lw['a_kn'], lw['c_qn'], lw['c_kn'], lw['bd'])


def _band_body(*refs, max_dist, has_sink, want_lse):
    if has_sink:
        sink_ref, refs = refs[0], refs[1:]
    q_ref, kp_ref, kc_ref, vp_ref, vc_ref = refs[:5]
    o_ref = refs[5]
    n = pl.program_id(1)
    q = q_ref[...]
    kcat = jnp.concatenate([kp_ref[...], kc_ref[...]], axis=0).astype(BF16)
    vcat = jnp.concatenate([vp_ref[...], vc_ref[...]], axis=0).astype(BF16)
    row = lax.broadcasted_iota(jnp.int32, (BLK, 2 * BLK), 0)
    col = lax.broadcasted_iota(jnp.int32, (BLK, 2 * BLK), 1)
    dist = row + BLK - col
    valid = (dist >= 0) & (dist <= max_dist) & ((col >= BLK) | (n > 0))
    g_lo = lax.broadcasted_iota(jnp.int32, (BLK, 128), 1) < HEAD_DIM
    zero = jnp.zeros((), BF16)
    outs, lses = [], []
    for r in range(2):
        qr = q[:, 128 * r:128 * (r + 1)]
        o_g, l_g = [], []
        for g in range(2):
            qm = jnp.where(g_lo if g == 0 else ~g_lo, qr, zero)
            s = jnp.where(valid, _dot_nt(qm, kcat), -jnp.inf)
            m = jnp.max(s, axis=-1, keepdims=True)
            if has_sink:
                sk = sink_ref[2 * g + r]
                m = jnp.maximum(m, sk)
            e = jnp.exp(s - m)
            den = jnp.sum(e, axis=-1, keepdims=True)
            if has_sink:
                den = den + jnp.exp(sk - m)
            pv = jnp.dot(e.astype(BF16), vcat, preferred_element_type=F32)
            o_g.append(pv / den)
            if want_lse:
                l_g.append(jnp.broadcast_to(m + jnp.log(den), (BLK, 128)))
        outs.append(jnp.where(g_lo, o_g[0], o_g[1]))
        if want_lse:
            lses.append(jnp.where(g_lo, l_g[0], l_g[1]))
    o_ref[...] = jnp.concatenate(outs, axis=1).astype(o_ref.dtype)
    if want_lse:
        refs[6][...] = jnp.concatenate(lses, axis=1)


def _banded(q, k, v, dil, max_dist, sinks=None, want_lse=False, out_dtype=BF16):
    L = q.shape[0]
    lf = L // dil
    nb = lf // BLK
    qv = q.reshape(lf, dil * Q_WIDTH)
    kv = k.reshape(lf, dil * KV_WIDTH)
    vv = v.reshape(lf, dil * KV_WIDTH)
    cur = lambda r, n: (n, r)
    prev = lambda r, n: (jnp.maximum(n - 1, 0), r)
    in_specs = [pl.BlockSpec((BLK, Q_WIDTH), cur), pl.BlockSpec((BLK, KV_WIDTH), prev),
                pl.BlockSpec((BLK, KV_WIDTH), cur), pl.BlockSpec((BLK, KV_WIDTH), prev),
                pl.BlockSpec((BLK, KV_WIDTH), cur)]
    args = [qv, kv, kv, vv, vv]
    if sinks is not None:
        in_specs = [pl.BlockSpec(memory_space=pltpu.SMEM)] + in_specs
        args = [sinks] + args
    out_specs = [pl.BlockSpec((BLK, Q_WIDTH), cur)]
    out_shape = [jax.ShapeDtypeStruct((lf, dil * Q_WIDTH), out_dtype)]
    if want_lse:
        out_specs.append(pl.BlockSpec((BLK, Q_WIDTH), cur))
        out_shape.append(jax.ShapeDtypeStruct((lf, dil * Q_WIDTH), F32))
    res = pl.pallas_call(
        functools.partial(_band_body, max_dist=max_dist, has_sink=sinks is not None, want_lse=want_lse),
        grid=(dil, nb), in_specs=in_specs, out_specs=out_specs, out_shape=out_shape,
        compiler_params=_params(("parallel", "parallel")),
        name=f"band_d{dil}",
    )(*args)
    return [t.reshape(L, Q_WIDTH) for t in res]


def _sample_attn_body(*refs, nb, has_sink):
    if has_sink:
        sink_ref, refs = refs[0], refs[1:]
    q_ref, kn_ref, vn_ref, kc_ref, vc_ref, mc_ref, mn_ref, o_ref = refs
    g_lo = lax.broadcasted_iota(jnp.int32, (TILE, 128), 1) < HEAD_DIM
    zero = jnp.zeros((), BF16)
    mult_c = mc_ref[...]
    mult_n = mn_ref[...]
    pad = jnp.zeros((128 - TILE, 128), BF16)
    for b in range(nb):
        q = q_ref[TILE * b:TILE * (b + 1), :]
        parts = []
        for r in range(2):
            qr = q[:, 128 * r:128 * (r + 1)]
            parts += [jnp.where(g_lo, qr, zero), jnp.where(g_lo, zero, qr)]
        qm = jnp.concatenate(parts, axis=0)
        kc = kc_ref[b].astype(BF16)
        vc = vc_ref[b].astype(BF16)
        kn = jnp.concatenate([kn_ref[TILE * b:TILE * (b + 1), :].astype(BF16), pad], axis=0)
        vn = jnp.concatenate([vn_ref[TILE * b:TILE * (b + 1), :].astype(BF16), pad], axis=0)
        sc = jnp.where(mult_c > 0, _dot_nt(qm, kc), -jnp.inf)
        sn = jnp.where(mult_n > 0, _dot_nt(qm, kn), -jnp.inf)
        m = jnp.maximum(jnp.max(sc, axis=-1, keepdims=True), jnp.max(sn, axis=-1, keepdims=True))
        if has_sink:
            sk = sink_ref[:, :1]
            m = jnp.maximum(m, sk)
        ec = mult_c * jnp.exp(sc - m)
        en = mult_n * jnp.exp(sn - m)
        den = jnp.sum(ec, axis=-1, keepdims=True) + jnp.sum(en, axis=-1, keepdims=True)
        if has_sink:
            den = den + jnp.exp(sk - m)
        o = (jnp.dot(ec.astype(BF16), vc, preferred_element_type=F32)
             + jnp.dot(en.astype(BF16), vn, preferred_element_type=F32)) / den
        o_ref[TILE * b:TILE * (b + 1), :] = jnp.concatenate(
            [jnp.where(g_lo, o[0:TILE], o[TILE:2 * TILE]),
             jnp.where(g_lo, o[2 * TILE:3 * TILE], o[3 * TILE:4 * TILE])], axis=1).astype(o_ref.dtype)


def _sample_attn(q, kn, vn, cache_k, cache_v, layer, mult_c, mult_n, sinks_rows, nb):
    rows = q.shape[0]
    nbatch = rows // TILE
    w = cache_k.shape[2]
    row_spec = lambda width: pl.BlockSpec((TILE * nb, width), lambda i: (i, 0))
    cache_spec = pl.BlockSpec((None, nb, w, KV_WIDTH), lambda i: (layer, i, 0, 0))
    in_specs = [row_spec(Q_WIDTH), row_spec(KV_WIDTH), row_spec(KV_WIDTH), cache_spec, cache_spec,
                _const_spec((4 * TILE, w)), _const_spec((4 * TILE, 128))]
    args = [q, kn, vn, cache_k, cache_v, mult_c, mult_n]
    if sinks_rows is not None:
        in_specs = [_const_spec((4 * TILE, 128))] + in_specs
        args = [sinks_rows] + args
    return pl.pallas_call(
        functools.partial(_sample_attn_body, nb=nb, has_sink=sinks_rows is not None),
        grid=(nbatch // nb,), in_specs=in_specs, out_specs=row_spec(Q_WIDTH),
        out_shape=jax.ShapeDtypeStruct((rows, Q_WIDTH), BF16),
        compiler_params=_params(("parallel",)),
        name=f"sample_attn_w{w}",
    )(*args)


def _sample_mult_tables():
    t = np.arange(TILE) - TOK0
    tq = np.maximum(t, 0)[:, None]
    j = np.arange(A_WINDOW)[None, :]
    da = A_WINDOW + tq - j
    ma_c = ((da >= 0) & (da < A_WINDOW)).astype(np.float32)
    tn = (np.arange(128) - TOK0)[None, :]
    dn = tq - tn
    new_ok = (tn >= 0) & (tn < TILE - TOK0) & (dn >= 0)
    ma_n = (new_ok & (dn < A_WINDOW)).astype(np.float32)

    def mult(d):
        out = np.zeros(d.shape, np.float32)
        for w, dil in C_PATTERNS:
            out += ((d >= 0) & (d <= w) & (d % dil == 0)).astype(np.float32)
        return out

    jc = np.arange(C_SPAN)[None, :]
    mc_c = mult(C_SPAN + tq - jc)
    mc_n = np.where(new_ok, mult(dn), 0.0).astype(np.float32)
    tile4 = lambda a: jnp.asarray(np.tile(a, (4, 1)))
    return tile4(ma_c), tile4(ma_n), tile4(mc_c), tile4(mc_n)


def _ssd_intra(act, dt_raw, dtb, alog, tri_sel, tri_mask, expand, row_ok):
    xs = act[:, :B_INNER]
    bb = act[:, B_INNER:B_INNER + 2 * B_STATE].astype(BF16)
    cb_ = act[:, B_INNER + 2 * B_STATE:].astype(BF16)
    dtv = _softplus(dt_raw + dtb)
    if row_ok is not None:
        dtv = jnp.where(row_ok, dtv, 0.0)
    a = -jnp.exp(alog) * dtv
    acum = _sel_dot_rhs(tri_sel, a)
    acum_t = acum.T
    dt_e = _sel_dot_lhs(dtv, expand)
    xdt = xs * dt_e
    lane_lo = lax.broadcasted_iota(jnp.int32, (BLK, 128), 1) < HEAD_DIM
    ys = []
    for g in range(2):
        cbm = _dot_nt(cb_[:, 128 * g:128 * (g + 1)], bb[:, 128 * g:128 * (g + 1)])
        for pair in range(2):
            xp = xdt[:, 128 * (2 * g + pair):128 * (2 * g + pair + 1)]
            acc = None
            for j in range(2):
                h = 4 * g + 2 * pair + j
                seg = acum[:, h:h + 1] - acum_t[h:h + 1, :]
                lm = jnp.exp(jnp.where(tri_mask, seg, -jnp.inf))
                mh = (cbm * lm).astype(BF16)
                xh = jnp.where(lane_lo if j == 0 else ~lane_lo, xp, 0.0).astype(BF16)
                t = jnp.dot(mh, xh, preferred_element_type=F32)
                acc = t if acc is None else acc + t
            ys.append(acc)
    ydiag = jnp.concatenate(ys, axis=1)
    return xs, bb, cb_, xdt, acum, acum_t, ydiag


def _ssd_finish(y, z, norm_w):
    y = y * _silu(z)
    ms = jnp.mean(y * y, axis=-1, keepdims=True)
    return (y * lax.rsqrt(ms + NORM_EPS) * norm_w).astype(BF16)


def _conv_act(xp_ref, cw_ref, cb_ref):
    t = BLK
    out = cb_ref[...] + xp_ref[5:5 + t, :] * cw_ref[0:1, :]
    out = out + xp_ref[6:6 + t, :] * cw_ref[1:2, :]
    out = out + xp_ref[7:7 + t, :] * cw_ref[2:3, :]
    out = out + xp_ref[8:8 + t, :] * cw_ref[3:4, :]
    return _silu(out)


def _ssd_prompt_body(xbc_ref, z_ref, dt_ref, cw_ref, cb_ref, dtb_ref, alog_ref, dskip_ref, nw_ref,
                     expand_ref, expand_t_ref, mo_ref, hs_ref, xp_ref):
    c = pl.program_id(0)
    t = BLK

    @pl.when(c == 0)
    def _():
        xp_ref[0:TILE, :] = jnp.zeros((TILE, CONV_DIM), F32)
        hs_ref[...] = jnp.zeros_like(hs_ref)

    xp_ref[TILE:, :] = xbc_ref[...]
    act = _conv_act(xp_ref, cw_ref, cb_ref)
    xp_ref[0:TILE, :] = xp_ref[t:t + TILE, :]

    row = lax.broadcasted_iota(jnp.int32, (t, t), 0)
    col = lax.broadcasted_iota(jnp.int32, (t, t), 1)
    tri_mask = col <= row
    tri_sel = tri_mask.astype(BF16)
    expand = expand_ref[...]
    xs, bb, cb_, xdt, acum, acum_t, ydiag = _ssd_intra(
        act, dt_ref[...], dtb_ref[...], alog_ref[...], tri_sel, tri_mask, expand, None)

    hst = hs_ref[...]
    hb = hst.astype(BF16)
    yoff = jnp.concatenate([_dot_nt(cb_[:, 128 * g:128 * (g + 1)], hb[256 * g:256 * (g + 1), :]) for g in range(2)],
                           axis=1)
    e_all = _sel_dot_lhs(jnp.exp(acum), expand)
    y = ydiag + yoff * e_all + xs * dskip_ref[...]
    mo_ref[...] = _ssd_finish(y, z_ref[...], nw_ref[...])

    dec_e = _sel_dot_lhs(jnp.exp(acum[t - 1:t, :] - acum), expand)
    xw = xdt * dec_e
    dcol = _sel_dot_rhs(expand_t_ref[...], jnp.exp(acum_t))
    cd = dcol[:, t - 1:t]
    for g in range(2):
        xw_t = xw[:, 256 * g:256 * (g + 1)].T.astype(BF16)
        dh = jnp.dot(xw_t, bb[:, 128 * g:128 * (g + 1)], preferred_element_type=F32)
        hs_ref[256 * g:256 * (g + 1), :] = hst[256 * g:256 * (g + 1), :] * cd[256 * g:256 * (g + 1), :] + dh


def _ssd_prompt(xbc, z, dt, lw):
    L = xbc.shape[0]
    row_spec = lambda w: pl.BlockSpec((BLK, w), lambda c: (c, 0))
    return pl.pallas_call(
        _ssd_prompt_body,
        grid=(L // BLK,),
        in_specs=[row_spec(CONV_DIM), row_spec(B_INNER), row_spec(128),
                  _const_spec((CONV_K, CONV_DIM)), _const_spec((1, CONV_DIM)), _const_spec((1, 128)),
                  _const_spec((1, 128)), _const_spec((1, B_INNER)), _const_spec((1, B_INNER)),
                  _const_spec((128, B_INNER)), _const_spec((B_INNER, 128))],
        out_specs=[row_spec(B_INNER), pl.BlockSpec((B_INNER, B_STATE), lambda c: (0, 0))],
        out_shape=[jax.ShapeDtypeStruct((L, B_INNER), BF16), jax.ShapeDtypeStruct((B_INNER, B_STATE), F32)],
        scratch_shapes=[pltpu.VMEM((BLK + TILE, CONV_DIM), F32)],
        compiler_params=_params(("arbitrary",)),
        name="ssd_prompt",
    )(xbc, z, dt, lw['conv_w'], lw['conv_b'], lw['dt_bias'], lw['a_log'], lw['d_skip'], lw['ssm_norm'],
      lw['expand'], lw['expand_t'])


def _ssd_sample_body(xbc_ref, pre_ref, z_ref, dt_ref, h0_ref, cw_ref, cb_ref, dtb_ref, alog_ref, dskip_ref,
                     nw_ref, expand_ref, expand_t_ref, mo_ref, hout_ref, xp_ref):
    t = BLK
    nbt = t // TILE
    rmod = lax.broadcasted_iota(jnp.int32, (t, 1), 0) & (TILE - 1)
    row_ok = rmod >= TOK0
    xp_ref[0:TILE, :] = jnp.zeros((TILE, CONV_DIM), F32)
    xp_ref[TILE:, :] = jnp.where(row_ok, xbc_ref[...], pre_ref[...])
    act = _conv_act(xp_ref, cw_ref, cb_ref)

    row = lax.broadcasted_iota(jnp.int32, (t, t), 0)
    col = lax.broadcasted_iota(jnp.int32, (t, t), 1)
    same = (row // TILE) == (col // TILE)
    tri_mask = (col <= row) & same
    tri_sel = tri_mask.astype(BF16)
    last_sel = (col == (row // TILE) * TILE + (TILE - 1)).astype(BF16)
    expand = expand_ref[...]
    xs, bb, cb_, xdt, acum, acum_t, ydiag = _ssd_intra(
        act, dt_ref[...], dtb_ref[...], alog_ref[...], tri_sel, tri_mask, expand, row_ok)

    yoffs = []
    for b in range(nbt):
        hb = h0_ref[b].astype(BF16)
        yoffs.append(jnp.concatenate(
            [_dot_nt(cb_[TILE * b:TILE * (b + 1), 128 * g:128 * (g + 1)], hb[256 * g:256 * (g + 1), :])
             for g in range(2)], axis=1))
    yoff = jnp.concatenate(yoffs, axis=0)
    e_all = _sel_dot_lhs(jnp.exp(acum), expand)
    y = ydiag + yoff * e_all + xs * dskip_ref[...]
    mo_ref[...] = _ssd_finish(y, z_ref[...], nw_ref[...])

    last = _sel_dot_rhs(last_sel, acum)
    dec_e = _sel_dot_lhs(jnp.exp(last - acum), expand)
    xw = xdt * dec_e
    dcol = _sel_dot_rhs(expand_t_ref[...], jnp.exp(acum_t))
    lane = lax.broadcasted_iota(jnp.int32, (256, t), 1)
    xw_ts = [xw[:, 256 * g:256 * (g + 1)].T for g in range(2)]
    for b in range(nbt):
        in_b = (lane // TILE) == b
        cd = dcol[:, TILE * b + TILE - 1:TILE * b + TILE]
        h0 = h0_ref[b]
        for g in range(2):
            lhs = jnp.where(in_b, xw_ts[g], 0.0).astype(BF16)
            dh = jnp.dot(lhs, bb[:, 128 * g:128 * (g + 1)], preferred_element_type=F32)
            hout_ref[b, 256 * g:256 * (g + 1), :] = (
                h0[256 * g:256 * (g + 1), :] * cd[256 * g:256 * (g + 1), :] + dh)


def _ssd_sample(xbc, prefix, z, dt, state, layer, lw):
    rows = xbc.shape[0]
    nbt = BLK // TILE
    row_spec = lambda w: pl.BlockSpec((BLK, w), lambda i: (i, 0))
    st_spec = pl.BlockSpec((None, nbt, B_INNER, B_STATE), lambda i: (layer, i, 0, 0))
    nbatch = rows // TILE
    return pl.pallas_call(
        _ssd_sample_body,
        grid=(rows // BLK,),
        in_specs=[row_spec(CONV_DIM), row_spec(CONV_DIM), row_spec(B_INNER), row_spec(128), st_spec,
                  _const_spec((CONV_K, CONV_DIM)), _const_spec((1, CONV_DIM)), _const_spec((1, 128)),
                  _const_spec((1, 128)), _const_spec((1, B_INNER)), _const_spec((1, B_INNER)),
                  _const_spec((128, B_INNER)), _const_spec((B_INNER, 128))],
        out_specs=[row_spec(B_INNER), pl.BlockSpec((nbt, B_INNER, B_STATE), lambda i: (i, 0, 0))],
        out_shape=[jax.ShapeDtypeStruct((rows, B_INNER), BF16),
                   jax.ShapeDtypeStruct((nbatch, B_INNER, B_STATE), F32)],
        scratch_shapes=[pltpu.VMEM((BLK + TILE, CONV_DIM), F32)],
        compiler_params=_params(("parallel",)),
        name="ssd_sample",
    )(xbc, prefix, z, dt, state, lw['conv_w'], lw['conv_b'], lw['dt_bias'], lw['a_log'], lw['d_skip'],
      lw['ssm_norm'], lw['expand'], lw['expand_t'])


def _out_ffn_body(*refs, merged):
    if merged:
        h_ref, a_ref, m_ref, c_ref = refs[:4]
        rest = refs[4:]
        c_mix = c_ref[...]
    else:
        h_ref, a_ref, m_ref = refs[:3]
        o0, l0, o1, l1, o2, l2 = refs[3:9]
        rest = refs[9:]
        la, lb, lc = l0[...], l1[...], l2[...]
        mx = jnp.maximum(jnp.maximum(la, lb), lc)
        ea, eb, ec = jnp.exp(la - mx), jnp.exp(lb - mx), jnp.exp(lc - mx)
        c_mix = ((ea * o0[...] + eb * o1[...] + ec * o2[...]) / (ea + eb + ec)).astype(BF16)
    wo_ref, g2_ref, wg_ref, wu_ref, wd_ref, out_ref = rest
    d = functools.partial(jnp.dot, preferred_element_type=F32)
    h1 = (h_ref[...] + d(a_ref[...], wo_ref[0:256, :]) + d(m_ref[...], wo_ref[256:768, :])
          + d(c_mix, wo_ref[768:1024, :]))
    ms = jnp.mean(h1 * h1, axis=-1, keepdims=True)
    u = (h1 * lax.rsqrt(ms + NORM_EPS) * g2_ref[...]).astype(BF16)
    out_ref[...] = h1
    for c in range(D_FF // FF_CHUNK):
        sl = slice(FF_CHUNK * c, FF_CHUNK * (c + 1))
        act = (_silu(d(u, wg_ref[:, sl])) * d(u, wu_ref[:, sl])).astype(BF16)
        out_ref[...] += d(act, wd_ref[sl, :])


def _out_ffn(h, a_o, m_o, c_parts, lw, tm):
    rows = h.shape[0]
    merged = len(c_parts) == 1
    row_spec = lambda w: pl.BlockSpec((tm, w), lambda i: (i, 0))
    in_specs = [row_spec(D_MODEL), row_spec(Q_WIDTH), row_spec(B_INNER)] + [row_spec(Q_WIDTH)] * len(c_parts)
    in_specs += [_const_spec((D_MODEL, D_MODEL)), _const_spec((1, D_MODEL)), _const_spec((D_MODEL, D_FF)),
                 _const_spec((D_MODEL, D_FF)), _const_spec((D_FF, D_MODEL))]
    return pl.pallas_call(
        functools.partial(_out_ffn_body, merged=merged),
        grid=(rows // tm,), in_specs=in_specs, out_specs=row_spec(D_MODEL),
        out_shape=jax.ShapeDtypeStruct((rows, D_MODEL), F32),
        compiler_params=_params(("parallel",)),
        name="out_ffn",
    )(h, a_o, m_o, *c_parts, lw['w_out'], lw['norm2'], lw['w_gate'], lw['w_up'], lw['w_down'])


def _rope_table(pos):
    half = ROT_DIM // 2
    inv = ROPE_THETA ** (-(jnp.arange(half, dtype=F32) * 2.0 / ROT_DIM))
    ang = pos.astype(F32)[:, None] * inv[None, :]
    cos, sin = jnp.cos(ang), jnp.sin(ang)
    n = pos.shape[0]
    c64 = jnp.concatenate([cos, cos, jnp.ones((n, HEAD_DIM - ROT_DIM), F32)], axis=1)
    s64 = jnp.concatenate([-sin, sin, jnp.zeros((n, HEAD_DIM - ROT_DIM), F32)], axis=1)
    return jnp.concatenate([c64, c64, s64, s64], axis=1)


def _layer_weights(l, norm1, w_in, a_qn, a_kn, a_sinks, c_qn, c_kn, conv_w, conv_b, dt_bias, a_log, d_skip,
                   ssm_norm, w_out, norm2, w_gate, w_up, w_down):
    w = w_in[l]
    cols = np.arange(N_IN)
    cols[_OFF['aq']:_OFF['aq'] + 256] = _OFF['aq'] + _HEAD_PERM
    cols[_OFF['cq']:_OFF['cq'] + 256] = _OFF['cq'] + _HEAD_PERM
    w = jnp.pad(w[:, cols], ((0, 0), (0, N_IN_PAD - N_IN))).astype(BF16)
    rows = np.arange(D_MODEL)
    rows[0:256] = _HEAD_PERM
    rows[768:1024] = 768 + _HEAD_PERM
    pad8 = lambda v: jnp.pad(v.astype(F32), (0, 128 - B_HEADS))[None, :]
    head_of = np.arange(B_INNER) // 64
    expand = (np.arange(128)[:, None] == head_of[None, :]).astype(np.float32)
    blk = (np.arange(256)[:, None] // 64 == np.arange(256)[None, :] // 64).astype(np.float32) / 64.0
    return dict(
        norm1=norm1[l][None, :], w_in=w,
        a_qn=jnp.tile(a_qn[l], 4)[None, :], a_kn=jnp.tile(a_kn[l], 2)[None, :],
        c_qn=jnp.tile(c_qn[l], 4)[None, :], c_kn=jnp.tile(c_kn[l], 2)[None, :],
        bd=jnp.asarray(blk, BF16),
        sinks=a_sinks[l].astype(F32),
        conv_w=conv_w[l], conv_b=conv_b[l][None, :], dt_bias=pad8(dt_bias[l]), a_log=pad8(a_log[l]),
        d_skip=jnp.repeat(d_skip[l].astype(F32), 64)[None, :], ssm_norm=ssm_norm[l][None, :],
        expand=jnp.asarray(expand, BF16), expand_t=jnp.asarray(expand.T, BF16),
        w_out=w_out[l][rows, :].astype(BF16), norm2=norm2[l][None, :],
        w_gate=w_gate[l].astype(BF16), w_up=w_up[l].astype(BF16), w_down=w_down[l].astype(BF16))


def kernel(x_prompt, x_sample, cache_a_k, cache_a_v, cache_c_k, cache_c_v, state_ssm, state_conv, norm1, w_in,
           a_qn, a_kn, a_sinks, c_qn, c_kn, conv_w, conv_b, dt_bias, a_log, d_skip, ssm_norm, w_out, norm2,
           w_gate, w_up, w_down):
    depth = w_in.shape[0]
    batch, seq, _ = x_prompt.shape
    nbatch, dec_seq, _ = x_sample.shape
    assert batch == 1 and dec_seq == TILE - TOK0 and seq % (16 * BLK) == 0 and nbatch % (BLK // TILE) == 0
    past_len = PAST_LEN
    a_buf, c_buf = cache_a_k.shape[2], cache_c_k.shape[2]
    assert a_buf == A_WINDOW and c_buf == C_SPAN

    hp = x_prompt.reshape(seq, D_MODEL)
    hs = jnp.pad(x_sample, ((0, 0), (TOK0, 0), (0, 0))).reshape(nbatch * TILE, D_MODEL)
    tab_p = _rope_table(jnp.arange(seq))
    pos_s = past_len + jnp.maximum(jnp.arange(TILE) - TOK0, 0)
    tab_s = jnp.tile(_rope_table(pos_s), (nbatch, 1))
    ma_c, ma_n, mc_c, mc_n = _sample_mult_tables()

    ca_k = cache_a_k.reshape(depth, nbatch, a_buf, KV_WIDTH)
    ca_v = cache_a_v.reshape(depth, nbatch, a_buf, KV_WIDTH)
    cc_k = cache_c_k.reshape(depth, nbatch, c_buf, KV_WIDTH)
    cc_v = cache_c_v.reshape(depth, nbatch, c_buf, KV_WIDTH)
    st = state_ssm.reshape(depth, nbatch, B_INNER, B_STATE)

    p_out = [[] for _ in range(6)]
    s_out = [[] for _ in range(6)]
    tm_p = 512
    tm_s = min(512, nbatch * TILE)
    for l in range(depth):
        lw = _layer_weights(l, norm1, w_in, a_qn, a_kn, a_sinks, c_qn, c_kn, conv_w, conv_b, dt_bias, a_log,
                            d_skip, ssm_norm, w_out, norm2, w_gate, w_up, w_down)
        qa, ka, va, qc, kc, vc, z, xbc, dt = _in_proj(hp, lw, tab_p, tm_p)
        (a_o,) = _banded(qa, ka, va, 1, A_WINDOW - 1, sinks=lw['sinks'])
        c_parts = []
        for w, dil in C_PATTERNS:
            c_parts += _banded(qc, kc, vc, dil, w // dil, want_lse=True, out_dtype=F32)
        m_o, h_fin = _ssd_prompt(xbc, z, dt, lw)
        hp = _out_ffn(hp, a_o, m_o, c_parts, lw, tm_p)
        p_out[0].append(ka[seq - a_buf:].reshape(1, a_buf, 2, HEAD_DIM))
        p_out[1].append(va[seq - a_buf:].reshape(1, a_buf, 2, HEAD_DIM))
        p_out[2].append(kc[seq - c_buf:].reshape(1, c_buf, 2, HEAD_DIM))
        p_out[3].append(vc[seq - c_buf:].reshape(1, c_buf, 2, HEAD_DIM))
        p_out[4].append(h_fin.reshape(1, B_HEADS, 64, B_STATE))
        p_out[5].append(xbc[seq - (CONV_K - 1):].reshape(1, CONV_K - 1, CONV_DIM))

        qa, ka, va, qc, kc, vc, z, xbc, dt = _in_proj(hs, lw, tab_s, tm_s)
        sink_rows = jnp.broadcast_to(jnp.repeat(lw['sinks'][jnp.asarray([0, 2, 1, 3])], TILE)[:, None], (4 * TILE, 128))
        a_o = _sample_attn(qa, ka, va, ca_k, ca_v, l, ma_c, ma_n, sink_rows, nb=8)
        c_o = _sample_attn(qc, kc, vc, cc_k, cc_v, l, mc_c, mc_n, None, nb=4)
        prefix = jnp.pad(state_conv[l], ((0, 0), (1, TILE - CONV_K), (0, 0))).reshape(nbatch * TILE, CONV_DIM)
        m_o, h_new = _ssd_sample(xbc, prefix, z, dt, st, l, lw)
        hs = _out_ffn(hs, a_o, m_o, [c_o], lw, tm_s)
        new = lambda t_, w_: t_.reshape(nbatch, TILE, *w_)[:, TOK0:]
        s_out[0].append(jnp.concatenate([cache_a_k[l][:, dec_seq:], new(ka, (2, HEAD_DIM))], axis=1))
        s_out[1].append(jnp.concatenate([cache_a_v[l][:, dec_seq:], new(va, (2, HEAD_DIM))], axis=1))
        s_out[2].append(jnp.concatenate([cache_c_k[l][:, dec_seq:], new(kc, (2, HEAD_DIM))], axis=1))
        s_out[3].append(jnp.concatenate([cache_c_v[l][:, dec_seq:], new(vc, (2, HEAD_DIM))], axis=1))
        s_out[4].append(h_new.reshape(nbatch, B_HEADS, 64, B_STATE))
        s_out[5].append(xbc.reshape(nbatch, TILE, CONV_DIM)[:, TILE - (CONV_K - 1):])

    outs_p = [jnp.stack(t, axis=0) for t in p_out]
    outs_s = [jnp.stack(t, axis=0) for t in s_out]
    y_p = hp.reshape(1, seq, D_MODEL)
    y_s = hs.reshape(nbatch, TILE, D_MODEL)[:, TOK0:]
    return (y_p, y_s, *outs_p, *outs_s)
```

```python
import functools
import math

import numpy as np
import jax
import jax.numpy as jnp
from jax import lax
from jax.experimental import pallas as pl
from jax.experimental.pallas import tpu as pltpu

F32 = jnp.float32
BF16 = jnp.bfloat16

D_MODEL = 1024
HEAD_DIM = 64
ROT_DIM = 16
ROPE_THETA = 500000.0
NORM_EPS = 1e-6
Q_WIDTH = 256
KV_WIDTH = 128
PAST_LEN = 16384
A_WINDOW = 128
C_PATTERNS = ((128, 1), (512, 4), (2048, 16))
C_SPAN = 2048
B_HEADS = 8
B_INNER = 512
B_STATE = 128
CONV_K = 4
CONV_DIM = 1024
D_FF = 2816
N_IN = 2568
N_IN_PAD = 2688
BLK = 128
TOK0 = 4
TILE = 8
FF_CHUNK = 256
VMEM_LIMIT = 56 * 1024 * 1024

_OFF = dict(aq=0, ak=256, av=384, cq=512, ck=768, cv=896, z=1024, xbc=1536, dt=2560, end=N_IN_PAD)
_HEAD_PERM = np.concatenate([np.arange(0, 64), np.arange(128, 192), np.arange(64, 128), np.arange(192, 256)])


def _const_spec(shape):
    nd = len(shape)
    return pl.BlockSpec(shape, lambda *_: (0,) * nd, pipeline_mode=pl.Buffered(1))


def _params(sem):
    return pltpu.CompilerParams(dimension_semantics=sem, vmem_limit_bytes=VMEM_LIMIT)


def _split3(v):
    hi = v.astype(BF16)
    r1 = v - hi.astype(F32)
    mid = r1.astype(BF16)
    lo = (r1 - mid.astype(F32)).astype(BF16)
    return hi, mid, lo


def _sel_dot_rhs(sel, v):
    hi, mid, lo = _split3(v)
    d = functools.partial(jnp.dot, preferred_element_type=F32)
    return d(sel, hi) + d(sel, mid) + d(sel, lo)


def _sel_dot_lhs(v, sel):
    hi, mid, lo = _split3(v)
    d = functools.partial(jnp.dot, preferred_element_type=F32)
    return d(hi, sel) + d(mid, sel) + d(lo, sel)


def _dot_nt(a, b):
    return lax.dot_general(a, b, (((1,), (1,)), ((), ())), preferred_element_type=F32)


def _silu(x):
    return x * jax.nn.sigmoid(x)


def _softplus(x):
    return jnp.maximum(x, 0.0) + jnp.log(1.0 + jnp.exp(-jnp.abs(x)))


def _head_norm(x, gain, bd):
    x2 = x * x
    hi = x2.astype(BF16)
    lo = (x2 - hi.astype(F32)).astype(BF16)
    ms = jnp.dot(hi, bd, preferred_element_type=F32) + jnp.dot(lo, bd, preferred_element_type=F32)
    return x * lax.rsqrt(ms + NORM_EPS) * gain


def _rope(x, cos_t, sin_t):
    w = x.shape[1]
    lane = lax.broadcasted_iota(jnp.int32, x.shape, 1) & (HEAD_DIM - 1)
    partner = jnp.where(lane < ROT_DIM // 2, pltpu.roll(x, w - ROT_DIM // 2, 1), pltpu.roll(x, ROT_DIM // 2, 1))
    return x * cos_t + partner * sin_t


def _in_proj_body(h_ref, g1_ref, w_ref, tab_ref, gqa_ref, gka_ref, gqc_ref, gkc_ref, bd_ref,
                  qa_ref, ka_ref, va_ref, qc_ref, kc_ref, vc_ref, z_ref, xbc_ref, dt_ref):
    x = h_ref[...]
    ms = jnp.mean(x * x, axis=-1, keepdims=True)
    u = (x * lax.rsqrt(ms + NORM_EPS) * g1_ref[...]).astype(BF16)
    tab = tab_ref[...]
    c1, s1 = tab[:, :128], tab[:, 128:]
    c2 = jnp.concatenate([c1, c1], axis=1)
    s2 = jnp.concatenate([s1, s1], axis=1)
    bd2 = bd_ref[...]
    bd1 = bd2[:128, :128]

    def proj(name, nxt):
        return jnp.dot(u, w_ref[:, _OFF[name]:_OFF[nxt]], preferred_element_type=F32)

    qa_ref[...] = (_rope(_head_norm(proj('aq', 'ak'), gqa_ref[...], bd2), c2, s2) * 0.125).astype(BF16)
    ka_ref[...] = _rope(_head_norm(proj('ak', 'av'), gka_ref[...], bd1), c1, s1)
    va_ref[...] = proj('av', 'cq')
    qc_ref[...] = (_rope(_head_norm(proj('cq', 'ck'), gqc_ref[...], bd2), c2, s2) * 0.125).astype(BF16)
    kc_ref[...] = _rope(_head_norm(proj('ck', 'cv'), gkc_ref[...], bd1), c1, s1)
    vc_ref[...] = proj('cv', 'z')
    z_ref[...] = proj('z', 'xbc')
    xbc_ref[...] = proj('xbc', 'dt')
    dt_ref[...] = proj('dt', 'end')


def _in_proj(h, lw, tab, tm):
    rows = h.shape[0]
    grid = (rows // tm,)
    row_spec = lambda w: pl.BlockSpec((tm, w), lambda i: (i, 0))
    widths = (Q_WIDTH, KV_WIDTH, KV_WIDTH, Q_WIDTH, KV_WIDTH, KV_WIDTH, B_INNER, CONV_DIM, 128)
    dtypes = (BF16, F32, F32, BF16, F32, F32, F32, F32, F32)
    return pl.pallas_call(
        _in_proj_body,
        grid=grid,
        in_specs=[row_spec(D_MODEL), _const_spec((1, D_MODEL)), _const_spec((D_MODEL, N_IN_PAD)),
                  row_spec(256), _const_spec((1, 256)), _const_spec((1, 128)), _const_spec((1, 256)),
                  _const_spec((1, 128)), _const_spec((256, 256))],
        out_specs=[row_spec(w) for w in widths],
        out_shape=[jax.ShapeDtypeStruct((rows, w), dt) for w, dt in zip(widths, dtypes)],
        compiler_params=_params(("parallel",)),
        name="in_proj",
    )(h, lw['norm1'], lw['w_in'], tab, lw['a_qn'], lw['a_kn'], lw['c_qn'], lw['c_kn'], lw['bd'])


def _band_body(*refs, max_dist, has_sink, want_lse):
    if has_sink:
        sink_ref, refs = refs[0], refs[1:]
    q_ref, kp_ref, kc_ref, vp_ref, vc_ref = refs[:5]
    o_ref = refs[5]
    n = pl.program_id(1)
    q = q_ref[...]
    kcat = jnp.concatenate([kp_ref[...], kc_ref[...]], axis=0).astype(BF16)
    vcat = jnp.concatenate([vp_ref[...], vc_ref[...]], axis=0).astype(BF16)
    row = lax.broadcasted_iota(jnp.int32, (BLK, 2 * BLK), 0)
    col = lax.broadcasted_iota(jnp.int32, (BLK, 2 * BLK), 1)
    dist = row + BLK - col
    valid = (dist >= 0) & (dist <= max_dist) & ((col >= BLK) | (n > 0))
    g_lo = lax.broadcasted_iota(jnp.int32, (BLK, 128), 1) < HEAD_DIM
    zero = jnp.zeros((), BF16)
    outs, lses = [], []
    for r in range(2):
        qr = q[:, 128 * r:128 * (r + 1)]
        o_g, l_g = [], []
        for g in range(2):
            qm = jnp.where(g_lo if g == 0 else ~g_lo, qr, zero)
            s = jnp.where(valid, _dot_nt(qm, kcat), -jnp.inf)
            m = jnp.max(s, axis=-1, keepdims=True)
            if has_sink:
                sk = sink_ref[2 * g + r]
                m = jnp.maximum(m, sk)
            e = jnp.exp(s - m)
            den = jnp.sum(e, axis=-1, keepdims=True)
            if has_sink:
                den = den + jnp.exp(sk - m)
            pv = jnp.dot(e.astype(BF16), vcat, preferred_element_type=F32)
            o_g.append(pv / den)
            if want_lse:
                l_g.append(jnp.broadcast_to(m + jnp.log(den), (BLK, 128)))
        outs.append(jnp.where(g_lo, o_g[0], o_g[1]))
        if want_lse:
            lses.append(jnp.where(g_lo, l_g[0], l_g[1]))
    o_ref[...] = jnp.concatenate(outs, axis=1).astype(o_ref.dtype)
    if want_lse:
        refs[6][...] = jnp.concatenate(lses, axis=1)


def _banded(q, k, v, dil, max_dist, sinks=None, want_lse=False, out_dtype=BF16):
    L = q.shape[0]
    lf = L // dil
    nb = lf // BLK
    qv = q.reshape(lf, dil * Q_WIDTH)
    kv = k.reshape(lf, dil * KV_WIDTH)
    vv = v.reshape(lf, dil * KV_WIDTH)
    cur = lambda r, n: (n, r)
    prev = lambda r, n: (jnp.maximum(n - 1, 0), r)
    in_specs = [pl.BlockSpec((BLK, Q_WIDTH), cur), pl.BlockSpec((BLK, KV_WIDTH), prev),
                pl.BlockSpec((BLK, KV_WIDTH), cur), pl.BlockSpec((BLK, KV_WIDTH), prev),
                pl.BlockSpec((BLK, KV_WIDTH), cur)]
    args = [qv, kv, kv, vv, vv]
    if sinks is not None:
        in_specs = [pl.BlockSpec(memory_space=pltpu.SMEM)] + in_specs
        args = [sinks] + args
    out_specs = [pl.BlockSpec((BLK, Q_WIDTH), cur)]
    out_shape = [jax.ShapeDtypeStruct((lf, dil * Q_WIDTH), out_dtype)]
    if want_lse:
        out_specs.append(pl.BlockSpec((BLK, Q_WIDTH), cur))
        out_shape.append(jax.ShapeDtypeStruct((lf, dil * Q_WIDTH), F32))
    res = pl.pallas_call(
        functools.partial(_band_body, max_dist=max_dist, has_sink=sinks is not None, want_lse=want_lse),
        grid=(dil, nb), in_specs=in_specs, out_specs=out_specs, out_shape=out_shape,
        compiler_params=_params(("parallel", "parallel")),
        name=f"band_d{dil}",
    )(*args)
    return [t.reshape(L, Q_WIDTH) for t in res]


def _sample_attn_body(*refs, nb, has_sink, n_alias, w):
    if has_sink:
        sink_ref, refs = refs[0], refs[1:]
    q_ref, kn_ref, vn_ref, kc_ref, vc_ref, mc_ref, mn_ref = refs[:7]
    o_ref, ko_ref, vo_ref = refs[7 + n_alias:]
    g_lo = lax.broadcasted_iota(jnp.int32, (TILE, 128), 1) < HEAD_DIM
    lane = lax.broadcasted_iota(jnp.int32, (128, 128), 1)
    is_new = lane >= 128 - (TILE - TOK0)
    zero = jnp.zeros((), BF16)
    mult_c = mc_ref[...]
    mult_n = mn_ref[...]
    pad = jnp.zeros((128 - TILE, 128), F32)
    for b in range(nb):
        q = q_ref[TILE * b:TILE * (b + 1), :]
        parts = []
        for r in range(2):
            qr = q[:, 128 * r:128 * (r + 1)]
            parts += [jnp.where(g_lo, qr, zero), jnp.where(g_lo, zero, qr)]
        qm = jnp.concatenate(parts, axis=0)
        kct = kc_ref[b]
        vct = vc_ref[b]
        knp = jnp.concatenate([kn_ref[TILE * b:TILE * (b + 1), :], pad], axis=0)
        vnp = jnp.concatenate([vn_ref[TILE * b:TILE * (b + 1), :], pad], axis=0)
        knt = knp.T
        vnt = vnp.T
        sc = jnp.where(mult_c > 0, jnp.dot(qm, kct.astype(BF16), preferred_element_type=F32), -jnp.inf)
        sn = jnp.where(mult_n > 0, jnp.dot(qm, knt.astype(BF16), preferred_element_type=F32), -jnp.inf)
        m = jnp.maximum(jnp.max(sc, axis=-1, keepdims=True), jnp.max(sn, axis=-1, keepdims=True))
        if has_sink:
            sk = sink_ref[:, :1]
            m = jnp.maximum(m, sk)
        ec = mult_c * jnp.exp(sc - m)
        en = mult_n * jnp.exp(sn - m)
        den = jnp.sum(ec, axis=-1, keepdims=True) + jnp.sum(en, axis=-1, keepdims=True)
        if has_sink:
            den = den + jnp.exp(sk - m)
        o = (_dot_nt(ec.astype(BF16), vct.astype(BF16))
             + jnp.dot(en.astype(BF16), vnp.astype(BF16), preferred_element_type=F32)) / den
        o_ref[TILE * b:TILE * (b + 1), :] = jnp.concatenate(
            [jnp.where(g_lo, o[0:TILE], o[TILE:2 * TILE]),
             jnp.where(g_lo, o[2 * TILE:3 * TILE], o[3 * TILE:4 * TILE])], axis=1).astype(o_ref.dtype)
        for src, new_t, dst in ((kct, knt, ko_ref), (vct, vnt, vo_ref)):
            shifted = pltpu.roll(src, w - (TILE - TOK0), 1)
            new_cols = pltpu.roll(new_t, 128 - TILE, 1)
            if w > 128:
                dst[b, :, 0:w - 128] = shifted[:, 0:w - 128]
            dst[b, :, w - 128:w] = jnp.where(is_new, new_cols, shifted[:, w - 128:w])


def _sample_attn(q, kn, vn, cache_k, cache_v, layer, mult_c, mult_n, sinks_rows, nb, prev_out):
    rows = q.shape[0]
    nbatch = rows // TILE
    depth, _, _, w = cache_k.shape
    row_spec = lambda width: pl.BlockSpec((TILE * nb, width), lambda i: (i, 0))
    cache_spec = pl.BlockSpec((None, nb, KV_WIDTH, w), lambda i: (layer, i, 0, 0))
    in_specs = [row_spec(Q_WIDTH), row_spec(KV_WIDTH), row_spec(KV_WIDTH), cache_spec, cache_spec,
                _const_spec((4 * TILE, w)), _const_spec((4 * TILE, 128))]
    args = [q, kn, vn, cache_k, cache_v, mult_c, mult_n]
    if sinks_rows is not None:
        in_specs = [_const_spec((4 * TILE, 128))] + in_specs
        args = [sinks_rows] + args
    aliases = {}
    if prev_out is not None:
        aliases = {len(args): 1, len(args) + 1: 2}
        in_specs = in_specs + [pl.BlockSpec(memory_space=pl.ANY)] * 2
        args = args + list(prev_out)
    cache_shape = jax.ShapeDtypeStruct(cache_k.shape, F32)
    return pl.pallas_call(
        functools.partial(_sample_attn_body, nb=nb, has_sink=sinks_rows is not None,
                          n_alias=0 if prev_out is None else 2, w=w),
        grid=(nbatch // nb,), in_specs=in_specs, out_specs=[row_spec(Q_WIDTH), cache_spec, cache_spec],
        out_shape=[jax.ShapeDtypeStruct((rows, Q_WIDTH), BF16), cache_shape, cache_shape],
        input_output_aliases=aliases,
        compiler_params=_params(("parallel",)),
        name=f"sample_attn_w{w}",
    )(*args)


def _sample_mult_tables():
    t = np.arange(TILE) - TOK0
    tq = np.maximum(t, 0)[:, None]
    j = np.arange(A_WINDOW)[None, :]
    da = A_WINDOW + tq - j
    ma_c = ((da >= 0) & (da < A_WINDOW)).astype(np.float32)
    tn = (np.arange(128) - TOK0)[None, :]
    dn = tq - tn
    new_ok = (tn >= 0) & (tn < TILE - TOK0) & (dn >= 0)
    ma_n = (new_ok & (dn < A_WINDOW)).astype(np.float32)

    def mult(d):
        out = np.zeros(d.shape, np.float32)
        for w, dil in C_PATTERNS:
            out += ((d >= 0) & (d <= w) & (d % dil == 0)).astype(np.float32)
        return out

    jc = np.arange(C_SPAN)[None, :]
    mc_c = mult(C_SPAN + tq - jc)
    mc_n = np.where(new_ok, mult(dn), 0.0).astype(np.float32)
    tile4 = lambda a: jnp.asarray(np.tile(a, (4, 1)))
    return tile4(ma_c), tile4(ma_n), tile4(mc_c), tile4(mc_n)


def _ssd_intra(act, dt_raw, dtb, alog, tri_sel, tri_mask, expand, row_ok):
    xs = act[:, :B_INNER]
    bb = act[:, B_INNER:B_INNER + 2 * B_STATE].astype(BF16)
    cb_ = act[:, B_INNER + 2 * B_STATE:].astype(BF16)
    dtv = _softplus(dt_raw + dtb)
    if row_ok is not None:
        dtv = jnp.where(row_ok, dtv, 0.0)
    a = -jnp.exp(alog) * dtv
    acum = _sel_dot_rhs(tri_sel, a)
    acum_t = acum.T
    dt_e = _sel_dot_lhs(dtv, expand)
    xdt = xs * dt_e
    lane_lo = lax.broadcasted_iota(jnp.int32, (BLK, 128), 1) < HEAD_DIM
    ys = []
    for g in range(2):
        cbm = _dot_nt(cb_[:, 128 * g:128 * (g + 1)], bb[:, 128 * g:128 * (g + 1)])
        for pair in range(2):
            xp = xdt[:, 128 * (2 * g + pair):128 * (2 * g + pair + 1)]
            acc = None
            for j in range(2):
                h = 4 * g + 2 * pair + j
                seg = acum[:, h:h + 1] - acum_t[h:h + 1, :]
                lm = jnp.exp(jnp.where(tri_mask, seg, -jnp.inf))
                mh = (cbm * lm).astype(BF16)
                xh = jnp.where(lane_lo if j == 0 else ~lane_lo, xp, 0.0).astype(BF16)
                t = jnp.dot(mh, xh, preferred_element_type=F32)
                acc = t if acc is None else acc + t
            ys.append(acc)
    ydiag = jnp.concatenate(ys, axis=1)
    return xs, bb, cb_, xdt, acum, acum_t, ydiag


def _ssd_finish(y, z, norm_w):
    y = y * _silu(z)
    ms = jnp.mean(y * y, axis=-1, keepdims=True)
    return (y * lax.rsqrt(ms + NORM_EPS) * norm_w).astype(BF16)


def _conv_act(xp_ref, cw_ref, cb_ref):
    t = BLK
    out = cb_ref[...] + xp_ref[5:5 + t, :] * cw_ref[0:1, :]
    out = out + xp_ref[6:6 + t, :] * cw_ref[1:2, :]
    out = out + xp_ref[7:7 + t, :] * cw_ref[2:3, :]
    out = out + xp_ref[8:8 + t, :] * cw_ref[3:4, :]
    return _silu(out)


def _ssd_prompt_body(xbc_ref, z_ref, dt_ref, cw_ref, cb_ref, dtb_ref, alog_ref, dskip_ref, nw_ref,
                     expand_ref, expand_t_ref, mo_ref, hs_ref, xp_ref):
    c = pl.program_id(0)
    t = BLK

    @pl.when(c == 0)
    def _():
        xp_ref[0:TILE, :] = jnp.zeros((TILE, CONV_DIM), F32)
        hs_ref[...] = jnp.zeros_like(hs_ref)

    xp_ref[TILE:, :] = xbc_ref[...]
    act = _conv_act(xp_ref, cw_ref, cb_ref)
    xp_ref[0:TILE, :] = xp_ref[t:t + TILE, :]

    row = lax.broadcasted_iota(jnp.int32, (t, t), 0)
    col = lax.broadcasted_iota(jnp.int32, (t, t), 1)
    tri_mask = col <= row
    tri_sel = tri_mask.astype(BF16)
    expand = expand_ref[...]
    xs, bb, cb_, xdt, acum, acum_t, ydiag = _ssd_intra(
        act, dt_ref[...], dtb_ref[...], alog_ref[...], tri_sel, tri_mask, expand, None)

    hst = hs_ref[...]
    hb = hst.astype(BF16)
    yoff = jnp.concatenate([_dot_nt(cb_[:, 128 * g:128 * (g + 1)], hb[256 * g:256 * (g + 1), :]) for g in range(2)],
                           axis=1)
    e_all = _sel_dot_lhs(jnp.exp(acum), expand)
    y = ydiag + yoff * e_all + xs * dskip_ref[...]
    mo_ref[...] = _ssd_finish(y, z_ref[...], nw_ref[...])

    dec_e = _sel_dot_lhs(jnp.exp(acum[t - 1:t, :] - acum), expand)
    xw = xdt * dec_e
    dcol = _sel_dot_rhs(expand_t_ref[...], jnp.exp(acum_t))
    cd = dcol[:, t - 1:t]
    for g in range(2):
        xw_t = xw[:, 256 * g:256 * (g + 1)].T.astype(BF16)
        dh = jnp.dot(xw_t, bb[:, 128 * g:128 * (g + 1)], preferred_element_type=F32)
        hs_ref[256 * g:256 * (g + 1), :] = hst[256 * g:256 * (g + 1), :] * cd[256 * g:256 * (g + 1), :] + dh


def _ssd_prompt(xbc, z, dt, lw):
    L = xbc.shape[0]
    row_spec = lambda w: pl.BlockSpec((BLK, w), lambda c: (c, 0))
    return pl.pallas_call(
        _ssd_prompt_body,
        grid=(L // BLK,),
        in_specs=[row_spec(CONV_DIM), row_spec(B_INNER), row_spec(128),
                  _const_spec((CONV_K, CONV_DIM)), _const_spec((1, CONV_DIM)), _const_spec((1, 128)),
                  _const_spec((1, 128)), _const_spec((1, B_INNER)), _const_spec((1, B_INNER)),
                  _const_spec((128, B_INNER)), _const_spec((B_INNER, 128))],
        out_specs=[row_spec(B_INNER), pl.BlockSpec((B_INNER, B_STATE), lambda c: (0, 0))],
        out_shape=[jax.ShapeDtypeStruct((L, B_INNER), BF16), jax.ShapeDtypeStruct((B_INNER, B_STATE), F32)],
        scratch_shapes=[pltpu.VMEM((BLK + TILE, CONV_DIM), F32)],
        compiler_params=_params(("arbitrary",)),
        name="ssd_prompt",
    )(xbc, z, dt, lw['conv_w'], lw['conv_b'], lw['dt_bias'], lw['a_log'], lw['d_skip'], lw['ssm_norm'],
      lw['expand'], lw['expand_t'])


def _ssd_sample_body(*refs):
    (xbc_ref, pre_ref, z_ref, dt_ref, h0_ref, cw_ref, cb_ref, dtb_ref, alog_ref, dskip_ref,
     nw_ref, expand_ref, expand_t_ref) = refs[:13]
    mo_ref, hout_ref, xp_ref = refs[-3:]
    t = BLK
    nbt = t // TILE
    rmod = lax.broadcasted_iota(jnp.int32, (t, 1), 0) & (TILE - 1)
    row_ok = rmod >= TOK0
    xp_ref[0:TILE, :] = jnp.zeros((TILE, CONV_DIM), F32)
    xp_ref[TILE:, :] = jnp.where(row_ok, xbc_ref[...], pre_ref[...])
    act = _conv_act(xp_ref, cw_ref, cb_ref)

    row = lax.broadcasted_iota(jnp.int32, (t, t), 0)
    col = lax.broadcasted_iota(jnp.int32, (t, t), 1)
    same = (row // TILE) == (col // TILE)
    tri_mask = (col <= row) & same
    tri_sel = tri_mask.astype(BF16)
    last_sel = (col == (row // TILE) * TILE + (TILE - 1)).astype(BF16)
    expand = expand_ref[...]
    xs, bb, cb_, xdt, acum, acum_t, ydiag = _ssd_intra(
        act, dt_ref[...], dtb_ref[...], alog_ref[...], tri_sel, tri_mask, expand, row_ok)

    yoffs = []
    for b in range(nbt):
        hb = h0_ref[b].astype(BF16)
        yoffs.append(jnp.concatenate(
            [_dot_nt(cb_[TILE * b:TILE * (b + 1), 128 * g:128 * (g + 1)], hb[256 * g:256 * (g + 1), :])
             for g in range(2)], axis=1))
    yoff = jnp.concatenate(yoffs, axis=0)
    e_all = _sel_dot_lhs(jnp.exp(acum), expand)
    y = ydiag + yoff * e_all + xs * dskip_ref[...]
    mo_ref[...] = _ssd_finish(y, z_ref[...], nw_ref[...])

    last = _sel_dot_rhs(last_sel, acum)
    dec_e = _sel_dot_lhs(jnp.exp(last - acum), expand)
    xw = xdt * dec_e
    dcol = _sel_dot_rhs(expand_t_ref[...], jnp.exp(acum_t))
    lane = lax.broadcasted_iota(jnp.int32, (256, t), 1)
    xw_ts = [xw[:, 256 * g:256 * (g + 1)].T for g in range(2)]
    for b in range(nbt):
        in_b = (lane // TILE) == b
        cd = dcol[:, TILE * b + TILE - 1:TILE * b + TILE]
        h0 = h0_ref[b]
        for g in range(2):
            lhs = jnp.where(in_b, xw_ts[g], 0.0).astype(BF16)
            dh = jnp.dot(lhs, bb[:, 128 * g:128 * (g + 1)], preferred_element_type=F32)
            hout_ref[b, 256 * g:256 * (g + 1), :] = (
                h0[256 * g:256 * (g + 1), :] * cd[256 * g:256 * (g + 1), :] + dh)


def _ssd_sample(xbc, prefix, z, dt, state, layer, lw, prev_out):
    rows = xbc.shape[0]
    nbt = BLK // TILE
    row_spec = lambda w: pl.BlockSpec((BLK, w), lambda i: (i, 0))
    st_spec = pl.BlockSpec((None, nbt, B_INNER, B_STATE), lambda i: (layer, i, 0, 0))
    in_specs = [row_spec(CONV_DIM), row_spec(CONV_DIM), row_spec(B_INNER), row_spec(128), st_spec,
                _const_spec((CONV_K, CONV_DIM)), _const_spec((1, CONV_DIM)), _const_spec((1, 128)),
                _const_spec((1, 128)), _const_spec((1, B_INNER)), _const_spec((1, B_INNER)),
                _const_spec((128, B_INNER)), _const_spec((B_INNER, 128))]
    args = [xbc, prefix, z, dt, state, lw['conv_w'], lw['conv_b'], lw['dt_bias'], lw['a_log'], lw['d_skip'],
            lw['ssm_norm'], lw['expand'], lw['expand_t']]
    aliases = {}
    if prev_out is not None:
        aliases = {len(args): 1}
        in_specs.append(pl.BlockSpec(memory_space=pl.ANY))
        args.append(prev_out)
    return pl.pallas_call(
        _ssd_sample_body,
        grid=(rows // BLK,),
        in_specs=in_specs,
        out_specs=[row_spec(B_INNER), st_spec],
        out_shape=[jax.ShapeDtypeStruct((rows, B_INNER), BF16), jax.ShapeDtypeStruct(state.shape, F32)],
        scratch_shapes=[pltpu.VMEM((BLK + TILE, CONV_DIM), F32)],
        input_output_aliases=aliases,
        compiler_params=_params(("parallel",)),
        name="ssd_sample",
    )(*args)


def _out_ffn_body(*refs, merged):
    if merged:
        h_ref, a_ref, m_ref, c_ref = refs[:4]
        rest = refs[4:]
        c_mix = c_ref[...]
    else:
        h_ref, a_ref, m_ref = refs[:3]
        o0, l0, o1, l1, o2, l2 = refs[3:9]
        rest = refs[9:]
        la, lb, lc = l0[...], l1[...], l2[...]
        mx = jnp.maximum(jnp.maximum(la, lb), lc)
        ea, eb, ec = jnp.exp(la - mx), jnp.exp(lb - mx), jnp.exp(lc - mx)
        c_mix = ((ea * o0[...] + eb * o1[...] + ec * o2[...]) / (ea + eb + ec)).astype(BF16)
    wo_ref, g2_ref, wg_ref, wu_ref, wd_ref, out_ref = rest
    d = functools.partial(jnp.dot, preferred_element_type=F32)
    h1 = (h_ref[...] + d(a_ref[...], wo_ref[0:256, :]) + d(m_ref[...], wo_ref[256:768, :])
          + d(c_mix, wo_ref[768:1024, :]))
    ms = jnp.mean(h1 * h1, axis=-1, keepdims=True)
    u = (h1 * lax.rsqrt(ms + NORM_EPS) * g2_ref[...]).astype(BF16)
    out_ref[...] = h1
    for c in range(D_FF // FF_CHUNK):
        sl = slice(FF_CHUNK * c, FF_CHUNK * (c + 1))
        act = (_silu(d(u, wg_ref[:, sl])) * d(u, wu_ref[:, sl])).astype(BF16)
        out_ref[...] += d(act, wd_ref[sl, :])


def _out_ffn(h, a_o, m_o, c_parts, lw, tm):
    rows = h.shape[0]
    merged = len(c_parts) == 1
    row_spec = lambda w: pl.BlockSpec((tm, w), lambda i: (i, 0))
    in_specs = [row_spec(D_MODEL), row_spec(Q_WIDTH), row_spec(B_INNER)] + [row_spec(Q_WIDTH)] * len(c_parts)
    in_specs += [_const_spec((D_MODEL, D_MODEL)), _const_spec((1, D_MODEL)), _const_spec((D_MODEL, D_FF)),
                 _const_spec((D_MODEL, D_FF)), _const_spec((D_FF, D_MODEL))]
    return pl.pallas_call(
        functools.partial(_out_ffn_body, merged=merged),
        grid=(rows // tm,), in_specs=in_specs, out_specs=row_spec(D_MODEL),
        out_shape=jax.ShapeDtypeStruct((rows, D_MODEL), F32),
        compiler_params=_params(("parallel",)),
        name="out_ffn",
    )(h, a_o, m_o, *c_parts, lw['w_out'], lw['norm2'], lw['w_gate'], lw['w_up'], lw['w_down'])


def _rope_table(pos):
    half = ROT_DIM // 2
    inv = ROPE_THETA ** (-(jnp.arange(half, dtype=F32) * 2.0 / ROT_DIM))
    ang = pos.astype(F32)[:, None] * inv[None, :]
    cos, sin = jnp.cos(ang), jnp.sin(ang)
    n = pos.shape[0]
    c64 = jnp.concatenate([cos, cos, jnp.ones((n, HEAD_DIM - ROT_DIM), F32)], axis=1)
    s64 = jnp.concatenate([-sin, sin, jnp.zeros((n, HEAD_DIM - ROT_DIM), F32)], axis=1)
    return jnp.concatenate([c64, c64, s64, s64], axis=1)


def _layer_weights(l, norm1, w_in, a_qn, a_kn, a_sinks, c_qn, c_kn, conv_w, conv_b, dt_bias, a_log, d_skip,
                   ssm_norm, w_out, norm2, w_gate, w_up, w_down):
    w = w_in[l]
    cols = np.arange(N_IN)
    cols[_OFF['aq']:_OFF['aq'] + 256] = _OFF['aq'] + _HEAD_PERM
    cols[_OFF['cq']:_OFF['cq'] + 256] = _OFF['cq'] + _HEAD_PERM
    w = jnp.pad(w[:, cols], ((0, 0), (0, N_IN_PAD - N_IN))).astype(BF16)
    rows = np.arange(D_MODEL)
    rows[0:256] = _HEAD_PERM
    rows[768:1024] = 768 + _HEAD_PERM
    pad8 = lambda v: jnp.pad(v.astype(F32), (0, 128 - B_HEADS))[None, :]
    head_of = np.arange(B_INNER) // 64
    expand = (np.arange(128)[:, None] == head_of[None, :]).astype(np.float32)
    blk = (np.arange(256)[:, None] // 64 == np.arange(256)[None, :] // 64).astype(np.float32) / 64.0
    return dict(
        norm1=norm1[l][None, :], w_in=w,
        a_qn=jnp.tile(a_qn[l], 4)[None, :], a_kn=jnp.tile(a_kn[l], 2)[None, :],
        c_qn=jnp.tile(c_qn[l], 4)[None, :], c_kn=jnp.tile(c_kn[l], 2)[None, :],
        bd=jnp.asarray(blk, BF16),
        sinks=a_sinks[l].astype(F32),
        conv_w=conv_w[l], conv_b=conv_b[l][None, :], dt_bias=pad8(dt_bias[l]), a_log=pad8(a_log[l]),
        d_skip=jnp.repeat(d_skip[l].astype(F32), 64)[None, :], ssm_norm=ssm_norm[l][None, :],
        expand=jnp.asarray(expand, BF16), expand_t=jnp.asarray(expand.T, BF16),
        w_out=w_out[l][rows, :].astype(BF16), norm2=norm2[l][None, :],
        w_gate=w_gate[l].astype(BF16), w_up=w_up[l].astype(BF16), w_down=w_down[l].astype(BF16))


def kernel(x_prompt, x_sample, cache_a_k, cache_a_v, cache_c_k, cache_c_v, state_ssm, state_conv, norm1, w_in,
           a_qn, a_kn, a_sinks, c_qn, c_kn, conv_w, conv_b, dt_bias, a_log, d_skip, ssm_norm, w_out, norm2,
           w_gate, w_up, w_down):
    depth = w_in.shape[0]
    batch, seq, _ = x_prompt.shape
    nbatch, dec_seq, _ = x_sample.shape
    assert batch == 1 and dec_seq == TILE - TOK0 and seq % (16 * BLK) == 0 and nbatch % (BLK // TILE) == 0
    past_len = PAST_LEN
    a_buf, c_buf = cache_a_k.shape[2], cache_c_k.shape[2]
    assert a_buf == A_WINDOW and c_buf == C_SPAN

    hp = x_prompt.reshape(seq, D_MODEL)
    hs = jnp.pad(x_sample, ((0, 0), (TOK0, 0), (0, 0))).reshape(nbatch * TILE, D_MODEL)
    tab_p = _rope_table(jnp.arange(seq))
    pos_s = past_len + jnp.maximum(jnp.arange(TILE) - TOK0, 0)
    tab_s = jnp.tile(_rope_table(pos_s), (nbatch, 1))
    ma_c, ma_n, mc_c, mc_n = _sample_mult_tables()

    to_fm = lambda c: jnp.transpose(c, (0, 1, 3, 4, 2)).reshape(depth, nbatch, KV_WIDTH, c.shape[2])
    from_fm = lambda c: jnp.transpose(c.reshape(depth, nbatch, 2, HEAD_DIM, c.shape[3]), (0, 1, 4, 2, 3))
    ca_k, ca_v, cc_k, cc_v = to_fm(cache_a_k), to_fm(cache_a_v), to_fm(cache_c_k), to_fm(cache_c_v)
    st = state_ssm.reshape(depth, nbatch, B_INNER, B_STATE)

    p_out = [[] for _ in range(6)]
    s_conv = []
    new_a = new_c = new_st = None
    tm_p = 512
    tm_s = min(512, nbatch * TILE)
    for l in range(depth):
        lw = _layer_weights(l, norm1, w_in, a_qn, a_kn, a_sinks, c_qn, c_kn, conv_w, conv_b, dt_bias, a_log,
                            d_skip, ssm_norm, w_out, norm2, w_gate, w_up, w_down)
        qa, ka, va, qc, kc, vc, z, xbc, dt = _in_proj(hp, lw, tab_p, tm_p)
        (a_o,) = _banded(qa, ka, va, 1, A_WINDOW - 1, sinks=lw['sinks'])
        c_parts = []
        for w, dil in C_PATTERNS:
            c_parts += _banded(qc, kc, vc, dil, w // dil, want_lse=True, out_dtype=F32)
        m_o, h_fin = _ssd_prompt(xbc, z, dt, lw)
        hp = _out_ffn(hp, a_o, m_o, c_parts, lw, tm_p)
        p_out[0].append(ka[seq - a_buf:].reshape(1, a_buf, 2, HEAD_DIM))
        p_out[1].append(va[seq - a_buf:].reshape(1, a_buf, 2, HEAD_DIM))
        p_out[2].append(kc[seq - c_buf:].reshape(1, c_buf, 2, HEAD_DIM))
        p_out[3].append(vc[seq - c_buf:].reshape(1, c_buf, 2, HEAD_DIM))
        p_out[4].append(h_fin.reshape(1, B_HEADS, 64, B_STATE))
        p_out[5].append(xbc[seq - (CONV_K - 1):].reshape(1, CONV_K - 1, CONV_DIM))

        qa, ka, va, qc, kc, vc, z, xbc, dt = _in_proj(hs, lw, tab_s, tm_s)
        sink_rows = jnp.broadcast_to(jnp.repeat(lw['sinks'][jnp.asarray([0, 2, 1, 3])], TILE)[:, None], (4 * TILE, 128))
        a_o, *new_a = _sample_attn(qa, ka, va, ca_k, ca_v, l, ma_c, ma_n, sink_rows, 8, new_a)
        c_o, *new_c = _sample_attn(qc, kc, vc, cc_k, cc_v, l, mc_c, mc_n, None, 4, new_c)
        prefix = jnp.pad(state_conv[l], ((0, 0), (1, TILE - CONV_K), (0, 0))).reshape(nbatch * TILE, CONV_DIM)
        m_o, new_st = _ssd_sample(xbc, prefix, z, dt, st, l, lw, new_st)
        hs = _out_ffn(hs, a_o, m_o, [c_o], lw, tm_s)
        s_conv.append(xbc.reshape(nbatch, TILE, CONV_DIM)[:, TILE - (CONV_K - 1):])

    outs_p = [jnp.stack(t, axis=0) for t in p_out]
    outs_s = [from_fm(new_a[0]), from_fm(new_a[1]), from_fm(new_c[0]), from_fm(new_c[1]),
              new_st.reshape(depth, nbatch, B_HEADS, 64, B_STATE), jnp.stack(s_conv, axis=0)]
    y_p = hp.reshape(1, seq, D_MODEL)
    y_s = hs.reshape(nbatch, TILE, D_MODEL)[:, TOK0:]
    return (y_p, y_s, *outs_p, *outs_s)
```

```python
import functools
import math

import numpy as np
import jax
import jax.numpy as jnp
from jax import lax
from jax.experimental import pallas as pl
from jax.experimental.pallas import tpu as pltpu

F32 = jnp.float32
BF16 = jnp.bfloat16

D_MODEL = 1024
HEAD_DIM = 64
ROT_DIM = 16
ROPE_THETA = 500000.0
NORM_EPS = 1e-6
Q_WIDTH = 256
KV_WIDTH = 128
PAST_LEN = 16384
A_WINDOW = 128
C_PATTERNS = ((128, 1), (512, 4), (2048, 16))
C_SPAN = 2048
B_HEADS = 8
B_INNER = 512
B_STATE = 128
CONV_K = 4
CONV_DIM = 1024
D_FF = 2816
N_IN = 2568
N_IN_PAD = 2688
BLK = 128
ATTN_SUPER = 2048
A_PATTERNS = ((1, A_WINDOW - 1),)
C_BANDS = tuple((d, w // d) for w, d in C_PATTERNS)
TOK0 = 4
TILE = 8
FF_CHUNK = 256
VMEM_LIMIT = 56 * 1024 * 1024

_OFF = dict(aq=0, ak=256, av=384, cq=512, ck=768, cv=896, z=1024, xbc=1536, dt=2560, end=N_IN_PAD)
_HEAD_PERM = np.concatenate([np.arange(0, 64), np.arange(128, 192), np.arange(64, 128), np.arange(192, 256)])


def _const_spec(shape):
    nd = len(shape)
    return pl.BlockSpec(shape, lambda *_: (0,) * nd, pipeline_mode=pl.Buffered(1))


def _params(sem):
    return pltpu.CompilerParams(dimension_semantics=sem, vmem_limit_bytes=VMEM_LIMIT)


def _split3(v):
    hi = v.astype(BF16)
    r1 = v - hi.astype(F32)
    mid = r1.astype(BF16)
    lo = (r1 - mid.astype(F32)).astype(BF16)
    return hi, mid, lo


def _sel_dot_rhs(sel, v):
    hi, mid, lo = _split3(v)
    d = functools.partial(jnp.dot, preferred_element_type=F32)
    return d(sel, hi) + d(sel, mid) + d(sel, lo)


def _sel_dot_lhs(v, sel):
    hi, mid, lo = _split3(v)
    d = functools.partial(jnp.dot, preferred_element_type=F32)
    return d(hi, sel) + d(mid, sel) + d(lo, sel)


def _dot_nt(a, b):
    return lax.dot_general(a, b, (((1,), (1,)), ((), ())), preferred_element_type=F32)


def _silu(x):
    return x * jax.nn.sigmoid(x)


def _softplus(x):
    return jnp.maximum(x, 0.0) + jnp.log(1.0 + jnp.exp(-jnp.abs(x)))


def _head_norm(x, gain, bd):
    x2 = x * x
    hi = x2.astype(BF16)
    lo = (x2 - hi.astype(F32)).astype(BF16)
    ms = jnp.dot(hi, bd, preferred_element_type=F32) + jnp.dot(lo, bd, preferred_element_type=F32)
    return x * lax.rsqrt(ms + NORM_EPS) * gain


def _rope(x, cos_t, sin_t):
    w = x.shape[1]
    lane = lax.broadcasted_iota(jnp.int32, x.shape, 1) & (HEAD_DIM - 1)
    partner = jnp.where(lane < ROT_DIM // 2, pltpu.roll(x, w - ROT_DIM // 2, 1), pltpu.roll(x, ROT_DIM // 2, 1))
    return x * cos_t + partner * sin_t


def _in_proj_body(h_ref, g1_ref, w_ref, tab_ref, gqa_ref, gka_ref, gqc_ref, gkc_ref, bd_ref,
                  qa0_ref, qa1_ref, ka_ref, va_ref, qc0_ref, qc1_ref, kc_ref, vc_ref, z_ref, xbc_ref, dt_ref):
    x = h_ref[...]
    ms = jnp.mean(x * x, axis=-1, keepdims=True)
    u = (x * lax.rsqrt(ms + NORM_EPS) * g1_ref[...]).astype(BF16)
    tab = tab_ref[...]
    c1, s1 = tab[:, :128], tab[:, 128:]
    c2 = jnp.concatenate([c1, c1], axis=1)
    s2 = jnp.concatenate([s1, s1], axis=1)
    bd2 = bd_ref[...]
    bd1 = bd2[:128, :128]

    def proj(name, nxt):
        return jnp.dot(u, w_ref[:, _OFF[name]:_OFF[nxt]], preferred_element_type=F32)

    qa = _rope(_head_norm(proj('aq', 'ak'), gqa_ref[...], bd2), c2, s2) * 0.125
    qa0_ref[...] = qa[:, :128]
    qa1_ref[...] = qa[:, 128:]
    ka_ref[...] = _rope(_head_norm(proj('ak', 'av'), gka_ref[...], bd1), c1, s1)
    va_ref[...] = proj('av', 'cq')
    qc = _rope(_head_norm(proj('cq', 'ck'), gqc_ref[...], bd2), c2, s2) * 0.125
    qc0_ref[...] = qc[:, :128]
    qc1_ref[...] = qc[:, 128:]
    kc_ref[...] = _rope(_head_norm(proj('ck', 'cv'), gkc_ref[...], bd1), c1, s1)
    vc_ref[...] = proj('cv', 'z')
    z_ref[...] = proj('z', 'xbc')
    xbc_ref[...] = proj('xbc', 'dt')
    dt_ref[...] = proj('dt', 'end')


def _in_proj(h, lw, tab, tm):
    rows = h.shape[0]
    grid = (rows // tm,)
    row_spec = lambda w: pl.BlockSpec((tm, w), lambda i: (i, 0))
    widths = (128, 128, KV_WIDTH, KV_WIDTH, 128, 128, KV_WIDTH, KV_WIDTH, B_INNER, CONV_DIM, 128)
    dtypes = (F32,) * len(widths)
    return pl.pallas_call(
        _in_proj_body,
        grid=grid,
        in_specs=[row_spec(D_MODEL), _const_spec((1, D_MODEL)), _const_spec((D_MODEL, N_IN_PAD)),
                  row_spec(256), _const_spec((1, 256)), _const_spec((1, 128)), _const_spec((1, 256)),
                  _const_spec((1, 128)), _const_spec((256, 256))],
        out_specs=[row_spec(w) for w in widths],
        out_shape=[jax.ShapeDtypeStruct((rows, w), dt) for w, dt in zip(widths, dtypes)],
        compiler_params=_params(("parallel",)),
        name="in_proj",
    )(h, lw['norm1'], lw['w_in'], tab, lw['a_qn'], lw['a_kn'], lw['c_qn'], lw['c_kn'], lw['bd'])


def _attn_body(*refs, patterns, has_sink, sb):
    if has_sink:
        sink_ref, refs = refs[0], refs[1:]
    q0_ref, q1_ref, kp_ref, kc_ref, vp_ref, vc_ref, o_ref, kk, vv, acc_s, m_s, l_s = refs
    q_refs = (q0_ref, q1_ref)
    j = pl.program_id(0)
    kk[0:sb, :] = kp_ref[...]
    kk[sb:2 * sb, :] = kc_ref[...]
    vv[0:sb, :] = vp_ref[...]
    vv[sb:2 * sb, :] = vc_ref[...]
    row4 = lax.broadcasted_iota(jnp.int32, (4 * BLK, BLK), 0) & (BLK - 1)
    col4 = lax.broadcasted_iota(jnp.int32, (4 * BLK, BLK), 1)
    lower4 = col4 <= row4
    g_lo = lax.broadcasted_iota(jnp.int32, (BLK, 128), 1) < HEAD_DIM
    nblk = sb // BLK

    for pi, (d, max_dist) in enumerate(patterns):
        nsub = nblk // d
        has_diag = max_dist == BLK

        def ld(ref, s0, d=d):
            if d == 1:
                return ref[pl.ds(s0, BLK), :]
            return ref[pl.ds(s0, BLK, stride=d), :]

        def block(t, carry, d=d, nsub=nsub, has_diag=has_diag, first=(pi == 0), ld=ld):
            r_ = t // nsub
            n = t - r_ * nsub
            start = r_ + BLK * d * n
            prev_ok = jnp.logical_or(j > 0, n > 0)
            off = jnp.where(prev_ok, 0, BLK)
            qb = [ld(q_refs[rr], start) for rr in range(2)]
            kprev, kcur = ld(kk, sb + start - BLK * d), ld(kk, sb + start)
            vprev, vcur = ld(vv, sb + start - BLK * d), ld(vv, sb + start)
            qm = jnp.concatenate([jnp.where(g_lo, qb[0], 0.0), jnp.where(g_lo, 0.0, qb[0]),
                                  jnp.where(g_lo, qb[1], 0.0), jnp.where(g_lo, 0.0, qb[1])], axis=0).astype(BF16)
            kcat = jnp.concatenate([kprev, kcur], axis=0).astype(BF16)
            vcat = jnp.concatenate([vprev, vcur], axis=0).astype(BF16)
            s2 = _dot_nt(qm, kcat)
            sp, sc = s2[:, :BLK], s2[:, BLK:]
            up_ok = col4 > row4 + off
            s = jnp.where(up_ok, sp, jnp.where(lower4, sc, -jnp.inf))
            mb = jnp.max(s, axis=-1, keepdims=True)
            if has_diag:
                sd = jnp.sum(jnp.where(col4 == row4 + off, sp, 0.0), axis=-1, keepdims=True)
                sd = sd + jnp.where(prev_ok, 0.0, -jnp.inf)
                mb = jnp.maximum(mb, sd)
            e = jnp.exp(s - mb)
            lb = jnp.sum(e, axis=-1, keepdims=True)
            ecat = jnp.concatenate([jnp.where(up_ok, e, 0.0), jnp.where(lower4, e, 0.0)], axis=1).astype(BF16)
            pv = jnp.dot(ecat, vcat, preferred_element_type=F32)
            if has_diag:
                ed = jnp.exp(sd - mb)
                lb = lb + ed
                pv = pv + ed * jnp.concatenate([vprev] * 4, axis=0)
            for rr in range(2):
                lo, hi = slice(2 * BLK * rr, 2 * BLK * rr + BLK), slice(2 * BLK * rr + BLK, 2 * BLK * (rr + 1))
                o_b = jnp.where(g_lo, pv[lo], pv[hi])
                m_b = jnp.where(g_lo, mb[lo], mb[hi])
                l_b = jnp.where(g_lo, lb[lo], lb[hi])
                rows = pl.ds(start, BLK) if d == 1 else pl.ds(start, BLK, stride=d)
                if first:
                    m_s[rr, rows, :] = m_b
                    l_s[rr, rows, :] = l_b
                    acc_s[rr, rows, :] = o_b
                else:
                    m_old = m_s[rr, rows, :]
                    m_new = jnp.maximum(m_old, m_b)
                    w_old = jnp.exp(m_old - m_new)
                    w_b = jnp.exp(m_b - m_new)
                    m_s[rr, rows, :] = m_new
                    l_s[rr, rows, :] = w_old * l_s[rr, rows, :] + w_b * l_b
                    acc_s[rr, rows, :] = w_old * acc_s[rr, rows, :] + w_b * o_b
            return carry

        lax.fori_loop(0, nblk, block, 0)

    chunk = 256
    for rr in range(2):
        for c in range(sb // chunk):
            rows = slice(chunk * c, chunk * (c + 1))
            m_f, l_f, a_f = m_s[rr, rows, :], l_s[rr, rows, :], acc_s[rr, rows, :]
            if has_sink:
                sk = sink_ref[rr:rr + 1, :]
                m2 = jnp.maximum(m_f, sk)
                w = jnp.exp(m_f - m2)
                o = a_f * w / (l_f * w + jnp.exp(sk - m2))
            else:
                o = a_f / l_f
            o_ref[rows, 128 * rr:128 * (rr + 1)] = o.astype(o_ref.dtype)


def _prompt_attn(q0, q1, k, v, patterns, sink_lanes=None, name="attn"):
    L = q0.shape[0]
    sb = ATTN_SUPER
    cur = lambda j: (j, 0)
    prev = lambda j: (jnp.maximum(j - 1, 0), 0)
    blk = lambda im: pl.BlockSpec((sb, 128), im)
    in_specs = [blk(cur), blk(cur), blk(prev), blk(cur), blk(prev), blk(cur)]
    args = [q0, q1, k, k, v, v]
    if sink_lanes is not None:
        in_specs = [_const_spec((2, 128))] + in_specs
        args = [sink_lanes] + args
    return pl.pallas_call(
        functools.partial(_attn_body, patterns=patterns, has_sink=sink_lanes is not None, sb=sb),
        grid=(L // sb,), in_specs=in_specs, out_specs=pl.BlockSpec((sb, Q_WIDTH), cur),
        out_shape=jax.ShapeDtypeStruct((L, Q_WIDTH), BF16),
        scratch_shapes=[pltpu.VMEM((2 * sb, 128), F32), pltpu.VMEM((2 * sb, 128), F32),
                        pltpu.VMEM((2, sb, 128), F32), pltpu.VMEM((2, sb, 128), F32), pltpu.VMEM((2, sb, 128), F32)],
        compiler_params=_params(("parallel",)),
        name=name,
    )(*args)


def _sample_attn_body(*refs, nb, has_sink, n_alias, w):
    if has_sink:
        sink_ref, refs = refs[0], refs[1:]
    q0_ref, q1_ref, kn_ref, vn_ref, kc_ref, vc_ref, mc_ref, mn_ref = refs[:8]
    o_ref, ko_ref, vo_ref = refs[8 + n_alias:]
    g_lo = lax.broadcasted_iota(jnp.int32, (TILE, 128), 1) < HEAD_DIM
    lane = lax.broadcasted_iota(jnp.int32, (128, 128), 1)
    is_new = lane >= 128 - (TILE - TOK0)
    mult_c = mc_ref[...]
    mult_n = mn_ref[...]
    pad = jnp.zeros((128 - TILE, 128), F32)
    for b in range(nb):
        parts = []
        for q_ref in (q0_ref, q1_ref):
            qr = q_ref[TILE * b:TILE * (b + 1), :]
            parts += [jnp.where(g_lo, qr, 0.0), jnp.where(g_lo, 0.0, qr)]
        qm = jnp.concatenate(parts, axis=0).astype(BF16)
        kct = kc_ref[b]
        vct = vc_ref[b]
        knp = jnp.concatenate([kn_ref[TILE * b:TILE * (b + 1), :], pad], axis=0)
        vnp = jnp.concatenate([vn_ref[TILE * b:TILE * (b + 1), :], pad], axis=0)
        knt = knp.T
        vnt = vnp.T
        sc = jnp.where(mult_c > 0, jnp.dot(qm, kct.astype(BF16), preferred_element_type=F32), -jnp.inf)
        sn = jnp.where(mult_n > 0, jnp.dot(qm, knt.astype(BF16), preferred_element_type=F32), -jnp.inf)
        m = jnp.maximum(jnp.max(sc, axis=-1, keepdims=True), jnp.max(sn, axis=-1, keepdims=True))
        if has_sink:
            sk = sink_ref[:, :1]
            m = jnp.maximum(m, sk)
        ec = mult_c * jnp.exp(sc - m)
        en = mult_n * jnp.exp(sn - m)
        den = jnp.sum(ec, axis=-1, keepdims=True) + jnp.sum(en, axis=-1, keepdims=True)
        if has_sink:
            den = den + jnp.exp(sk - m)
        o = (_dot_nt(ec.astype(BF16), vct.astype(BF16))
             + jnp.dot(en.astype(BF16), vnp.astype(BF16), preferred_element_type=F32)) / den
        o_ref[TILE * b:TILE * (b + 1), :] = jnp.concatenate(
            [jnp.where(g_lo, o[0:TILE], o[TILE:2 * TILE]),
             jnp.where(g_lo, o[2 * TILE:3 * TILE], o[3 * TILE:4 * TILE])], axis=1).astype(o_ref.dtype)
        for src, new_t, dst in ((kct, knt, ko_ref), (vct, vnt, vo_ref)):
            shifted = pltpu.roll(src, w - (TILE - TOK0), 1)
            new_cols = pltpu.roll(new_t, 128 - TILE, 1)
            if w > 128:
                dst[b, :, 0:w - 128] = shifted[:, 0:w - 128]
            dst[b, :, w - 128:w] = jnp.where(is_new, new_cols, shifted[:, w - 128:w])


def _sample_attn(q0, q1, kn, vn, cache_k, cache_v, layer, mult_c, mult_n, sinks_rows, nb, prev_out):
    rows = q0.shape[0]
    nbatch = rows // TILE
    depth, _, _, w = cache_k.shape
    row_spec = lambda width: pl.BlockSpec((TILE * nb, width), lambda i: (i, 0))
    cache_spec = pl.BlockSpec((None, nb, KV_WIDTH, w), lambda i: (layer, i, 0, 0))
    in_specs = [row_spec(128), row_spec(128), row_spec(KV_WIDTH), row_spec(KV_WIDTH), cache_spec, cache_spec,
                _const_spec((4 * TILE, w)), _const_spec((4 * TILE, 128))]
    args = [q0, q1, kn, vn, cache_k, cache_v, mult_c, mult_n]
    if sinks_rows is not None:
        in_specs = [_const_spec((4 * TILE, 128))] + in_specs
        args = [sinks_rows] + args
    aliases = {}
    if prev_out is not None:
        aliases = {len(args): 1, len(args) + 1: 2}
        in_specs = in_specs + [pl.BlockSpec(memory_space=pl.ANY)] * 2
        args = args + list(prev_out)
    cache_shape = jax.ShapeDtypeStruct(cache_k.shape, F32)
    return pl.pallas_call(
        functools.partial(_sample_attn_body, nb=nb, has_sink=sinks_rows is not None,
                          n_alias=0 if prev_out is None else 2, w=w),
        grid=(nbatch // nb,), in_specs=in_specs, out_specs=[row_spec(Q_WIDTH), cache_spec, cache_spec],
        out_shape=[jax.ShapeDtypeStruct((rows, Q_WIDTH), BF16), cache_shape, cache_shape],
        input_output_aliases=aliases,
        compiler_params=_params(("parallel",)),
        name=f"sample_attn_w{w}",
    )(*args)


def _sample_mult_tables():
    t = np.arange(TILE) - TOK0
    tq = np.maximum(t, 0)[:, None]
    j = np.arange(A_WINDOW)[None, :]
    da = A_WINDOW + tq - j
    ma_c = ((da >= 0) & (da < A_WINDOW)).astype(np.float32)
    tn = (np.arange(128) - TOK0)[None, :]
    dn = tq - tn
    new_ok = (tn >= 0) & (tn < TILE - TOK0) & (dn >= 0)
    ma_n = (new_ok & (dn < A_WINDOW)).astype(np.float32)

    def mult(d):
        out = np.zeros(d.shape, np.float32)
        for w, dil in C_PATTERNS:
            out += ((d >= 0) & (d <= w) & (d % dil == 0)).astype(np.float32)
        return out

    jc = np.arange(C_SPAN)[None, :]
    mc_c = mult(C_SPAN + tq - jc)
    mc_n = np.where(new_ok, mult(dn), 0.0).astype(np.float32)
    tile4 = lambda a: jnp.asarray(np.tile(a, (4, 1)))
    return tile4(ma_c), tile4(ma_n), tile4(mc_c), tile4(mc_n)


def _ssd_intra(act, dt_raw, dtb, alog, tri_sel, tri_mask, expand, row_ok):
    xs = act[:, :B_INNER]
    bb = act[:, B_INNER:B_INNER + 2 * B_STATE].astype(BF16)
    cb_ = act[:, B_INNER + 2 * B_STATE:].astype(BF16)
    dtv = _softplus(dt_raw + dtb)
    if row_ok is not None:
        dtv = jnp.where(row_ok, dtv, 0.0)
    a = -jnp.exp(alog) * dtv
    acum = _sel_dot_rhs(tri_sel, a)
    acum_t = acum.T
    dt_e = _sel_dot_lhs(dtv, expand)
    xdt = xs * dt_e
    lane_lo = lax.broadcasted_iota(jnp.int32, (BLK, 128), 1) < HEAD_DIM
    ys = []
    for g in range(2):
        cbm = _dot_nt(cb_[:, 128 * g:128 * (g + 1)], bb[:, 128 * g:128 * (g + 1)])
        for pair in range(2):
            xp = xdt[:, 128 * (2 * g + pair):128 * (2 * g + pair + 1)]
            acc = None
            for j in range(2):
                h = 4 * g + 2 * pair + j
                seg = acum[:, h:h + 1] - acum_t[h:h + 1, :]
                lm = jnp.exp(jnp.where(tri_mask, seg, -jnp.inf))
                mh = (cbm * lm).astype(BF16)
                xh = jnp.where(lane_lo if j == 0 else ~lane_lo, xp, 0.0).astype(BF16)
                t = jnp.dot(mh, xh, preferred_element_type=F32)
                acc = t if acc is None else acc + t
            ys.append(acc)
    ydiag = jnp.concatenate(ys, axis=1)
    return xs, bb, cb_, xdt, acum, acum_t, ydiag


def _ssd_finish(y, z, norm_w):
    y = y * _silu(z)
    ms = jnp.mean(y * y, axis=-1, keepdims=True)
    return (y * lax.rsqrt(ms + NORM_EPS) * norm_w).astype(BF16)


def _conv_act(xp_ref, cw_ref, cb_ref):
    t = BLK
    out = cb_ref[...] + xp_ref[5:5 + t, :] * cw_ref[0:1, :]
    out = out + xp_ref[6:6 + t, :] * cw_ref[1:2, :]
    out = out + xp_ref[7:7 + t, :] * cw_ref[2:3, :]
    out = out + xp_ref[8:8 + t, :] * cw_ref[3:4, :]
    return _silu(out)


def _ssd_prompt_body(xbc_ref, z_ref, dt_ref, cw_ref, cb_ref, dtb_ref, alog_ref, dskip_ref, nw_ref,
                     expand_ref, expand_t_ref, mo_ref, hs_ref, xp_ref):
    c = pl.program_id(0)
    t = BLK

    @pl.when(c == 0)
    def _():
        xp_ref[0:TILE, :] = jnp.zeros((TILE, CONV_DIM), F32)
        hs_ref[...] = jnp.zeros_like(hs_ref)

    xp_ref[TILE:, :] = xbc_ref[...]
    act = _conv_act(xp_ref, cw_ref, cb_ref)
    xp_ref[0:TILE, :] = xp_ref[t:t + TILE, :]

    row = lax.broadcasted_iota(jnp.int32, (t, t), 0)
    col = lax.broadcasted_iota(jnp.int32, (t, t), 1)
    tri_mask = col <= row
    tri_sel = tri_mask.astype(BF16)
    expand = expand_ref[...]
    xs, bb, cb_, xdt, acum, acum_t, ydiag = _ssd_intra(
        act, dt_ref[...], dtb_ref[...], alog_ref[...], tri_sel, tri_mask, expand, None)

    hst = hs_ref[...]
    hb = hst.astype(BF16)
    yoff = jnp.concatenate([_dot_nt(cb_[:, 128 * g:128 * (g + 1)], hb[256 * g:256 * (g + 1), :]) for g in range(2)],
                           axis=1)
    e_all = _sel_dot_lhs(jnp.exp(acum), expand)
    y = ydiag + yoff * e_all + xs * dskip_ref[...]
    mo_ref[...] = _ssd_finish(y, z_ref[...], nw_ref[...])

    dec_e = _sel_dot_lhs(jnp.exp(acum[t - 1:t, :] - acum), expand)
    xw = xdt * dec_e
    dcol = _sel_dot_rhs(expand_t_ref[...], jnp.exp(acum_t))
    cd = dcol[:, t - 1:t]
    for g in range(2):
        xw_t = xw[:, 256 * g:256 * (g + 1)].T.astype(BF16)
        dh = jnp.dot(xw_t, bb[:, 128 * g:128 * (g + 1)], preferred_element_type=F32)
        hs_ref[256 * g:256 * (g + 1), :] = hst[256 * g:256 * (g + 1), :] * cd[256 * g:256 * (g + 1), :] + dh


def _ssd_prompt(xbc, z, dt, lw):
    L = xbc.shape[0]
    row_spec = lambda w: pl.BlockSpec((BLK, w), lambda c: (c, 0))
    return pl.pallas_call(
        _ssd_prompt_body,
        grid=(L // BLK,),
        in_specs=[row_spec(CONV_DIM), row_spec(B_INNER), row_spec(128),
                  _const_spec((CONV_K, CONV_DIM)), _const_spec((1, CONV_DIM)), _const_spec((1, 128)),
                  _const_spec((1, 128)), _const_spec((1, B_INNER)), _const_spec((1, B_INNER)),
                  _const_spec((128, B_INNER)), _const_spec((B_INNER, 128))],
        out_specs=[row_spec(B_INNER), pl.BlockSpec((B_INNER, B_STATE), lambda c: (0, 0))],
        out_shape=[jax.ShapeDtypeStruct((L, B_INNER), BF16), jax.ShapeDtypeStruct((B_INNER, B_STATE), F32)],
        scratch_shapes=[pltpu.VMEM((BLK + TILE, CONV_DIM), F32)],
        compiler_params=_params(("arbitrary",)),
        name="ssd_prompt",
    )(xbc, z, dt, lw['conv_w'], lw['conv_b'], lw['dt_bias'], lw['a_log'], lw['d_skip'], lw['ssm_norm'],
      lw['expand'], lw['expand_t'])


def _ssd_sample_body(*refs):
    (xbc_ref, pre_ref, z_ref, dt_ref, h0_ref, cw_ref, cb_ref, dtb_ref, alog_ref, dskip_ref,
     nw_ref, expand_ref, expand_t_ref) = refs[:13]
    mo_ref, hout_ref, xp_ref = refs[-3:]
    t = BLK
    nbt = t // TILE
    rmod = lax.broadcasted_iota(jnp.int32, (t, 1), 0) & (TILE - 1)
    row_ok = rmod >= TOK0
    xp_ref[0:TILE, :] = jnp.zeros((TILE, CONV_DIM), F32)
    xp_ref[TILE:, :] = jnp.where(row_ok, xbc_ref[...], pre_ref[...])
    act = _conv_act(xp_ref, cw_ref, cb_ref)

    row = lax.broadcasted_iota(jnp.int32, (t, t), 0)
    col = lax.broadcasted_iota(jnp.int32, (t, t), 1)
    same = (row // TILE) == (col // TILE)
    tri_mask = (col <= row) & same
    tri_sel = tri_mask.astype(BF16)
    last_sel = (col == (row // TILE) * TILE + (TILE - 1)).astype(BF16)
    expand = expand_ref[...]
    xs, bb, cb_, xdt, acum, acum_t, ydiag = _ssd_intra(
        act, dt_ref[...], dtb_ref[...], alog_ref[...], tri_sel, tri_mask, expand, row_ok)

    yoffs = []
    for b in range(nbt):
        hb = h0_ref[b].astype(BF16)
        yoffs.append(jnp.concatenate(
            [_dot_nt(cb_[TILE * b:TILE * (b + 1), 128 * g:128 * (g + 1)], hb[256 * g:256 * (g + 1), :])
             for g in range(2)], axis=1))
    yoff = jnp.concatenate(yoffs, axis=0)
    e_all = _sel_dot_lhs(jnp.exp(acum), expand)
    y = ydiag + yoff * e_all + xs * dskip_ref[...]
    mo_ref[...] = _ssd_finish(y, z_ref[...], nw_ref[...])

    last = _sel_dot_rhs(last_sel, acum)
    dec_e = _sel_dot_lhs(jnp.exp(last - acum), expand)
    xw = xdt * dec_e
    dcol = _sel_dot_rhs(expand_t_ref[...], jnp.exp(acum_t))
    lane = lax.broadcasted_iota(jnp.int32, (256, t), 1)
    xw_ts = [xw[:, 256 * g:256 * (g + 1)].T for g in range(2)]
    for b in range(nbt):
        in_b = (lane // TILE) == b
        cd = dcol[:, TILE * b + TILE - 1:TILE * b + TILE]
        h0 = h0_ref[b]
        for g in range(2):
            lhs = jnp.where(in_b, xw_ts[g], 0.0).astype(BF16)
            dh = jnp.dot(lhs, bb[:, 128 * g:128 * (g + 1)], preferred_element_type=F32)
            hout_ref[b, 256 * g:256 * (g + 1), :] = (
                h0[256 * g:256 * (g + 1), :] * cd[256 * g:256 * (g + 1), :] + dh)


def _ssd_sample(xbc, prefix, z, dt, state, layer, lw, prev_out):
    rows = xbc.shape[0]
    nbt = BLK // TILE
    row_spec = lambda w: pl.BlockSpec((BLK, w), lambda i: (i, 0))
    st_spec = pl.BlockSpec((None, nbt, B_INNER, B_STATE), lambda i: (layer, i, 0, 0))
    in_specs = [row_spec(CONV_DIM), row_spec(CONV_DIM), row_spec(B_INNER), row_spec(128), st_spec,
                _const_spec((CONV_K, CONV_DIM)), _const_spec((1, CONV_DIM)), _const_spec((1, 128)),
                _const_spec((1, 128)), _const_spec((1, B_INNER)), _const_spec((1, B_INNER)),
                _const_spec((128, B_INNER)), _const_spec((B_INNER, 128))]
    args = [xbc, prefix, z, dt, state, lw['conv_w'], lw['conv_b'], lw['dt_bias'], lw['a_log'], lw['d_skip'],
            lw['ssm_norm'], lw['expand'], lw['expand_t']]
    aliases = {}
    if prev_out is not None:
        aliases = {len(args): 1}
        in_specs.append(pl.BlockSpec(memory_space=pl.ANY))
        args.append(prev_out)
    return pl.pallas_call(
        _ssd_sample_body,
        grid=(rows // BLK,),
        in_specs=in_specs,
        out_specs=[row_spec(B_INNER), st_spec],
        out_shape=[jax.ShapeDtypeStruct((rows, B_INNER), BF16), jax.ShapeDtypeStruct(state.shape, F32)],
        scratch_shapes=[pltpu.VMEM((BLK + TILE, CONV_DIM), F32)],
        input_output_aliases=aliases,
        compiler_params=_params(("parallel",)),
        name="ssd_sample",
    )(*args)


def _out_ffn_body(h_ref, a_ref, m_ref, c_ref, wo_ref, g2_ref, wg_ref, wu_ref, wd_ref, out_ref):
    d = functools.partial(jnp.dot, preferred_element_type=F32)
    h1 = (h_ref[...] + d(a_ref[...], wo_ref[0:256, :]) + d(m_ref[...], wo_ref[256:768, :])
          + d(c_ref[...], wo_ref[768:1024, :]))
    ms = jnp.mean(h1 * h1, axis=-1, keepdims=True)
    u = (h1 * lax.rsqrt(ms + NORM_EPS) * g2_ref[...]).astype(BF16)
    out_ref[...] = h1
    for c in range(D_FF // FF_CHUNK):
        sl = slice(FF_CHUNK * c, FF_CHUNK * (c + 1))
        act = (_silu(d(u, wg_ref[:, sl])) * d(u, wu_ref[:, sl])).astype(BF16)
        out_ref[...] += d(act, wd_ref[sl, :])


def _out_ffn(h, a_o, m_o, c_o, lw, tm):
    rows = h.shape[0]
    row_spec = lambda w: pl.BlockSpec((tm, w), lambda i: (i, 0))
    in_specs = [row_spec(D_MODEL), row_spec(Q_WIDTH), row_spec(B_INNER), row_spec(Q_WIDTH),
                _const_spec((D_MODEL, D_MODEL)), _const_spec((1, D_MODEL)), _const_spec((D_MODEL, D_FF)),
                _const_spec((D_MODEL, D_FF)), _const_spec((D_FF, D_MODEL))]
    return pl.pallas_call(
        _out_ffn_body,
        grid=(rows // tm,), in_specs=in_specs, out_specs=row_spec(D_MODEL),
        out_shape=jax.ShapeDtypeStruct((rows, D_MODEL), F32),
        compiler_params=_params(("parallel",)),
        name="out_ffn",
    )(h, a_o, m_o, c_o, lw['w_out'], lw['norm2'], lw['w_gate'], lw['w_up'], lw['w_down'])


def _rope_table(pos):
    half = ROT_DIM // 2
    inv = ROPE_THETA ** (-(jnp.arange(half, dtype=F32) * 2.0 / ROT_DIM))
    ang = pos.astype(F32)[:, None] * inv[None, :]
    cos, sin = jnp.cos(ang), jnp.sin(ang)
    n = pos.shape[0]
    c64 = jnp.concatenate([cos, cos, jnp.ones((n, HEAD_DIM - ROT_DIM), F32)], axis=1)
    s64 = jnp.concatenate([-sin, sin, jnp.zeros((n, HEAD_DIM - ROT_DIM), F32)], axis=1)
    return jnp.concatenate([c64, c64, s64, s64], axis=1)


def _layer_weights(l, norm1, w_in, a_qn, a_kn, a_sinks, c_qn, c_kn, conv_w, conv_b, dt_bias, a_log, d_skip,
                   ssm_norm, w_out, norm2, w_gate, w_up, w_down):
    w = w_in[l]
    cols = np.arange(N_IN)
    cols[_OFF['aq']:_OFF['aq'] + 256] = _OFF['aq'] + _HEAD_PERM
    cols[_OFF['cq']:_OFF['cq'] + 256] = _OFF['cq'] + _HEAD_PERM
    w = jnp.pad(w[:, cols], ((0, 0), (0, N_IN_PAD - N_IN))).astype(BF16)
    rows = np.arange(D_MODEL)
    rows[0:256] = _HEAD_PERM
    rows[768:1024] = 768 + _HEAD_PERM
    pad8 = lambda v: jnp.pad(v.astype(F32), (0, 128 - B_HEADS))[None, :]
    head_of = np.arange(B_INNER) // 64
    expand = (np.arange(128)[:, None] == head_of[None, :]).astype(np.float32)
    blk = (np.arange(256)[:, None] // 64 == np.arange(256)[None, :] // 64).astype(np.float32) / 64.0
    return dict(
        norm1=norm1[l][None, :], w_in=w,
        a_qn=jnp.tile(a_qn[l], 4)[None, :], a_kn=jnp.tile(a_kn[l], 2)[None, :],
        c_qn=jnp.tile(c_qn[l], 4)[None, :], c_kn=jnp.tile(c_kn[l], 2)[None, :],
        bd=jnp.asarray(blk, BF16),
        sink_lanes=jnp.repeat(a_sinks[l].astype(F32)[jnp.asarray([0, 2, 1, 3])], HEAD_DIM).reshape(2, 128),
        sink_rows=jnp.broadcast_to(
            jnp.repeat(a_sinks[l].astype(F32)[jnp.asarray([0, 2, 1, 3])], TILE)[:, None], (4 * TILE, 128)),
        conv_w=conv_w[l], conv_b=conv_b[l][None, :], dt_bias=pad8(dt_bias[l]), a_log=pad8(a_log[l]),
        d_skip=jnp.repeat(d_skip[l].astype(F32), 64)[None, :], ssm_norm=ssm_norm[l][None, :],
        expand=jnp.asarray(expand, BF16), expand_t=jnp.asarray(expand.T, BF16),
        w_out=w_out[l][rows, :].astype(BF16), norm2=norm2[l][None, :],
        w_gate=w_gate[l].astype(BF16), w_up=w_up[l].astype(BF16), w_down=w_down[l].astype(BF16))


def kernel(x_prompt, x_sample, cache_a_k, cache_a_v, cache_c_k, cache_c_v, state_ssm, state_conv, norm1, w_in,
           a_qn, a_kn, a_sinks, c_qn, c_kn, conv_w, conv_b, dt_bias, a_log, d_skip, ssm_norm, w_out, norm2,
           w_gate, w_up, w_down):
    depth = w_in.shape[0]
    batch, seq, _ = x_prompt.shape
    nbatch, dec_seq, _ = x_sample.shape
    assert batch == 1 and dec_seq == TILE - TOK0 and seq % (16 * BLK) == 0 and nbatch % (BLK // TILE) == 0
    past_len = PAST_LEN
    a_buf, c_buf = cache_a_k.shape[2], cache_c_k.shape[2]
    assert a_buf == A_WINDOW and c_buf == C_SPAN

    hp = x_prompt.reshape(seq, D_MODEL)
    hs = jnp.pad(x_sample, ((0, 0), (TOK0, 0), (0, 0))).reshape(nbatch * TILE, D_MODEL)
    tab_p = _rope_table(jnp.arange(seq))
    pos_s = past_len + jnp.maximum(jnp.arange(TILE) - TOK0, 0)
    tab_s = jnp.tile(_rope_table(pos_s), (nbatch, 1))
    ma_c, ma_n, mc_c, mc_n = _sample_mult_tables()

    to_fm = lambda c: jnp.transpose(c, (0, 1, 3, 4, 2)).reshape(depth, nbatch, KV_WIDTH, c.shape[2])
    from_fm = lambda c: jnp.transpose(c.reshape(depth, nbatch, 2, HEAD_DIM, c.shape[3]), (0, 1, 4, 2, 3))
    ca_k, ca_v, cc_k, cc_v = to_fm(cache_a_k), to_fm(cache_a_v), to_fm(cache_c_k), to_fm(cache_c_v)
    st = state_ssm.reshape(depth, nbatch, B_INNER, B_STATE)

    p_out = [[] for _ in range(6)]
    s_conv = []
    new_a = new_c = new_st = None
    tm_p = 512
    tm_s = min(512, nbatch * TILE)
    for l in range(depth):
        lw = _layer_weights(l, norm1, w_in, a_qn, a_kn, a_sinks, c_qn, c_kn, conv_w, conv_b, dt_bias, a_log,
                            d_skip, ssm_norm, w_out, norm2, w_gate, w_up, w_down)
        qa0, qa1, ka, va, qc0, qc1, kc, vc, z, xbc, dt = _in_proj(hp, lw, tab_p, tm_p)
        a_o = _prompt_attn(qa0, qa1, ka, va, A_PATTERNS, sink_lanes=lw['sink_lanes'], name="attn_a")
        c_o = _prompt_attn(qc0, qc1, kc, vc, C_BANDS, name="attn_c")
        m_o, h_fin = _ssd_prompt(xbc, z, dt, lw)
        hp = _out_ffn(hp, a_o, m_o, c_o, lw, tm_p)
        p_out[0].append(ka[seq - a_buf:].reshape(1, a_buf, 2, HEAD_DIM))
        p_out[1].append(va[seq - a_buf:].reshape(1, a_buf, 2, HEAD_DIM))
        p_out[2].append(kc[seq - c_buf:].reshape(1, c_buf, 2, HEAD_DIM))
        p_out[3].append(vc[seq - c_buf:].reshape(1, c_buf, 2, HEAD_DIM))
        p_out[4].append(h_fin.reshape(1, B_HEADS, 64, B_STATE))
        p_out[5].append(xbc[seq - (CONV_K - 1):].reshape(1, CONV_K - 1, CONV_DIM))

        qa0, qa1, ka, va, qc0, qc1, kc, vc, z, xbc, dt = _in_proj(hs, lw, tab_s, tm_s)
        a_o, *new_a = _sample_attn(qa0, qa1, ka, va, ca_k, ca_v, l, ma_c, ma_n, lw['sink_rows'], 8, new_a)
        c_o, *new_c = _sample_attn(qc0, qc1, kc, vc, cc_k, cc_v, l, mc_c, mc_n, None, 4, new_c)
        prefix = jnp.pad(state_conv[l], ((0, 0), (1, TILE - CONV_K), (0, 0))).reshape(nbatch * TILE, CONV_DIM)
        m_o, new_st = _ssd_sample(xbc, prefix, z, dt, st, l, lw, new_st)
        hs = _out_ffn(hs, a_o, m_o, c_o, lw, tm_s)
        s_conv.append(xbc.reshape(nbatch, TILE, CONV_DIM)[:, TILE - (CONV_K - 1):])

    outs_p = [jnp.stack(t, axis=0) for t in p_out]
    outs_s = [from_fm(new_a[0]), from_fm(new_a[1]), from_fm(new_c[0]), from_fm(new_c[1]),
              new_st.reshape(depth, nbatch, B_HEADS, 64, B_STATE), jnp.stack(s_conv, axis=0)]
    y_p = hp.reshape(1, seq, D_MODEL)
    y_s = hs.reshape(nbatch, TILE, D_MODEL)[:, TOK0:]
    return (y_p, y_s, *outs_p, *outs_s)
```

```python
import functools
import math

import numpy as np
import jax
import jax.numpy as jnp
from jax import lax
from jax.experimental import pallas as pl
from jax.experimental.pallas import tpu as pltpu

F32 = jnp.float32
BF16 = jnp.bfloat16

D_MODEL = 1024
HEAD_DIM = 64
ROT_DIM = 16
ROPE_THETA = 500000.0
NORM_EPS = 1e-6
Q_WIDTH = 256
KV_WIDTH = 128
PAST_LEN = 16384
A_WINDOW = 128
C_PATTERNS = ((128, 1), (512, 4), (2048, 16))
C_SPAN = 2048
B_HEADS = 8
B_INNER = 512
B_STATE = 128
CONV_K = 4
CONV_DIM = 1024
D_FF = 2816
N_IN = 2568
N_IN_PAD = 2688
BLK = 128
ATTN_SUPER = 2048
A_PATTERNS = ((1, A_WINDOW - 1),)
C_BANDS = tuple((d, w // d) for w, d in C_PATTERNS)
TOK0 = 4
TILE = 8
FF_CHUNK = 256
VMEM_LIMIT = 56 * 1024 * 1024

_OFF = dict(aq=0, ak=256, av=384, cq=512, ck=768, cv=896, z=1024, xbc=1536, dt=2560, end=N_IN_PAD)
_HEAD_PERM = np.concatenate([np.arange(0, 64), np.arange(128, 192), np.arange(64, 128), np.arange(192, 256)])


def _const_spec(shape):
    nd = len(shape)
    return pl.BlockSpec(shape, lambda *_: (0,) * nd, pipeline_mode=pl.Buffered(1))


def _params(sem):
    return pltpu.CompilerParams(dimension_semantics=sem, vmem_limit_bytes=VMEM_LIMIT)


def _split3(v):
    hi = v.astype(BF16)
    r1 = v - hi.astype(F32)
    mid = r1.astype(BF16)
    lo = (r1 - mid.astype(F32)).astype(BF16)
    return hi, mid, lo


def _sel_dot_rhs(sel, v):
    hi, mid, lo = _split3(v)
    d = functools.partial(jnp.dot, preferred_element_type=F32)
    return d(sel, hi) + d(sel, mid) + d(sel, lo)


def _sel_dot_lhs(v, sel):
    hi, mid, lo = _split3(v)
    d = functools.partial(jnp.dot, preferred_element_type=F32)
    return d(hi, sel) + d(mid, sel) + d(lo, sel)


def _dot_nt(a, b):
    return lax.dot_general(a, b, (((1,), (1,)), ((), ())), preferred_element_type=F32)


def _silu(x):
    return x * jax.nn.sigmoid(x)


def _softplus(x):
    return jnp.maximum(x, 0.0) + jnp.log(1.0 + jnp.exp(-jnp.abs(x)))


def _head_norm(x, gain, bd):
    x2 = x * x
    hi = x2.astype(BF16)
    lo = (x2 - hi.astype(F32)).astype(BF16)
    ms = jnp.dot(hi, bd, preferred_element_type=F32) + jnp.dot(lo, bd, preferred_element_type=F32)
    return x * lax.rsqrt(ms + NORM_EPS) * gain


def _rope(x, cos_t, sin_t):
    w = x.shape[1]
    lane = lax.broadcasted_iota(jnp.int32, x.shape, 1) & (HEAD_DIM - 1)
    partner = jnp.where(lane < ROT_DIM // 2, pltpu.roll(x, w - ROT_DIM // 2, 1), pltpu.roll(x, ROT_DIM // 2, 1))
    return x * cos_t + partner * sin_t


def _in_proj_body(h_ref, g1_ref, w_ref, tab_ref, gqa_ref, gka_ref, gqc_ref, gkc_ref, bd_ref,
                  qa0_ref, qa1_ref, ka_ref, va_ref, qc0_ref, qc1_ref, kc_ref, vc_ref, z_ref, xbc_ref, dt_ref):
    x = h_ref[...]
    ms = jnp.mean(x * x, axis=-1, keepdims=True)
    u = (x * lax.rsqrt(ms + NORM_EPS) * g1_ref[...]).astype(BF16)
    tab = tab_ref[...]
    c1, s1 = tab[:, :128], tab[:, 128:]
    c2 = jnp.concatenate([c1, c1], axis=1)
    s2 = jnp.concatenate([s1, s1], axis=1)
    bd2 = bd_ref[...]
    bd1 = bd2[:128, :128]

    def proj(name, nxt):
        return jnp.dot(u, w_ref[:, _OFF[name]:_OFF[nxt]], preferred_element_type=F32)

    qa = _rope(_head_norm(proj('aq', 'ak'), gqa_ref[...], bd2), c2, s2) * 0.125
    qa0_ref[...] = qa[:, :128]
    qa1_ref[...] = qa[:, 128:]
    ka_ref[...] = _rope(_head_norm(proj('ak', 'av'), gka_ref[...], bd1), c1, s1)
    va_ref[...] = proj('av', 'cq')
    qc = _rope(_head_norm(proj('cq', 'ck'), gqc_ref[...], bd2), c2, s2) * 0.125
    qc0_ref[...] = qc[:, :128]
    qc1_ref[...] = qc[:, 128:]
    kc_ref[...] = _rope(_head_norm(proj('ck', 'cv'), gkc_ref[...], bd1), c1, s1)
    vc_ref[...] = proj('cv', 'z')
    z_ref[...] = proj('z', 'xbc')
    xbc_ref[...] = proj('xbc', 'dt')
    dt_ref[...] = proj('dt', 'end')


def _in_proj(h, lw, tab, tm):
    rows = h.shape[0]
    grid = (rows // tm,)
    row_spec = lambda w: pl.BlockSpec((tm, w), lambda i: (i, 0))
    widths = (128, 128, KV_WIDTH, KV_WIDTH, 128, 128, KV_WIDTH, KV_WIDTH, B_INNER, CONV_DIM, 128)
    dtypes = (F32,) * len(widths)
    return pl.pallas_call(
        _in_proj_body,
        grid=grid,
        in_specs=[row_spec(D_MODEL), _const_spec((1, D_MODEL)), _const_spec((D_MODEL, N_IN_PAD)),
                  row_spec(256), _const_spec((1, 256)), _const_spec((1, 128)), _const_spec((1, 256)),
                  _const_spec((1, 128)), _const_spec((256, 256))],
        out_specs=[row_spec(w) for w in widths],
        out_shape=[jax.ShapeDtypeStruct((rows, w), dt) for w, dt in zip(widths, dtypes)],
        compiler_params=_params(("parallel",)),
        name="in_proj",
    )(h, lw['norm1'], lw['w_in'], tab, lw['a_qn'], lw['a_kn'], lw['c_qn'], lw['c_kn'], lw['bd'])


def _attn_body(*refs, patterns, has_sink, sb):
    if has_sink:
        sink_ref, refs = refs[0], refs[1:]
    q0_ref, q1_ref, kp_ref, kc_ref, vp_ref, vc_ref, o_ref, kk, vv, acc_s, m_s, l_s = refs
    q_refs = (q0_ref, q1_ref)
    j = pl.program_id(0)
    kk[0:sb, :] = kp_ref[...]
    kk[sb:2 * sb, :] = kc_ref[...]
    vv[0:sb, :] = vp_ref[...]
    vv[sb:2 * sb, :] = vc_ref[...]
    row4 = lax.broadcasted_iota(jnp.int32, (4 * BLK, BLK), 0) & (BLK - 1)
    col4 = lax.broadcasted_iota(jnp.int32, (4 * BLK, BLK), 1)
    upper4 = col4 > row4
    g_lo = lax.broadcasted_iota(jnp.int32, (BLK, 128), 1) < HEAD_DIM
    ones_cols = jnp.ones((2 * BLK, 128), BF16)
    nblk = sb // BLK

    diag_here = [md == BLK for _, md in patterns]
    boosts = [[] for _ in patterns]
    for p, (d, md) in enumerate(patterns):
        for q in range(p + 1, len(patterns)):
            dq, mdq = patterns[q]
            if diag_here[p] and (md * d) % dq == 0 and (md * d) // dq < min(BLK, mdq + 1):
                boosts[q].append((md * d) // dq)
                diag_here[p] = False

    for pi, (d, max_dist) in enumerate(patterns):
        nsub = nblk // d
        has_diag = diag_here[pi]
        fdist = jnp.where(upper4, row4 + BLK - col4, row4 - col4)
        mult = jnp.ones((4 * BLK, BLK), F32)
        for f in boosts[pi]:
            mult = mult + (fdist == f).astype(F32)
        w_up = jnp.where(upper4, mult, 0.0)
        w_lo = jnp.where(upper4, 0.0, mult)

        def ld(ref, s0, d=d):
            if d == 1:
                return ref[pl.ds(s0, BLK), :]
            return ref[pl.ds(s0, BLK, stride=d), :]

        def block(t, carry, d=d, nsub=nsub, has_diag=has_diag, first=(pi == 0), ld=ld, w_up=w_up, w_lo=w_lo):
            r_ = t // nsub
            n = t - r_ * nsub
            start = r_ + BLK * d * n
            prev_ok = jnp.logical_or(j > 0, n > 0)
            neg = jnp.where(prev_ok, 0.0, -jnp.inf)
            qb = [ld(q_refs[rr], start) for rr in range(2)]
            kprev, kcur = ld(kk, sb + start - BLK * d), ld(kk, sb + start)
            vprev, vcur = ld(vv, sb + start - BLK * d), ld(vv, sb + start)
            qm = jnp.concatenate([jnp.where(g_lo, qb[0], 0.0), jnp.where(g_lo, 0.0, qb[0]),
                                  jnp.where(g_lo, qb[1], 0.0), jnp.where(g_lo, 0.0, qb[1])], axis=0).astype(BF16)
            kcat = jnp.concatenate([kprev, kcur], axis=0).astype(BF16)
            vcat = jnp.concatenate([jnp.concatenate([vprev, vcur], axis=0).astype(BF16), ones_cols], axis=1)
            s2 = _dot_nt(qm, kcat)
            sp, sc = s2[:, :BLK] + neg, s2[:, BLK:]
            s = jnp.where(upper4, sp, sc)
            mb = jnp.max(s, axis=-1, keepdims=True)
            if has_diag:
                sd = jnp.sum(jnp.where(col4 == row4, sp, 0.0), axis=-1, keepdims=True)
                mb = jnp.maximum(mb, sd)
            e = jnp.exp(s - mb)
            ecat = jnp.concatenate([e * w_up, e * w_lo], axis=1).astype(BF16)
            pvl = jnp.dot(ecat, vcat, preferred_element_type=F32)
            pv, lb = pvl[:, :128], pvl[:, 128:]
            if has_diag:
                ed = jnp.exp(sd - mb)
                lb = lb + ed
                pv = pv + ed * jnp.concatenate([vprev] * 4, axis=0)
            for rr in range(2):
                lo, hi = slice(2 * BLK * rr, 2 * BLK * rr + BLK), slice(2 * BLK * rr + BLK, 2 * BLK * (rr + 1))
                o_b = jnp.where(g_lo, pv[lo], pv[hi])
                m_b = jnp.where(g_lo, mb[lo], mb[hi])
                l_b = jnp.where(g_lo, lb[lo], lb[hi])
                rows = pl.ds(start, BLK) if d == 1 else pl.ds(start, BLK, stride=d)
                if first:
                    m_s[rr, rows, :] = m_b
                    l_s[rr, rows, :] = l_b
                    acc_s[rr, rows, :] = o_b
                else:
                    m_old = m_s[rr, rows, :]
                    m_new = jnp.maximum(m_old, m_b)
                    w_old = jnp.exp(m_old - m_new)
                    w_b = jnp.exp(m_b - m_new)
                    m_s[rr, rows, :] = m_new
                    l_s[rr, rows, :] = w_old * l_s[rr, rows, :] + w_b * l_b
                    acc_s[rr, rows, :] = w_old * acc_s[rr, rows, :] + w_b * o_b
            return carry

        lax.fori_loop(0, nblk, block, 0, unroll=2)

    chunk = 256
    for rr in range(2):
        for c in range(sb // chunk):
            rows = slice(chunk * c, chunk * (c + 1))
            m_f, l_f, a_f = m_s[rr, rows, :], l_s[rr, rows, :], acc_s[rr, rows, :]
            if has_sink:
                sk = sink_ref[rr:rr + 1, :]
                m2 = jnp.maximum(m_f, sk)
                w = jnp.exp(m_f - m2)
                o = a_f * w / (l_f * w + jnp.exp(sk - m2))
            else:
                o = a_f / l_f
            o_ref[rows, 128 * rr:128 * (rr + 1)] = o.astype(o_ref.dtype)


def _prompt_attn(q0, q1, k, v, patterns, sink_lanes=None, name="attn"):
    L = q0.shape[0]
    sb = ATTN_SUPER
    cur = lambda j: (j, 0)
    prev = lambda j: (jnp.maximum(j - 1, 0), 0)
    blk = lambda im: pl.BlockSpec((sb, 128), im)
    in_specs = [blk(cur), blk(cur), blk(prev), blk(cur), blk(prev), blk(cur)]
    args = [q0, q1, k, k, v, v]
    if sink_lanes is not None:
        in_specs = [_const_spec((2, 128))] + in_specs
        args = [sink_lanes] + args
    return pl.pallas_call(
        functools.partial(_attn_body, patterns=patterns, has_sink=sink_lanes is not None, sb=sb),
        grid=(L // sb,), in_specs=in_specs, out_specs=pl.BlockSpec((sb, Q_WIDTH), cur),
        out_shape=jax.ShapeDtypeStruct((L, Q_WIDTH), BF16),
        scratch_shapes=[pltpu.VMEM((2 * sb, 128), F32), pltpu.VMEM((2 * sb, 128), F32),
                        pltpu.VMEM((2, sb, 128), F32), pltpu.VMEM((2, sb, 128), F32), pltpu.VMEM((2, sb, 128), F32)],
        compiler_params=_params(("parallel",)),
        name=name,
    )(*args)


def _sample_attn_body(*refs, nb, has_sink, n_alias, w):
    if has_sink:
        sink_ref, refs = refs[0], refs[1:]
    q0_ref, q1_ref, kn_ref, vn_ref, kc_ref, vc_ref, mc_ref, mn_ref = refs[:8]
    o_ref, ko_ref, vo_ref = refs[8 + n_alias:]
    g_lo = lax.broadcasted_iota(jnp.int32, (TILE, 128), 1) < HEAD_DIM
    lane = lax.broadcasted_iota(jnp.int32, (128, 128), 1)
    is_new = lane >= 128 - (TILE - TOK0)
    mult_c = mc_ref[...]
    mult_n = mn_ref[...]
    pad = jnp.zeros((128 - TILE, 128), F32)
    for b in range(nb):
        parts = []
        for q_ref in (q0_ref, q1_ref):
            qr = q_ref[TILE * b:TILE * (b + 1), :]
            parts += [jnp.where(g_lo, qr, 0.0), jnp.where(g_lo, 0.0, qr)]
        qm = jnp.concatenate(parts, axis=0).astype(BF16)
        kct = kc_ref[b]
        vct = vc_ref[b]
        knp = jnp.concatenate([kn_ref[TILE * b:TILE * (b + 1), :], pad], axis=0)
        vnp = jnp.concatenate([vn_ref[TILE * b:TILE * (b + 1), :], pad], axis=0)
        knt = knp.T
        vnt = vnp.T
        sc = jnp.where(mult_c > 0, jnp.dot(qm, kct.astype(BF16), preferred_element_type=F32), -jnp.inf)
        sn = jnp.where(mult_n > 0, jnp.dot(qm, knt.astype(BF16), preferred_element_type=F32), -jnp.inf)
        m = jnp.maximum(jnp.max(sc, axis=-1, keepdims=True), jnp.max(sn, axis=-1, keepdims=True))
        if has_sink:
            sk = sink_ref[:, :1]
            m = jnp.maximum(m, sk)
        ec = mult_c * jnp.exp(sc - m)
        en = mult_n * jnp.exp(sn - m)
        den = jnp.sum(ec, axis=-1, keepdims=True) + jnp.sum(en, axis=-1, keepdims=True)
        if has_sink:
            den = den + jnp.exp(sk - m)
        o = (_dot_nt(ec.astype(BF16), vct.astype(BF16))
             + jnp.dot(en.astype(BF16), vnp.astype(BF16), preferred_element_type=F32)) / den
        o_ref[TILE * b:TILE * (b + 1), :] = jnp.concatenate(
            [jnp.where(g_lo, o[0:TILE], o[TILE:2 * TILE]),
             jnp.where(g_lo, o[2 * TILE:3 * TILE], o[3 * TILE:4 * TILE])], axis=1).astype(o_ref.dtype)
        for src, new_t, dst in ((kct, knt, ko_ref), (vct, vnt, vo_ref)):
            shifted = pltpu.roll(src, w - (TILE - TOK0), 1)
            new_cols = pltpu.roll(new_t, 128 - TILE, 1)
            if w > 128:
                dst[b, :, 0:w - 128] = shifted[:, 0:w - 128]
            dst[b, :, w - 128:w] = jnp.where(is_new, new_cols, shifted[:, w - 128:w])


def _sample_attn(q0, q1, kn, vn, cache_k, cache_v, layer, mult_c, mult_n, sinks_rows, nb, prev_out):
    rows = q0.shape[0]
    nbatch = rows // TILE
    depth, _, _, w = cache_k.shape
    row_spec = lambda width: pl.BlockSpec((TILE * nb, width), lambda i: (i, 0))
    cache_spec = pl.BlockSpec((None, nb, KV_WIDTH, w), lambda i: (layer, i, 0, 0))
    in_specs = [row_spec(128), row_spec(128), row_spec(KV_WIDTH), row_spec(KV_WIDTH), cache_spec, cache_spec,
                _const_spec((4 * TILE, w)), _const_spec((4 * TILE, 128))]
    args = [q0, q1, kn, vn, cache_k, cache_v, mult_c, mult_n]
    if sinks_rows is not None:
        in_specs = [_const_spec((4 * TILE, 128))] + in_specs
        args = [sinks_rows] + args
    aliases = {}
    if prev_out is not None:
        aliases = {len(args): 1, len(args) + 1: 2}
        in_specs = in_specs + [pl.BlockSpec(memory_space=pl.ANY)] * 2
        args = args + list(prev_out)
    cache_shape = jax.ShapeDtypeStruct(cache_k.shape, F32)
    return pl.pallas_call(
        functools.partial(_sample_attn_body, nb=nb, has_sink=sinks_rows is not None,
                          n_alias=0 if prev_out is None else 2, w=w),
        grid=(nbatch // nb,), in_specs=in_specs, out_specs=[row_spec(Q_WIDTH), cache_spec, cache_spec],
        out_shape=[jax.ShapeDtypeStruct((rows, Q_WIDTH), BF16), cache_shape, cache_shape],
        input_output_aliases=aliases,
        compiler_params=_params(("parallel",)),
        name=f"sample_attn_w{w}",
    )(*args)


def _sample_mult_tables():
    t = np.arange(TILE) - TOK0
    tq = np.maximum(t, 0)[:, None]
    j = np.arange(A_WINDOW)[None, :]
    da = A_WINDOW + tq - j
    ma_c = ((da >= 0) & (da < A_WINDOW)).astype(np.float32)
    tn = (np.arange(128) - TOK0)[None, :]
    dn = tq - tn
    new_ok = (tn >= 0) & (tn < TILE - TOK0) & (dn >= 0)
    ma_n = (new_ok & (dn < A_WINDOW)).astype(np.float32)

    def mult(d):
        out = np.zeros(d.shape, np.float32)
        for w, dil in C_PATTERNS:
            out += ((d >= 0) & (d <= w) & (d % dil == 0)).astype(np.float32)
        return out

    jc = np.arange(C_SPAN)[None, :]
    mc_c = mult(C_SPAN + tq - jc)
    mc_n = np.where(new_ok, mult(dn), 0.0).astype(np.float32)
    tile4 = lambda a: jnp.asarray(np.tile(a, (4, 1)))
    return tile4(ma_c), tile4(ma_n), tile4(mc_c), tile4(mc_n)


def _ssd_intra(act, dt_raw, dtb, alog, tri_sel, tri_mask, expand, row_ok):
    xs = act[:, :B_INNER]
    bb = act[:, B_INNER:B_INNER + 2 * B_STATE].astype(BF16)
    cb_ = act[:, B_INNER + 2 * B_STATE:].astype(BF16)
    dtv = _softplus(dt_raw + dtb)
    if row_ok is not None:
        dtv = jnp.where(row_ok, dtv, 0.0)
    a = -jnp.exp(alog) * dtv
    acum = _sel_dot_rhs(tri_sel, a)
    acum_t = acum.T
    dt_e = _sel_dot_lhs(dtv, expand)
    xdt = xs * dt_e
    lane_lo = lax.broadcasted_iota(jnp.int32, (BLK, 128), 1) < HEAD_DIM
    ys = []
    for g in range(2):
        cbm = _dot_nt(cb_[:, 128 * g:128 * (g + 1)], bb[:, 128 * g:128 * (g + 1)])
        for pair in range(2):
            xp = xdt[:, 128 * (2 * g + pair):128 * (2 * g + pair + 1)]
            acc = None
            for j in range(2):
                h = 4 * g + 2 * pair + j
                seg = acum[:, h:h + 1] - acum_t[h:h + 1, :]
                lm = jnp.exp(jnp.where(tri_mask, seg, -jnp.inf))
                mh = (cbm * lm).astype(BF16)
                xh = jnp.where(lane_lo if j == 0 else ~lane_lo, xp, 0.0).astype(BF16)
                t = jnp.dot(mh, xh, preferred_element_type=F32)
                acc = t if acc is None else acc + t
            ys.append(acc)
    ydiag = jnp.concatenate(ys, axis=1)
    return xs, bb, cb_, xdt, acum, acum_t, ydiag


def _ssd_finish(y, z, norm_w):
    y = y * _silu(z)
    ms = jnp.mean(y * y, axis=-1, keepdims=True)
    return (y * lax.rsqrt(ms + NORM_EPS) * norm_w).astype(BF16)


def _conv_act(xp_ref, cw_ref, cb_ref):
    t = BLK
    out = cb_ref[...] + xp_ref[5:5 + t, :] * cw_ref[0:1, :]
    out = out + xp_ref[6:6 + t, :] * cw_ref[1:2, :]
    out = out + xp_ref[7:7 + t, :] * cw_ref[2:3, :]
    out = out + xp_ref[8:8 + t, :] * cw_ref[3:4, :]
    return _silu(out)


def _ssd_prompt_body(xbc_ref, z_ref, dt_ref, cw_ref, cb_ref, dtb_ref, alog_ref, dskip_ref, nw_ref,
                     expand_ref, expand_t_ref, mo_ref, hs_ref, xp_ref):
    c = pl.program_id(0)
    t = BLK

    @pl.when(c == 0)
    def _():
        xp_ref[0:TILE, :] = jnp.zeros((TILE, CONV_DIM), F32)
        hs_ref[...] = jnp.zeros_like(hs_ref)

    xp_ref[TILE:, :] = xbc_ref[...]
    act = _conv_act(xp_ref, cw_ref, cb_ref)
    xp_ref[0:TILE, :] = xp_ref[t:t + TILE, :]

    row = lax.broadcasted_iota(jnp.int32, (t, t), 0)
    col = lax.broadcasted_iota(jnp.int32, (t, t), 1)
    tri_mask = col <= row
    tri_sel = tri_mask.astype(BF16)
    expand = expand_ref[...]
    xs, bb, cb_, xdt, acum, acum_t, ydiag = _ssd_intra(
        act, dt_ref[...], dtb_ref[...], alog_ref[...], tri_sel, tri_mask, expand, None)

    hst = hs_ref[...]
    hb = hst.astype(BF16)
    yoff = jnp.concatenate([_dot_nt(cb_[:, 128 * g:128 * (g + 1)], hb[256 * g:256 * (g + 1), :]) for g in range(2)],
                           axis=1)
    e_all = _sel_dot_lhs(jnp.exp(acum), expand)
    y = ydiag + yoff * e_all + xs * dskip_ref[...]
    mo_ref[...] = _ssd_finish(y, z_ref[...], nw_ref[...])

    dec_e = _sel_dot_lhs(jnp.exp(acum[t - 1:t, :] - acum), expand)
    xw = xdt * dec_e
    dcol = _sel_dot_rhs(expand_t_ref[...], jnp.exp(acum_t))
    cd = dcol[:, t - 1:t]
    for g in range(2):
        xw_t = xw[:, 256 * g:256 * (g + 1)].T.astype(BF16)
        dh = jnp.dot(xw_t, bb[:, 128 * g:128 * (g + 1)], preferred_element_type=F32)
        hs_ref[256 * g:256 * (g + 1), :] = hst[256 * g:256 * (g + 1), :] * cd[256 * g:256 * (g + 1), :] + dh


def _ssd_prompt(xbc, z, dt, lw):
    L = xbc.shape[0]
    row_spec = lambda w: pl.BlockSpec((BLK, w), lambda c: (c, 0))
    return pl.pallas_call(
        _ssd_prompt_body,
        grid=(L // BLK,),
        in_specs=[row_spec(CONV_DIM), row_spec(B_INNER), row_spec(128),
                  _const_spec((CONV_K, CONV_DIM)), _const_spec((1, CONV_DIM)), _const_spec((1, 128)),
                  _const_spec((1, 128)), _const_spec((1, B_INNER)), _const_spec((1, B_INNER)),
                  _const_spec((128, B_INNER)), _const_spec((B_INNER, 128))],
        out_specs=[row_spec(B_INNER), pl.BlockSpec((B_INNER, B_STATE), lambda c: (0, 0))],
        out_shape=[jax.ShapeDtypeStruct((L, B_INNER), BF16), jax.ShapeDtypeStruct((B_INNER, B_STATE), F32)],
        scratch_shapes=[pltpu.VMEM((BLK + TILE, CONV_DIM), F32)],
        compiler_params=_params(("arbitrary",)),
        name="ssd_prompt",
    )(xbc, z, dt, lw['conv_w'], lw['conv_b'], lw['dt_bias'], lw['a_log'], lw['d_skip'], lw['ssm_norm'],
      lw['expand'], lw['expand_t'])


def _ssd_sample_body(*refs):
    (xbc_ref, pre_ref, z_ref, dt_ref, h0_ref, cw_ref, cb_ref, dtb_ref, alog_ref, dskip_ref,
     nw_ref, expand_ref, expand_t_ref) = refs[:13]
    mo_ref, hout_ref, xp_ref = refs[-3:]
    t = BLK
    nbt = t // TILE
    rmod = lax.broadcasted_iota(jnp.int32, (t, 1), 0) & (TILE - 1)
    row_ok = rmod >= TOK0
    xp_ref[0:TILE, :] = jnp.zeros((TILE, CONV_DIM), F32)
    xp_ref[TILE:, :] = jnp.where(row_ok, xbc_ref[...], pre_ref[...])
    act = _conv_act(xp_ref, cw_ref, cb_ref)

    row = lax.broadcasted_iota(jnp.int32, (t, t), 0)
    col = lax.broadcasted_iota(jnp.int32, (t, t), 1)
    same = (row // TILE) == (col // TILE)
    tri_mask = (col <= row) & same
    tri_sel = tri_mask.astype(BF16)
    last_sel = (col == (row // TILE) * TILE + (TILE - 1)).astype(BF16)
    expand = expand_ref[...]
    xs, bb, cb_, xdt, acum, acum_t, ydiag = _ssd_intra(
        act, dt_ref[...], dtb_ref[...], alog_ref[...], tri_sel, tri_mask, expand, row_ok)

    yoffs = []
    for b in range(nbt):
        hb = h0_ref[b].astype(BF16)
        yoffs.append(jnp.concatenate(
            [_dot_nt(cb_[TILE * b:TILE * (b + 1), 128 * g:128 * (g + 1)], hb[256 * g:256 * (g + 1), :])
             for g in range(2)], axis=1))
    yoff = jnp.concatenate(yoffs, axis=0)
    e_all = _sel_dot_lhs(jnp.exp(acum), expand)
    y = ydiag + yoff * e_all + xs * dskip_ref[...]
    mo_ref[...] = _ssd_finish(y, z_ref[...], nw_ref[...])

    last = _sel_dot_rhs(last_sel, acum)
    dec_e = _sel_dot_lhs(jnp.exp(last - acum), expand)
    xw = xdt * dec_e
    dcol = _sel_dot_rhs(expand_t_ref[...], jnp.exp(acum_t))
    lane = lax.broadcasted_iota(jnp.int32, (256, t), 1)
    xw_ts = [xw[:, 256 * g:256 * (g + 1)].T for g in range(2)]
    for b in range(nbt):
        in_b = (lane // TILE) == b
        cd = dcol[:, TILE * b + TILE - 1:TILE * b + TILE]
        h0 = h0_ref[b]
        for g in range(2):
            lhs = jnp.where(in_b, xw_ts[g], 0.0).astype(BF16)
            dh = jnp.dot(lhs, bb[:, 128 * g:128 * (g + 1)], preferred_element_type=F32)
            hout_ref[b, 256 * g:256 * (g + 1), :] = (
                h0[256 * g:256 * (g + 1), :] * cd[256 * g:256 * (g + 1), :] + dh)


def _ssd_sample(xbc, prefix, z, dt, state, layer, lw, prev_out):
    rows = xbc.shape[0]
    nbt = BLK // TILE
    row_spec = lambda w: pl.BlockSpec((BLK, w), lambda i: (i, 0))
    st_spec = pl.BlockSpec((None, nbt, B_INNER, B_STATE), lambda i: (layer, i, 0, 0))
    in_specs = [row_spec(CONV_DIM), row_spec(CONV_DIM), row_spec(B_INNER), row_spec(128), st_spec,
                _const_spec((CONV_K, CONV_DIM)), _const_spec((1, CONV_DIM)), _const_spec((1, 128)),
                _const_spec((1, 128)), _const_spec((1, B_INNER)), _const_spec((1, B_INNER)),
                _const_spec((128, B_INNER)), _const_spec((B_INNER, 128))]
    args = [xbc, prefix, z, dt, state, lw['conv_w'], lw['conv_b'], lw['dt_bias'], lw['a_log'], lw['d_skip'],
            lw['ssm_norm'], lw['expand'], lw['expand_t']]
    aliases = {}
    if prev_out is not None:
        aliases = {len(args): 1}
        in_specs.append(pl.BlockSpec(memory_space=pl.ANY))
        args.append(prev_out)
    return pl.pallas_call(
        _ssd_sample_body,
        grid=(rows // BLK,),
        in_specs=in_specs,
        out_specs=[row_spec(B_INNER), st_spec],
        out_shape=[jax.ShapeDtypeStruct((rows, B_INNER), BF16), jax.ShapeDtypeStruct(state.shape, F32)],
        scratch_shapes=[pltpu.VMEM((BLK + TILE, CONV_DIM), F32)],
        input_output_aliases=aliases,
        compiler_params=_params(("parallel",)),
        name="ssd_sample",
    )(*args)


def _out_ffn_body(h_ref, a_ref, m_ref, c_ref, wo_ref, g2_ref, wg_ref, wu_ref, wd_ref, out_ref):
    d = functools.partial(jnp.dot, preferred_element_type=F32)
    h1 = (h_ref[...] + d(a_ref[...], wo_ref[0:256, :]) + d(m_ref[...], wo_ref[256:768, :])
          + d(c_ref[...], wo_ref[768:1024, :]))
    ms = jnp.mean(h1 * h1, axis=-1, keepdims=True)
    u = (h1 * lax.rsqrt(ms + NORM_EPS) * g2_ref[...]).astype(BF16)
    out_ref[...] = h1
    for c in range(D_FF // FF_CHUNK):
        sl = slice(FF_CHUNK * c, FF_CHUNK * (c + 1))
        act = (_silu(d(u, wg_ref[:, sl])) * d(u, wu_ref[:, sl])).astype(BF16)
        out_ref[...] += d(act, wd_ref[sl, :])


def _out_ffn(h, a_o, m_o, c_o, lw, tm):
    rows = h.shape[0]
    row_spec = lambda w: pl.BlockSpec((tm, w), lambda i: (i, 0))
    in_specs = [row_spec(D_MODEL), row_spec(Q_WIDTH), row_spec(B_INNER), row_spec(Q_WIDTH),
                _const_spec((D_MODEL, D_MODEL)), _const_spec((1, D_MODEL)), _const_spec((D_MODEL, D_FF)),
                _const_spec((D_MODEL, D_FF)), _const_spec((D_FF, D_MODEL))]
    return pl.pallas_call(
        _out_ffn_body,
        grid=(rows // tm,), in_specs=in_specs, out_specs=row_spec(D_MODEL),
        out_shape=jax.ShapeDtypeStruct((rows, D_MODEL), F32),
        compiler_params=_params(("parallel",)),
        name="out_ffn",
    )(h, a_o, m_o, c_o, lw['w_out'], lw['norm2'], lw['w_gate'], lw['w_up'], lw['w_down'])


def _rope_table(pos):
    half = ROT_DIM // 2
    inv = ROPE_THETA ** (-(jnp.arange(half, dtype=F32) * 2.0 / ROT_DIM))
    ang = pos.astype(F32)[:, None] * inv[None, :]
    cos, sin = jnp.cos(ang), jnp.sin(ang)
    n = pos.shape[0]
    c64 = jnp.concatenate([cos, cos, jnp.ones((n, HEAD_DIM - ROT_DIM), F32)], axis=1)
    s64 = jnp.concatenate([-sin, sin, jnp.zeros((n, HEAD_DIM - ROT_DIM), F32)], axis=1)
    return jnp.concatenate([c64, c64, s64, s64], axis=1)


def _layer_weights(l, norm1, w_in, a_qn, a_kn, a_sinks, c_qn, c_kn, conv_w, conv_b, dt_bias, a_log, d_skip,
                   ssm_norm, w_out, norm2, w_gate, w_up, w_down):
    w = w_in[l]
    cols = np.arange(N_IN)
    cols[_OFF['aq']:_OFF['aq'] + 256] = _OFF['aq'] + _HEAD_PERM
    cols[_OFF['cq']:_OFF['cq'] + 256] = _OFF['cq'] + _HEAD_PERM
    w = jnp.pad(w[:, cols], ((0, 0), (0, N_IN_PAD - N_IN))).astype(BF16)
    rows = np.arange(D_MODEL)
    rows[0:256] = _HEAD_PERM
    rows[768:1024] = 768 + _HEAD_PERM
    pad8 = lambda v: jnp.pad(v.astype(F32), (0, 128 - B_HEADS))[None, :]
    head_of = np.arange(B_INNER) // 64
    expand = (np.arange(128)[:, None] == head_of[None, :]).astype(np.float32)
    blk = (np.arange(256)[:, None] // 64 == np.arange(256)[None, :] // 64).astype(np.float32) / 64.0
    return dict(
        norm1=norm1[l][None, :], w_in=w,
        a_qn=jnp.tile(a_qn[l], 4)[None, :], a_kn=jnp.tile(a_kn[l], 2)[None, :],
        c_qn=jnp.tile(c_qn[l], 4)[None, :], c_kn=jnp.tile(c_kn[l], 2)[None, :],
        bd=jnp.asarray(blk, BF16),
        sink_lanes=jnp.repeat(a_sinks[l].astype(F32)[jnp.asarray([0, 2, 1, 3])], HEAD_DIM).reshape(2, 128),
        sink_rows=jnp.broadcast_to(
            jnp.repeat(a_sinks[l].astype(F32)[jnp.asarray([0, 2, 1, 3])], TILE)[:, None], (4 * TILE, 128)),
        conv_w=conv_w[l], conv_b=conv_b[l][None, :], dt_bias=pad8(dt_bias[l]), a_log=pad8(a_log[l]),
        d_skip=jnp.repeat(d_skip[l].astype(F32), 64)[None, :], ssm_norm=ssm_norm[l][None, :],
        expand=jnp.asarray(expand, BF16), expand_t=jnp.asarray(expand.T, BF16),
        w_out=w_out[l][rows, :].astype(BF16), norm2=norm2[l][None, :],
        w_gate=w_gate[l].astype(BF16), w_up=w_up[l].astype(BF16), w_down=w_down[l].astype(BF16))


def kernel(x_prompt, x_sample, cache_a_k, cache_a_v, cache_c_k, cache_c_v, state_ssm, state_conv, norm1, w_in,
           a_qn, a_kn, a_sinks, c_qn, c_kn, conv_w, conv_b, dt_bias, a_log, d_skip, ssm_norm, w_out, norm2,
           w_gate, w_up, w_down):
    depth = w_in.shape[0]
    batch, seq, _ = x_prompt.shape
    nbatch, dec_seq, _ = x_sample.shape
    assert batch == 1 and dec_seq == TILE - TOK0 and seq % (16 * BLK) == 0 and nbatch % (BLK // TILE) == 0
    past_len = PAST_LEN
    a_buf, c_buf = cache_a_k.shape[2], cache_c_k.shape[2]
    assert a_buf == A_WINDOW and c_buf == C_SPAN

    hp = x_prompt.reshape(seq, D_MODEL)
    hs = jnp.pad(x_sample, ((0, 0), (TOK0, 0), (0, 0))).reshape(nbatch * TILE, D_MODEL)
    tab_p = _rope_table(jnp.arange(seq))
    pos_s = past_len + jnp.maximum(jnp.arange(TILE) - TOK0, 0)
    tab_s = jnp.tile(_rope_table(pos_s), (nbatch, 1))
    ma_c, ma_n, mc_c, mc_n = _sample_mult_tables()

    to_fm = lambda c: jnp.transpose(c, (0, 1, 3, 4, 2)).reshape(depth, nbatch, KV_WIDTH, c.shape[2])
    from_fm = lambda c: jnp.transpose(c.reshape(depth, nbatch, 2, HEAD_DIM, c.shape[3]), (0, 1, 4, 2, 3))
    ca_k, ca_v, cc_k, cc_v = to_fm(cache_a_k), to_fm(cache_a_v), to_fm(cache_c_k), to_fm(cache_c_v)
    st = state_ssm.reshape(depth, nbatch, B_INNER, B_STATE)

    p_out = [[] for _ in range(6)]
    s_conv = []
    new_a = new_c = new_st = None
    tm_p = 512
    tm_s = min(512, nbatch * TILE)
    for l in range(depth):
        lw = _layer_weights(l, norm1, w_in, a_qn, a_kn, a_sinks, c_qn, c_kn, conv_w, conv_b, dt_bias, a_log,
                            d_skip, ssm_norm, w_out, norm2, w_gate, w_up, w_down)
        qa0, qa1, ka, va, qc0, qc1, kc, vc, z, xbc, dt = _in_proj(hp, lw, tab_p, tm_p)
        a_o = _prompt_attn(qa0, qa1, ka, va, A_PATTERNS, sink_lanes=lw['sink_lanes'], name="attn_a")
        c_o = _prompt_attn(qc0, qc1, kc, vc, C_BANDS, name="attn_c")
        m_o, h_fin = _ssd_prompt(xbc, z, dt, lw)
        hp = _out_ffn(hp, a_o, m_o, c_o, lw, tm_p)
        p_out[0].append(ka[seq - a_buf:].reshape(1, a_buf, 2, HEAD_DIM))
        p_out[1].append(va[seq - a_buf:].reshape(1, a_buf, 2, HEAD_DIM))
        p_out[2].append(kc[seq - c_buf:].reshape(1, c_buf, 2, HEAD_DIM))
        p_out[3].append(vc[seq - c_buf:].reshape(1, c_buf, 2, HEAD_DIM))
        p_out[4].append(h_fin.reshape(1, B_HEADS, 64, B_STATE))
        p_out[5].append(xbc[seq - (CONV_K - 1):].reshape(1, CONV_K - 1, CONV_DIM))

        qa0, qa1, ka, va, qc0, qc1, kc, vc, z, xbc, dt = _in_proj(hs, lw, tab_s, tm_s)
        a_o, *new_a = _sample_attn(qa0, qa1, ka, va, ca_k, ca_v, l, ma_c, ma_n, lw['sink_rows'], 8, new_a)
        c_o, *new_c = _sample_attn(qc0, qc1, kc, vc, cc_k, cc_v, l, mc_c, mc_n, None, 4, new_c)
        prefix = jnp.pad(state_conv[l], ((0, 0), (1, TILE - CONV_K), (0, 0))).reshape(nbatch * TILE, CONV_DIM)
        m_o, new_st = _ssd_sample(xbc, prefix, z, dt, st, l, lw, new_st)
        hs = _out_ffn(hs, a_o, m_o, c_o, lw, tm_s)
        s_conv.append(xbc.reshape(nbatch, TILE, CONV_DIM)[:, TILE - (CONV_K - 1):])

    outs_p = [jnp.stack(t, axis=0) for t in p_out]
    outs_s = [from_fm(new_a[0]), from_fm(new_a[1]), from_fm(new_c[0]), from_fm(new_c[1]),
              new_st.reshape(depth, nbatch, B_HEADS, 64, B_STATE), jnp.stack(s_conv, axis=0)]
    y_p = hp.reshape(1, seq, D_MODEL)
    y_s = hs.reshape(nbatch, TILE, D_MODEL)[:, TOK0:]
    return (y_p, y_s, *outs_p, *outs_s)
```

```python
import functools
import math

import numpy as np
import jax
import jax.numpy as jnp
from jax import lax
from jax.experimental import pallas as pl
from jax.experimental.pallas import tpu as pltpu

F32 = jnp.float32
BF16 = jnp.bfloat16

D_MODEL = 1024
HEAD_DIM = 64
ROT_DIM = 16
ROPE_THETA = 500000.0
NORM_EPS = 1e-6
Q_WIDTH = 256
KV_WIDTH = 128
PAST_LEN = 16384
A_WINDOW = 128
C_PATTERNS = ((128, 1), (512, 4), (2048, 16))
C_SPAN = 2048
B_HEADS = 8
B_INNER = 512
B_STATE = 128
CONV_K = 4
CONV_DIM = 1024
D_FF = 2816
N_IN = 2568
N_IN_PAD = 2688
BLK = 128
ATTN_SUPER = 2048
A_PATTERNS = ((1, A_WINDOW - 1),)
C_BANDS = tuple((d, w // d) for w, d in C_PATTERNS)
TOK0 = 4
TILE = 8
FF_CHUNK = 256
FFN_ROWS = 1024
VMEM_LIMIT = 56 * 1024 * 1024

_OFF = dict(aq=0, ak=256, av=384, cq=512, ck=768, cv=896, z=1024, xbc=1536, dt=2560, end=N_IN_PAD)
_HEAD_PERM = np.concatenate([np.arange(0, 64), np.arange(128, 192), np.arange(64, 128), np.arange(192, 256)])


def _const_spec(shape):
    nd = len(shape)
    return pl.BlockSpec(shape, lambda *_: (0,) * nd, pipeline_mode=pl.Buffered(1))


def _params(sem):
    return pltpu.CompilerParams(dimension_semantics=sem, vmem_limit_bytes=VMEM_LIMIT)


def _split3(v):
    hi = v.astype(BF16)
    r1 = v - hi.astype(F32)
    mid = r1.astype(BF16)
    lo = (r1 - mid.astype(F32)).astype(BF16)
    return hi, mid, lo


def _sel_dot_rhs(sel, v):
    hi, mid, lo = _split3(v)
    d = functools.partial(jnp.dot, preferred_element_type=F32)
    return d(sel, hi) + d(sel, mid) + d(sel, lo)


def _sel_dot_lhs(v, sel):
    hi, mid, lo = _split3(v)
    d = functools.partial(jnp.dot, preferred_element_type=F32)
    return d(hi, sel) + d(mid, sel) + d(lo, sel)


def _dot_nt(a, b):
    return lax.dot_general(a, b, (((1,), (1,)), ((), ())), preferred_element_type=F32)


def _silu(x):
    return x * jax.nn.sigmoid(x)


def _softplus(x):
    return jnp.maximum(x, 0.0) + jnp.log(1.0 + jnp.exp(-jnp.abs(x)))


def _head_norm(x, gain, bd):
    x2 = x * x
    hi = x2.astype(BF16)
    lo = (x2 - hi.astype(F32)).astype(BF16)
    ms = jnp.dot(hi, bd, preferred_element_type=F32) + jnp.dot(lo, bd, preferred_element_type=F32)
    return x * lax.rsqrt(ms + NORM_EPS) * gain


def _rope(x, cos_t, sin_t):
    w = x.shape[1]
    lane = lax.broadcasted_iota(jnp.int32, x.shape, 1) & (HEAD_DIM - 1)
    partner = jnp.where(lane < ROT_DIM // 2, pltpu.roll(x, w - ROT_DIM // 2, 1), pltpu.roll(x, ROT_DIM // 2, 1))
    return x * cos_t + partner * sin_t


def _in_proj_body(h_ref, g1_ref, w_ref, cs_ref, sel_ref, gqa_ref, gka_ref, gqc_ref, gkc_ref, bd_ref,
                  qa0_ref, qa1_ref, ka_ref, va_ref, qc0_ref, qc1_ref, kc_ref, vc_ref, z_ref, xbc_ref, dt_ref):
    x = h_ref[...]
    tm = x.shape[0]
    ms = jnp.mean(x * x, axis=-1, keepdims=True)
    u = (x * lax.rsqrt(ms + NORM_EPS) * g1_ref[...]).astype(BF16)
    cs_t = jnp.concatenate([cs_ref[...], jnp.zeros((128 - 3 * ROT_DIM, tm), F32)], axis=0).T.astype(BF16)
    tab = jnp.dot(cs_t, sel_ref[...], preferred_element_type=F32)
    unrot = ((lax.broadcasted_iota(jnp.int32, (1, 128), 1) & (HEAD_DIM - 1)) >= ROT_DIM).astype(F32)
    c1, s1 = tab[:, :128] + unrot, tab[:, 128:]
    c2 = jnp.concatenate([c1, c1], axis=1)
    s2 = jnp.concatenate([s1, s1], axis=1)
    bd2 = bd_ref[...]
    bd1 = bd2[:128, :128]

    def proj(name, nxt):
        return jnp.dot(u, w_ref[:, _OFF[name]:_OFF[nxt]], preferred_element_type=F32)

    qa = _rope(_head_norm(proj('aq', 'ak'), gqa_ref[...], bd2), c2, s2) * 0.125
    qa0_ref[...] = qa[:, :128]
    qa1_ref[...] = qa[:, 128:]
    kva = proj('ak', 'cq')
    ka_ref[...] = _rope(_head_norm(kva[:, :128], gka_ref[...], bd1), c1, s1)
    va_ref[...] = kva[:, 128:]
    qc = _rope(_head_norm(proj('cq', 'ck'), gqc_ref[...], bd2), c2, s2) * 0.125
    qc0_ref[...] = qc[:, :128]
    qc1_ref[...] = qc[:, 128:]
    kvc = proj('ck', 'z')
    kc_ref[...] = _rope(_head_norm(kvc[:, :128], gkc_ref[...], bd1), c1, s1)
    vc_ref[...] = kvc[:, 128:]
    z_ref[...] = proj('z', 'xbc')
    xbc_ref[...] = proj('xbc', 'dt')
    dt_ref[...] = proj('dt', 'end')


def _in_proj(h, lw, tab, tm):
    rows = h.shape[0]
    grid = (rows // tm,)
    row_spec = lambda w: pl.BlockSpec((tm, w), lambda i: (i, 0))
    widths = (128, 128, KV_WIDTH, KV_WIDTH, 128, 128, KV_WIDTH, KV_WIDTH, B_INNER, CONV_DIM, 128)
    dtypes = (F32,) * len(widths)
    return pl.pallas_call(
        _in_proj_body,
        grid=grid,
        in_specs=[row_spec(D_MODEL), _const_spec((1, D_MODEL)), _const_spec((D_MODEL, N_IN_PAD)),
                  pl.BlockSpec((3 * ROT_DIM, tm), lambda i: (0, i)), _const_spec((128, 256)),
                  _const_spec((1, 256)), _const_spec((1, 128)), _const_spec((1, 256)),
                  _const_spec((1, 128)), _const_spec((256, 256))],
        out_specs=[row_spec(w) for w in widths],
        out_shape=[jax.ShapeDtypeStruct((rows, w), dt) for w, dt in zip(widths, dtypes)],
        compiler_params=_params(("parallel",)),
        name="in_proj",
    )(h, lw['norm1'], lw['w_in'], tab, jnp.asarray(_ROPE_SEL, BF16), lw['a_qn'], lw['a_kn'], lw['c_qn'], lw['c_kn'], lw['bd'])


def _attn_body(*refs, patterns, has_sink, sb):
    if has_sink:
        sink_ref, refs = refs[0], refs[1:]
    q0_ref, q1_ref, kp_ref, kc_ref, vp_ref, vc_ref, o_ref, kk, vv, acc_s, m_s, l_s = refs
    q_refs = (q0_ref, q1_ref)
    j = pl.program_id(0)
    kk[0:sb, :] = kp_ref[...]
    kk[sb:2 * sb, :] = kc_ref[...]
    vv[0:sb, :] = vp_ref[...]
    vv[sb:2 * sb, :] = vc_ref[...]
    row4 = lax.broadcasted_iota(jnp.int32, (4 * BLK, BLK), 0) & (BLK - 1)
    col4 = lax.broadcasted_iota(jnp.int32, (4 * BLK, BLK), 1)
    upper4 = col4 > row4
    g_lo = lax.broadcasted_iota(jnp.int32, (BLK, 128), 1) < HEAD_DIM
    ones_cols = jnp.ones((2 * BLK, 128), BF16)
    nblk = sb // BLK

    diag_here = [md == BLK for _, md in patterns]
    boosts = [[] for _ in patterns]
    for p, (d, md) in enumerate(patterns):
        for q in range(p + 1, len(patterns)):
            dq, mdq = patterns[q]
            if diag_here[p] and (md * d) % dq == 0 and (md * d) // dq < min(BLK, mdq + 1):
                boosts[q].append((md * d) // dq)
                diag_here[p] = False

    for pi, (d, max_dist) in enumerate(patterns):
        nsub = nblk // d
        has_diag = diag_here[pi]
        fdist = jnp.where(upper4, row4 + BLK - col4, row4 - col4)
        mult = jnp.ones((4 * BLK, BLK), F32)
        for f in boosts[pi]:
            mult = mult + (fdist == f).astype(F32)
        w_up = jnp.where(upper4, mult, 0.0)
        w_lo = jnp.where(upper4, 0.0, mult)

        def ld(ref, s0, d=d):
            if d == 1:
                return ref[pl.ds(s0, BLK), :]
            return ref[pl.ds(s0, BLK, stride=d), :]

        def block(t, carry, d=d, nsub=nsub, has_diag=has_diag, first=(pi == 0), ld=ld, w_up=w_up, w_lo=w_lo):
            r_ = t // nsub
            n = t - r_ * nsub
            start = r_ + BLK * d * n
            prev_ok = jnp.logical_or(j > 0, n > 0)
            neg = jnp.where(prev_ok, 0.0, -jnp.inf)
            qb = [ld(q_refs[rr], start) for rr in range(2)]
            kprev, kcur = ld(kk, sb + start - BLK * d), ld(kk, sb + start)
            vprev, vcur = ld(vv, sb + start - BLK * d), ld(vv, sb + start)
            qm = jnp.concatenate([jnp.where(g_lo, qb[0], 0.0), jnp.where(g_lo, 0.0, qb[0]),
                                  jnp.where(g_lo, qb[1], 0.0), jnp.where(g_lo, 0.0, qb[1])], axis=0).astype(BF16)
            kcat = jnp.concatenate([kprev, kcur], axis=0).astype(BF16)
            vcat = jnp.concatenate([jnp.concatenate([vprev, vcur], axis=0).astype(BF16), ones_cols], axis=1)
            s2 = _dot_nt(qm, kcat)
            sp, sc = s2[:, :BLK] + neg, s2[:, BLK:]
            s = jnp.where(upper4, sp, sc)
            mb = jnp.max(s, axis=-1, keepdims=True)
            if has_diag:
                sd = jnp.sum(jnp.where(col4 == row4, sp, 0.0), axis=-1, keepdims=True)
                mb = jnp.maximum(mb, sd)
            e = jnp.exp(s - mb)
            ecat = jnp.concatenate([e * w_up, e * w_lo], axis=1).astype(BF16)
            pvl = jnp.dot(ecat, vcat, preferred_element_type=F32)
            pv, lb = pvl[:, :128], pvl[:, 128:]
            if has_diag:
                ed = jnp.exp(sd - mb)
                lb = lb + ed
                pv = pv + ed * jnp.concatenate([vprev] * 4, axis=0)
            for rr in range(2):
                lo, hi = slice(2 * BLK * rr, 2 * BLK * rr + BLK), slice(2 * BLK * rr + BLK, 2 * BLK * (rr + 1))
                o_b = jnp.where(g_lo, pv[lo], pv[hi])
                m_b = jnp.where(g_lo, mb[lo], mb[hi])
                l_b = jnp.where(g_lo, lb[lo], lb[hi])
                rows = pl.ds(start, BLK) if d == 1 else pl.ds(start, BLK, stride=d)
                if first:
                    m_s[rr, rows, :] = m_b
                    l_s[rr, rows, :] = l_b
                    acc_s[rr, rows, :] = o_b
                else:
                    m_old = m_s[rr, rows, :]
                    m_new = jnp.maximum(m_old, m_b)
                    w_old = jnp.exp(m_old - m_new)
                    w_b = jnp.exp(m_b - m_new)
                    m_s[rr, rows, :] = m_new
                    l_s[rr, rows, :] = w_old * l_s[rr, rows, :] + w_b * l_b
                    acc_s[rr, rows, :] = w_old * acc_s[rr, rows, :] + w_b * o_b
            return carry

        lax.fori_loop(0, nblk, block, 0, unroll=2)

    chunk = 256
    for rr in range(2):
        for c in range(sb // chunk):
            rows = slice(chunk * c, chunk * (c + 1))
            m_f, l_f, a_f = m_s[rr, rows, :], l_s[rr, rows, :], acc_s[rr, rows, :]
            if has_sink:
                sk = sink_ref[rr:rr + 1, :]
                m2 = jnp.maximum(m_f, sk)
                w = jnp.exp(m_f - m2)
                o = a_f * w / (l_f * w + jnp.exp(sk - m2))
            else:
                o = a_f / l_f
            o_ref[rows, 128 * rr:128 * (rr + 1)] = o.astype(o_ref.dtype)


def _prompt_attn(q0, q1, k, v, patterns, sink_lanes=None, name="attn"):
    L = q0.shape[0]
    sb = ATTN_SUPER
    cur = lambda j: (j, 0)
    prev = lambda j: (jnp.maximum(j - 1, 0), 0)
    blk = lambda im: pl.BlockSpec((sb, 128), im)
    in_specs = [blk(cur), blk(cur), blk(prev), blk(cur), blk(prev), blk(cur)]
    args = [q0, q1, k, k, v, v]
    if sink_lanes is not None:
        in_specs = [_const_spec((2, 128))] + in_specs
        args = [sink_lanes] + args
    return pl.pallas_call(
        functools.partial(_attn_body, patterns=patterns, has_sink=sink_lanes is not None, sb=sb),
        grid=(L // sb,), in_specs=in_specs, out_specs=pl.BlockSpec((sb, Q_WIDTH), cur),
        out_shape=jax.ShapeDtypeStruct((L, Q_WIDTH), BF16),
        scratch_shapes=[pltpu.VMEM((2 * sb, 128), F32), pltpu.VMEM((2 * sb, 128), F32),
                        pltpu.VMEM((2, sb, 128), F32), pltpu.VMEM((2, sb, 128), F32), pltpu.VMEM((2, sb, 128), F32)],
        compiler_params=_params(("parallel",)),
        name=name,
    )(*args)


def _sample_attn_body(*refs, nb, has_sink, n_alias, w):
    if has_sink:
        sink_ref, refs = refs[0], refs[1:]
    q0_ref, q1_ref, kn_ref, vn_ref, kc_ref, vc_ref, mc_ref, mn_ref = refs[:8]
    o_ref, ko_ref, vo_ref = refs[8 + n_alias:]
    g_lo = lax.broadcasted_iota(jnp.int32, (TILE, 128), 1) < HEAD_DIM
    lane = lax.broadcasted_iota(jnp.int32, (128, 128), 1)
    is_new = lane >= 128 - (TILE - TOK0)
    mult_c = mc_ref[...]
    mult_n = mn_ref[...]
    pad = jnp.zeros((128 - TILE, 128), F32)
    for b in range(nb):
        parts = []
        for q_ref in (q0_ref, q1_ref):
            qr = q_ref[TILE * b:TILE * (b + 1), :]
            parts += [jnp.where(g_lo, qr, 0.0), jnp.where(g_lo, 0.0, qr)]
        qm = jnp.concatenate(parts, axis=0).astype(BF16)
        kct = kc_ref[b]
        vct = vc_ref[b]
        knp = jnp.concatenate([kn_ref[TILE * b:TILE * (b + 1), :], pad], axis=0)
        vnp = jnp.concatenate([vn_ref[TILE * b:TILE * (b + 1), :], pad], axis=0)
        knt = knp.T
        vnt = vnp.T
        sc = jnp.where(mult_c > 0, jnp.dot(qm, kct.astype(BF16), preferred_element_type=F32), -jnp.inf)
        sn = jnp.where(mult_n > 0, jnp.dot(qm, knt.astype(BF16), preferred_element_type=F32), -jnp.inf)
        m = jnp.maximum(jnp.max(sc, axis=-1, keepdims=True), jnp.max(sn, axis=-1, keepdims=True))
        if has_sink:
            sk = sink_ref[:, :1]
            m = jnp.maximum(m, sk)
        ec = mult_c * jnp.exp(sc - m)
        en = mult_n * jnp.exp(sn - m)
        den = jnp.sum(ec, axis=-1, keepdims=True) + jnp.sum(en, axis=-1, keepdims=True)
        if has_sink:
            den = den + jnp.exp(sk - m)
        o = (_dot_nt(ec.astype(BF16), vct.astype(BF16))
             + jnp.dot(en.astype(BF16), vnp.astype(BF16), preferred_element_type=F32)) / den
        o_ref[TILE * b:TILE * (b + 1), :] = jnp.concatenate(
            [jnp.where(g_lo, o[0:TILE], o[TILE:2 * TILE]),
             jnp.where(g_lo, o[2 * TILE:3 * TILE], o[3 * TILE:4 * TILE])], axis=1).astype(o_ref.dtype)
        for src, new_t, dst in ((kct, knt, ko_ref), (vct, vnt, vo_ref)):
            shifted = pltpu.roll(src, w - (TILE - TOK0), 1)
            new_cols = pltpu.roll(new_t, 128 - TILE, 1)
            if w > 128:
                dst[b, :, 0:w - 128] = shifted[:, 0:w - 128]
            dst[b, :, w - 128:w] = jnp.where(is_new, new_cols, shifted[:, w - 128:w])


def _sample_attn(q0, q1, kn, vn, cache_k, cache_v, layer, mult_c, mult_n, sinks_rows, nb, prev_out):
    rows = q0.shape[0]
    nbatch = rows // TILE
    depth, _, _, w = cache_k.shape
    row_spec = lambda width: pl.BlockSpec((TILE * nb, width), lambda i: (i, 0))
    cache_spec = pl.BlockSpec((None, nb, KV_WIDTH, w), lambda i: (layer, i, 0, 0))
    in_specs = [row_spec(128), row_spec(128), row_spec(KV_WIDTH), row_spec(KV_WIDTH), cache_spec, cache_spec,
                _const_spec((4 * TILE, w)), _const_spec((4 * TILE, 128))]
    args = [q0, q1, kn, vn, cache_k, cache_v, mult_c, mult_n]
    if sinks_rows is not None:
        in_specs = [_const_spec((4 * TILE, 128))] + in_specs
        args = [sinks_rows] + args
    aliases = {}
    if prev_out is not None:
        aliases = {len(args): 1, len(args) + 1: 2}
        in_specs = in_specs + [pl.BlockSpec(memory_space=pl.ANY)] * 2
        args = args + list(prev_out)
    cache_shape = jax.ShapeDtypeStruct(cache_k.shape, F32)
    return pl.pallas_call(
        functools.partial(_sample_attn_body, nb=nb, has_sink=sinks_rows is not None,
                          n_alias=0 if prev_out is None else 2, w=w),
        grid=(nbatch // nb,), in_specs=in_specs, out_specs=[row_spec(Q_WIDTH), cache_spec, cache_spec],
        out_shape=[jax.ShapeDtypeStruct((rows, Q_WIDTH), BF16), cache_shape, cache_shape],
        input_output_aliases=aliases,
        compiler_params=_params(("parallel",)),
        name=f"sample_attn_w{w}",
    )(*args)


def _sample_mult_tables():
    t = np.arange(TILE) - TOK0
    tq = np.maximum(t, 0)[:, None]
    j = np.arange(A_WINDOW)[None, :]
    da = A_WINDOW + tq - j
    ma_c = ((da >= 0) & (da < A_WINDOW)).astype(np.float32)
    tn = (np.arange(128) - TOK0)[None, :]
    dn = tq - tn
    new_ok = (tn >= 0) & (tn < TILE - TOK0) & (dn >= 0)
    ma_n = (new_ok & (dn < A_WINDOW)).astype(np.float32)

    def mult(d):
        out = np.zeros(d.shape, np.float32)
        for w, dil in C_PATTERNS:
            out += ((d >= 0) & (d <= w) & (d % dil == 0)).astype(np.float32)
        return out

    jc = np.arange(C_SPAN)[None, :]
    mc_c = mult(C_SPAN + tq - jc)
    mc_n = np.where(new_ok, mult(dn), 0.0).astype(np.float32)
    tile4 = lambda a: jnp.asarray(np.tile(a, (4, 1)))
    return tile4(ma_c), tile4(ma_n), tile4(mc_c), tile4(mc_n)


def _ssd_intra(act, dt_raw, dtb, alog, tri_sel, tri_mask, expand, row_ok):
    xs = act[:, :B_INNER]
    bb = act[:, B_INNER:B_INNER + 2 * B_STATE].astype(BF16)
    cb_ = act[:, B_INNER + 2 * B_STATE:].astype(BF16)
    dtv = _softplus(dt_raw + dtb)
    if row_ok is not None:
        dtv = jnp.where(row_ok, dtv, 0.0)
    a = -jnp.exp(alog) * dtv
    acum = _sel_dot_rhs(tri_sel, a)
    acum_t = acum.T
    dt_e = _sel_dot_lhs(dtv, expand)
    xdt = xs * dt_e
    lane_lo = lax.broadcasted_iota(jnp.int32, (BLK, 128), 1) < HEAD_DIM
    ys = []
    for g in range(2):
        cbm = _dot_nt(cb_[:, 128 * g:128 * (g + 1)], bb[:, 128 * g:128 * (g + 1)])
        for pair in range(2):
            xp = xdt[:, 128 * (2 * g + pair):128 * (2 * g + pair + 1)]
            acc = None
            for j in range(2):
                h = 4 * g + 2 * pair + j
                seg = acum[:, h:h + 1] - acum_t[h:h + 1, :]
                lm = jnp.exp(jnp.where(tri_mask, seg, -jnp.inf))
                mh = (cbm * lm).astype(BF16)
                xh = jnp.where(lane_lo if j == 0 else ~lane_lo, xp, 0.0).astype(BF16)
                t = jnp.dot(mh, xh, preferred_element_type=F32)
                acc = t if acc is None else acc + t
            ys.append(acc)
    ydiag = jnp.concatenate(ys, axis=1)
    return xs, bb, cb_, xdt, acum, acum_t, ydiag


def _ssd_finish(y, z, norm_w):
    y = y * _silu(z)
    ms = jnp.mean(y * y, axis=-1, keepdims=True)
    return (y * lax.rsqrt(ms + NORM_EPS) * norm_w).astype(BF16)


def _conv_act(xp_ref, cw_ref, cb_ref):
    t = BLK
    out = cb_ref[...] + xp_ref[5:5 + t, :] * cw_ref[0:1, :]
    out = out + xp_ref[6:6 + t, :] * cw_ref[1:2, :]
    out = out + xp_ref[7:7 + t, :] * cw_ref[2:3, :]
    out = out + xp_ref[8:8 + t, :] * cw_ref[3:4, :]
    return _silu(out)


def _ssd_prompt_body(xbc_ref, z_ref, dt_ref, cw_ref, cb_ref, dtb_ref, alog_ref, dskip_ref, nw_ref,
                     expand_ref, expand_t_ref, mo_ref, hs_ref, xp_ref):
    c = pl.program_id(0)
    t = BLK

    @pl.when(c == 0)
    def _():
        xp_ref[0:TILE, :] = jnp.zeros((TILE, CONV_DIM), F32)
        hs_ref[...] = jnp.zeros_like(hs_ref)

    xp_ref[TILE:, :] = xbc_ref[...]
    act = _conv_act(xp_ref, cw_ref, cb_ref)
    xp_ref[0:TILE, :] = xp_ref[t:t + TILE, :]

    row = lax.broadcasted_iota(jnp.int32, (t, t), 0)
    col = lax.broadcasted_iota(jnp.int32, (t, t), 1)
    tri_mask = col <= row
    tri_sel = tri_mask.astype(BF16)
    expand = expand_ref[...]
    xs, bb, cb_, xdt, acum, acum_t, ydiag = _ssd_intra(
        act, dt_ref[...], dtb_ref[...], alog_ref[...], tri_sel, tri_mask, expand, None)

    hst = hs_ref[...]
    hb = hst.astype(BF16)
    yoff = jnp.concatenate([_dot_nt(cb_[:, 128 * g:128 * (g + 1)], hb[256 * g:256 * (g + 1), :]) for g in range(2)],
                           axis=1)
    e_all = _sel_dot_lhs(jnp.exp(acum), expand)
    y = ydiag + yoff * e_all + xs * dskip_ref[...]
    mo_ref[...] = _ssd_finish(y, z_ref[...], nw_ref[...])

    dec_e = _sel_dot_lhs(jnp.exp(acum[t - 1:t, :] - acum), expand)
    xw = xdt * dec_e
    dcol = _sel_dot_rhs(expand_t_ref[...], jnp.exp(acum_t))
    cd = dcol[:, t - 1:t]
    for g in range(2):
        xw_t = xw[:, 256 * g:256 * (g + 1)].T.astype(BF16)
        dh = jnp.dot(xw_t, bb[:, 128 * g:128 * (g + 1)], preferred_element_type=F32)
        hs_ref[256 * g:256 * (g + 1), :] = hst[256 * g:256 * (g + 1), :] * cd[256 * g:256 * (g + 1), :] + dh


def _ssd_prompt(xbc, z, dt, lw):
    L = xbc.shape[0]
    row_spec = lambda w: pl.BlockSpec((BLK, w), lambda c: (c, 0))
    return pl.pallas_call(
        _ssd_prompt_body,
        grid=(L // BLK,),
        in_specs=[row_spec(CONV_DIM), row_spec(B_INNER), row_spec(128),
                  _const_spec((CONV_K, CONV_DIM)), _const_spec((1, CONV_DIM)), _const_spec((1, 128)),
                  _const_spec((1, 128)), _const_spec((1, B_INNER)), _const_spec((1, B_INNER)),
                  _const_spec((128, B_INNER)), _const_spec((B_INNER, 128))],
        out_specs=[row_spec(B_INNER), pl.BlockSpec((B_INNER, B_STATE), lambda c: (0, 0))],
        out_shape=[jax.ShapeDtypeStruct((L, B_INNER), BF16), jax.ShapeDtypeStruct((B_INNER, B_STATE), F32)],
        scratch_shapes=[pltpu.VMEM((BLK + TILE, CONV_DIM), F32)],
        compiler_params=_params(("arbitrary",)),
        name="ssd_prompt",
    )(xbc, z, dt, lw['conv_w'], lw['conv_b'], lw['dt_bias'], lw['a_log'], lw['d_skip'], lw['ssm_norm'],
      lw['expand'], lw['expand_t'])


def _ssd_sample_body(*refs):
    (xbc_ref, pre_ref, z_ref, dt_ref, h0_ref, cw_ref, cb_ref, dtb_ref, alog_ref, dskip_ref,
     nw_ref, expand_ref, expand_t_ref) = refs[:13]
    mo_ref, hout_ref, xp_ref = refs[-3:]
    t = BLK
    nbt = t // TILE
    rmod = lax.broadcasted_iota(jnp.int32, (t, 1), 0) & (TILE - 1)
    row_ok = rmod >= TOK0
    xp_ref[0:TILE, :] = jnp.zeros((TILE, CONV_DIM), F32)
    xp_ref[TILE:, :] = jnp.where(row_ok, xbc_ref[...], pre_ref[...])
    act = _conv_act(xp_ref, cw_ref, cb_ref)

    row = lax.broadcasted_iota(jnp.int32, (t, t), 0)
    col = lax.broadcasted_iota(jnp.int32, (t, t), 1)
    same = (row // TILE) == (col // TILE)
    tri_mask = (col <= row) & same
    tri_sel = tri_mask.astype(BF16)
    last_sel = (col == (row // TILE) * TILE + (TILE - 1)).astype(BF16)
    expand = expand_ref[...]
    xs, bb, cb_, xdt, acum, acum_t, ydiag = _ssd_intra(
        act, dt_ref[...], dtb_ref[...], alog_ref[...], tri_sel, tri_mask, expand, row_ok)

    yoffs = []
    for b in range(nbt):
        hb = h0_ref[b].astype(BF16)
        yoffs.append(jnp.concatenate(
            [_dot_nt(cb_[TILE * b:TILE * (b + 1), 128 * g:128 * (g + 1)], hb[256 * g:256 * (g + 1), :])
             for g in range(2)], axis=1))
    yoff = jnp.concatenate(yoffs, axis=0)
    e_all = _sel_dot_lhs(jnp.exp(acum), expand)
    y = ydiag + yoff * e_all + xs * dskip_ref[...]
    mo_ref[...] = _ssd_finish(y, z_ref[...], nw_ref[...])

    last = _sel_dot_rhs(last_sel, acum)
    dec_e = _sel_dot_lhs(jnp.exp(last - acum), expand)
    xw = xdt * dec_e
    dcol = _sel_dot_rhs(expand_t_ref[...], jnp.exp(acum_t))
    lane = lax.broadcasted_iota(jnp.int32, (256, t), 1)
    xw_ts = [xw[:, 256 * g:256 * (g + 1)].T for g in range(2)]
    for b in range(nbt):
        in_b = (lane // TILE) == b
        cd = dcol[:, TILE * b + TILE - 1:TILE * b + TILE]
        h0 = h0_ref[b]
        for g in range(2):
            lhs = jnp.where(in_b, xw_ts[g], 0.0).astype(BF16)
            dh = jnp.dot(lhs, bb[:, 128 * g:128 * (g + 1)], preferred_element_type=F32)
            hout_ref[b, 256 * g:256 * (g + 1), :] = (
                h0[256 * g:256 * (g + 1), :] * cd[256 * g:256 * (g + 1), :] + dh)


def _ssd_sample(xbc, prefix, z, dt, state, layer, lw, prev_out):
    rows = xbc.shape[0]
    nbt = BLK // TILE
    row_spec = lambda w: pl.BlockSpec((BLK, w), lambda i: (i, 0))
    st_spec = pl.BlockSpec((None, nbt, B_INNER, B_STATE), lambda i: (layer, i, 0, 0))
    in_specs = [row_spec(CONV_DIM), row_spec(CONV_DIM), row_spec(B_INNER), row_spec(128), st_spec,
                _const_spec((CONV_K, CONV_DIM)), _const_spec((1, CONV_DIM)), _const_spec((1, 128)),
                _const_spec((1, 128)), _const_spec((1, B_INNER)), _const_spec((1, B_INNER)),
                _const_spec((128, B_INNER)), _const_spec((B_INNER, 128))]
    args = [xbc, prefix, z, dt, state, lw['conv_w'], lw['conv_b'], lw['dt_bias'], lw['a_log'], lw['d_skip'],
            lw['ssm_norm'], lw['expand'], lw['expand_t']]
    aliases = {}
    if prev_out is not None:
        aliases = {len(args): 1}
        in_specs.append(pl.BlockSpec(memory_space=pl.ANY))
        args.append(prev_out)
    return pl.pallas_call(
        _ssd_sample_body,
        grid=(rows // BLK,),
        in_specs=in_specs,
        out_specs=[row_spec(B_INNER), st_spec],
        out_shape=[jax.ShapeDtypeStruct((rows, B_INNER), BF16), jax.ShapeDtypeStruct(state.shape, F32)],
        scratch_shapes=[pltpu.VMEM((BLK + TILE, CONV_DIM), F32)],
        input_output_aliases=aliases,
        compiler_params=_params(("parallel",)),
        name="ssd_sample",
    )(*args)


def _out_ffn_body(h_ref, a_ref, m_ref, c_ref, wo_ref, g2_ref, wg_ref, wu_ref, wd_ref, out_ref):
    d = functools.partial(jnp.dot, preferred_element_type=F32)
    h1 = (h_ref[...] + d(a_ref[...], wo_ref[0:256, :]) + d(m_ref[...], wo_ref[256:768, :])
          + d(c_ref[...], wo_ref[768:1024, :]))
    ms = jnp.mean(h1 * h1, axis=-1, keepdims=True)
    u = (h1 * lax.rsqrt(ms + NORM_EPS) * g2_ref[...]).astype(BF16)
    out_ref[...] = h1
    for c in range(D_FF // FF_CHUNK):
        sl = slice(FF_CHUNK * c, FF_CHUNK * (c + 1))
        act = (_silu(d(u, wg_ref[:, sl])) * d(u, wu_ref[:, sl])).astype(BF16)
        out_ref[...] += d(act, wd_ref[sl, :])


def _out_ffn(h, a_o, m_o, c_o, lw, tm):
    rows = h.shape[0]
    row_spec = lambda w: pl.BlockSpec((tm, w), lambda i: (i, 0))
    in_specs = [row_spec(D_MODEL), row_spec(Q_WIDTH), row_spec(B_INNER), row_spec(Q_WIDTH),
                _const_spec((D_MODEL, D_MODEL)), _const_spec((1, D_MODEL)), _const_spec((D_MODEL, D_FF)),
                _const_spec((D_MODEL, D_FF)), _const_spec((D_FF, D_MODEL))]
    return pl.pallas_call(
        _out_ffn_body,
        grid=(rows // tm,), in_specs=in_specs, out_specs=row_spec(D_MODEL),
        out_shape=jax.ShapeDtypeStruct((rows, D_MODEL), F32),
        compiler_params=_params(("parallel",)),
        name="out_ffn",
    )(h, a_o, m_o, c_o, lw['w_out'], lw['norm2'], lw['w_gate'], lw['w_up'], lw['w_down'])


def _rope_table(pos):
    half = ROT_DIM // 2
    inv = ROPE_THETA ** (-(jnp.arange(half, dtype=F32) * 2.0 / ROT_DIM))
    ang = pos.astype(F32)[None, :] * inv[:, None]
    cs = jnp.concatenate([jnp.cos(ang), jnp.sin(ang)], axis=0)
    hi = cs.astype(BF16).astype(F32)
    mid = (cs - hi).astype(BF16).astype(F32)
    lo = (cs - hi - mid).astype(BF16).astype(F32)
    return jnp.concatenate([hi, mid, lo], axis=0)


def _rope_selection():
    half = ROT_DIM // 2
    sel = np.zeros((128, 256), np.float32)
    for term in range(3):
        base = term * ROT_DIM
        for lane in range(128):
            f = lane % HEAD_DIM
            if f < half:
                sel[base + f, lane] = 1.0
                sel[base + half + f, 128 + lane] = -1.0
            elif f < ROT_DIM:
                sel[base + f - half, lane] = 1.0
                sel[base + f, 128 + lane] = 1.0
    return sel


_ROPE_SEL = _rope_selection()


def _permute_heads(m, start, axis):
    sl = lambda a, b: lax.slice_in_dim(m, a, b, axis=axis)
    pieces = [sl(0, start)] if start else []
    pieces += [sl(start, start + 64), sl(start + 128, start + 192), sl(start + 64, start + 128),
               sl(start + 192, start + 256), sl(start + 256, m.shape[axis])]
    return jnp.concatenate(pieces, axis=axis)


def _layer_weights(l, norm1, w_in, a_qn, a_kn, a_sinks, c_qn, c_kn, conv_w, conv_b, dt_bias, a_log, d_skip,
                   ssm_norm, w_out, norm2, w_gate, w_up, w_down):
    w = _permute_heads(_permute_heads(w_in[l].astype(BF16), _OFF['aq'], 1), _OFF['cq'], 1)
    w = jnp.pad(w, ((0, 0), (0, N_IN_PAD - N_IN)))
    wo = _permute_heads(_permute_heads(w_out[l].astype(BF16), 0, 0), 768, 0)
    pad8 = lambda v: jnp.pad(v.astype(F32), (0, 128 - B_HEADS))[None, :]
    head_of = np.arange(B_INNER) // 64
    expand = (np.arange(128)[:, None] == head_of[None, :]).astype(np.float32)
    blk = (np.arange(256)[:, None] // 64 == np.arange(256)[None, :] // 64).astype(np.float32) / 64.0
    return dict(
        norm1=norm1[l][None, :], w_in=w,
        a_qn=jnp.tile(a_qn[l], 4)[None, :], a_kn=jnp.tile(a_kn[l], 2)[None, :],
        c_qn=jnp.tile(c_qn[l], 4)[None, :], c_kn=jnp.tile(c_kn[l], 2)[None, :],
        bd=jnp.asarray(blk, BF16),
        sink_lanes=jnp.repeat(a_sinks[l].astype(F32)[jnp.asarray([0, 2, 1, 3])], HEAD_DIM).reshape(2, 128),
        sink_rows=jnp.broadcast_to(
            jnp.repeat(a_sinks[l].astype(F32)[jnp.asarray([0, 2, 1, 3])], TILE)[:, None], (4 * TILE, 128)),
        conv_w=conv_w[l], conv_b=conv_b[l][None, :], dt_bias=pad8(dt_bias[l]), a_log=pad8(a_log[l]),
        d_skip=jnp.repeat(d_skip[l].astype(F32), 64)[None, :], ssm_norm=ssm_norm[l][None, :],
        expand=jnp.asarray(expand, BF16), expand_t=jnp.asarray(expand.T, BF16),
        w_out=wo, norm2=norm2[l][None, :],
        w_gate=w_gate[l].astype(BF16), w_up=w_up[l].astype(BF16), w_down=w_down[l].astype(BF16))


def kernel(x_prompt, x_sample, cache_a_k, cache_a_v, cache_c_k, cache_c_v, state_ssm, state_conv, norm1, w_in,
           a_qn, a_kn, a_sinks, c_qn, c_kn, conv_w, conv_b, dt_bias, a_log, d_skip, ssm_norm, w_out, norm2,
           w_gate, w_up, w_down):
    depth = w_in.shape[0]
    batch, seq, _ = x_prompt.shape
    nbatch, dec_seq, _ = x_sample.shape
    assert batch == 1 and dec_seq == TILE - TOK0 and seq % (16 * BLK) == 0 and nbatch % (BLK // TILE) == 0
    past_len = PAST_LEN
    a_buf, c_buf = cache_a_k.shape[2], cache_c_k.shape[2]
    assert a_buf == A_WINDOW and c_buf == C_SPAN

    hp = x_prompt.reshape(seq, D_MODEL)
    hs = jnp.pad(x_sample, ((0, 0), (TOK0, 0), (0, 0))).reshape(nbatch * TILE, D_MODEL)
    tab_p = _rope_table(jnp.arange(seq))
    pos_s = past_len + jnp.maximum(jnp.arange(TILE) - TOK0, 0)
    tab_s = jnp.tile(_rope_table(pos_s), (1, nbatch))
    ma_c, ma_n, mc_c, mc_n = _sample_mult_tables()

    to_fm = lambda c: jnp.transpose(c, (0, 1, 3, 4, 2)).reshape(depth, nbatch, KV_WIDTH, c.shape[2])
    from_fm = lambda c: jnp.transpose(c.reshape(depth, nbatch, 2, HEAD_DIM, c.shape[3]), (0, 1, 4, 2, 3))
    ca_k, ca_v, cc_k, cc_v = to_fm(cache_a_k), to_fm(cache_a_v), to_fm(cache_c_k), to_fm(cache_c_v)
    st = state_ssm.reshape(depth, nbatch, B_INNER, B_STATE)

    p_out = [[] for _ in range(6)]
    s_conv = []
    new_a = new_c = new_st = None
    tm_p = 512
    tm_s = min(512, nbatch * TILE)
    for l in range(depth):
        lw = _layer_weights(l, norm1, w_in, a_qn, a_kn, a_sinks, c_qn, c_kn, conv_w, conv_b, dt_bias, a_log,
                            d_skip, ssm_norm, w_out, norm2, w_gate, w_up, w_down)
        qa0, qa1, ka, va, qc0, qc1, kc, vc, z, xbc, dt = _in_proj(hp, lw, tab_p, tm_p)
        a_o = _prompt_attn(qa0, qa1, ka, va, A_PATTERNS, sink_lanes=lw['sink_lanes'], name="attn_a")
        c_o = _prompt_attn(qc0, qc1, kc, vc, C_BANDS, name="attn_c")
        m_o, h_fin = _ssd_prompt(xbc, z, dt, lw)
        hp = _out_ffn(hp, a_o, m_o, c_o, lw, FFN_ROWS)
        p_out[0].append(ka[seq - a_buf:].reshape(1, a_buf, 2, HEAD_DIM))
        p_out[1].append(va[seq - a_buf:].reshape(1, a_buf, 2, HEAD_DIM))
        p_out[2].append(kc[seq - c_buf:].reshape(1, c_buf, 2, HEAD_DIM))
        p_out[3].append(vc[seq - c_buf:].reshape(1, c_buf, 2, HEAD_DIM))
        p_out[4].append(h_fin.reshape(1, B_HEADS, 64, B_STATE))
        p_out[5].append(xbc[seq - (CONV_K - 1):].reshape(1, CONV_K - 1, CONV_DIM))

        qa0, qa1, ka, va, qc0, qc1, kc, vc, z, xbc, dt = _in_proj(hs, lw, tab_s, tm_s)
        a_o, *new_a = _sample_attn(qa0, qa1, ka, va, ca_k, ca_v, l, ma_c, ma_n, lw['sink_rows'], 8, new_a)
        c_o, *new_c = _sample_attn(qc0, qc1, kc, vc, cc_k, cc_v, l, mc_c, mc_n, None, 4, new_c)
        prefix = jnp.pad(state_conv[l], ((0, 0), (1, TILE - CONV_K), (0, 0))).reshape(nbatch * TILE, CONV_DIM)
        m_o, new_st = _ssd_sample(xbc, prefix, z, dt, st, l, lw, new_st)
        hs = _out_ffn(hs, a_o, m_o, c_o, lw, tm_s)
        s_conv.append(xbc.reshape(nbatch, TILE, CONV_DIM)[:, TILE - (CONV_K - 1):])

    outs_p = [jnp.stack(t, axis=0) for t in p_out]
    outs_s = [from_fm(new_a[0]), from_fm(new_a[1]), from_fm(new_c[0]), from_fm(new_c[1]),
              new_st.reshape(depth, nbatch, B_HEADS, 64, B_STATE), jnp.stack(s_conv, axis=0)]
    y_p = hp.reshape(1, seq, D_MODEL)
    y_s = hs.reshape(nbatch, TILE, D_MODEL)[:, TOK0:]
    return (y_p, y_s, *outs_p, *outs_s)
```

```python
import functools
import math

import numpy as np
import jax
import jax.numpy as jnp
from jax import lax
from jax.experimental import pallas as pl
from jax.experimental.pallas import tpu as pltpu

F32 = jnp.float32
BF16 = jnp.bfloat16

D_MODEL = 1024
HEAD_DIM = 64
ROT_DIM = 16
ROPE_THETA = 500000.0
NORM_EPS = 1e-6
Q_WIDTH = 256
KV_WIDTH = 128
PAST_LEN = 16384
A_WINDOW = 128
C_PATTERNS = ((128, 1), (512, 4), (2048, 16))
C_SPAN = 2048
B_HEADS = 8
B_INNER = 512
B_STATE = 128
CONV_K = 4
CONV_DIM = 1024
D_FF = 2816
N_IN = 2568
N_IN_PAD = 2688
BLK = 128
ATTN_SUPER = 2048
A_PATTERNS = ((1, A_WINDOW - 1),)
C_BANDS = tuple((d, w // d) for w, d in C_PATTERNS)
TOK0 = 4
TILE = 8
FF_CHUNK = 256
SSD_ROWS = 256
FFN_ROWS = 1024
VMEM_LIMIT = 56 * 1024 * 1024

_OFF = dict(aq=0, ak=256, av=384, cq=512, ck=768, cv=896, z=1024, xbc=1536, dt=2560, end=N_IN_PAD)
_HEAD_PERM = np.concatenate([np.arange(0, 64), np.arange(128, 192), np.arange(64, 128), np.arange(192, 256)])


def _const_spec(shape):
    nd = len(shape)
    return pl.BlockSpec(shape, lambda *_: (0,) * nd, pipeline_mode=pl.Buffered(1))


def _params(sem):
    return pltpu.CompilerParams(dimension_semantics=sem, vmem_limit_bytes=VMEM_LIMIT)


def _split3(v):
    hi = v.astype(BF16)
    r1 = v - hi.astype(F32)
    mid = r1.astype(BF16)
    lo = (r1 - mid.astype(F32)).astype(BF16)
    return hi, mid, lo


def _sel_dot_rhs(sel, v):
    hi, mid, lo = _split3(v)
    d = functools.partial(jnp.dot, preferred_element_type=F32)
    return d(sel, hi) + d(sel, mid) + d(sel, lo)


def _sel_dot_lhs(v, sel):
    hi, mid, lo = _split3(v)
    d = functools.partial(jnp.dot, preferred_element_type=F32)
    return d(hi, sel) + d(mid, sel) + d(lo, sel)


def _dot_nt(a, b):
    return lax.dot_general(a, b, (((1,), (1,)), ((), ())), preferred_element_type=F32)


def _silu(x):
    return x * jax.nn.sigmoid(x)


def _softplus(x):
    return jnp.maximum(x, 0.0) + jnp.log(1.0 + jnp.exp(-jnp.abs(x)))


def _head_norm(x, gain, bd):
    x2 = x * x
    hi = x2.astype(BF16)
    lo = (x2 - hi.astype(F32)).astype(BF16)
    ms = jnp.dot(hi, bd, preferred_element_type=F32) + jnp.dot(lo, bd, preferred_element_type=F32)
    return x * lax.rsqrt(ms + NORM_EPS) * gain


def _rope(x, cos_t, sin_t):
    w = x.shape[1]
    lane = lax.broadcasted_iota(jnp.int32, x.shape, 1) & (HEAD_DIM - 1)
    partner = jnp.where(lane < ROT_DIM // 2, pltpu.roll(x, w - ROT_DIM // 2, 1), pltpu.roll(x, ROT_DIM // 2, 1))
    return x * cos_t + partner * sin_t


def _in_proj_body(h_ref, g1_ref, w_ref, cs_ref, sel_ref, gqa_ref, gka_ref, gqc_ref, gkc_ref, bd_ref,
                  qa0_ref, qa1_ref, ka_ref, va_ref, qc0_ref, qc1_ref, kc_ref, vc_ref, z_ref, xbc_ref, dt_ref):
    x = h_ref[...]
    tm = x.shape[0]
    ms = jnp.mean(x * x, axis=-1, keepdims=True)
    u = (x * lax.rsqrt(ms + NORM_EPS) * g1_ref[...]).astype(BF16)

    def proj(name, nxt):
        return jnp.dot(u, w_ref[:, _OFF[name]:_OFF[nxt]], preferred_element_type=F32)

    xbc_ref[...] = proj('xbc', 'dt')
    z_ref[...] = proj('z', 'xbc')
    dt_ref[...] = proj('dt', 'end')
    cs_t = jnp.concatenate([cs_ref[...], jnp.zeros((128 - 3 * ROT_DIM, tm), F32)], axis=0).T.astype(BF16)
    tab = jnp.dot(cs_t, sel_ref[...], preferred_element_type=F32)
    unrot = ((lax.broadcasted_iota(jnp.int32, (1, 128), 1) & (HEAD_DIM - 1)) >= ROT_DIM).astype(F32)
    c1, s1 = tab[:, :128] + unrot, tab[:, 128:]
    c2 = jnp.concatenate([c1, c1], axis=1)
    s2 = jnp.concatenate([s1, s1], axis=1)
    bd2 = bd_ref[...]
    bd1 = bd2[:128, :128]
    qa = _rope(_head_norm(proj('aq', 'ak'), gqa_ref[...], bd2), c2, s2) * 0.125
    qa0_ref[...] = qa[:, :128]
    qa1_ref[...] = qa[:, 128:]
    kva = proj('ak', 'cq')
    ka_ref[...] = _rope(_head_norm(kva[:, :128], gka_ref[...], bd1), c1, s1)
    va_ref[...] = kva[:, 128:]
    qc = _rope(_head_norm(proj('cq', 'ck'), gqc_ref[...], bd2), c2, s2) * 0.125
    qc0_ref[...] = qc[:, :128]
    qc1_ref[...] = qc[:, 128:]
    kvc = proj('ck', 'z')
    kc_ref[...] = _rope(_head_norm(kvc[:, :128], gkc_ref[...], bd1), c1, s1)
    vc_ref[...] = kvc[:, 128:]


def _in_proj(h, lw, tab, tm):
    rows = h.shape[0]
    grid = (rows // tm,)
    row_spec = lambda w: pl.BlockSpec((tm, w), lambda i: (i, 0))
    widths = (128, 128, KV_WIDTH, KV_WIDTH, 128, 128, KV_WIDTH, KV_WIDTH, B_INNER, CONV_DIM, 128)
    dtypes = (F32,) * len(widths)
    return pl.pallas_call(
        _in_proj_body,
        grid=grid,
        in_specs=[row_spec(D_MODEL), _const_spec((1, D_MODEL)), _const_spec((D_MODEL, N_IN_PAD)),
                  pl.BlockSpec((3 * ROT_DIM, tm), lambda i: (0, i)), _const_spec((128, 256)),
                  _const_spec((1, 256)), _const_spec((1, 128)), _const_spec((1, 256)),
                  _const_spec((1, 128)), _const_spec((256, 256))],
        out_specs=[row_spec(w) for w in widths],
        out_shape=[jax.ShapeDtypeStruct((rows, w), dt) for w, dt in zip(widths, dtypes)],
        compiler_params=_params(("parallel",)),
        name="in_proj",
    )(h, lw['norm1'], lw['w_in'], tab, jnp.asarray(_ROPE_SEL, BF16), lw['a_qn'], lw['a_kn'], lw['c_qn'], lw['c_kn'], lw['bd'])


def _attn_body(*refs, patterns, has_sink, sb):
    if has_sink:
        sink_ref, refs = refs[0], refs[1:]
    q0_ref, q1_ref, kp_ref, kc_ref, vp_ref, vc_ref, o_ref, kk, vv, acc_s, m_s, l_s = refs
    q_refs = (q0_ref, q1_ref)
    j = pl.program_id(0)
    kk[0:sb, :] = kp_ref[...]
    kk[sb:2 * sb, :] = kc_ref[...]
    vv[0:sb, :] = vp_ref[...]
    vv[sb:2 * sb, :] = vc_ref[...]
    row4 = lax.broadcasted_iota(jnp.int32, (4 * BLK, BLK), 0) & (BLK - 1)
    col4 = lax.broadcasted_iota(jnp.int32, (4 * BLK, BLK), 1)
    upper4 = col4 > row4
    g_lo = lax.broadcasted_iota(jnp.int32, (BLK, 128), 1) < HEAD_DIM
    ones_cols = jnp.ones((2 * BLK, 128), BF16)
    nblk = sb // BLK

    diag_here = [md == BLK for _, md in patterns]
    boosts = [[] for _ in patterns]
    for p, (d, md) in enumerate(patterns):
        for q in range(p + 1, len(patterns)):
            dq, mdq = patterns[q]
            if diag_here[p] and (md * d) % dq == 0 and (md * d) // dq < min(BLK, mdq + 1):
                boosts[q].append((md * d) // dq)
                diag_here[p] = False

    for pi, (d, max_dist) in enumerate(patterns):
        nsub = nblk // d
        has_diag = diag_here[pi]
        fdist = jnp.where(upper4, row4 + BLK - col4, row4 - col4)
        mult = jnp.ones((4 * BLK, BLK), F32)
        for f in boosts[pi]:
            mult = mult + (fdist == f).astype(F32)
        w_up = jnp.where(upper4, mult, 0.0)
        w_lo = jnp.where(upper4, 0.0, mult)

        def ld(ref, s0, d=d):
            if d == 1:
                return ref[pl.ds(s0, BLK), :]
            return ref[pl.ds(s0, BLK, stride=d), :]

        def block(t, carry, d=d, nsub=nsub, has_diag=has_diag, first=(pi == 0), ld=ld, w_up=w_up, w_lo=w_lo):
            r_ = t // nsub
            n = t - r_ * nsub
            start = r_ + BLK * d * n
            prev_ok = jnp.logical_or(j > 0, n > 0)
            neg = jnp.where(prev_ok, 0.0, -jnp.inf)
            qb = [ld(q_refs[rr], start) for rr in range(2)]
            kprev, kcur = ld(kk, sb + start - BLK * d), ld(kk, sb + start)
            vprev, vcur = ld(vv, sb + start - BLK * d), ld(vv, sb + start)
            qm = jnp.concatenate([jnp.where(g_lo, qb[0], 0.0), jnp.where(g_lo, 0.0, qb[0]),
                                  jnp.where(g_lo, qb[1], 0.0), jnp.where(g_lo, 0.0, qb[1])], axis=0).astype(BF16)
            kcat = jnp.concatenate([kprev, kcur], axis=0).astype(BF16)
            vcat = jnp.concatenate([jnp.concatenate([vprev, vcur], axis=0).astype(BF16), ones_cols], axis=1)
            s2 = _dot_nt(qm, kcat)
            sp, sc = s2[:, :BLK] + neg, s2[:, BLK:]
            s = jnp.where(upper4, sp, sc)
            mb = jnp.max(s, axis=-1, keepdims=True)
            if has_diag:
                sd = jnp.sum(jnp.where(col4 == row4, sp, 0.0), axis=-1, keepdims=True)
                mb = jnp.maximum(mb, sd)
            e = jnp.exp(s - mb)
            ecat = jnp.concatenate([e * w_up, e * w_lo], axis=1).astype(BF16)
            pvl = jnp.dot(ecat, vcat, preferred_element_type=F32)
            pv, lb = pvl[:, :128], pvl[:, 128:]
            if has_diag:
                ed = jnp.exp(sd - mb)
                lb = lb + ed
                pv = pv + ed * jnp.concatenate([vprev] * 4, axis=0)
            for rr in range(2):
                lo, hi = slice(2 * BLK * rr, 2 * BLK * rr + BLK), slice(2 * BLK * rr + BLK, 2 * BLK * (rr + 1))
                o_b = jnp.where(g_lo, pv[lo], pv[hi])
                m_b = jnp.where(g_lo, mb[lo], mb[hi])
                l_b = jnp.where(g_lo, lb[lo], lb[hi])
                rows = pl.ds(start, BLK) if d == 1 else pl.ds(start, BLK, stride=d)
                if first:
                    m_s[rr, rows, :] = m_b
                    l_s[rr, rows, :] = l_b
                    acc_s[rr, rows, :] = o_b
                else:
                    m_old = m_s[rr, rows, :]
                    m_new = jnp.maximum(m_old, m_b)
                    w_old = jnp.exp(m_old - m_new)
                    w_b = jnp.exp(m_b - m_new)
                    m_s[rr, rows, :] = m_new
                    l_s[rr, rows, :] = w_old * l_s[rr, rows, :] + w_b * l_b
                    acc_s[rr, rows, :] = w_old * acc_s[rr, rows, :] + w_b * o_b
            return carry

        lax.fori_loop(0, nblk, block, 0, unroll=2)

    chunk = 256
    for rr in range(2):
        for c in range(sb // chunk):
            rows = slice(chunk * c, chunk * (c + 1))
            m_f, l_f, a_f = m_s[rr, rows, :], l_s[rr, rows, :], acc_s[rr, rows, :]
            if has_sink:
                sk = sink_ref[rr:rr + 1, :]
                m2 = jnp.maximum(m_f, sk)
                w = jnp.exp(m_f - m2)
                o = a_f * w / (l_f * w + jnp.exp(sk - m2))
            else:
                o = a_f / l_f
            o_ref[rows, 128 * rr:128 * (rr + 1)] = o.astype(o_ref.dtype)


def _prompt_attn(q0, q1, k, v, patterns, sink_lanes=None, name="attn"):
    L = q0.shape[0]
    sb = ATTN_SUPER
    cur = lambda j: (j, 0)
    prev = lambda j: (jnp.maximum(j - 1, 0), 0)
    blk = lambda im: pl.BlockSpec((sb, 128), im)
    in_specs = [blk(cur), blk(cur), blk(prev), blk(cur), blk(prev), blk(cur)]
    args = [q0, q1, k, k, v, v]
    if sink_lanes is not None:
        in_specs = [_const_spec((2, 128))] + in_specs
        args = [sink_lanes] + args
    return pl.pallas_call(
        functools.partial(_attn_body, patterns=patterns, has_sink=sink_lanes is not None, sb=sb),
        grid=(L // sb,), in_specs=in_specs, out_specs=pl.BlockSpec((sb, Q_WIDTH), cur),
        out_shape=jax.ShapeDtypeStruct((L, Q_WIDTH), BF16),
        scratch_shapes=[pltpu.VMEM((2 * sb, 128), F32), pltpu.VMEM((2 * sb, 128), F32),
                        pltpu.VMEM((2, sb, 128), F32), pltpu.VMEM((2, sb, 128), F32), pltpu.VMEM((2, sb, 128), F32)],
        compiler_params=_params(("parallel",)),
        name=name,
    )(*args)


def _sample_attn_body(*refs, nb, has_sink, n_alias, w):
    if has_sink:
        sink_ref, refs = refs[0], refs[1:]
    q0_ref, q1_ref, kn_ref, vn_ref, kc_ref, vc_ref, mc_ref, mn_ref = refs[:8]
    o_ref, ko_ref, vo_ref = refs[8 + n_alias:]
    g_lo = lax.broadcasted_iota(jnp.int32, (TILE, 128), 1) < HEAD_DIM
    lane = lax.broadcasted_iota(jnp.int32, (128, 128), 1)
    is_new = lane >= 128 - (TILE - TOK0)
    mult_c = mc_ref[...]
    mult_n = mn_ref[...]
    pad = jnp.zeros((128 - TILE, 128), F32)
    for b in range(nb):
        parts = []
        for q_ref in (q0_ref, q1_ref):
            qr = q_ref[TILE * b:TILE * (b + 1), :]
            parts += [jnp.where(g_lo, qr, 0.0), jnp.where(g_lo, 0.0, qr)]
        qm = jnp.concatenate(parts, axis=0).astype(BF16)
        kct = kc_ref[b]
        vct = vc_ref[b]
        knp = jnp.concatenate([kn_ref[TILE * b:TILE * (b + 1), :], pad], axis=0)
        vnp = jnp.concatenate([vn_ref[TILE * b:TILE * (b + 1), :], pad], axis=0)
        knt = knp.T
        vnt = vnp.T
        sc = jnp.where(mult_c > 0, jnp.dot(qm, kct.astype(BF16), preferred_element_type=F32), -jnp.inf)
        sn = jnp.where(mult_n > 0, jnp.dot(qm, knt.astype(BF16), preferred_element_type=F32), -jnp.inf)
        m = jnp.maximum(jnp.max(sc, axis=-1, keepdims=True), jnp.max(sn, axis=-1, keepdims=True))
        if has_sink:
            sk = sink_ref[:, :1]
            m = jnp.maximum(m, sk)
        ec = mult_c * jnp.exp(sc - m)
        en = mult_n * jnp.exp(sn - m)
        den = jnp.sum(ec, axis=-1, keepdims=True) + jnp.sum(en, axis=-1, keepdims=True)
        if has_sink:
            den = den + jnp.exp(sk - m)
        o = (_dot_nt(ec.astype(BF16), vct.astype(BF16))
             + jnp.dot(en.astype(BF16), vnp.astype(BF16), preferred_element_type=F32)) / den
        o_ref[TILE * b:TILE * (b + 1), :] = jnp.concatenate(
            [jnp.where(g_lo, o[0:TILE], o[TILE:2 * TILE]),
             jnp.where(g_lo, o[2 * TILE:3 * TILE], o[3 * TILE:4 * TILE])], axis=1).astype(o_ref.dtype)
        for src, new_t, dst in ((kct, knt, ko_ref), (vct, vnt, vo_ref)):
            shifted = pltpu.roll(src, w - (TILE - TOK0), 1)
            new_cols = pltpu.roll(new_t, 128 - TILE, 1)
            if w > 128:
                dst[b, :, 0:w - 128] = shifted[:, 0:w - 128]
            dst[b, :, w - 128:w] = jnp.where(is_new, new_cols, shifted[:, w - 128:w])


def _sample_attn(q0, q1, kn, vn, cache_k, cache_v, layer, mult_c, mult_n, sinks_rows, nb, prev_out):
    rows = q0.shape[0]
    nbatch = rows // TILE
    depth, _, _, w = cache_k.shape
    row_spec = lambda width: pl.BlockSpec((TILE * nb, width), lambda i: (i, 0))
    cache_spec = pl.BlockSpec((None, nb, KV_WIDTH, w), lambda i: (layer, i, 0, 0))
    in_specs = [row_spec(128), row_spec(128), row_spec(KV_WIDTH), row_spec(KV_WIDTH), cache_spec, cache_spec,
                _const_spec((4 * TILE, w)), _const_spec((4 * TILE, 128))]
    args = [q0, q1, kn, vn, cache_k, cache_v, mult_c, mult_n]
    if sinks_rows is not None:
        in_specs = [_const_spec((4 * TILE, 128))] + in_specs
        args = [sinks_rows] + args
    aliases = {}
    if prev_out is not None:
        aliases = {len(args): 1, len(args) + 1: 2}
        in_specs = in_specs + [pl.BlockSpec(memory_space=pl.ANY)] * 2
        args = args + list(prev_out)
    cache_shape = jax.ShapeDtypeStruct(cache_k.shape, F32)
    return pl.pallas_call(
        functools.partial(_sample_attn_body, nb=nb, has_sink=sinks_rows is not None,
                          n_alias=0 if prev_out is None else 2, w=w),
        grid=(nbatch // nb,), in_specs=in_specs, out_specs=[row_spec(Q_WIDTH), cache_spec, cache_spec],
        out_shape=[jax.ShapeDtypeStruct((rows, Q_WIDTH), BF16), cache_shape, cache_shape],
        input_output_aliases=aliases,
        compiler_params=_params(("parallel",)),
        name=f"sample_attn_w{w}",
    )(*args)


def _sample_mult_tables():
    t = np.arange(TILE) - TOK0
    tq = np.maximum(t, 0)[:, None]
    j = np.arange(A_WINDOW)[None, :]
    da = A_WINDOW + tq - j
    ma_c = ((da >= 0) & (da < A_WINDOW)).astype(np.float32)
    tn = (np.arange(128) - TOK0)[None, :]
    dn = tq - tn
    new_ok = (tn >= 0) & (tn < TILE - TOK0) & (dn >= 0)
    ma_n = (new_ok & (dn < A_WINDOW)).astype(np.float32)

    def mult(d):
        out = np.zeros(d.shape, np.float32)
        for w, dil in C_PATTERNS:
            out += ((d >= 0) & (d <= w) & (d % dil == 0)).astype(np.float32)
        return out

    jc = np.arange(C_SPAN)[None, :]
    mc_c = mult(C_SPAN + tq - jc)
    mc_n = np.where(new_ok, mult(dn), 0.0).astype(np.float32)
    tile4 = lambda a: jnp.asarray(np.tile(a, (4, 1)))
    return tile4(ma_c), tile4(ma_n), tile4(mc_c), tile4(mc_n)


def _ssd_intra(act, dt_raw, dtb, alog, tri_sel, tri_mask, expand, row_ok):
    xs = act[:, :B_INNER]
    bb = act[:, B_INNER:B_INNER + 2 * B_STATE].astype(BF16)
    cb_ = act[:, B_INNER + 2 * B_STATE:].astype(BF16)
    dtv = _softplus(dt_raw + dtb)
    if row_ok is not None:
        dtv = jnp.where(row_ok, dtv, 0.0)
    a = -jnp.exp(alog) * dtv
    acum = _sel_dot_rhs(tri_sel, a)
    acum_t = acum.T
    dt_e = _sel_dot_lhs(dtv, expand)
    xdt = xs * dt_e
    lane_lo = lax.broadcasted_iota(jnp.int32, (BLK, 128), 1) < HEAD_DIM
    ys = []
    for g in range(2):
        cbm = _dot_nt(cb_[:, 128 * g:128 * (g + 1)], bb[:, 128 * g:128 * (g + 1)])
        for pair in range(2):
            xp = xdt[:, 128 * (2 * g + pair):128 * (2 * g + pair + 1)]
            acc = None
            for j in range(2):
                h = 4 * g + 2 * pair + j
                seg = acum[:, h:h + 1] - acum_t[h:h + 1, :]
                lm = jnp.exp(jnp.where(tri_mask, seg, -jnp.inf))
                mh = (cbm * lm).astype(BF16)
                xh = jnp.where(lane_lo if j == 0 else ~lane_lo, xp, 0.0).astype(BF16)
                t = jnp.dot(mh, xh, preferred_element_type=F32)
                acc = t if acc is None else acc + t
            ys.append(acc)
    ydiag = jnp.concatenate(ys, axis=1)
    return xs, bb, cb_, xdt, acum, acum_t, ydiag


def _ssd_finish(y, z, norm_w):
    y = y * _silu(z)
    ms = jnp.mean(y * y, axis=-1, keepdims=True)
    return (y * lax.rsqrt(ms + NORM_EPS) * norm_w).astype(BF16)


def _conv_act(xp_ref, cw_ref, cb_ref, r0=0):
    t = BLK
    base = r0 + TILE - (CONV_K - 1)
    out = cb_ref[...] + xp_ref[base:base + t, :] * cw_ref[0:1, :]
    for j in range(1, CONV_K):
        out = out + xp_ref[base + j:base + j + t, :] * cw_ref[j:j + 1, :]
    return _silu(out)


def _split2_dot(v, sel):
    hi = v.astype(BF16)
    lo = (v - hi.astype(F32)).astype(BF16)
    return jnp.dot(hi, sel, preferred_element_type=F32) + jnp.dot(lo, sel, preferred_element_type=F32)


def _ssd_prompt_chunk(act, z, dt_raw, dtb, alog, dskip, nw, expand, expand_t, hs_ref):
    t = BLK
    row = lax.broadcasted_iota(jnp.int32, (t, t), 0)
    col = lax.broadcasted_iota(jnp.int32, (t, t), 1)
    tri_mask = col <= row
    xs = act[:, :B_INNER]
    bb = act[:, B_INNER:B_INNER + 2 * B_STATE].astype(BF16)
    cb_ = act[:, B_INNER + 2 * B_STATE:].astype(BF16)
    dtv = _softplus(dt_raw + dtb)
    a = -jnp.exp(alog) * dtv
    acum = _sel_dot_rhs(tri_mask.astype(BF16), a)
    acum_t = acum.T
    spread = _split2_dot(jnp.concatenate([dtv, jnp.exp(acum), jnp.exp(acum[t - 1:t, :] - acum)], axis=0), expand)
    dt_e, e_all, dec_e = spread[0:t], spread[t:2 * t], spread[2 * t:3 * t]
    xdt = xs * dt_e
    lane_lo = lax.broadcasted_iota(jnp.int32, (BLK, 128), 1) < HEAD_DIM
    ys = []
    for g in range(2):
        cbm = _dot_nt(cb_[:, 128 * g:128 * (g + 1)], bb[:, 128 * g:128 * (g + 1)])
        for pair in range(2):
            xp = xdt[:, 128 * (2 * g + pair):128 * (2 * g + pair + 1)]
            acc = None
            for j in range(2):
                h = 4 * g + 2 * pair + j
                seg = acum[:, h:h + 1] - acum_t[h:h + 1, :]
                lm = jnp.exp(jnp.where(tri_mask, seg, -jnp.inf))
                mh = (cbm * lm).astype(BF16)
                xh = jnp.where(lane_lo if j == 0 else ~lane_lo, xp, 0.0).astype(BF16)
                part = jnp.dot(mh, xh, preferred_element_type=F32)
                acc = part if acc is None else acc + part
            ys.append(acc)
    ydiag = jnp.concatenate(ys, axis=1)

    hst = hs_ref[...]
    hb = hst.astype(BF16)
    yoff = jnp.concatenate([_dot_nt(cb_[:, 128 * g:128 * (g + 1)], hb[256 * g:256 * (g + 1), :]) for g in range(2)],
                           axis=1)
    y = ydiag + yoff * e_all + xs * dskip
    out = _ssd_finish(y, z, nw)

    xw = xdt * dec_e
    last_t = jnp.exp(jnp.broadcast_to(acum_t[:, t - 1:t], (128, 128)))
    hi = last_t.astype(BF16)
    lo = (last_t - hi.astype(F32)).astype(BF16)
    cd = (jnp.dot(expand_t, hi, preferred_element_type=F32)
          + jnp.dot(expand_t, lo, preferred_element_type=F32))
    for g in range(2):
        xw_t = xw[:, 256 * g:256 * (g + 1)].T.astype(BF16)
        dh = jnp.dot(xw_t, bb[:, 128 * g:128 * (g + 1)], preferred_element_type=F32)
        hs_ref[256 * g:256 * (g + 1), :] = hst[256 * g:256 * (g + 1), :] * cd[256 * g:256 * (g + 1), :] + dh
    return out


def _ssd_prompt_body(xbc_ref, z_ref, dt_ref, cw_ref, cb_ref, dtb_ref, alog_ref, dskip_ref, nw_ref,
                     expand_ref, expand_t_ref, mo_ref, hs_ref, xp_ref):
    c = pl.program_id(0)
    rows = xbc_ref.shape[0]

    @pl.when(c == 0)
    def _():
        xp_ref[0:TILE, :] = jnp.zeros((TILE, CONV_DIM), F32)
        hs_ref[...] = jnp.zeros_like(hs_ref)

    xp_ref[TILE:, :] = xbc_ref[...]
    for ci in range(rows // BLK):
        r0 = BLK * ci
        act = _conv_act(xp_ref, cw_ref, cb_ref, r0)
        mo_ref[r0:r0 + BLK, :] = _ssd_prompt_chunk(
            act, z_ref[r0:r0 + BLK, :], dt_ref[r0:r0 + BLK, :], dtb_ref[...], alog_ref[...], dskip_ref[...],
            nw_ref[...], expand_ref[...], expand_t_ref[...], hs_ref)
    xp_ref[0:TILE, :] = xp_ref[rows:rows + TILE, :]


def _ssd_prompt(xbc, z, dt, lw):
    L = xbc.shape[0]
    rows = SSD_ROWS
    row_spec = lambda w: pl.BlockSpec((rows, w), lambda c: (c, 0))
    return pl.pallas_call(
        _ssd_prompt_body,
        grid=(L // rows,),
        in_specs=[row_spec(CONV_DIM), row_spec(B_INNER), row_spec(128),
                  _const_spec((CONV_K, CONV_DIM)), _const_spec((1, CONV_DIM)), _const_spec((1, 128)),
                  _const_spec((1, 128)), _const_spec((1, B_INNER)), _const_spec((1, B_INNER)),
                  _const_spec((128, B_INNER)), _const_spec((B_INNER, 128))],
        out_specs=[row_spec(B_INNER), pl.BlockSpec((B_INNER, B_STATE), lambda c: (0, 0))],
        out_shape=[jax.ShapeDtypeStruct((L, B_INNER), BF16), jax.ShapeDtypeStruct((B_INNER, B_STATE), F32)],
        scratch_shapes=[pltpu.VMEM((rows + TILE, CONV_DIM), F32)],
        compiler_params=_params(("arbitrary",)),
        name="ssd_prompt",
    )(xbc, z, dt, lw['conv_w'], lw['conv_b'], lw['dt_bias'], lw['a_log'], lw['d_skip'], lw['ssm_norm'],
      lw['expand'], lw['expand_t'])


def _ssd_sample_body(*refs):
    (xbc_ref, pre_ref, z_ref, dt_ref, h0_ref, cw_ref, cb_ref, dtb_ref, alog_ref, dskip_ref,
     nw_ref, expand_ref, expand_t_ref) = refs[:13]
    mo_ref, hout_ref, xp_ref = refs[-3:]
    t = BLK
    nbt = t // TILE
    rmod = lax.broadcasted_iota(jnp.int32, (t, 1), 0) & (TILE - 1)
    row_ok = rmod >= TOK0
    xp_ref[0:TILE, :] = jnp.zeros((TILE, CONV_DIM), F32)
    xp_ref[TILE:, :] = jnp.where(row_ok, xbc_ref[...], pre_ref[...])
    act = _conv_act(xp_ref, cw_ref, cb_ref)

    row = lax.broadcasted_iota(jnp.int32, (t, t), 0)
    col = lax.broadcasted_iota(jnp.int32, (t, t), 1)
    same = (row // TILE) == (col // TILE)
    tri_mask = (col <= row) & same
    tri_sel = tri_mask.astype(BF16)
    last_sel = (col == (row // TILE) * TILE + (TILE - 1)).astype(BF16)
    expand = expand_ref[...]
    xs, bb, cb_, xdt, acum, acum_t, ydiag = _ssd_intra(
        act, dt_ref[...], dtb_ref[...], alog_ref[...], tri_sel, tri_mask, expand, row_ok)

    yoffs = []
    for b in range(nbt):
        hb = h0_ref[b].astype(BF16)
        yoffs.append(jnp.concatenate(
            [_dot_nt(cb_[TILE * b:TILE * (b + 1), 128 * g:128 * (g + 1)], hb[256 * g:256 * (g + 1), :])
             for g in range(2)], axis=1))
    yoff = jnp.concatenate(yoffs, axis=0)
    e_all = _sel_dot_lhs(jnp.exp(acum), expand)
    y = ydiag + yoff * e_all + xs * dskip_ref[...]
    mo_ref[...] = _ssd_finish(y, z_ref[...], nw_ref[...])

    last = _sel_dot_rhs(last_sel, acum)
    dec_e = _sel_dot_lhs(jnp.exp(last - acum), expand)
    xw = xdt * dec_e
    dcol = _sel_dot_rhs(expand_t_ref[...], jnp.exp(acum_t))
    lane = lax.broadcasted_iota(jnp.int32, (256, t), 1)
    xw_ts = [xw[:, 256 * g:256 * (g + 1)].T for g in range(2)]
    for b in range(nbt):
        in_b = (lane // TILE) == b
        cd = dcol[:, TILE * b + TILE - 1:TILE * b + TILE]
        h0 = h0_ref[b]
        for g in range(2):
            lhs = jnp.where(in_b, xw_ts[g], 0.0).astype(BF16)
            dh = jnp.dot(lhs, bb[:, 128 * g:128 * (g + 1)], preferred_element_type=F32)
            hout_ref[b, 256 * g:256 * (g + 1), :] = (
                h0[256 * g:256 * (g + 1), :] * cd[256 * g:256 * (g + 1), :] + dh)


def _ssd_sample(xbc, prefix, z, dt, state, layer, lw, prev_out):
    rows = xbc.shape[0]
    nbt = BLK // TILE
    row_spec = lambda w: pl.BlockSpec((BLK, w), lambda i: (i, 0))
    st_spec = pl.BlockSpec((None, nbt, B_INNER, B_STATE), lambda i: (layer, i, 0, 0))
    in_specs = [row_spec(CONV_DIM), row_spec(CONV_DIM), row_spec(B_INNER), row_spec(128), st_spec,
                _const_spec((CONV_K, CONV_DIM)), _const_spec((1, CONV_DIM)), _const_spec((1, 128)),
                _const_spec((1, 128)), _const_spec((1, B_INNER)), _const_spec((1, B_INNER)),
                _const_spec((128, B_INNER)), _const_spec((B_INNER, 128))]
    args = [xbc, prefix, z, dt, state, lw['conv_w'], lw['conv_b'], lw['dt_bias'], lw['a_log'], lw['d_skip'],
            lw['ssm_norm'], lw['expand'], lw['expand_t']]
    aliases = {}
    if prev_out is not None:
        aliases = {len(args): 1}
        in_specs.append(pl.BlockSpec(memory_space=pl.ANY))
        args.append(prev_out)
    return pl.pallas_call(
        _ssd_sample_body,
        grid=(rows // BLK,),
        in_specs=in_specs,
        out_specs=[row_spec(B_INNER), st_spec],
        out_shape=[jax.ShapeDtypeStruct((rows, B_INNER), BF16), jax.ShapeDtypeStruct(state.shape, F32)],
        scratch_shapes=[pltpu.VMEM((BLK + TILE, CONV_DIM), F32)],
        input_output_aliases=aliases,
        compiler_params=_params(("parallel",)),
        name="ssd_sample",
    )(*args)


def _out_ffn_body(h_ref, a_ref, m_ref, c_ref, wo_ref, g2_ref, wg_ref, wu_ref, wd_ref, out_ref):
    d = functools.partial(jnp.dot, preferred_element_type=F32)
    h1 = (h_ref[...] + d(a_ref[...], wo_ref[0:256, :]) + d(m_ref[...], wo_ref[256:768, :])
          + d(c_ref[...], wo_ref[768:1024, :]))
    ms = jnp.mean(h1 * h1, axis=-1, keepdims=True)
    u = (h1 * lax.rsqrt(ms + NORM_EPS) * g2_ref[...]).astype(BF16)
    out_ref[...] = h1
    for c in range(D_FF // FF_CHUNK):
        sl = slice(FF_CHUNK * c, FF_CHUNK * (c + 1))
        act = (_silu(d(u, wg_ref[:, sl])) * d(u, wu_ref[:, sl])).astype(BF16)
        out_ref[...] += d(act, wd_ref[sl, :])


def _out_ffn(h, a_o, m_o, c_o, lw, tm):
    rows = h.shape[0]
    row_spec = lambda w: pl.BlockSpec((tm, w), lambda i: (i, 0))
    in_specs = [row_spec(D_MODEL), row_spec(Q_WIDTH), row_spec(B_INNER), row_spec(Q_WIDTH),
                _const_spec((D_MODEL, D_MODEL)), _const_spec((1, D_MODEL)), _const_spec((D_MODEL, D_FF)),
                _const_spec((D_MODEL, D_FF)), _const_spec((D_FF, D_MODEL))]
    return pl.pallas_call(
        _out_ffn_body,
        grid=(rows // tm,), in_specs=in_specs, out_specs=row_spec(D_MODEL),
        out_shape=jax.ShapeDtypeStruct((rows, D_MODEL), F32),
        compiler_params=_params(("parallel",)),
        name="out_ffn",
    )(h, a_o, m_o, c_o, lw['w_out'], lw['norm2'], lw['w_gate'], lw['w_up'], lw['w_down'])


def _rope_table(pos):
    half = ROT_DIM // 2
    inv = ROPE_THETA ** (-(jnp.arange(half, dtype=F32) * 2.0 / ROT_DIM))
    ang = pos.astype(F32)[None, :] * inv[:, None]
    cs = jnp.concatenate([jnp.cos(ang), jnp.sin(ang)], axis=0)
    hi = cs.astype(BF16).astype(F32)
    mid = (cs - hi).astype(BF16).astype(F32)
    lo = (cs - hi - mid).astype(BF16).astype(F32)
    return jnp.concatenate([hi, mid, lo], axis=0)


def _rope_selection():
    half = ROT_DIM // 2
    sel = np.zeros((128, 256), np.float32)
    for term in range(3):
        base = term * ROT_DIM
        for lane in range(128):
            f = lane % HEAD_DIM
            if f < half:
                sel[base + f, lane] = 1.0
                sel[base + half + f, 128 + lane] = -1.0
            elif f < ROT_DIM:
                sel[base + f - half, lane] = 1.0
                sel[base + f, 128 + lane] = 1.0
    return sel


_ROPE_SEL = _rope_selection()


def _permute_heads(m, start, axis):
    sl = lambda a, b: lax.slice_in_dim(m, a, b, axis=axis)
    pieces = [sl(0, start)] if start else []
    pieces += [sl(start, start + 64), sl(start + 128, start + 192), sl(start + 64, start + 128),
               sl(start + 192, start + 256), sl(start + 256, m.shape[axis])]
    return jnp.concatenate(pieces, axis=axis)


def _layer_weights(l, norm1, w_in, a_qn, a_kn, a_sinks, c_qn, c_kn, conv_w, conv_b, dt_bias, a_log, d_skip,
                   ssm_norm, w_out, norm2, w_gate, w_up, w_down):
    w = _permute_heads(_permute_heads(w_in[l].astype(BF16), _OFF['aq'], 1), _OFF['cq'], 1)
    w = jnp.pad(w, ((0, 0), (0, N_IN_PAD - N_IN)))
    wo = _permute_heads(_permute_heads(w_out[l].astype(BF16), 0, 0), 768, 0)
    pad8 = lambda v: jnp.pad(v.astype(F32), (0, 128 - B_HEADS))[None, :]
    head_of = np.arange(B_INNER) // 64
    expand = (np.arange(128)[:, None] == head_of[None, :]).astype(np.float32)
    blk = (np.arange(256)[:, None] // 64 == np.arange(256)[None, :] // 64).astype(np.float32) / 64.0
    return dict(
        norm1=norm1[l][None, :], w_in=w,
        a_qn=jnp.tile(a_qn[l], 4)[None, :], a_kn=jnp.tile(a_kn[l], 2)[None, :],
        c_qn=jnp.tile(c_qn[l], 4)[None, :], c_kn=jnp.tile(c_kn[l], 2)[None, :],
        bd=jnp.asarray(blk, BF16),
        sink_lanes=jnp.repeat(a_sinks[l].astype(F32)[jnp.asarray([0, 2, 1, 3])], HEAD_DIM).reshape(2, 128),
        sink_rows=jnp.broadcast_to(
            jnp.repeat(a_sinks[l].astype(F32)[jnp.asarray([0, 2, 1, 3])], TILE)[:, None], (4 * TILE, 128)),
        conv_w=conv_w[l], conv_b=conv_b[l][None, :], dt_bias=pad8(dt_bias[l]), a_log=pad8(a_log[l]),
        d_skip=jnp.repeat(d_skip[l].astype(F32), 64)[None, :], ssm_norm=ssm_norm[l][None, :],
        expand=jnp.asarray(expand, BF16), expand_t=jnp.asarray(expand.T, BF16),
        w_out=wo, norm2=norm2[l][None, :],
        w_gate=w_gate[l].astype(BF16), w_up=w_up[l].astype(BF16), w_down=w_down[l].astype(BF16))


def kernel(x_prompt, x_sample, cache_a_k, cache_a_v, cache_c_k, cache_c_v, state_ssm, state_conv, norm1, w_in,
           a_qn, a_kn, a_sinks, c_qn, c_kn, conv_w, conv_b, dt_bias, a_log, d_skip, ssm_norm, w_out, norm2,
           w_gate, w_up, w_down):
    depth = w_in.shape[0]
    batch, seq, _ = x_prompt.shape
    nbatch, dec_seq, _ = x_sample.shape
    assert batch == 1 and dec_seq == TILE - TOK0 and seq % (16 * BLK) == 0 and nbatch % (BLK // TILE) == 0
    past_len = PAST_LEN
    a_buf, c_buf = cache_a_k.shape[2], cache_c_k.shape[2]
    assert a_buf == A_WINDOW and c_buf == C_SPAN

    hp = x_prompt.reshape(seq, D_MODEL)
    hs = jnp.pad(x_sample, ((0, 0), (TOK0, 0), (0, 0))).reshape(nbatch * TILE, D_MODEL)
    tab_p = _rope_table(jnp.arange(seq))
    pos_s = past_len + jnp.maximum(jnp.arange(TILE) - TOK0, 0)
    tab_s = jnp.tile(_rope_table(pos_s), (1, nbatch))
    ma_c, ma_n, mc_c, mc_n = _sample_mult_tables()

    to_fm = lambda c: jnp.transpose(c, (0, 1, 3, 4, 2)).reshape(depth, nbatch, KV_WIDTH, c.shape[2])
    from_fm = lambda c: jnp.transpose(c.reshape(depth, nbatch, 2, HEAD_DIM, c.shape[3]), (0, 1, 4, 2, 3))
    ca_k, ca_v, cc_k, cc_v = to_fm(cache_a_k), to_fm(cache_a_v), to_fm(cache_c_k), to_fm(cache_c_v)
    st = state_ssm.reshape(depth, nbatch, B_INNER, B_STATE)

    p_out = [[] for _ in range(6)]
    s_conv = []
    new_a = new_c = new_st = None
    tm_p = 512
    tm_s = min(512, nbatch * TILE)
    for l in range(depth):
        lw = _layer_weights(l, norm1, w_in, a_qn, a_kn, a_sinks, c_qn, c_kn, conv_w, conv_b, dt_bias, a_log,
                            d_skip, ssm_norm, w_out, norm2, w_gate, w_up, w_down)
        qa0, qa1, ka, va, qc0, qc1, kc, vc, z, xbc, dt = _in_proj(hp, lw, tab_p, tm_p)
        a_o = _prompt_attn(qa0, qa1, ka, va, A_PATTERNS, sink_lanes=lw['sink_lanes'], name="attn_a")
        c_o = _prompt_attn(qc0, qc1, kc, vc, C_BANDS, name="attn_c")
        m_o, h_fin = _ssd_prompt(xbc, z, dt, lw)
        hp = _out_ffn(hp, a_o, m_o, c_o, lw, FFN_ROWS)
        p_out[0].append(ka[seq - a_buf:].reshape(1, a_buf, 2, HEAD_DIM))
        p_out[1].append(va[seq - a_buf:].reshape(1, a_buf, 2, HEAD_DIM))
        p_out[2].append(kc[seq - c_buf:].reshape(1, c_buf, 2, HEAD_DIM))
        p_out[3].append(vc[seq - c_buf:].reshape(1, c_buf, 2, HEAD_DIM))
        p_out[4].append(h_fin.reshape(1, B_HEADS, 64, B_STATE))
        p_out[5].append(xbc[seq - (CONV_K - 1):].reshape(1, CONV_K - 1, CONV_DIM))

        qa0, qa1, ka, va, qc0, qc1, kc, vc, z, xbc, dt = _in_proj(hs, lw, tab_s, tm_s)
        a_o, *new_a = _sample_attn(qa0, qa1, ka, va, ca_k, ca_v, l, ma_c, ma_n, lw['sink_rows'], 8, new_a)
        c_o, *new_c = _sample_attn(qc0, qc1, kc, vc, cc_k, cc_v, l, mc_c, mc_n, None, 4, new_c)
        prefix = jnp.pad(state_conv[l], ((0, 0), (1, TILE - CONV_K), (0, 0))).reshape(nbatch * TILE, CONV_DIM)
        m_o, new_st = _ssd_sample(xbc, prefix, z, dt, st, l, lw, new_st)
        hs = _out_ffn(hs, a_o, m_o, c_o, lw, tm_s)
        s_conv.append(xbc.reshape(nbatch, TILE, CONV_DIM)[:, TILE - (CONV_K - 1):])

    outs_p = [jnp.stack(t, axis=0) for t in p_out]
    outs_s = [from_fm(new_a[0]), from_fm(new_a[1]), from_fm(new_c[0]), from_fm(new_c[1]),
              new_st.reshape(depth, nbatch, B_HEADS, 64, B_STATE), jnp.stack(s_conv, axis=0)]
    y_p = hp.reshape(1, seq, D_MODEL)
    y_s = hs.reshape(nbatch, TILE, D_MODEL)[:, TOK0:]
    return (y_p, y_s, *outs_p, *outs_s)
```

```python
import functools
import math

import numpy as np
import jax
import jax.numpy as jnp
from jax import lax
from jax.experimental import pallas as pl
from jax.experimental.pallas import tpu as pltpu

F32 = jnp.float32
BF16 = jnp.bfloat16

D_MODEL = 1024
HEAD_DIM = 64
ROT_DIM = 16
ROPE_THETA = 500000.0
NORM_EPS = 1e-6
Q_WIDTH = 256
KV_WIDTH = 128
PAST_LEN = 16384
A_WINDOW = 128
C_PATTERNS = ((128, 1), (512, 4), (2048, 16))
C_SPAN = 2048
B_HEADS = 8
B_INNER = 512
B_STATE = 128
CONV_K = 4
CONV_DIM = 1024
D_FF = 2816
N_IN = 2568
N_IN_PAD = 2688
BLK = 128
ATTN_SUPER = 2048
A_PATTERNS = ((1, A_WINDOW - 1),)
C_BANDS = tuple((d, w // d) for w, d in C_PATTERNS)
TOK0 = 4
TILE = 8
FF_CHUNK = 256
SSD_ROWS = 256
FFN_ROWS = 1024
VMEM_LIMIT = 56 * 1024 * 1024

_OFF = dict(aq=0, ak=256, av=384, cq=512, ck=768, cv=896, z=1024, xbc=1536, dt=2560, end=N_IN_PAD)
_HEAD_PERM = np.concatenate([np.arange(0, 64), np.arange(128, 192), np.arange(64, 128), np.arange(192, 256)])


def _const_spec(shape):
    nd = len(shape)
    return pl.BlockSpec(shape, lambda *_: (0,) * nd, pipeline_mode=pl.Buffered(1))


def _params(sem):
    return pltpu.CompilerParams(dimension_semantics=sem, vmem_limit_bytes=VMEM_LIMIT)


def _split3(v):
    hi = v.astype(BF16)
    r1 = v - hi.astype(F32)
    mid = r1.astype(BF16)
    lo = (r1 - mid.astype(F32)).astype(BF16)
    return hi, mid, lo


def _sel_dot_rhs(sel, v):
    hi, mid, lo = _split3(v)
    d = functools.partial(jnp.dot, preferred_element_type=F32)
    return d(sel, hi) + d(sel, mid) + d(sel, lo)


def _sel_dot_lhs(v, sel):
    hi, mid, lo = _split3(v)
    d = functools.partial(jnp.dot, preferred_element_type=F32)
    return d(hi, sel) + d(mid, sel) + d(lo, sel)


def _dot_nt(a, b):
    return lax.dot_general(a, b, (((1,), (1,)), ((), ())), preferred_element_type=F32)


def _silu(x):
    return x * jax.nn.sigmoid(x)


def _softplus(x):
    return jnp.maximum(x, 0.0) + jnp.log(1.0 + jnp.exp(-jnp.abs(x)))


def _head_norm(x, gain, bd):
    x2 = x * x
    hi = x2.astype(BF16)
    lo = (x2 - hi.astype(F32)).astype(BF16)
    ms = jnp.dot(hi, bd, preferred_element_type=F32) + jnp.dot(lo, bd, preferred_element_type=F32)
    return x * lax.rsqrt(ms + NORM_EPS) * gain


def _rope(x, cos_t, sin_t):
    w = x.shape[1]
    lane = lax.broadcasted_iota(jnp.int32, x.shape, 1) & (HEAD_DIM - 1)
    partner = jnp.where(lane < ROT_DIM // 2, pltpu.roll(x, w - ROT_DIM // 2, 1), pltpu.roll(x, ROT_DIM // 2, 1))
    return x * cos_t + partner * sin_t


def _in_proj_body(h_ref, g1_ref, w_ref, cs_ref, sel_ref, gqa_ref, gka_ref, gqc_ref, gkc_ref, bd_ref,
                  qa0_ref, qa1_ref, ka_ref, va_ref, qc0_ref, qc1_ref, kc_ref, vc_ref, z_ref, xbc_ref, dt_ref):
    x = h_ref[...]
    tm = x.shape[0]
    ms = jnp.mean(x * x, axis=-1, keepdims=True)
    u = (x * lax.rsqrt(ms + NORM_EPS) * g1_ref[...]).astype(BF16)

    def proj(name, nxt):
        return jnp.dot(u, w_ref[:, _OFF[name]:_OFF[nxt]], preferred_element_type=F32)

    xbc_ref[...] = proj('xbc', 'dt')
    z_ref[...] = proj('z', 'xbc')
    dt_ref[...] = proj('dt', 'end')
    cs_t = jnp.concatenate([cs_ref[...], jnp.zeros((128 - 3 * ROT_DIM, tm), F32)], axis=0).T.astype(BF16)
    tab = jnp.dot(cs_t, sel_ref[...], preferred_element_type=F32)
    unrot = ((lax.broadcasted_iota(jnp.int32, (1, 128), 1) & (HEAD_DIM - 1)) >= ROT_DIM).astype(F32)
    c1, s1 = tab[:, :128] + unrot, tab[:, 128:]
    c2 = jnp.concatenate([c1, c1], axis=1)
    s2 = jnp.concatenate([s1, s1], axis=1)
    bd2 = bd_ref[...]
    bd1 = bd2[:128, :128]
    qa = _rope(_head_norm(proj('aq', 'ak'), gqa_ref[...], bd2), c2, s2) * 0.125
    qa0_ref[...] = qa[:, :128]
    qa1_ref[...] = qa[:, 128:]
    kva = proj('ak', 'cq')
    ka_ref[...] = _rope(_head_norm(kva[:, :128], gka_ref[...], bd1), c1, s1)
    va_ref[...] = kva[:, 128:]
    qc = _rope(_head_norm(proj('cq', 'ck'), gqc_ref[...], bd2), c2, s2) * 0.125
    qc0_ref[...] = qc[:, :128]
    qc1_ref[...] = qc[:, 128:]
    kvc = proj('ck', 'z')
    kc_ref[...] = _rope(_head_norm(kvc[:, :128], gkc_ref[...], bd1), c1, s1)
    vc_ref[...] = kvc[:, 128:]


def _in_proj(h, lw, tab, tm):
    rows = h.shape[0]
    grid = (rows // tm,)
    row_spec = lambda w: pl.BlockSpec((tm, w), lambda i: (i, 0))
    widths = (128, 128, KV_WIDTH, KV_WIDTH, 128, 128, KV_WIDTH, KV_WIDTH, B_INNER, CONV_DIM, 128)
    dtypes = (F32,) * len(widths)
    return pl.pallas_call(
        _in_proj_body,
        grid=grid,
        in_specs=[row_spec(D_MODEL), _const_spec((1, D_MODEL)), _const_spec((D_MODEL, N_IN_PAD)),
                  pl.BlockSpec((3 * ROT_DIM, tm), lambda i: (0, i)), _const_spec((128, 256)),
                  _const_spec((1, 256)), _const_spec((1, 128)), _const_spec((1, 256)),
                  _const_spec((1, 128)), _const_spec((256, 256))],
        out_specs=[row_spec(w) for w in widths],
        out_shape=[jax.ShapeDtypeStruct((rows, w), dt) for w, dt in zip(widths, dtypes)],
        compiler_params=_params(("parallel",)),
        name="in_proj",
    )(h, lw['norm1'], lw['w_in'], tab, jnp.asarray(_ROPE_SEL, BF16), lw['a_qn'], lw['a_kn'], lw['c_qn'], lw['c_kn'], lw['bd'])


def _attn_body(*refs, patterns, has_sink, sb):
    if has_sink:
        sink_ref, refs = refs[0], refs[1:]
    q0_ref, q1_ref, kp_ref, kc_ref, vp_ref, vc_ref, o_ref, kk, vv, acc_s, m_s, l_s = refs
    q_refs = (q0_ref, q1_ref)
    j = pl.program_id(0)
    kk[0:sb, :] = kp_ref[...]
    kk[sb:2 * sb, :] = kc_ref[...]
    vv[0:sb, :] = vp_ref[...]
    vv[sb:2 * sb, :] = vc_ref[...]
    row4 = lax.broadcasted_iota(jnp.int32, (4 * BLK, BLK), 0) & (BLK - 1)
    col4 = lax.broadcasted_iota(jnp.int32, (4 * BLK, BLK), 1)
    upper4 = col4 > row4
    g_lo = lax.broadcasted_iota(jnp.int32, (BLK, 128), 1) < HEAD_DIM
    ones_cols = jnp.ones((2 * BLK, 128), BF16)
    nblk = sb // BLK

    diag_here = [md == BLK for _, md in patterns]
    boosts = [[] for _ in patterns]
    for p, (d, md) in enumerate(patterns):
        for q in range(p + 1, len(patterns)):
            dq, mdq = patterns[q]
            if diag_here[p] and (md * d) % dq == 0 and (md * d) // dq < min(BLK, mdq + 1):
                boosts[q].append((md * d) // dq)
                diag_here[p] = False

    for pi, (d, max_dist) in enumerate(patterns):
        nsub = nblk // d
        has_diag = diag_here[pi]
        fdist = jnp.where(upper4, row4 + BLK - col4, row4 - col4)
        mult = jnp.ones((4 * BLK, BLK), F32)
        for f in boosts[pi]:
            mult = mult + (fdist == f).astype(F32)
        w_up = jnp.where(upper4, mult, 0.0)
        w_lo = jnp.where(upper4, 0.0, mult)

        def ld(ref, s0, d=d):
            if d == 1:
                return ref[pl.ds(s0, BLK), :]
            return ref[pl.ds(s0, BLK, stride=d), :]

        def block(t, carry, d=d, nsub=nsub, has_diag=has_diag, first=(pi == 0), ld=ld, w_up=w_up, w_lo=w_lo):
            r_ = t // nsub
            n = t - r_ * nsub
            start = r_ + BLK * d * n
            prev_ok = jnp.logical_or(j > 0, n > 0)
            neg = jnp.where(prev_ok, 0.0, -jnp.inf)
            qb = [ld(q_refs[rr], start) for rr in range(2)]
            kprev, kcur = ld(kk, sb + start - BLK * d), ld(kk, sb + start)
            vprev, vcur = ld(vv, sb + start - BLK * d), ld(vv, sb + start)
            qm = jnp.concatenate([jnp.where(g_lo, qb[0], 0.0), jnp.where(g_lo, 0.0, qb[0]),
                                  jnp.where(g_lo, qb[1], 0.0), jnp.where(g_lo, 0.0, qb[1])], axis=0).astype(BF16)
            kcat = jnp.concatenate([kprev, kcur], axis=0).astype(BF16)
            vcat = jnp.concatenate([jnp.concatenate([vprev, vcur], axis=0).astype(BF16), ones_cols], axis=1)
            s2 = _dot_nt(qm, kcat)
            sp, sc = s2[:, :BLK] + neg, s2[:, BLK:]
            s = jnp.where(upper4, sp, sc)
            mb = jnp.max(s, axis=-1, keepdims=True)
            if has_diag:
                sd = jnp.sum(jnp.where(col4 == row4, sp, 0.0), axis=-1, keepdims=True)
                mb = jnp.maximum(mb, sd)
            e = jnp.exp(s - mb)
            ecat = jnp.concatenate([e * w_up, e * w_lo], axis=1).astype(BF16)
            pvl = jnp.dot(ecat, vcat, preferred_element_type=F32)
            pv, lb = pvl[:, :128], pvl[:, 128:]
            if has_diag:
                ed = jnp.exp(sd - mb)
                lb = lb + ed
                pv = pv + ed * jnp.concatenate([vprev] * 4, axis=0)
            for rr in range(2):
                lo, hi = slice(2 * BLK * rr, 2 * BLK * rr + BLK), slice(2 * BLK * rr + BLK, 2 * BLK * (rr + 1))
                o_b = jnp.where(g_lo, pv[lo], pv[hi])
                m_b = jnp.where(g_lo, mb[lo], mb[hi])
                l_b = jnp.where(g_lo, lb[lo], lb[hi])
                rows = pl.ds(start, BLK) if d == 1 else pl.ds(start, BLK, stride=d)
                if first:
                    m_s[rr, rows, :] = m_b
                    l_s[rr, rows, :] = l_b
                    acc_s[rr, rows, :] = o_b
                else:
                    m_old = m_s[rr, rows, :]
                    m_new = jnp.maximum(m_old, m_b)
                    w_old = jnp.exp(m_old - m_new)
                    w_b = jnp.exp(m_b - m_new)
                    m_s[rr, rows, :] = m_new
                    l_s[rr, rows, :] = w_old * l_s[rr, rows, :] + w_b * l_b
                    acc_s[rr, rows, :] = w_old * acc_s[rr, rows, :] + w_b * o_b
            return carry

        lax.fori_loop(0, nblk, block, 0, unroll=2)

    chunk = 256
    for rr in range(2):
        for c in range(sb // chunk):
            rows = slice(chunk * c, chunk * (c + 1))
            m_f, l_f, a_f = m_s[rr, rows, :], l_s[rr, rows, :], acc_s[rr, rows, :]
            if has_sink:
                sk = sink_ref[rr:rr + 1, :]
                m2 = jnp.maximum(m_f, sk)
                w = jnp.exp(m_f - m2)
                o = a_f * w / (l_f * w + jnp.exp(sk - m2))
            else:
                o = a_f / l_f
            o_ref[rows, 128 * rr:128 * (rr + 1)] = o.astype(o_ref.dtype)


def _prompt_attn(q0, q1, k, v, patterns, sink_lanes=None, name="attn"):
    L = q0.shape[0]
    sb = ATTN_SUPER
    cur = lambda j: (j, 0)
    prev = lambda j: (jnp.maximum(j - 1, 0), 0)
    blk = lambda im: pl.BlockSpec((sb, 128), im)
    in_specs = [blk(cur), blk(cur), blk(prev), blk(cur), blk(prev), blk(cur)]
    args = [q0, q1, k, k, v, v]
    if sink_lanes is not None:
        in_specs = [_const_spec((2, 128))] + in_specs
        args = [sink_lanes] + args
    return pl.pallas_call(
        functools.partial(_attn_body, patterns=patterns, has_sink=sink_lanes is not None, sb=sb),
        grid=(L // sb,), in_specs=in_specs, out_specs=pl.BlockSpec((sb, Q_WIDTH), cur),
        out_shape=jax.ShapeDtypeStruct((L, Q_WIDTH), BF16),
        scratch_shapes=[pltpu.VMEM((2 * sb, 128), F32), pltpu.VMEM((2 * sb, 128), F32),
                        pltpu.VMEM((2, sb, 128), F32), pltpu.VMEM((2, sb, 128), F32), pltpu.VMEM((2, sb, 128), F32)],
        compiler_params=_params(("parallel",)),
        name=name,
    )(*args)


def _sample_attn_body(*refs, nb, has_sink, n_alias, w):
    if has_sink:
        sink_ref, refs = refs[0], refs[1:]
    q0_ref, q1_ref, kn_ref, vn_ref, kc_ref, vc_ref, mc_ref, mn_ref = refs[:8]
    o_ref, ko_ref, vo_ref = refs[8 + n_alias:]
    g_lo = lax.broadcasted_iota(jnp.int32, (TILE, 128), 1) < HEAD_DIM
    lane = lax.broadcasted_iota(jnp.int32, (128, 128), 1)
    is_new = lane >= 128 - (TILE - TOK0)
    mult_c = mc_ref[...]
    mult_n = mn_ref[...]
    pad = jnp.zeros((128 - TILE, 128), F32)
    for b in range(nb):
        parts = []
        for q_ref in (q0_ref, q1_ref):
            qr = q_ref[TILE * b:TILE * (b + 1), :]
            parts += [jnp.where(g_lo, qr, 0.0), jnp.where(g_lo, 0.0, qr)]
        qm = jnp.concatenate(parts, axis=0).astype(BF16)
        kct = kc_ref[b]
        vct = vc_ref[b]
        knp = jnp.concatenate([kn_ref[TILE * b:TILE * (b + 1), :], pad], axis=0)
        vnp = jnp.concatenate([vn_ref[TILE * b:TILE * (b + 1), :], pad], axis=0)
        knt = knp.T
        vnt = vnp.T
        sc = jnp.where(mult_c > 0, jnp.dot(qm, kct.astype(BF16), preferred_element_type=F32), -jnp.inf)
        sn = jnp.where(mult_n > 0, jnp.dot(qm, knt.astype(BF16), preferred_element_type=F32), -jnp.inf)
        m = jnp.maximum(jnp.max(sc, axis=-1, keepdims=True), jnp.max(sn, axis=-1, keepdims=True))
        if has_sink:
            sk = sink_ref[:, :1]
            m = jnp.maximum(m, sk)
        ec = mult_c * jnp.exp(sc - m)
        en = mult_n * jnp.exp(sn - m)
        den = jnp.sum(ec, axis=-1, keepdims=True) + jnp.sum(en, axis=-1, keepdims=True)
        if has_sink:
            den = den + jnp.exp(sk - m)
        o = (_dot_nt(ec.astype(BF16), vct.astype(BF16))
             + jnp.dot(en.astype(BF16), vnp.astype(BF16), preferred_element_type=F32)) / den
        o_ref[TILE * b:TILE * (b + 1), :] = jnp.concatenate(
            [jnp.where(g_lo, o[0:TILE], o[TILE:2 * TILE]),
             jnp.where(g_lo, o[2 * TILE:3 * TILE], o[3 * TILE:4 * TILE])], axis=1).astype(o_ref.dtype)
        for src, new_t, dst in ((kct, knt, ko_ref), (vct, vnt, vo_ref)):
            shifted = pltpu.roll(src, w - (TILE - TOK0), 1)
            new_cols = pltpu.roll(new_t, 128 - TILE, 1)
            if w > 128:
                dst[b, :, 0:w - 128] = shifted[:, 0:w - 128]
            dst[b, :, w - 128:w] = jnp.where(is_new, new_cols, shifted[:, w - 128:w])


def _sample_attn(q0, q1, kn, vn, cache_k, cache_v, layer, mult_c, mult_n, sinks_rows, nb, prev_out):
    rows = q0.shape[0]
    nbatch = rows // TILE
    depth, _, _, w = cache_k.shape
    row_spec = lambda width: pl.BlockSpec((TILE * nb, width), lambda i: (i, 0))
    cache_spec = pl.BlockSpec((None, nb, KV_WIDTH, w), lambda i: (layer, i, 0, 0))
    in_specs = [row_spec(128), row_spec(128), row_spec(KV_WIDTH), row_spec(KV_WIDTH), cache_spec, cache_spec,
                _const_spec((4 * TILE, w)), _const_spec((4 * TILE, 128))]
    args = [q0, q1, kn, vn, cache_k, cache_v, mult_c, mult_n]
    if sinks_rows is not None:
        in_specs = [_const_spec((4 * TILE, 128))] + in_specs
        args = [sinks_rows] + args
    aliases = {}
    if prev_out is not None:
        aliases = {len(args): 1, len(args) + 1: 2}
        in_specs = in_specs + [pl.BlockSpec(memory_space=pl.ANY)] * 2
        args = args + list(prev_out)
    cache_shape = jax.ShapeDtypeStruct(cache_k.shape, F32)
    return pl.pallas_call(
        functools.partial(_sample_attn_body, nb=nb, has_sink=sinks_rows is not None,
                          n_alias=0 if prev_out is None else 2, w=w),
        grid=(nbatch // nb,), in_specs=in_specs, out_specs=[row_spec(Q_WIDTH), cache_spec, cache_spec],
        out_shape=[jax.ShapeDtypeStruct((rows, Q_WIDTH), BF16), cache_shape, cache_shape],
        input_output_aliases=aliases,
        compiler_params=_params(("parallel",)),
        name=f"sample_attn_w{w}",
    )(*args)


def _sample_mult_tables():
    t = np.arange(TILE) - TOK0
    tq = np.maximum(t, 0)[:, None]
    j = np.arange(A_WINDOW)[None, :]
    da = A_WINDOW + tq - j
    ma_c = ((da >= 0) & (da < A_WINDOW)).astype(np.float32)
    tn = (np.arange(128) - TOK0)[None, :]
    dn = tq - tn
    new_ok = (tn >= 0) & (tn < TILE - TOK0) & (dn >= 0)
    ma_n = (new_ok & (dn < A_WINDOW)).astype(np.float32)

    def mult(d):
        out = np.zeros(d.shape, np.float32)
        for w, dil in C_PATTERNS:
            out += ((d >= 0) & (d <= w) & (d % dil == 0)).astype(np.float32)
        return out

    jc = np.arange(C_SPAN)[None, :]
    mc_c = mult(C_SPAN + tq - jc)
    mc_n = np.where(new_ok, mult(dn), 0.0).astype(np.float32)
    tile4 = lambda a: jnp.asarray(np.tile(a, (4, 1)))
    return tile4(ma_c), tile4(ma_n), tile4(mc_c), tile4(mc_n)


def _ssd_intra(act, dt_raw, dtb, alog, tri_sel, tri_mask, expand, row_ok):
    xs = act[:, :B_INNER]
    bb = act[:, B_INNER:B_INNER + 2 * B_STATE].astype(BF16)
    cb_ = act[:, B_INNER + 2 * B_STATE:].astype(BF16)
    dtv = _softplus(dt_raw + dtb)
    if row_ok is not None:
        dtv = jnp.where(row_ok, dtv, 0.0)
    a = -jnp.exp(alog) * dtv
    acum = _sel_dot_rhs(tri_sel, a)
    acum_t = acum.T
    dt_e = _sel_dot_lhs(dtv, expand)
    xdt = xs * dt_e
    lane_lo = lax.broadcasted_iota(jnp.int32, (BLK, 128), 1) < HEAD_DIM
    ys = []
    for g in range(2):
        cbm = _dot_nt(cb_[:, 128 * g:128 * (g + 1)], bb[:, 128 * g:128 * (g + 1)])
        for pair in range(2):
            xp = xdt[:, 128 * (2 * g + pair):128 * (2 * g + pair + 1)]
            acc = None
            for j in range(2):
                h = 4 * g + 2 * pair + j
                seg = acum[:, h:h + 1] - acum_t[h:h + 1, :]
                lm = jnp.exp(jnp.where(tri_mask, seg, -jnp.inf))
                mh = (cbm * lm).astype(BF16)
                xh = jnp.where(lane_lo if j == 0 else ~lane_lo, xp, 0.0).astype(BF16)
                t = jnp.dot(mh, xh, preferred_element_type=F32)
                acc = t if acc is None else acc + t
            ys.append(acc)
    ydiag = jnp.concatenate(ys, axis=1)
    return xs, bb, cb_, xdt, acum, acum_t, ydiag


def _ssd_finish(y, z, norm_w):
    y = y * _silu(z)
    ms = jnp.mean(y * y, axis=-1, keepdims=True)
    return (y * lax.rsqrt(ms + NORM_EPS) * norm_w).astype(BF16)


def _conv_act(xp_ref, cw_ref, cb_ref, r0=0):
    t = BLK
    base = r0 + TILE - (CONV_K - 1)
    out = cb_ref[...] + xp_ref[base:base + t, :] * cw_ref[0:1, :]
    for j in range(1, CONV_K):
        out = out + xp_ref[base + j:base + j + t, :] * cw_ref[j:j + 1, :]
    return _silu(out)


def _split2_dot(v, sel):
    hi = v.astype(BF16)
    lo = (v - hi.astype(F32)).astype(BF16)
    return jnp.dot(hi, sel, preferred_element_type=F32) + jnp.dot(lo, sel, preferred_element_type=F32)


def _ssd_prompt_chunk(act, z, dt_raw, dtb, alog, dskip, nw, expand, expand_t, hs_ref):
    t = BLK
    row = lax.broadcasted_iota(jnp.int32, (t, t), 0)
    col = lax.broadcasted_iota(jnp.int32, (t, t), 1)
    tri_mask = col <= row
    xs = act[:, :B_INNER]
    bb = act[:, B_INNER:B_INNER + 2 * B_STATE].astype(BF16)
    cb_ = act[:, B_INNER + 2 * B_STATE:].astype(BF16)
    dtv = _softplus(dt_raw + dtb)
    a = -jnp.exp(alog) * dtv
    acum = _sel_dot_rhs(tri_mask.astype(BF16), a)
    acum_t = acum.T
    spread = _split2_dot(jnp.concatenate([dtv, jnp.exp(acum), jnp.exp(acum[t - 1:t, :] - acum)], axis=0), expand)
    dt_e, e_all, dec_e = spread[0:t], spread[t:2 * t], spread[2 * t:3 * t]
    xdt = xs * dt_e
    lane_lo = lax.broadcasted_iota(jnp.int32, (BLK, 128), 1) < HEAD_DIM
    ys = []
    for g in range(2):
        cbm = _dot_nt(cb_[:, 128 * g:128 * (g + 1)], bb[:, 128 * g:128 * (g + 1)])
        for pair in range(2):
            xp = xdt[:, 128 * (2 * g + pair):128 * (2 * g + pair + 1)]
            acc = None
            for j in range(2):
                h = 4 * g + 2 * pair + j
                seg = acum[:, h:h + 1] - acum_t[h:h + 1, :]
                lm = jnp.exp(jnp.where(tri_mask, seg, -jnp.inf))
                mh = (cbm * lm).astype(BF16)
                xh = jnp.where(lane_lo if j == 0 else ~lane_lo, xp, 0.0).astype(BF16)
                part = jnp.dot(mh, xh, preferred_element_type=F32)
                acc = part if acc is None else acc + part
            ys.append(acc)
    ydiag = jnp.concatenate(ys, axis=1)

    hst = hs_ref[...]
    hb = hst.astype(BF16)
    yoff = jnp.concatenate([_dot_nt(cb_[:, 128 * g:128 * (g + 1)], hb[256 * g:256 * (g + 1), :]) for g in range(2)],
                           axis=1)
    y = ydiag + yoff * e_all + xs * dskip
    out = _ssd_finish(y, z, nw)

    xw = xdt * dec_e
    last_t = jnp.exp(jnp.broadcast_to(acum_t[:, t - 1:t], (128, 128)))
    hi = last_t.astype(BF16)
    lo = (last_t - hi.astype(F32)).astype(BF16)
    cd = (jnp.dot(expand_t, hi, preferred_element_type=F32)
          + jnp.dot(expand_t, lo, preferred_element_type=F32))
    for g in range(2):
        xw_t = xw[:, 256 * g:256 * (g + 1)].T.astype(BF16)
        dh = jnp.dot(xw_t, bb[:, 128 * g:128 * (g + 1)], preferred_element_type=F32)
        hs_ref[256 * g:256 * (g + 1), :] = hst[256 * g:256 * (g + 1), :] * cd[256 * g:256 * (g + 1), :] + dh
    return out


def _ssd_prompt_body(xbc_ref, z_ref, dt_ref, cw_ref, cb_ref, dtb_ref, alog_ref, dskip_ref, nw_ref,
                     expand_ref, expand_t_ref, mo_ref, hs_ref, xp_ref):
    c = pl.program_id(0)
    rows = xbc_ref.shape[0]

    @pl.when(c == 0)
    def _():
        xp_ref[0:TILE, :] = jnp.zeros((TILE, CONV_DIM), F32)
        hs_ref[...] = jnp.zeros_like(hs_ref)

    xp_ref[TILE:, :] = xbc_ref[...]
    for ci in range(rows // BLK):
        r0 = BLK * ci
        act = _conv_act(xp_ref, cw_ref, cb_ref, r0)
        mo_ref[r0:r0 + BLK, :] = _ssd_prompt_chunk(
            act, z_ref[r0:r0 + BLK, :], dt_ref[r0:r0 + BLK, :], dtb_ref[...], alog_ref[...], dskip_ref[...],
            nw_ref[...], expand_ref[...], expand_t_ref[...], hs_ref)
    xp_ref[0:TILE, :] = xp_ref[rows:rows + TILE, :]


_N_SSD_IN = 11
_N_SATTN_IN = 8


def _ssd_with_sample_attn_body(*refs, nb, n_alias, w):
    ssd_in = refs[:_N_SSD_IN]
    sattn_in = refs[_N_SSD_IN:_N_SSD_IN + _N_SATTN_IN + n_alias]
    mo_ref, hs_ref, o_ref, ko_ref, vo_ref, xp_ref = refs[_N_SSD_IN + _N_SATTN_IN + n_alias:]
    _ssd_prompt_body(*ssd_in, mo_ref, hs_ref, xp_ref)
    _sample_attn_body(*sattn_in, o_ref, ko_ref, vo_ref, nb=nb, has_sink=False, n_alias=n_alias, w=w)


def _ssd_prompt(xbc, z, dt, lw, sample_c):
    L = xbc.shape[0]
    rows = SSD_ROWS
    steps = L // rows
    q0, q1, kn, vn, cache_k, cache_v, layer, mult_c, mult_n, prev_out = sample_c
    srows = q0.shape[0]
    nb = srows // TILE // steps
    assert nb * steps * TILE == srows and (nb * TILE) % 16 == 0
    w = cache_k.shape[3]
    row_spec = lambda wd: pl.BlockSpec((rows, wd), lambda c: (c, 0))
    srow_spec = lambda wd: pl.BlockSpec((TILE * nb, wd), lambda c: (c, 0))
    cache_spec = pl.BlockSpec((None, nb, KV_WIDTH, w), lambda c: (layer, c, 0, 0))
    in_specs = [row_spec(CONV_DIM), row_spec(B_INNER), row_spec(128),
                _const_spec((CONV_K, CONV_DIM)), _const_spec((1, CONV_DIM)), _const_spec((1, 128)),
                _const_spec((1, 128)), _const_spec((1, B_INNER)), _const_spec((1, B_INNER)),
                _const_spec((128, B_INNER)), _const_spec((B_INNER, 128)),
                srow_spec(128), srow_spec(128), srow_spec(KV_WIDTH), srow_spec(KV_WIDTH), cache_spec, cache_spec,
                _const_spec((4 * TILE, w)), _const_spec((4 * TILE, 128))]
    args = [xbc, z, dt, lw['conv_w'], lw['conv_b'], lw['dt_bias'], lw['a_log'], lw['d_skip'], lw['ssm_norm'],
            lw['expand'], lw['expand_t'], q0, q1, kn, vn, cache_k, cache_v, mult_c, mult_n]
    assert len(args) == _N_SSD_IN + _N_SATTN_IN
    aliases = {}
    if prev_out is not None:
        aliases = {len(args): 3, len(args) + 1: 4}
        in_specs = in_specs + [pl.BlockSpec(memory_space=pl.ANY)] * 2
        args = args + list(prev_out)
    cache_shape = jax.ShapeDtypeStruct(cache_k.shape, F32)
    return pl.pallas_call(
        functools.partial(_ssd_with_sample_attn_body, nb=nb, n_alias=len(aliases), w=w),
        grid=(steps,),
        in_specs=in_specs,
        out_specs=[row_spec(B_INNER), pl.BlockSpec((B_INNER, B_STATE), lambda c: (0, 0)),
                   srow_spec(Q_WIDTH), cache_spec, cache_spec],
        out_shape=[jax.ShapeDtypeStruct((L, B_INNER), BF16), jax.ShapeDtypeStruct((B_INNER, B_STATE), F32),
                   jax.ShapeDtypeStruct((srows, Q_WIDTH), BF16), cache_shape, cache_shape],
        scratch_shapes=[pltpu.VMEM((rows + TILE, CONV_DIM), F32)],
        input_output_aliases=aliases,
        compiler_params=_params(("arbitrary",)),
        name="ssd_prompt",
    )(*args)


def _ssd_sample_body(*refs):
    (xbc_ref, pre_ref, z_ref, dt_ref, h0_ref, cw_ref, cb_ref, dtb_ref, alog_ref, dskip_ref,
     nw_ref, expand_ref, expand_t_ref) = refs[:13]
    mo_ref, hout_ref, xp_ref = refs[-3:]
    t = BLK
    nbt = t // TILE
    rmod = lax.broadcasted_iota(jnp.int32, (t, 1), 0) & (TILE - 1)
    row_ok = rmod >= TOK0
    xp_ref[0:TILE, :] = jnp.zeros((TILE, CONV_DIM), F32)
    xp_ref[TILE:, :] = jnp.where(row_ok, xbc_ref[...], pre_ref[...])
    act = _conv_act(xp_ref, cw_ref, cb_ref)

    row = lax.broadcasted_iota(jnp.int32, (t, t), 0)
    col = lax.broadcasted_iota(jnp.int32, (t, t), 1)
    same = (row // TILE) == (col // TILE)
    tri_mask = (col <= row) & same
    tri_sel = tri_mask.astype(BF16)
    last_sel = (col == (row // TILE) * TILE + (TILE - 1)).astype(BF16)
    expand = expand_ref[...]
    xs, bb, cb_, xdt, acum, acum_t, ydiag = _ssd_intra(
        act, dt_ref[...], dtb_ref[...], alog_ref[...], tri_sel, tri_mask, expand, row_ok)

    yoffs = []
    for b in range(nbt):
        hb = h0_ref[b].astype(BF16)
        yoffs.append(jnp.concatenate(
            [_dot_nt(cb_[TILE * b:TILE * (b + 1), 128 * g:128 * (g + 1)], hb[256 * g:256 * (g + 1), :])
             for g in range(2)], axis=1))
    yoff = jnp.concatenate(yoffs, axis=0)
    e_all = _sel_dot_lhs(jnp.exp(acum), expand)
    y = ydiag + yoff * e_all + xs * dskip_ref[...]
    mo_ref[...] = _ssd_finish(y, z_ref[...], nw_ref[...])

    last = _sel_dot_rhs(last_sel, acum)
    dec_e = _sel_dot_lhs(jnp.exp(last - acum), expand)
    xw = xdt * dec_e
    dcol = _sel_dot_rhs(expand_t_ref[...], jnp.exp(acum_t))
    lane = lax.broadcasted_iota(jnp.int32, (256, t), 1)
    xw_ts = [xw[:, 256 * g:256 * (g + 1)].T for g in range(2)]
    for b in range(nbt):
        in_b = (lane // TILE) == b
        cd = dcol[:, TILE * b + TILE - 1:TILE * b + TILE]
        h0 = h0_ref[b]
        for g in range(2):
            lhs = jnp.where(in_b, xw_ts[g], 0.0).astype(BF16)
            dh = jnp.dot(lhs, bb[:, 128 * g:128 * (g + 1)], preferred_element_type=F32)
            hout_ref[b, 256 * g:256 * (g + 1), :] = (
                h0[256 * g:256 * (g + 1), :] * cd[256 * g:256 * (g + 1), :] + dh)


def _ssd_sample(xbc, prefix, z, dt, state, layer, lw, prev_out):
    rows = xbc.shape[0]
    nbt = BLK // TILE
    row_spec = lambda w: pl.BlockSpec((BLK, w), lambda i: (i, 0))
    st_spec = pl.BlockSpec((None, nbt, B_INNER, B_STATE), lambda i: (layer, i, 0, 0))
    in_specs = [row_spec(CONV_DIM), row_spec(CONV_DIM), row_spec(B_INNER), row_spec(128), st_spec,
                _const_spec((CONV_K, CONV_DIM)), _const_spec((1, CONV_DIM)), _const_spec((1, 128)),
                _const_spec((1, 128)), _const_spec((1, B_INNER)), _const_spec((1, B_INNER)),
                _const_spec((128, B_INNER)), _const_spec((B_INNER, 128))]
    args = [xbc, prefix, z, dt, state, lw['conv_w'], lw['conv_b'], lw['dt_bias'], lw['a_log'], lw['d_skip'],
            lw['ssm_norm'], lw['expand'], lw['expand_t']]
    aliases = {}
    if prev_out is not None:
        aliases = {len(args): 1}
        in_specs.append(pl.BlockSpec(memory_space=pl.ANY))
        args.append(prev_out)
    return pl.pallas_call(
        _ssd_sample_body,
        grid=(rows // BLK,),
        in_specs=in_specs,
        out_specs=[row_spec(B_INNER), st_spec],
        out_shape=[jax.ShapeDtypeStruct((rows, B_INNER), BF16), jax.ShapeDtypeStruct(state.shape, F32)],
        scratch_shapes=[pltpu.VMEM((BLK + TILE, CONV_DIM), F32)],
        input_output_aliases=aliases,
        compiler_params=_params(("parallel",)),
        name="ssd_sample",
    )(*args)


def _out_ffn_body(h_ref, a_ref, m_ref, c_ref, wo_ref, g2_ref, wg_ref, wu_ref, wd_ref, out_ref):
    d = functools.partial(jnp.dot, preferred_element_type=F32)
    h1 = (h_ref[...] + d(a_ref[...], wo_ref[0:256, :]) + d(m_ref[...], wo_ref[256:768, :])
          + d(c_ref[...], wo_ref[768:1024, :]))
    ms = jnp.mean(h1 * h1, axis=-1, keepdims=True)
    u = (h1 * lax.rsqrt(ms + NORM_EPS) * g2_ref[...]).astype(BF16)
    out_ref[...] = h1
    for c in range(D_FF // FF_CHUNK):
        sl = slice(FF_CHUNK * c, FF_CHUNK * (c + 1))
        act = (_silu(d(u, wg_ref[:, sl])) * d(u, wu_ref[:, sl])).astype(BF16)
        out_ref[...] += d(act, wd_ref[sl, :])


def _out_ffn(h, a_o, m_o, c_o, lw, tm):
    rows = h.shape[0]
    row_spec = lambda w: pl.BlockSpec((tm, w), lambda i: (i, 0))
    in_specs = [row_spec(D_MODEL), row_spec(Q_WIDTH), row_spec(B_INNER), row_spec(Q_WIDTH),
                _const_spec((D_MODEL, D_MODEL)), _const_spec((1, D_MODEL)), _const_spec((D_MODEL, D_FF)),
                _const_spec((D_MODEL, D_FF)), _const_spec((D_FF, D_MODEL))]
    return pl.pallas_call(
        _out_ffn_body,
        grid=(rows // tm,), in_specs=in_specs, out_specs=row_spec(D_MODEL),
        out_shape=jax.ShapeDtypeStruct((rows, D_MODEL), F32),
        compiler_params=_params(("parallel",)),
        name="out_ffn",
    )(h, a_o, m_o, c_o, lw['w_out'], lw['norm2'], lw['w_gate'], lw['w_up'], lw['w_down'])


def _rope_table(pos):
    half = ROT_DIM // 2
    inv = ROPE_THETA ** (-(jnp.arange(half, dtype=F32) * 2.0 / ROT_DIM))
    ang = pos.astype(F32)[None, :] * inv[:, None]
    cs = jnp.concatenate([jnp.cos(ang), jnp.sin(ang)], axis=0)
    hi = cs.astype(BF16).astype(F32)
    mid = (cs - hi).astype(BF16).astype(F32)
    lo = (cs - hi - mid).astype(BF16).astype(F32)
    return jnp.concatenate([hi, mid, lo], axis=0)


def _rope_selection():
    half = ROT_DIM // 2
    sel = np.zeros((128, 256), np.float32)
    for term in range(3):
        base = term * ROT_DIM
        for lane in range(128):
            f = lane % HEAD_DIM
            if f < half:
                sel[base + f, lane] = 1.0
                sel[base + half + f, 128 + lane] = -1.0
            elif f < ROT_DIM:
                sel[base + f - half, lane] = 1.0
                sel[base + f, 128 + lane] = 1.0
    return sel


_ROPE_SEL = _rope_selection()


def _permute_heads(m, start, axis):
    sl = lambda a, b: lax.slice_in_dim(m, a, b, axis=axis)
    pieces = [sl(0, start)] if start else []
    pieces += [sl(start, start + 64), sl(start + 128, start + 192), sl(start + 64, start + 128),
               sl(start + 192, start + 256), sl(start + 256, m.shape[axis])]
    return jnp.concatenate(pieces, axis=axis)


def _layer_weights(l, norm1, w_in, a_qn, a_kn, a_sinks, c_qn, c_kn, conv_w, conv_b, dt_bias, a_log, d_skip,
                   ssm_norm, w_out, norm2, w_gate, w_up, w_down):
    w = _permute_heads(_permute_heads(w_in[l].astype(BF16), _OFF['aq'], 1), _OFF['cq'], 1)
    w = jnp.pad(w, ((0, 0), (0, N_IN_PAD - N_IN)))
    wo = _permute_heads(_permute_heads(w_out[l].astype(BF16), 0, 0), 768, 0)
    pad8 = lambda v: jnp.pad(v.astype(F32), (0, 128 - B_HEADS))[None, :]
    head_of = np.arange(B_INNER) // 64
    expand = (np.arange(128)[:, None] == head_of[None, :]).astype(np.float32)
    blk = (np.arange(256)[:, None] // 64 == np.arange(256)[None, :] // 64).astype(np.float32) / 64.0
    return dict(
        norm1=norm1[l][None, :], w_in=w,
        a_qn=jnp.tile(a_qn[l], 4)[None, :], a_kn=jnp.tile(a_kn[l], 2)[None, :],
        c_qn=jnp.tile(c_qn[l], 4)[None, :], c_kn=jnp.tile(c_kn[l], 2)[None, :],
        bd=jnp.asarray(blk, BF16),
        sink_lanes=jnp.repeat(a_sinks[l].astype(F32)[jnp.asarray([0, 2, 1, 3])], HEAD_DIM).reshape(2, 128),
        sink_rows=jnp.broadcast_to(
            jnp.repeat(a_sinks[l].astype(F32)[jnp.asarray([0, 2, 1, 3])], TILE)[:, None], (4 * TILE, 128)),
        conv_w=conv_w[l], conv_b=conv_b[l][None, :], dt_bias=pad8(dt_bias[l]), a_log=pad8(a_log[l]),
        d_skip=jnp.repeat(d_skip[l].astype(F32), 64)[None, :], ssm_norm=ssm_norm[l][None, :],
        expand=jnp.asarray(expand, BF16), expand_t=jnp.asarray(expand.T, BF16),
        w_out=wo, norm2=norm2[l][None, :],
        w_gate=w_gate[l].astype(BF16), w_up=w_up[l].astype(BF16), w_down=w_down[l].astype(BF16))


def kernel(x_prompt, x_sample, cache_a_k, cache_a_v, cache_c_k, cache_c_v, state_ssm, state_conv, norm1, w_in,
           a_qn, a_kn, a_sinks, c_qn, c_kn, conv_w, conv_b, dt_bias, a_log, d_skip, ssm_norm, w_out, norm2,
           w_gate, w_up, w_down):
    depth = w_in.shape[0]
    batch, seq, _ = x_prompt.shape
    nbatch, dec_seq, _ = x_sample.shape
    assert batch == 1 and dec_seq == TILE - TOK0 and seq % (16 * BLK) == 0 and nbatch % (BLK // TILE) == 0
    past_len = PAST_LEN
    a_buf, c_buf = cache_a_k.shape[2], cache_c_k.shape[2]
    assert a_buf == A_WINDOW and c_buf == C_SPAN

    hp = x_prompt.reshape(seq, D_MODEL)
    hs = jnp.pad(x_sample, ((0, 0), (TOK0, 0), (0, 0))).reshape(nbatch * TILE, D_MODEL)
    tab_p = _rope_table(jnp.arange(seq))
    pos_s = past_len + jnp.maximum(jnp.arange(TILE) - TOK0, 0)
    tab_s = jnp.tile(_rope_table(pos_s), (1, nbatch))
    ma_c, ma_n, mc_c, mc_n = _sample_mult_tables()

    to_fm = lambda c: jnp.transpose(c, (0, 1, 3, 4, 2)).reshape(depth, nbatch, KV_WIDTH, c.shape[2])
    from_fm = lambda c: jnp.transpose(c.reshape(depth, nbatch, 2, HEAD_DIM, c.shape[3]), (0, 1, 4, 2, 3))
    ca_k, ca_v, cc_k, cc_v = to_fm(cache_a_k), to_fm(cache_a_v), to_fm(cache_c_k), to_fm(cache_c_v)
    st = state_ssm.reshape(depth, nbatch, B_INNER, B_STATE)

    p_out = [[] for _ in range(6)]
    s_conv = []
    new_a = new_c = new_st = None
    tm_p = 512
    tm_s = min(512, nbatch * TILE)
    for l in range(depth):
        lw = _layer_weights(l, norm1, w_in, a_qn, a_kn, a_sinks, c_qn, c_kn, conv_w, conv_b, dt_bias, a_log,
                            d_skip, ssm_norm, w_out, norm2, w_gate, w_up, w_down)
        s_qa0, s_qa1, s_ka, s_va, s_qc0, s_qc1, s_kc, s_vc, s_z, s_xbc, s_dt = _in_proj(hs, lw, tab_s, tm_s)

        qa0, qa1, ka, va, qc0, qc1, kc, vc, z, xbc, dt = _in_proj(hp, lw, tab_p, tm_p)
        a_o = _prompt_attn(qa0, qa1, ka, va, A_PATTERNS, sink_lanes=lw['sink_lanes'], name="attn_a")
        c_o = _prompt_attn(qc0, qc1, kc, vc, C_BANDS, name="attn_c")
        m_o, h_fin, s_c_o, *new_c = _ssd_prompt(
            xbc, z, dt, lw, (s_qc0, s_qc1, s_kc, s_vc, cc_k, cc_v, l, mc_c, mc_n, new_c))
        hp = _out_ffn(hp, a_o, m_o, c_o, lw, FFN_ROWS)
        p_out[0].append(ka[seq - a_buf:].reshape(1, a_buf, 2, HEAD_DIM))
        p_out[1].append(va[seq - a_buf:].reshape(1, a_buf, 2, HEAD_DIM))
        p_out[2].append(kc[seq - c_buf:].reshape(1, c_buf, 2, HEAD_DIM))
        p_out[3].append(vc[seq - c_buf:].reshape(1, c_buf, 2, HEAD_DIM))
        p_out[4].append(h_fin.reshape(1, B_HEADS, 64, B_STATE))
        p_out[5].append(xbc[seq - (CONV_K - 1):].reshape(1, CONV_K - 1, CONV_DIM))

        a_o, *new_a = _sample_attn(s_qa0, s_qa1, s_ka, s_va, ca_k, ca_v, l, ma_c, ma_n, lw['sink_rows'], 8, new_a)
        prefix = jnp.pad(state_conv[l], ((0, 0), (1, TILE - CONV_K), (0, 0))).reshape(nbatch * TILE, CONV_DIM)
        m_o, new_st = _ssd_sample(s_xbc, prefix, s_z, s_dt, st, l, lw, new_st)
        hs = _out_ffn(hs, a_o, m_o, s_c_o, lw, tm_s)
        s_conv.append(s_xbc.reshape(nbatch, TILE, CONV_DIM)[:, TILE - (CONV_K - 1):])

    outs_p = [jnp.stack(t, axis=0) for t in p_out]
    outs_s = [from_fm(new_a[0]), from_fm(new_a[1]), from_fm(new_c[0]), from_fm(new_c[1]),
              new_st.reshape(depth, nbatch, B_HEADS, 64, B_STATE), jnp.stack(s_conv, axis=0)]
    y_p = hp.reshape(1, seq, D_MODEL)
    y_s = hs.reshape(nbatch, TILE, D_MODEL)[:, TOK0:]
    return (y_p, y_s, *outs_p, *outs_s)
```

```python
import functools
import math

import numpy as np
import jax
import jax.numpy as jnp
from jax import lax
from jax.experimental import pallas as pl
from jax.experimental.pallas import tpu as pltpu

F32 = jnp.float32
BF16 = jnp.bfloat16

D_MODEL = 1024
HEAD_DIM = 64
ROT_DIM = 16
ROPE_THETA = 500000.0
NORM_EPS = 1e-6
Q_WIDTH = 256
KV_WIDTH = 128
PAST_LEN = 16384
A_WINDOW = 128
C_PATTERNS = ((128, 1), (512, 4), (2048, 16))
C_SPAN = 2048
B_HEADS = 8
B_INNER = 512
B_STATE = 128
CONV_K = 4
CONV_DIM = 1024
D_FF = 2816
N_IN = 2568
N_IN_PAD = 2688
BLK = 128
ATTN_SUPER = 2048
A_PATTERNS = ((1, A_WINDOW - 1),)
C_BANDS = tuple((d, w // d) for w, d in C_PATTERNS)
TOK0 = 4
TILE = 8
FF_CHUNK = 256
SSD_ROWS = 256
FFN_ROWS = 1024
VMEM_LIMIT = 56 * 1024 * 1024

_OFF = dict(aq=0, ak=256, av=384, cq=512, ck=768, cv=896, z=1024, xbc=1536, dt=2560, end=N_IN_PAD)
_HEAD_PERM = np.concatenate([np.arange(0, 64), np.arange(128, 192), np.arange(64, 128), np.arange(192, 256)])


def _const_spec(shape):
    nd = len(shape)
    return pl.BlockSpec(shape, lambda *_: (0,) * nd, pipeline_mode=pl.Buffered(1))


def _layer_spec(shape, layer):
    nd = len(shape)
    return pl.BlockSpec((None,) + tuple(shape), lambda *_: (layer,) + (0,) * nd, pipeline_mode=pl.Buffered(1))


def _params(sem):
    return pltpu.CompilerParams(dimension_semantics=sem, vmem_limit_bytes=VMEM_LIMIT)


def _split3(v):
    hi = v.astype(BF16)
    r1 = v - hi.astype(F32)
    mid = r1.astype(BF16)
    lo = (r1 - mid.astype(F32)).astype(BF16)
    return hi, mid, lo


def _sel_dot_rhs(sel, v):
    hi, mid, lo = _split3(v)
    d = functools.partial(jnp.dot, preferred_element_type=F32)
    return d(sel, hi) + d(sel, mid) + d(sel, lo)


def _sel_dot_lhs(v, sel):
    hi, mid, lo = _split3(v)
    d = functools.partial(jnp.dot, preferred_element_type=F32)
    return d(hi, sel) + d(mid, sel) + d(lo, sel)


def _dot_nt(a, b):
    return lax.dot_general(a, b, (((1,), (1,)), ((), ())), preferred_element_type=F32)


def _silu(x):
    return x * jax.nn.sigmoid(x)


def _softplus(x):
    return jnp.maximum(x, 0.0) + jnp.log(1.0 + jnp.exp(-jnp.abs(x)))


def _head_norm(x, gain, bd):
    x2 = x * x
    hi = x2.astype(BF16)
    lo = (x2 - hi.astype(F32)).astype(BF16)
    ms = jnp.dot(hi, bd, preferred_element_type=F32) + jnp.dot(lo, bd, preferred_element_type=F32)
    return x * lax.rsqrt(ms + NORM_EPS) * gain


def _rope(x, cos_t, sin_t):
    w = x.shape[1]
    lane = lax.broadcasted_iota(jnp.int32, x.shape, 1) & (HEAD_DIM - 1)
    partner = jnp.where(lane < ROT_DIM // 2, pltpu.roll(x, w - ROT_DIM // 2, 1), pltpu.roll(x, ROT_DIM // 2, 1))
    return x * cos_t + partner * sin_t


def _in_proj_body(h_ref, g1_ref, w_ref, cs_ref, sel_ref, gqa_ref, gka_ref, gqc_ref, gkc_ref, bd_ref,
                  qa0_ref, qa1_ref, ka_ref, va_ref, qc0_ref, qc1_ref, kc_ref, vc_ref, z_ref, xbc_ref, dt_ref):
    x = h_ref[...]
    tm = x.shape[0]
    ms = jnp.mean(x * x, axis=-1, keepdims=True)
    u = (x * lax.rsqrt(ms + NORM_EPS) * g1_ref[...]).astype(BF16)

    def proj(name, nxt):
        return jnp.dot(u, w_ref[:, _OFF[name]:_OFF[nxt]], preferred_element_type=F32)

    xbc_ref[...] = proj('xbc', 'dt')
    z_ref[...] = proj('z', 'xbc')
    dt_ref[...] = proj('dt', 'end')
    cs_t = jnp.concatenate([cs_ref[...], jnp.zeros((128 - 3 * ROT_DIM, tm), F32)], axis=0).T.astype(BF16)
    tab = jnp.dot(cs_t, sel_ref[...], preferred_element_type=F32)
    unrot = ((lax.broadcasted_iota(jnp.int32, (1, 128), 1) & (HEAD_DIM - 1)) >= ROT_DIM).astype(F32)
    c1, s1 = tab[:, :128] + unrot, tab[:, 128:]
    c2 = jnp.concatenate([c1, c1], axis=1)
    s2 = jnp.concatenate([s1, s1], axis=1)
    bd2 = bd_ref[...]
    bd1 = bd2[:128, :128]
    qa = _rope(_head_norm(proj('aq', 'ak'), gqa_ref[...], bd2), c2, s2) * 0.125
    qa0_ref[...] = qa[:, :128]
    qa1_ref[...] = qa[:, 128:]
    kva = proj('ak', 'cq')
    ka_ref[...] = _rope(_head_norm(kva[:, :128], gka_ref[...], bd1), c1, s1)
    va_ref[...] = kva[:, 128:]
    qc = _rope(_head_norm(proj('cq', 'ck'), gqc_ref[...], bd2), c2, s2) * 0.125
    qc0_ref[...] = qc[:, :128]
    qc1_ref[...] = qc[:, 128:]
    kvc = proj('ck', 'z')
    kc_ref[...] = _rope(_head_norm(kvc[:, :128], gkc_ref[...], bd1), c1, s1)
    vc_ref[...] = kvc[:, 128:]


def _in_proj(h, lw, tab, tm):
    rows = h.shape[0]
    grid = (rows // tm,)
    row_spec = lambda w: pl.BlockSpec((tm, w), lambda i: (i, 0))
    widths = (128, 128, KV_WIDTH, KV_WIDTH, 128, 128, KV_WIDTH, KV_WIDTH, B_INNER, CONV_DIM, 128)
    dtypes = (F32,) * len(widths)
    return pl.pallas_call(
        _in_proj_body,
        grid=grid,
        in_specs=[row_spec(D_MODEL), _const_spec((1, D_MODEL)), _layer_spec((D_MODEL, N_IN_PAD), lw['layer']),
                  pl.BlockSpec((3 * ROT_DIM, tm), lambda i: (0, i)), _const_spec((128, 256)),
                  _const_spec((1, 256)), _const_spec((1, 128)), _const_spec((1, 256)),
                  _const_spec((1, 128)), _const_spec((256, 256))],
        out_specs=[row_spec(w) for w in widths],
        out_shape=[jax.ShapeDtypeStruct((rows, w), dt) for w, dt in zip(widths, dtypes)],
        compiler_params=_params(("parallel",)),
        name="in_proj",
    )(h, lw['norm1'], lw['w_in'], tab, jnp.asarray(_ROPE_SEL, BF16), lw['a_qn'], lw['a_kn'], lw['c_qn'], lw['c_kn'], lw['bd'])


def _attn_body(*refs, patterns, has_sink, sb):
    if has_sink:
        sink_ref, refs = refs[0], refs[1:]
    q0_ref, q1_ref, kp_ref, kc_ref, vp_ref, vc_ref, o_ref, kk, vv, acc_s, m_s, l_s = refs
    q_refs = (q0_ref, q1_ref)
    j = pl.program_id(0)
    kk[0:sb, :] = kp_ref[...]
    kk[sb:2 * sb, :] = kc_ref[...]
    vv[0:sb, :] = vp_ref[...]
    vv[sb:2 * sb, :] = vc_ref[...]
    row4 = lax.broadcasted_iota(jnp.int32, (4 * BLK, BLK), 0) & (BLK - 1)
    col4 = lax.broadcasted_iota(jnp.int32, (4 * BLK, BLK), 1)
    upper4 = col4 > row4
    g_lo = lax.broadcasted_iota(jnp.int32, (BLK, 128), 1) < HEAD_DIM
    ones_cols = jnp.ones((2 * BLK, 128), BF16)
    nblk = sb // BLK

    diag_here = [md == BLK for _, md in patterns]
    boosts = [[] for _ in patterns]
    for p, (d, md) in enumerate(patterns):
        for q in range(p + 1, len(patterns)):
            dq, mdq = patterns[q]
            if diag_here[p] and (md * d) % dq == 0 and (md * d) // dq < min(BLK, mdq + 1):
                boosts[q].append((md * d) // dq)
                diag_here[p] = False

    for pi, (d, max_dist) in enumerate(patterns):
        nsub = nblk // d
        has_diag = diag_here[pi]
        fdist = jnp.where(upper4, row4 + BLK - col4, row4 - col4)
        mult = jnp.ones((4 * BLK, BLK), F32)
        for f in boosts[pi]:
            mult = mult + (fdist == f).astype(F32)
        w_up = jnp.where(upper4, mult, 0.0)
        w_lo = jnp.where(upper4, 0.0, mult)

        def ld(ref, s0, d=d):
            if d == 1:
                return ref[pl.ds(s0, BLK), :]
            return ref[pl.ds(s0, BLK, stride=d), :]

        def block(t, carry, d=d, nsub=nsub, has_diag=has_diag, first=(pi == 0), ld=ld, w_up=w_up, w_lo=w_lo):
            r_ = t // nsub
            n = t - r_ * nsub
            start = r_ + BLK * d * n
            prev_ok = jnp.logical_or(j > 0, n > 0)
            neg = jnp.where(prev_ok, 0.0, -jnp.inf)
            qb = [ld(q_refs[rr], start) for rr in range(2)]
            kprev, kcur = ld(kk, sb + start - BLK * d), ld(kk, sb + start)
            vprev, vcur = ld(vv, sb + start - BLK * d), ld(vv, sb + start)
            qm = jnp.concatenate([jnp.where(g_lo, qb[0], 0.0), jnp.where(g_lo, 0.0, qb[0]),
                                  jnp.where(g_lo, qb[1], 0.0), jnp.where(g_lo, 0.0, qb[1])], axis=0).astype(BF16)
            kcat = jnp.concatenate([kprev, kcur], axis=0).astype(BF16)
            vcat = jnp.concatenate([jnp.concatenate([vprev, vcur], axis=0).astype(BF16), ones_cols], axis=1)
            s2 = _dot_nt(qm, kcat)
            sp, sc = s2[:, :BLK] + neg, s2[:, BLK:]
            s = jnp.where(upper4, sp, sc)
            mb = jnp.max(s, axis=-1, keepdims=True)
            if has_diag:
                sd = jnp.sum(jnp.where(col4 == row4, sp, 0.0), axis=-1, keepdims=True)
                mb = jnp.maximum(mb, sd)
            e = jnp.exp(s - mb)
            ecat = jnp.concatenate([e * w_up, e * w_lo], axis=1).astype(BF16)
            pvl = jnp.dot(ecat, vcat, preferred_element_type=F32)
            pv, lb = pvl[:, :128], pvl[:, 128:]
            if has_diag:
                ed = jnp.exp(sd - mb)
                lb = lb + ed
                pv = pv + ed * jnp.concatenate([vprev] * 4, axis=0)
            for rr in range(2):
                lo, hi = slice(2 * BLK * rr, 2 * BLK * rr + BLK), slice(2 * BLK * rr + BLK, 2 * BLK * (rr + 1))
                o_b = jnp.where(g_lo, pv[lo], pv[hi])
                m_b = jnp.where(g_lo, mb[lo], mb[hi])
                l_b = jnp.where(g_lo, lb[lo], lb[hi])
                rows = pl.ds(start, BLK) if d == 1 else pl.ds(start, BLK, stride=d)
                if first:
                    m_s[rr, rows, :] = m_b
                    l_s[rr, rows, :] = l_b
                    acc_s[rr, rows, :] = o_b
                else:
                    m_old = m_s[rr, rows, :]
                    m_new = jnp.maximum(m_old, m_b)
                    w_old = jnp.exp(m_old - m_new)
                    w_b = jnp.exp(m_b - m_new)
                    m_s[rr, rows, :] = m_new
                    l_s[rr, rows, :] = w_old * l_s[rr, rows, :] + w_b * l_b
                    acc_s[rr, rows, :] = w_old * acc_s[rr, rows, :] + w_b * o_b
            return carry

        lax.fori_loop(0, nblk, block, 0, unroll=2)

    chunk = 256
    for rr in range(2):
        for c in range(sb // chunk):
            rows = slice(chunk * c, chunk * (c + 1))
            m_f, l_f, a_f = m_s[rr, rows, :], l_s[rr, rows, :], acc_s[rr, rows, :]
            if has_sink:
                sk = sink_ref[rr:rr + 1, :]
                m2 = jnp.maximum(m_f, sk)
                w = jnp.exp(m_f - m2)
                o = a_f * w / (l_f * w + jnp.exp(sk - m2))
            else:
                o = a_f / l_f
            o_ref[rows, 128 * rr:128 * (rr + 1)] = o.astype(o_ref.dtype)


def _prompt_attn(q0, q1, k, v, patterns, sink_lanes=None, name="attn"):
    L = q0.shape[0]
    sb = ATTN_SUPER
    cur = lambda j: (j, 0)
    prev = lambda j: (jnp.maximum(j - 1, 0), 0)
    blk = lambda im: pl.BlockSpec((sb, 128), im)
    in_specs = [blk(cur), blk(cur), blk(prev), blk(cur), blk(prev), blk(cur)]
    args = [q0, q1, k, k, v, v]
    if sink_lanes is not None:
        in_specs = [_const_spec((2, 128))] + in_specs
        args = [sink_lanes] + args
    return pl.pallas_call(
        functools.partial(_attn_body, patterns=patterns, has_sink=sink_lanes is not None, sb=sb),
        grid=(L // sb,), in_specs=in_specs, out_specs=pl.BlockSpec((sb, Q_WIDTH), cur),
        out_shape=jax.ShapeDtypeStruct((L, Q_WIDTH), BF16),
        scratch_shapes=[pltpu.VMEM((2 * sb, 128), F32), pltpu.VMEM((2 * sb, 128), F32),
                        pltpu.VMEM((2, sb, 128), F32), pltpu.VMEM((2, sb, 128), F32), pltpu.VMEM((2, sb, 128), F32)],
        compiler_params=_params(("parallel",)),
        name=name,
    )(*args)


def _sample_attn_body(*refs, nb, has_sink, n_alias, w, batches=None):
    if has_sink:
        sink_ref, refs = refs[0], refs[1:]
    q0_ref, q1_ref, kn_ref, vn_ref, kc_ref, vc_ref, mc_ref, mn_ref = refs[:8]
    o_ref, ko_ref, vo_ref = refs[8 + n_alias:]
    g_lo = lax.broadcasted_iota(jnp.int32, (TILE, 128), 1) < HEAD_DIM
    lane = lax.broadcasted_iota(jnp.int32, (128, 128), 1)
    is_new = lane >= 128 - (TILE - TOK0)
    mult_c = mc_ref[...]
    mult_n = mn_ref[...]
    pad = jnp.zeros((128 - TILE, 128), F32)
    for b in (range(nb) if batches is None else batches):
        parts = []
        for q_ref in (q0_ref, q1_ref):
            qr = q_ref[TILE * b:TILE * (b + 1), :]
            parts += [jnp.where(g_lo, qr, 0.0), jnp.where(g_lo, 0.0, qr)]
        qm = jnp.concatenate(parts, axis=0).astype(BF16)
        kct = kc_ref[b]
        vct = vc_ref[b]
        knp = jnp.concatenate([kn_ref[TILE * b:TILE * (b + 1), :], pad], axis=0)
        vnp = jnp.concatenate([vn_ref[TILE * b:TILE * (b + 1), :], pad], axis=0)
        knt = knp.T
        vnt = vnp.T
        sc = jnp.where(mult_c > 0, jnp.dot(qm, kct.astype(BF16), preferred_element_type=F32), -jnp.inf)
        sn = jnp.where(mult_n > 0, jnp.dot(qm, knt.astype(BF16), preferred_element_type=F32), -jnp.inf)
        m = jnp.maximum(jnp.max(sc, axis=-1, keepdims=True), jnp.max(sn, axis=-1, keepdims=True))
        if has_sink:
            sk = sink_ref[:, :1]
            m = jnp.maximum(m, sk)
        ec = mult_c * jnp.exp(sc - m)
        en = mult_n * jnp.exp(sn - m)
        den = jnp.sum(ec, axis=-1, keepdims=True) + jnp.sum(en, axis=-1, keepdims=True)
        if has_sink:
            den = den + jnp.exp(sk - m)
        o = (_dot_nt(ec.astype(BF16), vct.astype(BF16))
             + jnp.dot(en.astype(BF16), vnp.astype(BF16), preferred_element_type=F32)) / den
        o_ref[TILE * b:TILE * (b + 1), :] = jnp.concatenate(
            [jnp.where(g_lo, o[0:TILE], o[TILE:2 * TILE]),
             jnp.where(g_lo, o[2 * TILE:3 * TILE], o[3 * TILE:4 * TILE])], axis=1).astype(o_ref.dtype)
        for src, new_t, dst in ((kct, knt, ko_ref), (vct, vnt, vo_ref)):
            shifted = pltpu.roll(src, w - (TILE - TOK0), 1)
            new_cols = pltpu.roll(new_t, 128 - TILE, 1)
            if w > 128:
                dst[b, :, 0:w - 128] = shifted[:, 0:w - 128]
            dst[b, :, w - 128:w] = jnp.where(is_new, new_cols, shifted[:, w - 128:w])


def _sample_attn(q0, q1, kn, vn, cache_k, cache_v, layer, mult_c, mult_n, sinks_rows, nb, prev_out):
    rows = q0.shape[0]
    nbatch = rows // TILE
    depth, _, _, w = cache_k.shape
    row_spec = lambda width: pl.BlockSpec((TILE * nb, width), lambda i: (i, 0))
    cache_spec = pl.BlockSpec((None, nb, KV_WIDTH, w), lambda i: (layer, i, 0, 0))
    in_specs = [row_spec(128), row_spec(128), row_spec(KV_WIDTH), row_spec(KV_WIDTH), cache_spec, cache_spec,
                _const_spec((4 * TILE, w)), _const_spec((4 * TILE, 128))]
    args = [q0, q1, kn, vn, cache_k, cache_v, mult_c, mult_n]
    if sinks_rows is not None:
        in_specs = [_const_spec((4 * TILE, 128))] + in_specs
        args = [sinks_rows] + args
    aliases = {}
    if prev_out is not None:
        aliases = {len(args): 1, len(args) + 1: 2}
        in_specs = in_specs + [pl.BlockSpec(memory_space=pl.ANY)] * 2
        args = args + list(prev_out)
    cache_shape = jax.ShapeDtypeStruct(cache_k.shape, F32)
    return pl.pallas_call(
        functools.partial(_sample_attn_body, nb=nb, has_sink=sinks_rows is not None,
                          n_alias=0 if prev_out is None else 2, w=w),
        grid=(nbatch // nb,), in_specs=in_specs, out_specs=[row_spec(Q_WIDTH), cache_spec, cache_spec],
        out_shape=[jax.ShapeDtypeStruct((rows, Q_WIDTH), BF16), cache_shape, cache_shape],
        input_output_aliases=aliases,
        compiler_params=_params(("parallel",)),
        name=f"sample_attn_w{w}",
    )(*args)


def _sample_mult_tables():
    t = np.arange(TILE) - TOK0
    tq = np.maximum(t, 0)[:, None]
    j = np.arange(A_WINDOW)[None, :]
    da = A_WINDOW + tq - j
    ma_c = ((da >= 0) & (da < A_WINDOW)).astype(np.float32)
    tn = (np.arange(128) - TOK0)[None, :]
    dn = tq - tn
    new_ok = (tn >= 0) & (tn < TILE - TOK0) & (dn >= 0)
    ma_n = (new_ok & (dn < A_WINDOW)).astype(np.float32)

    def mult(d):
        out = np.zeros(d.shape, np.float32)
        for w, dil in C_PATTERNS:
            out += ((d >= 0) & (d <= w) & (d % dil == 0)).astype(np.float32)
        return out

    jc = np.arange(C_SPAN)[None, :]
    mc_c = mult(C_SPAN + tq - jc)
    mc_n = np.where(new_ok, mult(dn), 0.0).astype(np.float32)
    tile4 = lambda a: jnp.asarray(np.tile(a, (4, 1)))
    return tile4(ma_c), tile4(ma_n), tile4(mc_c), tile4(mc_n)


def _ssd_intra(act, dt_raw, dtb, alog, tri_sel, tri_mask, expand, row_ok):
    xs = act[:, :B_INNER]
    bb = act[:, B_INNER:B_INNER + 2 * B_STATE].astype(BF16)
    cb_ = act[:, B_INNER + 2 * B_STATE:].astype(BF16)
    dtv = _softplus(dt_raw + dtb)
    if row_ok is not None:
        dtv = jnp.where(row_ok, dtv, 0.0)
    a = -jnp.exp(alog) * dtv
    acum = _sel_dot_rhs(tri_sel, a)
    acum_t = acum.T
    dt_e = _sel_dot_lhs(dtv, expand)
    xdt = xs * dt_e
    lane_lo = lax.broadcasted_iota(jnp.int32, (BLK, 128), 1) < HEAD_DIM
    ys = []
    for g in range(2):
        cbm = _dot_nt(cb_[:, 128 * g:128 * (g + 1)], bb[:, 128 * g:128 * (g + 1)])
        for pair in range(2):
            xp = xdt[:, 128 * (2 * g + pair):128 * (2 * g + pair + 1)]
            acc = None
            for j in range(2):
                h = 4 * g + 2 * pair + j
                seg = acum[:, h:h + 1] - acum_t[h:h + 1, :]
                lm = jnp.exp(jnp.where(tri_mask, seg, -jnp.inf))
                mh = (cbm * lm).astype(BF16)
                xh = jnp.where(lane_lo if j == 0 else ~lane_lo, xp, 0.0).astype(BF16)
                t = jnp.dot(mh, xh, preferred_element_type=F32)
                acc = t if acc is None else acc + t
            ys.append(acc)
    ydiag = jnp.concatenate(ys, axis=1)
    return xs, bb, cb_, xdt, acum, acum_t, ydiag


def _ssd_finish(y, z, norm_w):
    y = y * _silu(z)
    ms = jnp.mean(y * y, axis=-1, keepdims=True)
    return (y * lax.rsqrt(ms + NORM_EPS) * norm_w).astype(BF16)


def _conv_act(xp_ref, cw_ref, cb_ref, r0=0):
    t = BLK
    base = r0 + TILE - (CONV_K - 1)
    out = cb_ref[...] + xp_ref[base:base + t, :] * cw_ref[0:1, :]
    for j in range(1, CONV_K):
        out = out + xp_ref[base + j:base + j + t, :] * cw_ref[j:j + 1, :]
    return _silu(out)


def _split2_dot(v, sel):
    hi = v.astype(BF16)
    lo = (v - hi.astype(F32)).astype(BF16)
    return jnp.dot(hi, sel, preferred_element_type=F32) + jnp.dot(lo, sel, preferred_element_type=F32)


def _ssd_prompt_chunk(act, z, dt_raw, dtb, alog, dskip, nw, expand, expand_t, hs_ref):
    t = BLK
    row = lax.broadcasted_iota(jnp.int32, (t, t), 0)
    col = lax.broadcasted_iota(jnp.int32, (t, t), 1)
    tri_mask = col <= row
    xs = act[:, :B_INNER]
    bb = act[:, B_INNER:B_INNER + 2 * B_STATE].astype(BF16)
    cb_ = act[:, B_INNER + 2 * B_STATE:].astype(BF16)
    dtv = _softplus(dt_raw + dtb)
    a = -jnp.exp(alog) * dtv
    acum = _sel_dot_rhs(tri_mask.astype(BF16), a)
    acum_t = acum.T
    spread = _split2_dot(jnp.concatenate([dtv, jnp.exp(acum), jnp.exp(acum[t - 1:t, :] - acum)], axis=0), expand)
    dt_e, e_all, dec_e = spread[0:t], spread[t:2 * t], spread[2 * t:3 * t]
    xdt = xs * dt_e
    lane_lo = lax.broadcasted_iota(jnp.int32, (BLK, 128), 1) < HEAD_DIM
    ys = []
    for g in range(2):
        cbm = _dot_nt(cb_[:, 128 * g:128 * (g + 1)], bb[:, 128 * g:128 * (g + 1)])
        for pair in range(2):
            xp = xdt[:, 128 * (2 * g + pair):128 * (2 * g + pair + 1)]
            acc = None
            for j in range(2):
                h = 4 * g + 2 * pair + j
                seg = acum[:, h:h + 1] - acum_t[h:h + 1, :]
                lm = jnp.exp(jnp.where(tri_mask, seg, -jnp.inf))
                mh = (cbm * lm).astype(BF16)
                xh = jnp.where(lane_lo if j == 0 else ~lane_lo, xp, 0.0).astype(BF16)
                part = jnp.dot(mh, xh, preferred_element_type=F32)
                acc = part if acc is None else acc + part
            ys.append(acc)
    ydiag = jnp.concatenate(ys, axis=1)

    hst = hs_ref[...]
    hb = hst.astype(BF16)
    yoff = jnp.concatenate([_dot_nt(cb_[:, 128 * g:128 * (g + 1)], hb[256 * g:256 * (g + 1), :]) for g in range(2)],
                           axis=1)
    y = ydiag + yoff * e_all + xs * dskip
    out = _ssd_finish(y, z, nw)

    xw = xdt * dec_e
    last_t = jnp.exp(jnp.broadcast_to(acum_t[:, t - 1:t], (128, 128)))
    hi = last_t.astype(BF16)
    lo = (last_t - hi.astype(F32)).astype(BF16)
    cd = (jnp.dot(expand_t, hi, preferred_element_type=F32)
          + jnp.dot(expand_t, lo, preferred_element_type=F32))
    for g in range(2):
        xw_t = xw[:, 256 * g:256 * (g + 1)].T.astype(BF16)
        dh = jnp.dot(xw_t, bb[:, 128 * g:128 * (g + 1)], preferred_element_type=F32)
        hs_ref[256 * g:256 * (g + 1), :] = hst[256 * g:256 * (g + 1), :] * cd[256 * g:256 * (g + 1), :] + dh
    return out


def _ssd_prompt_body(xbc_ref, z_ref, dt_ref, cw_ref, cb_ref, dtb_ref, alog_ref, dskip_ref, nw_ref,
                     expand_ref, expand_t_ref, mo_ref, hs_ref, xp_ref, before_chunk=None):
    c = pl.program_id(0)
    rows = xbc_ref.shape[0]

    @pl.when(c == 0)
    def _():
        xp_ref[0:TILE, :] = jnp.zeros((TILE, CONV_DIM), F32)
        hs_ref[...] = jnp.zeros_like(hs_ref)

    xp_ref[TILE:, :] = xbc_ref[...]
    for ci in range(rows // BLK):
        r0 = BLK * ci
        if before_chunk is not None:
            before_chunk(ci)
        act = _conv_act(xp_ref, cw_ref, cb_ref, r0)
        mo_ref[r0:r0 + BLK, :] = _ssd_prompt_chunk(
            act, z_ref[r0:r0 + BLK, :], dt_ref[r0:r0 + BLK, :], dtb_ref[...], alog_ref[...], dskip_ref[...],
            nw_ref[...], expand_ref[...], expand_t_ref[...], hs_ref)
    xp_ref[0:TILE, :] = xp_ref[rows:rows + TILE, :]


_N_SSD_IN = 11
_N_SATTN_IN = 8


def _ssd_with_sample_attn_body(*refs, nb, n_alias, w):
    ssd_in = refs[:_N_SSD_IN]
    sattn_in = refs[_N_SSD_IN:_N_SSD_IN + _N_SATTN_IN + n_alias]
    mo_ref, hs_ref, o_ref, ko_ref, vo_ref, xp_ref = refs[_N_SSD_IN + _N_SATTN_IN + n_alias:]
    nchunk = ssd_in[0].shape[0] // BLK

    def sample_part(ci):
        share = range(ci * nb // nchunk, (ci + 1) * nb // nchunk)
        _sample_attn_body(*sattn_in, o_ref, ko_ref, vo_ref, nb=nb, has_sink=False, n_alias=n_alias, w=w,
                          batches=share)

    _ssd_prompt_body(*ssd_in, mo_ref, hs_ref, xp_ref, before_chunk=sample_part)


def _ssd_prompt(xbc, z, dt, lw, sample_c):
    L = xbc.shape[0]
    rows = SSD_ROWS
    steps = L // rows
    q0, q1, kn, vn, cache_k, cache_v, layer, mult_c, mult_n, prev_out = sample_c
    srows = q0.shape[0]
    nb = srows // TILE // steps
    assert nb * steps * TILE == srows and (nb * TILE) % 16 == 0
    w = cache_k.shape[3]
    row_spec = lambda wd: pl.BlockSpec((rows, wd), lambda c: (c, 0))
    srow_spec = lambda wd: pl.BlockSpec((TILE * nb, wd), lambda c: (c, 0))
    cache_spec = pl.BlockSpec((None, nb, KV_WIDTH, w), lambda c: (layer, c, 0, 0))
    in_specs = [row_spec(CONV_DIM), row_spec(B_INNER), row_spec(128),
                _const_spec((CONV_K, CONV_DIM)), _const_spec((1, CONV_DIM)), _const_spec((1, 128)),
                _const_spec((1, 128)), _const_spec((1, B_INNER)), _const_spec((1, B_INNER)),
                _const_spec((128, B_INNER)), _const_spec((B_INNER, 128)),
                srow_spec(128), srow_spec(128), srow_spec(KV_WIDTH), srow_spec(KV_WIDTH), cache_spec, cache_spec,
                _const_spec((4 * TILE, w)), _const_spec((4 * TILE, 128))]
    args = [xbc, z, dt, lw['conv_w'], lw['conv_b'], lw['dt_bias'], lw['a_log'], lw['d_skip'], lw['ssm_norm'],
            lw['expand'], lw['expand_t'], q0, q1, kn, vn, cache_k, cache_v, mult_c, mult_n]
    assert len(args) == _N_SSD_IN + _N_SATTN_IN
    aliases = {}
    if prev_out is not None:
        aliases = {len(args): 3, len(args) + 1: 4}
        in_specs = in_specs + [pl.BlockSpec(memory_space=pl.ANY)] * 2
        args = args + list(prev_out)
    cache_shape = jax.ShapeDtypeStruct(cache_k.shape, F32)
    return pl.pallas_call(
        functools.partial(_ssd_with_sample_attn_body, nb=nb, n_alias=len(aliases), w=w),
        grid=(steps,),
        in_specs=in_specs,
        out_specs=[row_spec(B_INNER), pl.BlockSpec((B_INNER, B_STATE), lambda c: (0, 0)),
                   srow_spec(Q_WIDTH), cache_spec, cache_spec],
        out_shape=[jax.ShapeDtypeStruct((L, B_INNER), BF16), jax.ShapeDtypeStruct((B_INNER, B_STATE), F32),
                   jax.ShapeDtypeStruct((srows, Q_WIDTH), BF16), cache_shape, cache_shape],
        scratch_shapes=[pltpu.VMEM((rows + TILE, CONV_DIM), F32)],
        input_output_aliases=aliases,
        compiler_params=_params(("arbitrary",)),
        name="ssd_prompt",
    )(*args)


def _ssd_sample_body(*refs):
    (xbc_ref, pre_ref, z_ref, dt_ref, h0_ref, cw_ref, cb_ref, dtb_ref, alog_ref, dskip_ref,
     nw_ref, expand_ref, expand_t_ref) = refs[:13]
    mo_ref, hout_ref, xp_ref = refs[-3:]
    t = BLK
    nbt = t // TILE
    rmod = lax.broadcasted_iota(jnp.int32, (t, 1), 0) & (TILE - 1)
    row_ok = rmod >= TOK0
    xp_ref[0:TILE, :] = jnp.zeros((TILE, CONV_DIM), F32)
    xp_ref[TILE:, :] = jnp.where(row_ok, xbc_ref[...], pre_ref[...])
    act = _conv_act(xp_ref, cw_ref, cb_ref)

    row = lax.broadcasted_iota(jnp.int32, (t, t), 0)
    col = lax.broadcasted_iota(jnp.int32, (t, t), 1)
    same = (row // TILE) == (col // TILE)
    tri_mask = (col <= row) & same
    tri_sel = tri_mask.astype(BF16)
    last_sel = (col == (row // TILE) * TILE + (TILE - 1)).astype(BF16)
    expand = expand_ref[...]
    xs, bb, cb_, xdt, acum, acum_t, ydiag = _ssd_intra(
        act, dt_ref[...], dtb_ref[...], alog_ref[...], tri_sel, tri_mask, expand, row_ok)

    yoffs = []
    for b in range(nbt):
        hb = h0_ref[b].astype(BF16)
        yoffs.append(jnp.concatenate(
            [_dot_nt(cb_[TILE * b:TILE * (b + 1), 128 * g:128 * (g + 1)], hb[256 * g:256 * (g + 1), :])
             for g in range(2)], axis=1))
    yoff = jnp.concatenate(yoffs, axis=0)
    e_all = _sel_dot_lhs(jnp.exp(acum), expand)
    y = ydiag + yoff * e_all + xs * dskip_ref[...]
    mo_ref[...] = _ssd_finish(y, z_ref[...], nw_ref[...])

    last = _sel_dot_rhs(last_sel, acum)
    dec_e = _sel_dot_lhs(jnp.exp(last - acum), expand)
    xw = xdt * dec_e
    dcol = _sel_dot_rhs(expand_t_ref[...], jnp.exp(acum_t))
    lane = lax.broadcasted_iota(jnp.int32, (256, t), 1)
    xw_ts = [xw[:, 256 * g:256 * (g + 1)].T for g in range(2)]
    for b in range(nbt):
        in_b = (lane // TILE) == b
        cd = dcol[:, TILE * b + TILE - 1:TILE * b + TILE]
        h0 = h0_ref[b]
        for g in range(2):
            lhs = jnp.where(in_b, xw_ts[g], 0.0).astype(BF16)
            dh = jnp.dot(lhs, bb[:, 128 * g:128 * (g + 1)], preferred_element_type=F32)
            hout_ref[b, 256 * g:256 * (g + 1), :] = (
                h0[256 * g:256 * (g + 1), :] * cd[256 * g:256 * (g + 1), :] + dh)


def _ssd_sample(xbc, prefix, z, dt, state, layer, lw, prev_out):
    rows = xbc.shape[0]
    nbt = BLK // TILE
    row_spec = lambda w: pl.BlockSpec((BLK, w), lambda i: (i, 0))
    st_spec = pl.BlockSpec((None, nbt, B_INNER, B_STATE), lambda i: (layer, i, 0, 0))
    in_specs = [row_spec(CONV_DIM), row_spec(CONV_DIM), row_spec(B_INNER), row_spec(128), st_spec,
                _const_spec((CONV_K, CONV_DIM)), _const_spec((1, CONV_DIM)), _const_spec((1, 128)),
                _const_spec((1, 128)), _const_spec((1, B_INNER)), _const_spec((1, B_INNER)),
                _const_spec((128, B_INNER)), _const_spec((B_INNER, 128))]
    args = [xbc, prefix, z, dt, state, lw['conv_w'], lw['conv_b'], lw['dt_bias'], lw['a_log'], lw['d_skip'],
            lw['ssm_norm'], lw['expand'], lw['expand_t']]
    aliases = {}
    if prev_out is not None:
        aliases = {len(args): 1}
        in_specs.append(pl.BlockSpec(memory_space=pl.ANY))
        args.append(prev_out)
    return pl.pallas_call(
        _ssd_sample_body,
        grid=(rows // BLK,),
        in_specs=in_specs,
        out_specs=[row_spec(B_INNER), st_spec],
        out_shape=[jax.ShapeDtypeStruct((rows, B_INNER), BF16), jax.ShapeDtypeStruct(state.shape, F32)],
        scratch_shapes=[pltpu.VMEM((BLK + TILE, CONV_DIM), F32)],
        input_output_aliases=aliases,
        compiler_params=_params(("parallel",)),
        name="ssd_sample",
    )(*args)


def _out_ffn_body(h_ref, a_ref, m_ref, c_ref, wo_ref, g2_ref, wg_ref, wu_ref, wd_ref, out_ref):
    d = functools.partial(jnp.dot, preferred_element_type=F32)
    h1 = (h_ref[...] + d(a_ref[...], wo_ref[0:256, :]) + d(m_ref[...], wo_ref[256:768, :])
          + d(c_ref[...], wo_ref[768:1024, :]))
    ms = jnp.mean(h1 * h1, axis=-1, keepdims=True)
    u = (h1 * lax.rsqrt(ms + NORM_EPS) * g2_ref[...]).astype(BF16)
    out_ref[...] = h1
    for c in range(D_FF // FF_CHUNK):
        sl = slice(FF_CHUNK * c, FF_CHUNK * (c + 1))
        act = (_silu(d(u, wg_ref[:, sl])) * d(u, wu_ref[:, sl])).astype(BF16)
        out_ref[...] += d(act, wd_ref[sl, :])


def _out_ffn(h, a_o, m_o, c_o, lw, tm):
    rows = h.shape[0]
    row_spec = lambda w: pl.BlockSpec((tm, w), lambda i: (i, 0))
    in_specs = [row_spec(D_MODEL), row_spec(Q_WIDTH), row_spec(B_INNER), row_spec(Q_WIDTH),
                _layer_spec((D_MODEL, D_MODEL), lw['layer']), _const_spec((1, D_MODEL)),
                _layer_spec((D_MODEL, D_FF), lw['layer']), _layer_spec((D_MODEL, D_FF), lw['layer']),
                _layer_spec((D_FF, D_MODEL), lw['layer'])]
    return pl.pallas_call(
        _out_ffn_body,
        grid=(rows // tm,), in_specs=in_specs, out_specs=row_spec(D_MODEL),
        out_shape=jax.ShapeDtypeStruct((rows, D_MODEL), F32),
        compiler_params=_params(("parallel",)),
        name="out_ffn",
    )(h, a_o, m_o, c_o, lw['w_out'], lw['norm2'], lw['w_gate'], lw['w_up'], lw['w_down'])


def _rope_table(pos):
    half = ROT_DIM // 2
    inv = ROPE_THETA ** (-(jnp.arange(half, dtype=F32) * 2.0 / ROT_DIM))
    ang = pos.astype(F32)[None, :] * inv[:, None]
    cs = jnp.concatenate([jnp.cos(ang), jnp.sin(ang)], axis=0)
    hi = cs.astype(BF16).astype(F32)
    mid = (cs - hi).astype(BF16).astype(F32)
    lo = (cs - hi - mid).astype(BF16).astype(F32)
    return jnp.concatenate([hi, mid, lo], axis=0)


def _rope_selection():
    half = ROT_DIM // 2
    sel = np.zeros((128, 256), np.float32)
    for term in range(3):
        base = term * ROT_DIM
        for lane in range(128):
            f = lane % HEAD_DIM
            if f < half:
                sel[base + f, lane] = 1.0
                sel[base + half + f, 128 + lane] = -1.0
            elif f < ROT_DIM:
                sel[base + f - half, lane] = 1.0
                sel[base + f, 128 + lane] = 1.0
    return sel


_ROPE_SEL = _rope_selection()


def _permute_heads(m, start, axis):
    sl = lambda a, b: lax.slice_in_dim(m, a, b, axis=axis)
    pieces = [sl(0, start)] if start else []
    pieces += [sl(start, start + 64), sl(start + 128, start + 192), sl(start + 64, start + 128),
               sl(start + 192, start + 256), sl(start + 256, m.shape[axis])]
    return jnp.concatenate(pieces, axis=axis)


def _matmul_weights(w_in, w_out, w_gate, w_up, w_down):
    w = _permute_heads(_permute_heads(w_in.astype(BF16), _OFF['aq'], 2), _OFF['cq'], 2)
    w = jnp.pad(w, ((0, 0), (0, 0), (0, N_IN_PAD - N_IN)))
    wo = _permute_heads(_permute_heads(w_out.astype(BF16), 0, 1), 768, 1)
    return dict(w_in=w, w_out=wo, w_gate=w_gate.astype(BF16), w_up=w_up.astype(BF16), w_down=w_down.astype(BF16))


def _layer_weights(l, mm, norm1, a_qn, a_kn, a_sinks, c_qn, c_kn, conv_w, conv_b, dt_bias, a_log, d_skip,
                   ssm_norm, norm2):
    pad8 = lambda v: jnp.pad(v.astype(F32), (0, 128 - B_HEADS))[None, :]
    head_of = np.arange(B_INNER) // 64
    expand = (np.arange(128)[:, None] == head_of[None, :]).astype(np.float32)
    blk = (np.arange(256)[:, None] // 64 == np.arange(256)[None, :] // 64).astype(np.float32) / 64.0
    return dict(
        layer=l, norm1=norm1[l][None, :], w_in=mm['w_in'],
        a_qn=jnp.tile(a_qn[l], 4)[None, :], a_kn=jnp.tile(a_kn[l], 2)[None, :],
        c_qn=jnp.tile(c_qn[l], 4)[None, :], c_kn=jnp.tile(c_kn[l], 2)[None, :],
        bd=jnp.asarray(blk, BF16),
        sink_lanes=jnp.repeat(a_sinks[l].astype(F32)[jnp.asarray([0, 2, 1, 3])], HEAD_DIM).reshape(2, 128),
        sink_rows=jnp.broadcast_to(
            jnp.repeat(a_sinks[l].astype(F32)[jnp.asarray([0, 2, 1, 3])], TILE)[:, None], (4 * TILE, 128)),
        conv_w=conv_w[l], conv_b=conv_b[l][None, :], dt_bias=pad8(dt_bias[l]), a_log=pad8(a_log[l]),
        d_skip=jnp.repeat(d_skip[l].astype(F32), 64)[None, :], ssm_norm=ssm_norm[l][None, :],
        expand=jnp.asarray(expand, BF16), expand_t=jnp.asarray(expand.T, BF16),
        w_out=mm['w_out'], norm2=norm2[l][None, :], w_gate=mm['w_gate'], w_up=mm['w_up'], w_down=mm['w_down'])


def kernel(x_prompt, x_sample, cache_a_k, cache_a_v, cache_c_k, cache_c_v, state_ssm, state_conv, norm1, w_in,
           a_qn, a_kn, a_sinks, c_qn, c_kn, conv_w, conv_b, dt_bias, a_log, d_skip, ssm_norm, w_out, norm2,
           w_gate, w_up, w_down):
    depth = w_in.shape[0]
    batch, seq, _ = x_prompt.shape
    nbatch, dec_seq, _ = x_sample.shape
    assert batch == 1 and dec_seq == TILE - TOK0 and seq % (16 * BLK) == 0 and nbatch % (BLK // TILE) == 0
    past_len = PAST_LEN
    a_buf, c_buf = cache_a_k.shape[2], cache_c_k.shape[2]
    assert a_buf == A_WINDOW and c_buf == C_SPAN

    hp = x_prompt.reshape(seq, D_MODEL)
    hs = jnp.pad(x_sample, ((0, 0), (TOK0, 0), (0, 0))).reshape(nbatch * TILE, D_MODEL)
    tab_p = _rope_table(jnp.arange(seq))
    pos_s = past_len + jnp.maximum(jnp.arange(TILE) - TOK0, 0)
    tab_s = jnp.tile(_rope_table(pos_s), (1, nbatch))
    ma_c, ma_n, mc_c, mc_n = _sample_mult_tables()

    to_fm = lambda c: jnp.transpose(c, (0, 1, 3, 4, 2)).reshape(depth, nbatch, KV_WIDTH, c.shape[2])
    from_fm = lambda c: jnp.transpose(c.reshape(depth, nbatch, 2, HEAD_DIM, c.shape[3]), (0, 1, 4, 2, 3))
    ca_k, ca_v, cc_k, cc_v = to_fm(cache_a_k), to_fm(cache_a_v), to_fm(cache_c_k), to_fm(cache_c_v)
    st = state_ssm.reshape(depth, nbatch, B_INNER, B_STATE)

    mm = _matmul_weights(w_in, w_out, w_gate, w_up, w_down)
    p_out = [[] for _ in range(6)]
    s_conv = []
    new_a = new_c = new_st = None
    tm_p = 512
    tm_s = min(512, nbatch * TILE)
    for l in range(depth):
        lw = _layer_weights(l, mm, norm1, a_qn, a_kn, a_sinks, c_qn, c_kn, conv_w, conv_b, dt_bias, a_log,
                            d_skip, ssm_norm, norm2)
        s_qa0, s_qa1, s_ka, s_va, s_qc0, s_qc1, s_kc, s_vc, s_z, s_xbc, s_dt = _in_proj(hs, lw, tab_s, tm_s)

        qa0, qa1, ka, va, qc0, qc1, kc, vc, z, xbc, dt = _in_proj(hp, lw, tab_p, tm_p)
        a_o = _prompt_attn(qa0, qa1, ka, va, A_PATTERNS, sink_lanes=lw['sink_lanes'], name="attn_a")
        c_o = _prompt_attn(qc0, qc1, kc, vc, C_BANDS, name="attn_c")
        m_o, h_fin, s_c_o, *new_c = _ssd_prompt(
            xbc, z, dt, lw, (s_qc0, s_qc1, s_kc, s_vc, cc_k, cc_v, l, mc_c, mc_n, new_c))
        hp = _out_ffn(hp, a_o, m_o, c_o, lw, FFN_ROWS)
        p_out[0].append(ka[seq - a_buf:].reshape(1, a_buf, 2, HEAD_DIM))
        p_out[1].append(va[seq - a_buf:].reshape(1, a_buf, 2, HEAD_DIM))
        p_out[2].append(kc[seq - c_buf:].reshape(1, c_buf, 2, HEAD_DIM))
        p_out[3].append(vc[seq - c_buf:].reshape(1, c_buf, 2, HEAD_DIM))
        p_out[4].append(h_fin.reshape(1, B_HEADS, 64, B_STATE))
        p_out[5].append(xbc[seq - (CONV_K - 1):].reshape(1, CONV_K - 1, CONV_DIM))

        a_o, *new_a = _sample_attn(s_qa0, s_qa1, s_ka, s_va, ca_k, ca_v, l, ma_c, ma_n, lw['sink_rows'], 8, new_a)
        prefix = jnp.pad(state_conv[l], ((0, 0), (1, TILE - CONV_K), (0, 0))).reshape(nbatch * TILE, CONV_DIM)
        m_o, new_st = _ssd_sample(s_xbc, prefix, s_z, s_dt, st, l, lw, new_st)
        hs = _out_ffn(hs, a_o, m_o, s_c_o, lw, tm_s)
        s_conv.append(s_xbc.reshape(nbatch, TILE, CONV_DIM)[:, TILE - (CONV_K - 1):])

    outs_p = [jnp.stack(t, axis=0) for t in p_out]
    outs_s = [from_fm(new_a[0]), from_fm(new_a[1]), from_fm(new_c[0]), from_fm(new_c[1]),
              new_st.reshape(depth, nbatch, B_HEADS, 64, B_STATE), jnp.stack(s_conv, axis=0)]
    y_p = hp.reshape(1, seq, D_MODEL)
    y_s = hs.reshape(nbatch, TILE, D_MODEL)[:, TOK0:]
    return (y_p, y_s, *outs_p, *outs_s)
```

```python
import functools
import math

import numpy as np
import jax
import jax.numpy as jnp
from jax import lax
from jax.experimental import pallas as pl
from jax.experimental.pallas import tpu as pltpu

F32 = jnp.float32
BF16 = jnp.bfloat16

D_MODEL = 1024
HEAD_DIM = 64
ROT_DIM = 16
ROPE_THETA = 500000.0
NORM_EPS = 1e-6
Q_WIDTH = 256
KV_WIDTH = 128
PAST_LEN = 16384
A_WINDOW = 128
C_PATTERNS = ((128, 1), (512, 4), (2048, 16))
C_SPAN = 2048
B_HEADS = 8
B_INNER = 512
B_STATE = 128
CONV_K = 4
CONV_DIM = 1024
D_FF = 2816
N_IN = 2568
N_IN_PAD = 2688
BLK = 128
ATTN_SUPER = 2048
A_PATTERNS = ((1, A_WINDOW - 1),)
C_BANDS = tuple((d, w // d) for w, d in C_PATTERNS)
TOK0 = 4
TILE = 8
FF_CHUNK = 256
SSD_ROWS = 256
FFN_ROWS = 512
VMEM_LIMIT = 56 * 1024 * 1024

_OFF = dict(aq=0, ak=256, av=384, cq=512, ck=768, cv=896, z=1024, xbc=1536, dt=2560, end=N_IN_PAD)
_HEAD_PERM = np.concatenate([np.arange(0, 64), np.arange(128, 192), np.arange(64, 128), np.arange(192, 256)])


def _const_spec(shape):
    nd = len(shape)
    return pl.BlockSpec(shape, lambda *_: (0,) * nd, pipeline_mode=pl.Buffered(1))


def _layer_spec(shape, layer):
    nd = len(shape)
    return pl.BlockSpec((None,) + tuple(shape), lambda *_: (layer,) + (0,) * nd, pipeline_mode=pl.Buffered(1))


def _params(sem):
    return pltpu.CompilerParams(dimension_semantics=sem, vmem_limit_bytes=VMEM_LIMIT)


def _split3(v):
    hi = v.astype(BF16)
    r1 = v - hi.astype(F32)
    mid = r1.astype(BF16)
    lo = (r1 - mid.astype(F32)).astype(BF16)
    return hi, mid, lo


def _sel_dot_rhs(sel, v):
    hi, mid, lo = _split3(v)
    d = functools.partial(jnp.dot, preferred_element_type=F32)
    return d(sel, hi) + d(sel, mid) + d(sel, lo)


def _sel_dot_lhs(v, sel):
    hi, mid, lo = _split3(v)
    d = functools.partial(jnp.dot, preferred_element_type=F32)
    return d(hi, sel) + d(mid, sel) + d(lo, sel)


def _dot_nt(a, b):
    return lax.dot_general(a, b, (((1,), (1,)), ((), ())), preferred_element_type=F32)


def _silu(x):
    return x * jax.nn.sigmoid(x)


def _softplus(x):
    return jnp.maximum(x, 0.0) + jnp.log(1.0 + jnp.exp(-jnp.abs(x)))


def _head_norm(x, gain, bd):
    x2 = x * x
    hi = x2.astype(BF16)
    lo = (x2 - hi.astype(F32)).astype(BF16)
    ms = jnp.dot(hi, bd, preferred_element_type=F32) + jnp.dot(lo, bd, preferred_element_type=F32)
    return x * lax.rsqrt(ms + NORM_EPS) * gain


def _rope(x, cos_t, sin_t):
    w = x.shape[1]
    lane = lax.broadcasted_iota(jnp.int32, x.shape, 1) & (HEAD_DIM - 1)
    partner = jnp.where(lane < ROT_DIM // 2, pltpu.roll(x, w - ROT_DIM // 2, 1), pltpu.roll(x, ROT_DIM // 2, 1))
    return x * cos_t + partner * sin_t


def _in_proj_body(h_ref, g1_ref, w_ref, cs_ref, sel_ref, gqa_ref, gka_ref, gqc_ref, gkc_ref, bd_ref,
                  qa0_ref, qa1_ref, ka_ref, va_ref, qc0_ref, qc1_ref, kc_ref, vc_ref, z_ref, xbc_ref, dt_ref):
    x = h_ref[...]
    tm = x.shape[0]
    ms = jnp.mean(x * x, axis=-1, keepdims=True)
    u = (x * lax.rsqrt(ms + NORM_EPS) * g1_ref[...]).astype(BF16)

    def proj(name, nxt):
        return jnp.dot(u, w_ref[:, _OFF[name]:_OFF[nxt]], preferred_element_type=F32)

    xbc_ref[...] = proj('xbc', 'dt')
    z_ref[...] = proj('z', 'xbc')
    dt_ref[...] = proj('dt', 'end')
    cs_t = jnp.concatenate([cs_ref[...], jnp.zeros((128 - 3 * ROT_DIM, tm), F32)], axis=0).T.astype(BF16)
    tab = jnp.dot(cs_t, sel_ref[...], preferred_element_type=F32)
    unrot = ((lax.broadcasted_iota(jnp.int32, (1, 128), 1) & (HEAD_DIM - 1)) >= ROT_DIM).astype(F32)
    c1, s1 = tab[:, :128] + unrot, tab[:, 128:]
    c2 = jnp.concatenate([c1, c1], axis=1)
    s2 = jnp.concatenate([s1, s1], axis=1)
    bd2 = bd_ref[...]
    bd1 = bd2[:128, :128]
    qa = _rope(_head_norm(proj('aq', 'ak'), gqa_ref[...], bd2), c2, s2) * 0.125
    qa0_ref[...] = qa[:, :128]
    qa1_ref[...] = qa[:, 128:]
    kva = proj('ak', 'cq')
    ka_ref[...] = _rope(_head_norm(kva[:, :128], gka_ref[...], bd1), c1, s1)
    va_ref[...] = kva[:, 128:]
    qc = _rope(_head_norm(proj('cq', 'ck'), gqc_ref[...], bd2), c2, s2) * 0.125
    qc0_ref[...] = qc[:, :128]
    qc1_ref[...] = qc[:, 128:]
    kvc = proj('ck', 'z')
    kc_ref[...] = _rope(_head_norm(kvc[:, :128], gkc_ref[...], bd1), c1, s1)
    vc_ref[...] = kvc[:, 128:]


def _in_proj(h, lw, tab, tm):
    rows = h.shape[0]
    grid = (rows // tm,)
    row_spec = lambda w: pl.BlockSpec((tm, w), lambda i: (i, 0))
    widths = (128, 128, KV_WIDTH, KV_WIDTH, 128, 128, KV_WIDTH, KV_WIDTH, B_INNER, CONV_DIM, 128)
    dtypes = (F32,) * len(widths)
    return pl.pallas_call(
        _in_proj_body,
        grid=grid,
        in_specs=[row_spec(D_MODEL), _const_spec((1, D_MODEL)), _layer_spec((D_MODEL, N_IN_PAD), lw['layer']),
                  pl.BlockSpec((3 * ROT_DIM, tm), lambda i: (0, i)), _const_spec((128, 256)),
                  _const_spec((1, 256)), _const_spec((1, 128)), _const_spec((1, 256)),
                  _const_spec((1, 128)), _const_spec((256, 256))],
        out_specs=[row_spec(w) for w in widths],
        out_shape=[jax.ShapeDtypeStruct((rows, w), dt) for w, dt in zip(widths, dtypes)],
        compiler_params=_params(("parallel",)),
        name="in_proj",
    )(h, lw['norm1'], lw['w_in'], tab, jnp.asarray(_ROPE_SEL, BF16), lw['a_qn'], lw['a_kn'], lw['c_qn'], lw['c_kn'], lw['bd'])


def _attn_body(*refs, patterns, has_sink, sb):
    if has_sink:
        sink_ref, refs = refs[0], refs[1:]
    q0_ref, q1_ref, kp_ref, kc_ref, vp_ref, vc_ref, o_ref, kk, vv, acc_s, m_s, l_s = refs
    q_refs = (q0_ref, q1_ref)
    j = pl.program_id(0)
    kk[0:sb, :] = kp_ref[...]
    kk[sb:2 * sb, :] = kc_ref[...]
    vv[0:sb, :] = vp_ref[...]
    vv[sb:2 * sb, :] = vc_ref[...]
    row4 = lax.broadcasted_iota(jnp.int32, (4 * BLK, BLK), 0) & (BLK - 1)
    col4 = lax.broadcasted_iota(jnp.int32, (4 * BLK, BLK), 1)
    upper4 = col4 > row4
    g_lo = lax.broadcasted_iota(jnp.int32, (BLK, 128), 1) < HEAD_DIM
    ones_cols = jnp.ones((2 * BLK, 128), BF16)
    nblk = sb // BLK

    diag_here = [md == BLK for _, md in patterns]
    boosts = [[] for _ in patterns]
    for p, (d, md) in enumerate(patterns):
        for q in range(p + 1, len(patterns)):
            dq, mdq = patterns[q]
            if diag_here[p] and (md * d) % dq == 0 and (md * d) // dq < min(BLK, mdq + 1):
                boosts[q].append((md * d) // dq)
                diag_here[p] = False

    for pi, (d, max_dist) in enumerate(patterns):
        nsub = nblk // d
        has_diag = diag_here[pi]
        fdist = jnp.where(upper4, row4 + BLK - col4, row4 - col4)
        mult = jnp.ones((4 * BLK, BLK), F32)
        for f in boosts[pi]:
            mult = mult + (fdist == f).astype(F32)
        w_up = jnp.where(upper4, mult, 0.0)
        w_lo = jnp.where(upper4, 0.0, mult)

        def ld(ref, s0, d=d):
            if d == 1:
                return ref[pl.ds(s0, BLK), :]
            return ref[pl.ds(s0, BLK, stride=d), :]

        def block(t, carry, d=d, nsub=nsub, has_diag=has_diag, first=(pi == 0), ld=ld, w_up=w_up, w_lo=w_lo):
            r_ = t // nsub
            n = t - r_ * nsub
            start = r_ + BLK * d * n
            prev_ok = jnp.logical_or(j > 0, n > 0)
            neg = jnp.where(prev_ok, 0.0, -jnp.inf)
            qb = [ld(q_refs[rr], start) for rr in range(2)]
            kprev, kcur = ld(kk, sb + start - BLK * d), ld(kk, sb + start)
            vprev, vcur = ld(vv, sb + start - BLK * d), ld(vv, sb + start)
            qm = jnp.concatenate([jnp.where(g_lo, qb[0], 0.0), jnp.where(g_lo, 0.0, qb[0]),
                                  jnp.where(g_lo, qb[1], 0.0), jnp.where(g_lo, 0.0, qb[1])], axis=0).astype(BF16)
            kcat = jnp.concatenate([kprev, kcur], axis=0).astype(BF16)
            vcat = jnp.concatenate([jnp.concatenate([vprev, vcur], axis=0).astype(BF16), ones_cols], axis=1)
            s2 = _dot_nt(qm, kcat)
            sp, sc = s2[:, :BLK] + neg, s2[:, BLK:]
            s = jnp.where(upper4, sp, sc)
            mb = jnp.max(s, axis=-1, keepdims=True)
            if has_diag:
                sd = jnp.sum(jnp.where(col4 == row4, sp, 0.0), axis=-1, keepdims=True)
                mb = jnp.maximum(mb, sd)
            e = jnp.exp(s - mb)
            ecat = jnp.concatenate([e * w_up, e * w_lo], axis=1).astype(BF16)
            pvl = jnp.dot(ecat, vcat, preferred_element_type=F32)
            pv, lb = pvl[:, :128], pvl[:, 128:]
            if has_diag:
                ed = jnp.exp(sd - mb)
                lb = lb + ed
                pv = pv + ed * jnp.concatenate([vprev] * 4, axis=0)
            for rr in range(2):
                lo, hi = slice(2 * BLK * rr, 2 * BLK * rr + BLK), slice(2 * BLK * rr + BLK, 2 * BLK * (rr + 1))
                o_b = jnp.where(g_lo, pv[lo], pv[hi])
                m_b = jnp.where(g_lo, mb[lo], mb[hi])
                l_b = jnp.where(g_lo, lb[lo], lb[hi])
                rows = pl.ds(start, BLK) if d == 1 else pl.ds(start, BLK, stride=d)
                if first:
                    m_s[rr, rows, :] = m_b
                    l_s[rr, rows, :] = l_b
                    acc_s[rr, rows, :] = o_b
                else:
                    m_old = m_s[rr, rows, :]
                    m_new = jnp.maximum(m_old, m_b)
                    w_old = jnp.exp(m_old - m_new)
                    w_b = jnp.exp(m_b - m_new)
                    m_s[rr, rows, :] = m_new
                    l_s[rr, rows, :] = w_old * l_s[rr, rows, :] + w_b * l_b
                    acc_s[rr, rows, :] = w_old * acc_s[rr, rows, :] + w_b * o_b
            return carry

        lax.fori_loop(0, nblk, block, 0, unroll=2)

    chunk = 256
    for rr in range(2):
        for c in range(sb // chunk):
            rows = slice(chunk * c, chunk * (c + 1))
            m_f, l_f, a_f = m_s[rr, rows, :], l_s[rr, rows, :], acc_s[rr, rows, :]
            if has_sink:
                sk = sink_ref[rr:rr + 1, :]
                m2 = jnp.maximum(m_f, sk)
                w = jnp.exp(m_f - m2)
                o = a_f * w / (l_f * w + jnp.exp(sk - m2))
            else:
                o = a_f / l_f
            o_ref[rows, 128 * rr:128 * (rr + 1)] = o.astype(o_ref.dtype)


def _prompt_attn(q0, q1, k, v, patterns, sink_lanes=None, name="attn"):
    L = q0.shape[0]
    sb = ATTN_SUPER
    cur = lambda j: (j, 0)
    prev = lambda j: (jnp.maximum(j - 1, 0), 0)
    blk = lambda im: pl.BlockSpec((sb, 128), im)
    in_specs = [blk(cur), blk(cur), blk(prev), blk(cur), blk(prev), blk(cur)]
    args = [q0, q1, k, k, v, v]
    if sink_lanes is not None:
        in_specs = [_const_spec((2, 128))] + in_specs
        args = [sink_lanes] + args
    return pl.pallas_call(
        functools.partial(_attn_body, patterns=patterns, has_sink=sink_lanes is not None, sb=sb),
        grid=(L // sb,), in_specs=in_specs, out_specs=pl.BlockSpec((sb, Q_WIDTH), cur),
        out_shape=jax.ShapeDtypeStruct((L, Q_WIDTH), BF16),
        scratch_shapes=[pltpu.VMEM((2 * sb, 128), F32), pltpu.VMEM((2 * sb, 128), F32),
                        pltpu.VMEM((2, sb, 128), F32), pltpu.VMEM((2, sb, 128), F32), pltpu.VMEM((2, sb, 128), F32)],
        compiler_params=_params(("parallel",)),
        name=name,
    )(*args)


def _shift_cache(src, new_t, dst, b, w):
    lane = lax.broadcasted_iota(jnp.int32, (128, 128), 1)
    is_new = lane >= 128 - (TILE - TOK0)
    shifted = pltpu.roll(src, w - (TILE - TOK0), 1)
    new_cols = pltpu.roll(new_t, 128 - TILE, 1)
    if w > 128:
        dst[b, :, 0:w - 128] = shifted[:, 0:w - 128]
    dst[b, :, w - 128:w] = jnp.where(is_new, new_cols, shifted[:, w - 128:w])


def _cache_shift_body(*refs, nb, w):
    new_ref, c_ref, o_ref = refs[0], refs[1], refs[-1]
    pad = jnp.zeros((128 - TILE, 128), F32)
    for b in range(nb):
        new_t = jnp.concatenate([new_ref[TILE * b:TILE * (b + 1), :], pad], axis=0).T
        _shift_cache(c_ref[b], new_t, o_ref, b, w)


def _sample_attn_body(*refs, nb, has_sink, n_alias, w, batches=None, write_k=True):
    if has_sink:
        sink_ref, refs = refs[0], refs[1:]
    q0_ref, q1_ref, kn_ref, vn_ref, kc_ref, vc_ref, mc_ref, mn_ref = refs[:8]
    if write_k:
        o_ref, ko_ref, vo_ref = refs[8 + n_alias:]
    else:
        (o_ref, vo_ref), ko_ref = refs[8 + n_alias:], None
    g_lo = lax.broadcasted_iota(jnp.int32, (TILE, 128), 1) < HEAD_DIM
    mult_c = mc_ref[...]
    mult_n = mn_ref[...]
    pad = jnp.zeros((128 - TILE, 128), F32)
    for b in (range(nb) if batches is None else batches):
        parts = []
        for q_ref in (q0_ref, q1_ref):
            qr = q_ref[TILE * b:TILE * (b + 1), :]
            parts += [jnp.where(g_lo, qr, 0.0), jnp.where(g_lo, 0.0, qr)]
        qm = jnp.concatenate(parts, axis=0).astype(BF16)
        kct = kc_ref[b]
        vct = vc_ref[b]
        knp = jnp.concatenate([kn_ref[TILE * b:TILE * (b + 1), :], pad], axis=0)
        vnp = jnp.concatenate([vn_ref[TILE * b:TILE * (b + 1), :], pad], axis=0)
        knt = knp.T
        vnt = vnp.T
        sc = jnp.where(mult_c > 0, jnp.dot(qm, kct.astype(BF16), preferred_element_type=F32), -jnp.inf)
        sn = jnp.where(mult_n > 0, jnp.dot(qm, knt.astype(BF16), preferred_element_type=F32), -jnp.inf)
        m = jnp.maximum(jnp.max(sc, axis=-1, keepdims=True), jnp.max(sn, axis=-1, keepdims=True))
        if has_sink:
            sk = sink_ref[:, :1]
            m = jnp.maximum(m, sk)
        ec = mult_c * jnp.exp(sc - m)
        en = mult_n * jnp.exp(sn - m)
        den = jnp.sum(ec, axis=-1, keepdims=True) + jnp.sum(en, axis=-1, keepdims=True)
        if has_sink:
            den = den + jnp.exp(sk - m)
        o = (_dot_nt(ec.astype(BF16), vct.astype(BF16))
             + jnp.dot(en.astype(BF16), vnp.astype(BF16), preferred_element_type=F32)) / den
        o_ref[TILE * b:TILE * (b + 1), :] = jnp.concatenate(
            [jnp.where(g_lo, o[0:TILE], o[TILE:2 * TILE]),
             jnp.where(g_lo, o[2 * TILE:3 * TILE], o[3 * TILE:4 * TILE])], axis=1).astype(o_ref.dtype)
        if write_k:
            _shift_cache(kct, knt, ko_ref, b, w)
        _shift_cache(vct, vnt, vo_ref, b, w)


def _sample_attn(q0, q1, kn, vn, cache_k, cache_v, layer, mult_c, mult_n, sinks_rows, nb, prev_out):
    rows = q0.shape[0]
    nbatch = rows // TILE
    depth, _, _, w = cache_k.shape
    row_spec = lambda width: pl.BlockSpec((TILE * nb, width), lambda i: (i, 0))
    cache_spec = pl.BlockSpec((None, nb, KV_WIDTH, w), lambda i: (layer, i, 0, 0))
    in_specs = [row_spec(128), row_spec(128), row_spec(KV_WIDTH), row_spec(KV_WIDTH), cache_spec, cache_spec,
                _const_spec((4 * TILE, w)), _const_spec((4 * TILE, 128))]
    args = [q0, q1, kn, vn, cache_k, cache_v, mult_c, mult_n]
    if sinks_rows is not None:
        in_specs = [_const_spec((4 * TILE, 128))] + in_specs
        args = [sinks_rows] + args
    aliases = {}
    if prev_out is not None:
        aliases = {len(args): 1, len(args) + 1: 2}
        in_specs = in_specs + [pl.BlockSpec(memory_space=pl.ANY)] * 2
        args = args + list(prev_out)
    cache_shape = jax.ShapeDtypeStruct(cache_k.shape, F32)
    return pl.pallas_call(
        functools.partial(_sample_attn_body, nb=nb, has_sink=sinks_rows is not None,
                          n_alias=0 if prev_out is None else 2, w=w),
        grid=(nbatch // nb,), in_specs=in_specs, out_specs=[row_spec(Q_WIDTH), cache_spec, cache_spec],
        out_shape=[jax.ShapeDtypeStruct((rows, Q_WIDTH), BF16), cache_shape, cache_shape],
        input_output_aliases=aliases,
        compiler_params=_params(("parallel",)),
        name=f"sample_attn_w{w}",
    )(*args)


def _sample_mult_tables():
    t = np.arange(TILE) - TOK0
    tq = np.maximum(t, 0)[:, None]
    j = np.arange(A_WINDOW)[None, :]
    da = A_WINDOW + tq - j
    ma_c = ((da >= 0) & (da < A_WINDOW)).astype(np.float32)
    tn = (np.arange(128) - TOK0)[None, :]
    dn = tq - tn
    new_ok = (tn >= 0) & (tn < TILE - TOK0) & (dn >= 0)
    ma_n = (new_ok & (dn < A_WINDOW)).astype(np.float32)

    def mult(d):
        out = np.zeros(d.shape, np.float32)
        for w, dil in C_PATTERNS:
            out += ((d >= 0) & (d <= w) & (d % dil == 0)).astype(np.float32)
        return out

    jc = np.arange(C_SPAN)[None, :]
    mc_c = mult(C_SPAN + tq - jc)
    mc_n = np.where(new_ok, mult(dn), 0.0).astype(np.float32)
    tile4 = lambda a: jnp.asarray(np.tile(a, (4, 1)))
    return tile4(ma_c), tile4(ma_n), tile4(mc_c), tile4(mc_n)


def _ssd_intra(act, dt_raw, dtb, alog, tri_sel, tri_mask, expand, row_ok):
    xs = act[:, :B_INNER]
    bb = act[:, B_INNER:B_INNER + 2 * B_STATE].astype(BF16)
    cb_ = act[:, B_INNER + 2 * B_STATE:].astype(BF16)
    dtv = _softplus(dt_raw + dtb)
    if row_ok is not None:
        dtv = jnp.where(row_ok, dtv, 0.0)
    a = -jnp.exp(alog) * dtv
    acum = _sel_dot_rhs(tri_sel, a)
    acum_t = acum.T
    dt_e = _sel_dot_lhs(dtv, expand)
    xdt = xs * dt_e
    lane_lo = lax.broadcasted_iota(jnp.int32, (BLK, 128), 1) < HEAD_DIM
    ys = []
    for g in range(2):
        cbm = _dot_nt(cb_[:, 128 * g:128 * (g + 1)], bb[:, 128 * g:128 * (g + 1)])
        for pair in range(2):
            xp = xdt[:, 128 * (2 * g + pair):128 * (2 * g + pair + 1)]
            acc = None
            for j in range(2):
                h = 4 * g + 2 * pair + j
                seg = acum[:, h:h + 1] - acum_t[h:h + 1, :]
                lm = jnp.exp(jnp.where(tri_mask, seg, -jnp.inf))
                mh = (cbm * lm).astype(BF16)
                xh = jnp.where(lane_lo if j == 0 else ~lane_lo, xp, 0.0).astype(BF16)
                t = jnp.dot(mh, xh, preferred_element_type=F32)
                acc = t if acc is None else acc + t
            ys.append(acc)
    ydiag = jnp.concatenate(ys, axis=1)
    return xs, bb, cb_, xdt, acum, acum_t, ydiag


def _ssd_finish(y, z, norm_w):
    y = y * _silu(z)
    ms = jnp.mean(y * y, axis=-1, keepdims=True)
    return (y * lax.rsqrt(ms + NORM_EPS) * norm_w).astype(BF16)


def _conv_act(xp_ref, cw_ref, cb_ref, r0=0):
    t = BLK
    base = r0 + TILE - (CONV_K - 1)
    out = cb_ref[...] + xp_ref[base:base + t, :] * cw_ref[0:1, :]
    for j in range(1, CONV_K):
        out = out + xp_ref[base + j:base + j + t, :] * cw_ref[j:j + 1, :]
    return _silu(out)


def _split2_dot(v, sel):
    hi = v.astype(BF16)
    lo = (v - hi.astype(F32)).astype(BF16)
    return jnp.dot(hi, sel, preferred_element_type=F32) + jnp.dot(lo, sel, preferred_element_type=F32)


def _ssd_prompt_chunk(act, z, dt_raw, dtb, alog, dskip, nw, expand, expand_t, hs_ref):
    t = BLK
    row = lax.broadcasted_iota(jnp.int32, (t, t), 0)
    col = lax.broadcasted_iota(jnp.int32, (t, t), 1)
    tri_mask = col <= row
    xs = act[:, :B_INNER]
    bb = act[:, B_INNER:B_INNER + 2 * B_STATE].astype(BF16)
    cb_ = act[:, B_INNER + 2 * B_STATE:].astype(BF16)
    dtv = _softplus(dt_raw + dtb)
    a = -jnp.exp(alog) * dtv
    acum = _sel_dot_rhs(tri_mask.astype(BF16), a)
    acum_t = acum.T
    spread = _split2_dot(jnp.concatenate([dtv, jnp.exp(acum), jnp.exp(acum[t - 1:t, :] - acum)], axis=0), expand)
    dt_e, e_all, dec_e = spread[0:t], spread[t:2 * t], spread[2 * t:3 * t]
    xdt = xs * dt_e
    lane_lo = lax.broadcasted_iota(jnp.int32, (BLK, 128), 1) < HEAD_DIM
    ys = []
    for g in range(2):
        cbm = _dot_nt(cb_[:, 128 * g:128 * (g + 1)], bb[:, 128 * g:128 * (g + 1)])
        for pair in range(2):
            xp = xdt[:, 128 * (2 * g + pair):128 * (2 * g + pair + 1)]
            acc = None
            for j in range(2):
                h = 4 * g + 2 * pair + j
                seg = acum[:, h:h + 1] - acum_t[h:h + 1, :]
                lm = jnp.exp(jnp.where(tri_mask, seg, -jnp.inf))
                mh = (cbm * lm).astype(BF16)
                xh = jnp.where(lane_lo if j == 0 else ~lane_lo, xp, 0.0).astype(BF16)
                part = jnp.dot(mh, xh, preferred_element_type=F32)
                acc = part if acc is None else acc + part
            ys.append(acc)
    ydiag = jnp.concatenate(ys, axis=1)

    hst = hs_ref[...]
    hb = hst.astype(BF16)
    yoff = jnp.concatenate([_dot_nt(cb_[:, 128 * g:128 * (g + 1)], hb[256 * g:256 * (g + 1), :]) for g in range(2)],
                           axis=1)
    y = ydiag + yoff * e_all + xs * dskip
    out = _ssd_finish(y, z, nw)

    xw = xdt * dec_e
    last_t = jnp.exp(jnp.broadcast_to(acum_t[:, t - 1:t], (128, 128)))
    hi = last_t.astype(BF16)
    lo = (last_t - hi.astype(F32)).astype(BF16)
    cd = (jnp.dot(expand_t, hi, preferred_element_type=F32)
          + jnp.dot(expand_t, lo, preferred_element_type=F32))
    for g in range(2):
        xw_t = xw[:, 256 * g:256 * (g + 1)].T.astype(BF16)
        dh = jnp.dot(xw_t, bb[:, 128 * g:128 * (g + 1)], preferred_element_type=F32)
        hs_ref[256 * g:256 * (g + 1), :] = hst[256 * g:256 * (g + 1), :] * cd[256 * g:256 * (g + 1), :] + dh
    return out


def _ssd_prompt_body(xbc_ref, z_ref, dt_ref, cw_ref, cb_ref, dtb_ref, alog_ref, dskip_ref, nw_ref,
                     expand_ref, expand_t_ref, mo_ref, hs_ref, xp_ref, before_chunk=None):
    c = pl.program_id(0)
    rows = xbc_ref.shape[0]

    @pl.when(c == 0)
    def _():
        xp_ref[0:TILE, :] = jnp.zeros((TILE, CONV_DIM), F32)
        hs_ref[...] = jnp.zeros_like(hs_ref)

    xp_ref[TILE:, :] = xbc_ref[...]
    for ci in range(rows // BLK):
        r0 = BLK * ci
        if before_chunk is not None:
            before_chunk(ci)
        act = _conv_act(xp_ref, cw_ref, cb_ref, r0)
        mo_ref[r0:r0 + BLK, :] = _ssd_prompt_chunk(
            act, z_ref[r0:r0 + BLK, :], dt_ref[r0:r0 + BLK, :], dtb_ref[...], alog_ref[...], dskip_ref[...],
            nw_ref[...], expand_ref[...], expand_t_ref[...], hs_ref)
    xp_ref[0:TILE, :] = xp_ref[rows:rows + TILE, :]


_N_SSD_IN = 11
_N_SATTN_IN = 8


def _ssd_with_sample_attn_body(*refs, nb, n_alias, w):
    ssd_in = refs[:_N_SSD_IN]
    sattn_in = refs[_N_SSD_IN:_N_SSD_IN + _N_SATTN_IN + n_alias]
    mo_ref, hs_ref, o_ref, vo_ref, xp_ref = refs[_N_SSD_IN + _N_SATTN_IN + n_alias:]
    nchunk = ssd_in[0].shape[0] // BLK

    def sample_part(ci):
        share = range(ci * nb // nchunk, (ci + 1) * nb // nchunk)
        _sample_attn_body(*sattn_in, o_ref, vo_ref, nb=nb, has_sink=False, n_alias=n_alias, w=w,
                          batches=share, write_k=False)

    _ssd_prompt_body(*ssd_in, mo_ref, hs_ref, xp_ref, before_chunk=sample_part)


def _hosted_sample_attn(sample_c, steps, n_host_in, n_host_out):
    q0, q1, kn, vn, cache_k, cache_v, layer, mult_c, mult_n, prev_v = sample_c
    nbatches = q0.shape[0] // TILE
    nb = nbatches // steps
    assert nb * steps == nbatches
    w = cache_k.shape[3]
    srow_spec = lambda wd: pl.BlockSpec((TILE * nb, wd), lambda c: (c, 0))
    cache_spec = pl.BlockSpec((None, nb, KV_WIDTH, w), lambda c: (layer, c, 0, 0))
    in_specs = [srow_spec(128), srow_spec(128), srow_spec(KV_WIDTH), srow_spec(KV_WIDTH), cache_spec, cache_spec,
                _const_spec((4 * TILE, w)), _const_spec((4 * TILE, 128))]
    args = [q0, q1, kn, vn, cache_k, cache_v, mult_c, mult_n]
    assert len(args) == _N_SATTN_IN
    aliases = {}
    if prev_v is not None:
        aliases = {n_host_in + len(args): n_host_out + 1}
        in_specs = in_specs + [pl.BlockSpec(memory_space=pl.ANY)]
        args = args + [prev_v]
    out_specs = [srow_spec(Q_WIDTH), cache_spec]
    out_shape = [jax.ShapeDtypeStruct((nbatches * TILE, Q_WIDTH), BF16), jax.ShapeDtypeStruct(cache_v.shape, F32)]
    return nb, w, in_specs, args, aliases, out_specs, out_shape


def _hosted_cache_shift(shift_c, steps, n_host_in, n_host_out):
    new_rows, cache, layer, prev_out = shift_c
    nbatches = new_rows.shape[0] // TILE
    nb = nbatches // steps
    assert nb * steps == nbatches
    w = cache.shape[3]
    cache_spec = pl.BlockSpec((None, nb, KV_WIDTH, w), lambda c: (layer, c, 0, 0))
    in_specs = [pl.BlockSpec((TILE * nb, KV_WIDTH), lambda c: (c, 0)), cache_spec]
    args = [new_rows, cache]
    aliases = {}
    if prev_out is not None:
        aliases = {n_host_in + len(args): n_host_out}
        in_specs = in_specs + [pl.BlockSpec(memory_space=pl.ANY)]
        args = args + [prev_out]
    return nb, w, in_specs, args, aliases, [cache_spec], [jax.ShapeDtypeStruct(cache.shape, F32)]


def _ssd_prompt(xbc, z, dt, lw, sample_c):
    L = xbc.shape[0]
    rows = SSD_ROWS
    steps = L // rows
    row_spec = lambda wd: pl.BlockSpec((rows, wd), lambda c: (c, 0))
    in_specs = [row_spec(CONV_DIM), row_spec(B_INNER), row_spec(128),
                _const_spec((CONV_K, CONV_DIM)), _const_spec((1, CONV_DIM)), _const_spec((1, 128)),
                _const_spec((1, 128)), _const_spec((1, B_INNER)), _const_spec((1, B_INNER)),
                _const_spec((128, B_INNER)), _const_spec((B_INNER, 128))]
    args = [xbc, z, dt, lw['conv_w'], lw['conv_b'], lw['dt_bias'], lw['a_log'], lw['d_skip'], lw['ssm_norm'],
            lw['expand'], lw['expand_t']]
    assert len(args) == _N_SSD_IN
    nb, w, s_specs, s_args, aliases, s_out_specs, s_out_shape = _hosted_sample_attn(sample_c, steps, _N_SSD_IN, 2)
    return pl.pallas_call(
        functools.partial(_ssd_with_sample_attn_body, nb=nb, n_alias=len(aliases), w=w),
        grid=(steps,),
        in_specs=in_specs + s_specs,
        out_specs=[row_spec(B_INNER), pl.BlockSpec((B_INNER, B_STATE), lambda c: (0, 0))] + s_out_specs,
        out_shape=[jax.ShapeDtypeStruct((L, B_INNER), BF16),
                   jax.ShapeDtypeStruct((B_INNER, B_STATE), F32)] + s_out_shape,
        scratch_shapes=[pltpu.VMEM((rows + TILE, CONV_DIM), F32)],
        input_output_aliases=aliases,
        compiler_params=_params(("arbitrary",)),
        name="ssd_prompt",
    )(*(args + s_args))


def _ssd_sample_body(*refs):
    (xbc_ref, pre_ref, z_ref, dt_ref, h0_ref, cw_ref, cb_ref, dtb_ref, alog_ref, dskip_ref,
     nw_ref, expand_ref, expand_t_ref) = refs[:13]
    mo_ref, hout_ref, xp_ref = refs[-3:]
    t = BLK
    nbt = t // TILE
    rmod = lax.broadcasted_iota(jnp.int32, (t, 1), 0) & (TILE - 1)
    row_ok = rmod >= TOK0
    xp_ref[0:TILE, :] = jnp.zeros((TILE, CONV_DIM), F32)
    xp_ref[TILE:, :] = jnp.where(row_ok, xbc_ref[...], pre_ref[...])
    act = _conv_act(xp_ref, cw_ref, cb_ref)

    row = lax.broadcasted_iota(jnp.int32, (t, t), 0)
    col = lax.broadcasted_iota(jnp.int32, (t, t), 1)
    same = (row // TILE) == (col // TILE)
    tri_mask = (col <= row) & same
    tri_sel = tri_mask.astype(BF16)
    last_sel = (col == (row // TILE) * TILE + (TILE - 1)).astype(BF16)
    expand = expand_ref[...]
    xs, bb, cb_, xdt, acum, acum_t, ydiag = _ssd_intra(
        act, dt_ref[...], dtb_ref[...], alog_ref[...], tri_sel, tri_mask, expand, row_ok)

    yoffs = []
    for b in range(nbt):
        hb = h0_ref[b].astype(BF16)
        yoffs.append(jnp.concatenate(
            [_dot_nt(cb_[TILE * b:TILE * (b + 1), 128 * g:128 * (g + 1)], hb[256 * g:256 * (g + 1), :])
             for g in range(2)], axis=1))
    yoff = jnp.concatenate(yoffs, axis=0)
    e_all = _sel_dot_lhs(jnp.exp(acum), expand)
    y = ydiag + yoff * e_all + xs * dskip_ref[...]
    mo_ref[...] = _ssd_finish(y, z_ref[...], nw_ref[...])

    last = _sel_dot_rhs(last_sel, acum)
    dec_e = _sel_dot_lhs(jnp.exp(last - acum), expand)
    xw = xdt * dec_e
    dcol = _sel_dot_rhs(expand_t_ref[...], jnp.exp(acum_t))
    lane = lax.broadcasted_iota(jnp.int32, (256, t), 1)
    xw_ts = [xw[:, 256 * g:256 * (g + 1)].T for g in range(2)]
    for b in range(nbt):
        in_b = (lane // TILE) == b
        cd = dcol[:, TILE * b + TILE - 1:TILE * b + TILE]
        h0 = h0_ref[b]
        for g in range(2):
            lhs = jnp.where(in_b, xw_ts[g], 0.0).astype(BF16)
            dh = jnp.dot(lhs, bb[:, 128 * g:128 * (g + 1)], preferred_element_type=F32)
            hout_ref[b, 256 * g:256 * (g + 1), :] = (
                h0[256 * g:256 * (g + 1), :] * cd[256 * g:256 * (g + 1), :] + dh)


def _ssd_sample(xbc, prefix, z, dt, state, layer, lw, prev_out):
    rows = xbc.shape[0]
    nbt = BLK // TILE
    row_spec = lambda w: pl.BlockSpec((BLK, w), lambda i: (i, 0))
    st_spec = pl.BlockSpec((None, nbt, B_INNER, B_STATE), lambda i: (layer, i, 0, 0))
    in_specs = [row_spec(CONV_DIM), row_spec(CONV_DIM), row_spec(B_INNER), row_spec(128), st_spec,
                _const_spec((CONV_K, CONV_DIM)), _const_spec((1, CONV_DIM)), _const_spec((1, 128)),
                _const_spec((1, 128)), _const_spec((1, B_INNER)), _const_spec((1, B_INNER)),
                _const_spec((128, B_INNER)), _const_spec((B_INNER, 128))]
    args = [xbc, prefix, z, dt, state, lw['conv_w'], lw['conv_b'], lw['dt_bias'], lw['a_log'], lw['d_skip'],
            lw['ssm_norm'], lw['expand'], lw['expand_t']]
    aliases = {}
    if prev_out is not None:
        aliases = {len(args): 1}
        in_specs.append(pl.BlockSpec(memory_space=pl.ANY))
        args.append(prev_out)
    return pl.pallas_call(
        _ssd_sample_body,
        grid=(rows // BLK,),
        in_specs=in_specs,
        out_specs=[row_spec(B_INNER), st_spec],
        out_shape=[jax.ShapeDtypeStruct((rows, B_INNER), BF16), jax.ShapeDtypeStruct(state.shape, F32)],
        scratch_shapes=[pltpu.VMEM((BLK + TILE, CONV_DIM), F32)],
        input_output_aliases=aliases,
        compiler_params=_params(("parallel",)),
        name="ssd_sample",
    )(*args)


def _out_ffn_body(h_ref, a_ref, m_ref, c_ref, wo_ref, g2_ref, wg_ref, wu_ref, wd_ref, out_ref):
    d = functools.partial(jnp.dot, preferred_element_type=F32)
    h1 = (h_ref[...] + d(a_ref[...], wo_ref[0:256, :]) + d(m_ref[...], wo_ref[256:768, :])
          + d(c_ref[...].astype(BF16), wo_ref[768:1024, :]))
    ms = jnp.mean(h1 * h1, axis=-1, keepdims=True)
    u = (h1 * lax.rsqrt(ms + NORM_EPS) * g2_ref[...]).astype(BF16)
    out_ref[...] = h1
    for c in range(D_FF // FF_CHUNK):
        sl = slice(FF_CHUNK * c, FF_CHUNK * (c + 1))
        act = (_silu(d(u, wg_ref[:, sl])) * d(u, wu_ref[:, sl])).astype(BF16)
        out_ref[...] += d(act, wd_ref[sl, :])


_N_FFN_IN = 9


def _out_ffn_with_cache_shift_body(*refs, nb, w):
    ffn_in = refs[:_N_FFN_IN]
    out_ref, ko_ref = refs[-2:]
    _cache_shift_body(*refs[_N_FFN_IN:-2], ko_ref, nb=nb, w=w)
    _out_ffn_body(*ffn_in, out_ref)


def _out_ffn(h, a_o, m_o, c_o, lw, tm, sample_c=None):
    rows = h.shape[0]
    steps = rows // tm
    row_spec = lambda w: pl.BlockSpec((tm, w), lambda i: (i, 0))
    in_specs = [row_spec(D_MODEL), row_spec(Q_WIDTH), row_spec(B_INNER), row_spec(Q_WIDTH),
                _layer_spec((D_MODEL, D_MODEL), lw['layer']), _const_spec((1, D_MODEL)),
                _layer_spec((D_MODEL, D_FF), lw['layer']), _layer_spec((D_MODEL, D_FF), lw['layer']),
                _layer_spec((D_FF, D_MODEL), lw['layer'])]
    args = [h, a_o, m_o, c_o, lw['w_out'], lw['norm2'], lw['w_gate'], lw['w_up'], lw['w_down']]
    assert len(args) == _N_FFN_IN
    out_spec = row_spec(D_MODEL)
    out_shape = jax.ShapeDtypeStruct((rows, D_MODEL), F32)
    if sample_c is None:
        return pl.pallas_call(
            _out_ffn_body,
            grid=(steps,), in_specs=in_specs, out_specs=out_spec, out_shape=out_shape,
            compiler_params=_params(("parallel",)),
            name="out_ffn",
        )(*args)
    nb, w, s_specs, s_args, aliases, s_out_specs, s_out_shape = _hosted_cache_shift(sample_c, steps, _N_FFN_IN, 1)
    return pl.pallas_call(
        functools.partial(_out_ffn_with_cache_shift_body, nb=nb, w=w),
        grid=(steps,), in_specs=in_specs + s_specs, out_specs=[out_spec] + s_out_specs,
        out_shape=[out_shape] + s_out_shape,
        input_output_aliases=aliases,
        compiler_params=_params(("parallel",)),
        name="out_ffn_host",
    )(*(args + s_args))


def _rope_table(pos):
    half = ROT_DIM // 2
    inv = ROPE_THETA ** (-(jnp.arange(half, dtype=F32) * 2.0 / ROT_DIM))
    ang = pos.astype(F32)[None, :] * inv[:, None]
    cs = jnp.concatenate([jnp.cos(ang), jnp.sin(ang)], axis=0)
    hi = cs.astype(BF16).astype(F32)
    mid = (cs - hi).astype(BF16).astype(F32)
    lo = (cs - hi - mid).astype(BF16).astype(F32)
    return jnp.concatenate([hi, mid, lo], axis=0)


def _rope_selection():
    half = ROT_DIM // 2
    sel = np.zeros((128, 256), np.float32)
    for term in range(3):
        base = term * ROT_DIM
        for lane in range(128):
            f = lane % HEAD_DIM
            if f < half:
                sel[base + f, lane] = 1.0
                sel[base + half + f, 128 + lane] = -1.0
            elif f < ROT_DIM:
                sel[base + f - half, lane] = 1.0
                sel[base + f, 128 + lane] = 1.0
    return sel


_ROPE_SEL = _rope_selection()


def _permute_heads(m, start, axis):
    sl = lambda a, b: lax.slice_in_dim(m, a, b, axis=axis)
    pieces = [sl(0, start)] if start else []
    pieces += [sl(start, start + 64), sl(start + 128, start + 192), sl(start + 64, start + 128),
               sl(start + 192, start + 256), sl(start + 256, m.shape[axis])]
    return jnp.concatenate(pieces, axis=axis)


def _matmul_weights(w_in, w_out, w_gate, w_up, w_down):
    w = _permute_heads(_permute_heads(w_in.astype(BF16), _OFF['aq'], 2), _OFF['cq'], 2)
    w = jnp.pad(w, ((0, 0), (0, 0), (0, N_IN_PAD - N_IN)))
    wo = _permute_heads(_permute_heads(w_out.astype(BF16), 0, 1), 768, 1)
    return dict(w_in=w, w_out=wo, w_gate=w_gate.astype(BF16), w_up=w_up.astype(BF16), w_down=w_down.astype(BF16))


def _layer_weights(l, mm, norm1, a_qn, a_kn, a_sinks, c_qn, c_kn, conv_w, conv_b, dt_bias, a_log, d_skip,
                   ssm_norm, norm2):
    pad8 = lambda v: jnp.pad(v.astype(F32), (0, 128 - B_HEADS))[None, :]
    head_of = np.arange(B_INNER) // 64
    expand = (np.arange(128)[:, None] == head_of[None, :]).astype(np.float32)
    blk = (np.arange(256)[:, None] // 64 == np.arange(256)[None, :] // 64).astype(np.float32) / 64.0
    return dict(
        layer=l, norm1=norm1[l][None, :], w_in=mm['w_in'],
        a_qn=jnp.tile(a_qn[l], 4)[None, :], a_kn=jnp.tile(a_kn[l], 2)[None, :],
        c_qn=jnp.tile(c_qn[l], 4)[None, :], c_kn=jnp.tile(c_kn[l], 2)[None, :],
        bd=jnp.asarray(blk, BF16),
        sink_lanes=jnp.repeat(a_sinks[l].astype(F32)[jnp.asarray([0, 2, 1, 3])], HEAD_DIM).reshape(2, 128),
        sink_rows=jnp.broadcast_to(
            jnp.repeat(a_sinks[l].astype(F32)[jnp.asarray([0, 2, 1, 3])], TILE)[:, None], (4 * TILE, 128)),
        conv_w=conv_w[l], conv_b=conv_b[l][None, :], dt_bias=pad8(dt_bias[l]), a_log=pad8(a_log[l]),
        d_skip=jnp.repeat(d_skip[l].astype(F32), 64)[None, :], ssm_norm=ssm_norm[l][None, :],
        expand=jnp.asarray(expand, BF16), expand_t=jnp.asarray(expand.T, BF16),
        w_out=mm['w_out'], norm2=norm2[l][None, :], w_gate=mm['w_gate'], w_up=mm['w_up'], w_down=mm['w_down'])


def kernel(x_prompt, x_sample, cache_a_k, cache_a_v, cache_c_k, cache_c_v, state_ssm, state_conv, norm1, w_in,
           a_qn, a_kn, a_sinks, c_qn, c_kn, conv_w, conv_b, dt_bias, a_log, d_skip, ssm_norm, w_out, norm2,
           w_gate, w_up, w_down):
    depth = w_in.shape[0]
    batch, seq, _ = x_prompt.shape
    nbatch, dec_seq, _ = x_sample.shape
    assert batch == 1 and dec_seq == TILE - TOK0 and seq % (16 * BLK) == 0 and nbatch % (BLK // TILE) == 0
    past_len = PAST_LEN
    a_buf, c_buf = cache_a_k.shape[2], cache_c_k.shape[2]
    assert a_buf == A_WINDOW and c_buf == C_SPAN

    hp = x_prompt.reshape(seq, D_MODEL)
    hs = jnp.pad(x_sample, ((0, 0), (TOK0, 0), (0, 0))).reshape(nbatch * TILE, D_MODEL)
    tab_p = _rope_table(jnp.arange(seq))
    pos_s = past_len + jnp.maximum(jnp.arange(TILE) - TOK0, 0)
    tab_s = jnp.tile(_rope_table(pos_s), (1, nbatch))
    ma_c, ma_n, mc_c, mc_n = _sample_mult_tables()

    to_fm = lambda c: jnp.transpose(c, (0, 1, 3, 4, 2)).reshape(depth, nbatch, KV_WIDTH, c.shape[2])
    from_fm = lambda c: jnp.transpose(c.reshape(depth, nbatch, 2, HEAD_DIM, c.shape[3]), (0, 1, 4, 2, 3))
    ca_k, ca_v, cc_k, cc_v = to_fm(cache_a_k), to_fm(cache_a_v), to_fm(cache_c_k), to_fm(cache_c_v)
    st = state_ssm.reshape(depth, nbatch, B_INNER, B_STATE)

    mm = _matmul_weights(w_in, w_out, w_gate, w_up, w_down)
    p_out = [[] for _ in range(6)]
    s_conv = []
    new_a = new_ck = new_cv = new_st = None
    tm_p = 512
    tm_s = min(512, nbatch * TILE)
    for l in range(depth):
        lw = _layer_weights(l, mm, norm1, a_qn, a_kn, a_sinks, c_qn, c_kn, conv_w, conv_b, dt_bias, a_log,
                            d_skip, ssm_norm, norm2)
        s_qa0, s_qa1, s_ka, s_va, s_qc0, s_qc1, s_kc, s_vc, s_z, s_xbc, s_dt = _in_proj(hs, lw, tab_s, tm_s)

        qa0, qa1, ka, va, qc0, qc1, kc, vc, z, xbc, dt = _in_proj(hp, lw, tab_p, tm_p)
        a_o = _prompt_attn(qa0, qa1, ka, va, A_PATTERNS, sink_lanes=lw['sink_lanes'], name="attn_a")
        c_o = _prompt_attn(qc0, qc1, kc, vc, C_BANDS, name="attn_c")
        m_o, h_fin, s_c_o, new_cv = _ssd_prompt(
            xbc, z, dt, lw, (s_qc0, s_qc1, s_kc, s_vc, cc_k, cc_v, l, mc_c, mc_n, new_cv))
        hp, new_ck = _out_ffn(hp, a_o, m_o, c_o, lw, FFN_ROWS, (s_kc, cc_k, l, new_ck))
        p_out[0].append(ka[seq - a_buf:].reshape(1, a_buf, 2, HEAD_DIM))
        p_out[1].append(va[seq - a_buf:].reshape(1, a_buf, 2, HEAD_DIM))
        p_out[2].append(kc[seq - c_buf:].reshape(1, c_buf, 2, HEAD_DIM))
        p_out[3].append(vc[seq - c_buf:].reshape(1, c_buf, 2, HEAD_DIM))
        p_out[4].append(h_fin.reshape(1, B_HEADS, 64, B_STATE))
        p_out[5].append(xbc[seq - (CONV_K - 1):].reshape(1, CONV_K - 1, CONV_DIM))

        a_o, *new_a = _sample_attn(s_qa0, s_qa1, s_ka, s_va, ca_k, ca_v, l, ma_c, ma_n, lw['sink_rows'], 8, new_a)
        prefix = jnp.pad(state_conv[l], ((0, 0), (1, TILE - CONV_K), (0, 0))).reshape(nbatch * TILE, CONV_DIM)
        m_o, new_st = _ssd_sample(s_xbc, prefix, s_z, s_dt, st, l, lw, new_st)
        hs = _out_ffn(hs, a_o, m_o, s_c_o, lw, tm_s)
        s_conv.append(s_xbc.reshape(nbatch, TILE, CONV_DIM)[:, TILE - (CONV_K - 1):])

    outs_p = [jnp.stack(t, axis=0) for t in p_out]
    outs_s = [from_fm(new_a[0]), from_fm(new_a[1]), from_fm(new_ck), from_fm(new_cv),
              new_st.reshape(depth, nbatch, B_HEADS, 64, B_STATE), jnp.stack(s_conv, axis=0)]
    y_p = hp.reshape(1, seq, D_MODEL)
    y_s = hs.reshape(nbatch, TILE, D_MODEL)[:, TOK0:]
    return (y_p, y_s, *outs_p, *outs_s)
```

```python
import functools
import math

import numpy as np
import jax
import jax.numpy as jnp
from jax import lax
from jax.experimental import pallas as pl
from jax.experimental.pallas import tpu as pltpu

F32 = jnp.float32
BF16 = jnp.bfloat16

D_MODEL = 1024
HEAD_DIM = 64
ROT_DIM = 16
ROPE_THETA = 500000.0
NORM_EPS = 1e-6
Q_WIDTH = 256
KV_WIDTH = 128
PAST_LEN = 16384
A_WINDOW = 128
C_PATTERNS = ((128, 1), (512, 4), (2048, 16))
C_SPAN = 2048
B_HEADS = 8
B_INNER = 512
B_STATE = 128
CONV_K = 4
CONV_DIM = 1024
D_FF = 2816
N_IN = 2568
N_IN_PAD = 2688
BLK = 128
ATTN_SUPER = 2048
A_PATTERNS = ((1, A_WINDOW - 1),)
C_BANDS = tuple((d, w // d) for w, d in C_PATTERNS)
TOK0 = 4
TILE = 8
FF_CHUNK = 256
SSD_ROWS = 256
FFN_ROWS = 512
VMEM_LIMIT = 56 * 1024 * 1024

_OFF = dict(aq=0, ak=256, av=384, cq=512, ck=768, cv=896, z=1024, xbc=1536, dt=2560, end=N_IN_PAD)
_HEAD_PERM = np.concatenate([np.arange(0, 64), np.arange(128, 192), np.arange(64, 128), np.arange(192, 256)])


def _const_spec(shape):
    nd = len(shape)
    return pl.BlockSpec(shape, lambda *_: (0,) * nd, pipeline_mode=pl.Buffered(1))


def _layer_spec(shape, layer):
    nd = len(shape)
    return pl.BlockSpec((None,) + tuple(shape), lambda *_: (layer,) + (0,) * nd, pipeline_mode=pl.Buffered(1))


def _params(sem):
    return pltpu.CompilerParams(dimension_semantics=sem, vmem_limit_bytes=VMEM_LIMIT)


def _split3(v):
    hi = v.astype(BF16)
    r1 = v - hi.astype(F32)
    mid = r1.astype(BF16)
    lo = (r1 - mid.astype(F32)).astype(BF16)
    return hi, mid, lo


def _sel_dot_rhs(sel, v):
    hi, mid, lo = _split3(v)
    d = functools.partial(jnp.dot, preferred_element_type=F32)
    return d(sel, hi) + d(sel, mid) + d(sel, lo)


def _sel_dot_lhs(v, sel):
    hi, mid, lo = _split3(v)
    d = functools.partial(jnp.dot, preferred_element_type=F32)
    return d(hi, sel) + d(mid, sel) + d(lo, sel)


def _dot_nt(a, b):
    return lax.dot_general(a, b, (((1,), (1,)), ((), ())), preferred_element_type=F32)


def _silu(x):
    return x * jax.nn.sigmoid(x)


def _softplus(x):
    return jnp.maximum(x, 0.0) + jnp.log(1.0 + jnp.exp(-jnp.abs(x)))


def _head_norm(x, gain, bd):
    x2 = x * x
    hi = x2.astype(BF16)
    lo = (x2 - hi.astype(F32)).astype(BF16)
    ms = jnp.dot(hi, bd, preferred_element_type=F32) + jnp.dot(lo, bd, preferred_element_type=F32)
    return x * lax.rsqrt(ms + NORM_EPS) * gain


def _rope(x, cos_t, sin_t):
    w = x.shape[1]
    lane = lax.broadcasted_iota(jnp.int32, x.shape, 1) & (HEAD_DIM - 1)
    partner = jnp.where(lane < ROT_DIM // 2, pltpu.roll(x, w - ROT_DIM // 2, 1), pltpu.roll(x, ROT_DIM // 2, 1))
    return x * cos_t + partner * sin_t


def _in_proj_body(h_ref, g1_ref, w_ref, cs_ref, sel_ref, gqa_ref, gka_ref, gqc_ref, gkc_ref, bd_ref,
                  qa0_ref, qa1_ref, ka_ref, va_ref, qc0_ref, qc1_ref, kc_ref, vc_ref, z_ref, xbc_ref, dt_ref):
    x = h_ref[...]
    tm = x.shape[0]
    ms = jnp.mean(x * x, axis=-1, keepdims=True)
    u = (x * lax.rsqrt(ms + NORM_EPS) * g1_ref[...]).astype(BF16)

    def proj(name, nxt):
        return jnp.dot(u, w_ref[:, _OFF[name]:_OFF[nxt]], preferred_element_type=F32)

    xbc_ref[...] = proj('xbc', 'dt')
    z_ref[...] = proj('z', 'xbc')
    dt_ref[...] = proj('dt', 'end')
    cs_t = jnp.concatenate([cs_ref[...], jnp.zeros((128 - 3 * ROT_DIM, tm), F32)], axis=0).T.astype(BF16)
    tab = jnp.dot(cs_t, sel_ref[...], preferred_element_type=F32)
    unrot = ((lax.broadcasted_iota(jnp.int32, (1, 128), 1) & (HEAD_DIM - 1)) >= ROT_DIM).astype(F32)
    c1, s1 = tab[:, :128] + unrot, tab[:, 128:]
    c2 = jnp.concatenate([c1, c1], axis=1)
    s2 = jnp.concatenate([s1, s1], axis=1)
    bd2 = bd_ref[...]
    bd1 = bd2[:128, :128]
    qa = _rope(_head_norm(proj('aq', 'ak'), gqa_ref[...], bd2), c2, s2) * 0.125
    qa0_ref[...] = qa[:, :128]
    qa1_ref[...] = qa[:, 128:]
    kva = proj('ak', 'cq')
    ka_ref[...] = _rope(_head_norm(kva[:, :128], gka_ref[...], bd1), c1, s1)
    va_ref[...] = kva[:, 128:]
    qc = _rope(_head_norm(proj('cq', 'ck'), gqc_ref[...], bd2), c2, s2) * 0.125
    qc0_ref[...] = qc[:, :128]
    qc1_ref[...] = qc[:, 128:]
    kvc = proj('ck', 'z')
    kc_ref[...] = _rope(_head_norm(kvc[:, :128], gkc_ref[...], bd1), c1, s1)
    vc_ref[...] = kvc[:, 128:]


def _in_proj(h, lw, tab, tm):
    rows = h.shape[0]
    grid = (rows // tm,)
    row_spec = lambda w: pl.BlockSpec((tm, w), lambda i: (i, 0))
    widths = (128, 128, KV_WIDTH, KV_WIDTH, 128, 128, KV_WIDTH, KV_WIDTH, B_INNER, CONV_DIM, 128)
    dtypes = (F32,) * len(widths)
    return pl.pallas_call(
        _in_proj_body,
        grid=grid,
        in_specs=[row_spec(D_MODEL), _const_spec((1, D_MODEL)), _layer_spec((D_MODEL, N_IN_PAD), lw['layer']),
                  pl.BlockSpec((3 * ROT_DIM, tm), lambda i: (0, i)), _const_spec((128, 256)),
                  _const_spec((1, 256)), _const_spec((1, 128)), _const_spec((1, 256)),
                  _const_spec((1, 128)), _const_spec((256, 256))],
        out_specs=[row_spec(w) for w in widths],
        out_shape=[jax.ShapeDtypeStruct((rows, w), dt) for w, dt in zip(widths, dtypes)],
        compiler_params=_params(("parallel",)),
        name="in_proj",
    )(h, lw['norm1'], lw['w_in'], tab, jnp.asarray(_ROPE_SEL, BF16), lw['a_qn'], lw['a_kn'], lw['c_qn'], lw['c_kn'], lw['bd'])


def _attn_body(*refs, patterns, has_sink, sb):
    if has_sink:
        sink_ref, refs = refs[0], refs[1:]
    q0_ref, q1_ref, kp_ref, kc_ref, vp_ref, vc_ref, o_ref, kk, vv, acc_s, m_s, l_s = refs
    q_refs = (q0_ref, q1_ref)
    j = pl.program_id(0)
    kk[0:sb, :] = kp_ref[...]
    kk[sb:2 * sb, :] = kc_ref[...]
    vv[0:sb, :] = vp_ref[...]
    vv[sb:2 * sb, :] = vc_ref[...]
    row4 = lax.broadcasted_iota(jnp.int32, (4 * BLK, BLK), 0) & (BLK - 1)
    col4 = lax.broadcasted_iota(jnp.int32, (4 * BLK, BLK), 1)
    upper4 = col4 > row4
    g_lo = lax.broadcasted_iota(jnp.int32, (BLK, 128), 1) < HEAD_DIM
    ones_cols = jnp.ones((2 * BLK, 128), BF16)
    nblk = sb // BLK

    diag_here = [md == BLK for _, md in patterns]
    boosts = [[] for _ in patterns]
    for p, (d, md) in enumerate(patterns):
        for q in range(p + 1, len(patterns)):
            dq, mdq = patterns[q]
            if diag_here[p] and (md * d) % dq == 0 and (md * d) // dq < min(BLK, mdq + 1):
                boosts[q].append((md * d) // dq)
                diag_here[p] = False

    for pi, (d, max_dist) in enumerate(patterns):
        nsub = nblk // d
        has_diag = diag_here[pi]
        fdist = jnp.where(upper4, row4 + BLK - col4, row4 - col4)
        mult = jnp.ones((4 * BLK, BLK), F32)
        for f in boosts[pi]:
            mult = mult + (fdist == f).astype(F32)
        w_up = jnp.where(upper4, mult, 0.0)
        w_lo = jnp.where(upper4, 0.0, mult)

        def ld(ref, s0, d=d):
            if d == 1:
                return ref[pl.ds(s0, BLK), :]
            return ref[pl.ds(s0, BLK, stride=d), :]

        def block(t, carry, d=d, nsub=nsub, has_diag=has_diag, first=(pi == 0), ld=ld, w_up=w_up, w_lo=w_lo):
            r_ = t // nsub
            n = t - r_ * nsub
            start = r_ + BLK * d * n
            prev_ok = jnp.logical_or(j > 0, n > 0)
            neg = jnp.where(prev_ok, 0.0, -jnp.inf)
            qb = [ld(q_refs[rr], start) for rr in range(2)]
            kprev, kcur = ld(kk, sb + start - BLK * d), ld(kk, sb + start)
            vprev, vcur = ld(vv, sb + start - BLK * d), ld(vv, sb + start)
            qm = jnp.concatenate([jnp.where(g_lo, qb[0], 0.0), jnp.where(g_lo, 0.0, qb[0]),
                                  jnp.where(g_lo, qb[1], 0.0), jnp.where(g_lo, 0.0, qb[1])], axis=0).astype(BF16)
            kcat = jnp.concatenate([kprev, kcur], axis=0).astype(BF16)
            vcat = jnp.concatenate([jnp.concatenate([vprev, vcur], axis=0).astype(BF16), ones_cols], axis=1)
            s2 = _dot_nt(qm, kcat)
            sp, sc = s2[:, :BLK] + neg, s2[:, BLK:]
            s = jnp.where(upper4, sp, sc)
            mb = jnp.max(s, axis=-1, keepdims=True)
            if has_diag:
                sd = jnp.sum(jnp.where(col4 == row4, sp, 0.0), axis=-1, keepdims=True)
                mb = jnp.maximum(mb, sd)
            e = jnp.exp(s - mb)
            ecat = jnp.concatenate([e * w_up, e * w_lo], axis=1).astype(BF16)
            pvl = jnp.dot(ecat, vcat, preferred_element_type=F32)
            pv, lb = pvl[:, :128], pvl[:, 128:]
            if has_diag:
                ed = jnp.exp(sd - mb)
                lb = lb + ed
                pv = pv + ed * jnp.concatenate([vprev] * 4, axis=0)
            for rr in range(2):
                lo, hi = slice(2 * BLK * rr, 2 * BLK * rr + BLK), slice(2 * BLK * rr + BLK, 2 * BLK * (rr + 1))
                o_b = jnp.where(g_lo, pv[lo], pv[hi])
                m_b = jnp.where(g_lo, mb[lo], mb[hi])
                l_b = jnp.where(g_lo, lb[lo], lb[hi])
                rows = pl.ds(start, BLK) if d == 1 else pl.ds(start, BLK, stride=d)
                if first:
                    m_s[rr, rows, :] = m_b
                    l_s[rr, rows, :] = l_b
                    acc_s[rr, rows, :] = o_b
                else:
                    m_old = m_s[rr, rows, :]
                    m_new = jnp.maximum(m_old, m_b)
                    w_old = jnp.exp(m_old - m_new)
                    w_b = jnp.exp(m_b - m_new)
                    m_s[rr, rows, :] = m_new
                    l_s[rr, rows, :] = w_old * l_s[rr, rows, :] + w_b * l_b
                    acc_s[rr, rows, :] = w_old * acc_s[rr, rows, :] + w_b * o_b
            return carry

        lax.fori_loop(0, nblk, block, 0, unroll={1: 16, 4: 8}.get(d, 4))

    chunk = 256
    for rr in range(2):
        for c in range(sb // chunk):
            rows = slice(chunk * c, chunk * (c + 1))
            m_f, l_f, a_f = m_s[rr, rows, :], l_s[rr, rows, :], acc_s[rr, rows, :]
            if has_sink:
                sk = sink_ref[rr:rr + 1, :]
                m2 = jnp.maximum(m_f, sk)
                w = jnp.exp(m_f - m2)
                o = a_f * w / (l_f * w + jnp.exp(sk - m2))
            else:
                o = a_f / l_f
            o_ref[rows, 128 * rr:128 * (rr + 1)] = o.astype(o_ref.dtype)


def _prompt_attn(q0, q1, k, v, patterns, sink_lanes=None, name="attn"):
    L = q0.shape[0]
    sb = ATTN_SUPER
    cur = lambda j: (j, 0)
    prev = lambda j: (jnp.maximum(j - 1, 0), 0)
    blk = lambda im: pl.BlockSpec((sb, 128), im)
    in_specs = [blk(cur), blk(cur), blk(prev), blk(cur), blk(prev), blk(cur)]
    args = [q0, q1, k, k, v, v]
    if sink_lanes is not None:
        in_specs = [_const_spec((2, 128))] + in_specs
        args = [sink_lanes] + args
    return pl.pallas_call(
        functools.partial(_attn_body, patterns=patterns, has_sink=sink_lanes is not None, sb=sb),
        grid=(L // sb,), in_specs=in_specs, out_specs=pl.BlockSpec((sb, Q_WIDTH), cur),
        out_shape=jax.ShapeDtypeStruct((L, Q_WIDTH), BF16),
        scratch_shapes=[pltpu.VMEM((2 * sb, 128), F32), pltpu.VMEM((2 * sb, 128), F32),
                        pltpu.VMEM((2, sb, 128), F32), pltpu.VMEM((2, sb, 128), F32), pltpu.VMEM((2, sb, 128), F32)],
        compiler_params=_params(("parallel",)),
        name=name,
    )(*args)


def _shift_cache(src, new_t, dst, b, w):
    lane = lax.broadcasted_iota(jnp.int32, (128, 128), 1)
    is_new = lane >= 128 - (TILE - TOK0)
    shifted = pltpu.roll(src, w - (TILE - TOK0), 1)
    new_cols = pltpu.roll(new_t, 128 - TILE, 1)
    if w > 128:
        dst[b, :, 0:w - 128] = shifted[:, 0:w - 128]
    dst[b, :, w - 128:w] = jnp.where(is_new, new_cols, shifted[:, w - 128:w])


def _cache_shift_body(*refs, nb, w):
    new_ref, c_ref, o_ref = refs[0], refs[1], refs[-1]
    pad = jnp.zeros((128 - TILE, 128), F32)
    for b in range(nb):
        new_t = jnp.concatenate([new_ref[TILE * b:TILE * (b + 1), :], pad], axis=0).T
        _shift_cache(c_ref[b], new_t, o_ref, b, w)


def _sample_attn_body(*refs, nb, has_sink, n_alias, w, batches=None, write_k=True):
    if has_sink:
        sink_ref, refs = refs[0], refs[1:]
    q0_ref, q1_ref, kn_ref, vn_ref, kc_ref, vc_ref, mc_ref, mn_ref = refs[:8]
    if write_k:
        o_ref, ko_ref, vo_ref = refs[8 + n_alias:]
    else:
        (o_ref, vo_ref), ko_ref = refs[8 + n_alias:], None
    g_lo = lax.broadcasted_iota(jnp.int32, (TILE, 128), 1) < HEAD_DIM
    mult_c = mc_ref[...]
    mult_n = mn_ref[...]
    pad = jnp.zeros((128 - TILE, 128), F32)
    for b in (range(nb) if batches is None else batches):
        parts = []
        for q_ref in (q0_ref, q1_ref):
            qr = q_ref[TILE * b:TILE * (b + 1), :]
            parts += [jnp.where(g_lo, qr, 0.0), jnp.where(g_lo, 0.0, qr)]
        qm = jnp.concatenate(parts, axis=0).astype(BF16)
        kct = kc_ref[b]
        vct = vc_ref[b]
        knp = jnp.concatenate([kn_ref[TILE * b:TILE * (b + 1), :], pad], axis=0)
        vnp = jnp.concatenate([vn_ref[TILE * b:TILE * (b + 1), :], pad], axis=0)
        knt = knp.T
        vnt = vnp.T
        sc = jnp.where(mult_c > 0, jnp.dot(qm, kct.astype(BF16), preferred_element_type=F32), -jnp.inf)
        sn = jnp.where(mult_n > 0, jnp.dot(qm, knt.astype(BF16), preferred_element_type=F32), -jnp.inf)
        m = jnp.maximum(jnp.max(sc, axis=-1, keepdims=True), jnp.max(sn, axis=-1, keepdims=True))
        if has_sink:
            sk = sink_ref[:, :1]
            m = jnp.maximum(m, sk)
        ec = mult_c * jnp.exp(sc - m)
        en = mult_n * jnp.exp(sn - m)
        den = jnp.sum(ec, axis=-1, keepdims=True) + jnp.sum(en, axis=-1, keepdims=True)
        if has_sink:
            den = den + jnp.exp(sk - m)
        o = (_dot_nt(ec.astype(BF16), vct.astype(BF16))
             + jnp.dot(en.astype(BF16), vnp.astype(BF16), preferred_element_type=F32)) / den
        o_ref[TILE * b:TILE * (b + 1), :] = jnp.concatenate(
            [jnp.where(g_lo, o[0:TILE], o[TILE:2 * TILE]),
             jnp.where(g_lo, o[2 * TILE:3 * TILE], o[3 * TILE:4 * TILE])], axis=1).astype(o_ref.dtype)
        if write_k:
            _shift_cache(kct, knt, ko_ref, b, w)
        _shift_cache(vct, vnt, vo_ref, b, w)


def _sample_attn(q0, q1, kn, vn, cache_k, cache_v, layer, mult_c, mult_n, sinks_rows, nb, prev_out):
    rows = q0.shape[0]
    nbatch = rows // TILE
    depth, _, _, w = cache_k.shape
    row_spec = lambda width: pl.BlockSpec((TILE * nb, width), lambda i: (i, 0))
    cache_spec = pl.BlockSpec((None, nb, KV_WIDTH, w), lambda i: (layer, i, 0, 0))
    in_specs = [row_spec(128), row_spec(128), row_spec(KV_WIDTH), row_spec(KV_WIDTH), cache_spec, cache_spec,
                _const_spec((4 * TILE, w)), _const_spec((4 * TILE, 128))]
    args = [q0, q1, kn, vn, cache_k, cache_v, mult_c, mult_n]
    if sinks_rows is not None:
        in_specs = [_const_spec((4 * TILE, 128))] + in_specs
        args = [sinks_rows] + args
    aliases = {}
    if prev_out is not None:
        aliases = {len(args): 1, len(args) + 1: 2}
        in_specs = in_specs + [pl.BlockSpec(memory_space=pl.ANY)] * 2
        args = args + list(prev_out)
    cache_shape = jax.ShapeDtypeStruct(cache_k.shape, F32)
    return pl.pallas_call(
        functools.partial(_sample_attn_body, nb=nb, has_sink=sinks_rows is not None,
                          n_alias=0 if prev_out is None else 2, w=w),
        grid=(nbatch // nb,), in_specs=in_specs, out_specs=[row_spec(Q_WIDTH), cache_spec, cache_spec],
        out_shape=[jax.ShapeDtypeStruct((rows, Q_WIDTH), BF16), cache_shape, cache_shape],
        input_output_aliases=aliases,
        compiler_params=_params(("parallel",)),
        name=f"sample_attn_w{w}",
    )(*args)


def _sample_mult_tables():
    t = np.arange(TILE) - TOK0
    tq = np.maximum(t, 0)[:, None]
    j = np.arange(A_WINDOW)[None, :]
    da = A_WINDOW + tq - j
    ma_c = ((da >= 0) & (da < A_WINDOW)).astype(np.float32)
    tn = (np.arange(128) - TOK0)[None, :]
    dn = tq - tn
    new_ok = (tn >= 0) & (tn < TILE - TOK0) & (dn >= 0)
    ma_n = (new_ok & (dn < A_WINDOW)).astype(np.float32)

    def mult(d):
        out = np.zeros(d.shape, np.float32)
        for w, dil in C_PATTERNS:
            out += ((d >= 0) & (d <= w) & (d % dil == 0)).astype(np.float32)
        return out

    jc = np.arange(C_SPAN)[None, :]
    mc_c = mult(C_SPAN + tq - jc)
    mc_n = np.where(new_ok, mult(dn), 0.0).astype(np.float32)
    tile4 = lambda a: jnp.asarray(np.tile(a, (4, 1)))
    return tile4(ma_c), tile4(ma_n), tile4(mc_c), tile4(mc_n)


def _ssd_intra(act, dt_raw, dtb, alog, tri_sel, tri_mask, expand, row_ok):
    xs = act[:, :B_INNER]
    bb = act[:, B_INNER:B_INNER + 2 * B_STATE].astype(BF16)
    cb_ = act[:, B_INNER + 2 * B_STATE:].astype(BF16)
    dtv = _softplus(dt_raw + dtb)
    if row_ok is not None:
        dtv = jnp.where(row_ok, dtv, 0.0)
    a = -jnp.exp(alog) * dtv
    acum = _sel_dot_rhs(tri_sel, a)
    acum_t = acum.T
    dt_e = _sel_dot_lhs(dtv, expand)
    xdt = xs * dt_e
    lane_lo = lax.broadcasted_iota(jnp.int32, (BLK, 128), 1) < HEAD_DIM
    ys = []
    for g in range(2):
        cbm = _dot_nt(cb_[:, 128 * g:128 * (g + 1)], bb[:, 128 * g:128 * (g + 1)])
        for pair in range(2):
            xp = xdt[:, 128 * (2 * g + pair):128 * (2 * g + pair + 1)]
            acc = None
            for j in range(2):
                h = 4 * g + 2 * pair + j
                seg = acum[:, h:h + 1] - acum_t[h:h + 1, :]
                lm = jnp.exp(jnp.where(tri_mask, seg, -jnp.inf))
                mh = (cbm * lm).astype(BF16)
                xh = jnp.where(lane_lo if j == 0 else ~lane_lo, xp, 0.0).astype(BF16)
                t = jnp.dot(mh, xh, preferred_element_type=F32)
                acc = t if acc is None else acc + t
            ys.append(acc)
    ydiag = jnp.concatenate(ys, axis=1)
    return xs, bb, cb_, xdt, acum, acum_t, ydiag


def _ssd_finish(y, z, norm_w):
    y = y * _silu(z)
    ms = jnp.mean(y * y, axis=-1, keepdims=True)
    return (y * lax.rsqrt(ms + NORM_EPS) * norm_w).astype(BF16)


def _conv_act(xp_ref, cw_ref, cb_ref, r0=0):
    t = BLK
    base = r0 + TILE - (CONV_K - 1)
    out = cb_ref[...] + xp_ref[base:base + t, :] * cw_ref[0:1, :]
    for j in range(1, CONV_K):
        out = out + xp_ref[base + j:base + j + t, :] * cw_ref[j:j + 1, :]
    return _silu(out)


def _split2_dot(v, sel):
    hi = v.astype(BF16)
    lo = (v - hi.astype(F32)).astype(BF16)
    return jnp.dot(hi, sel, preferred_element_type=F32) + jnp.dot(lo, sel, preferred_element_type=F32)


def _ssd_prompt_chunk(act, z, dt_raw, dtb, alog, dskip, nw, expand, expand_t, hs_ref):
    t = BLK
    row = lax.broadcasted_iota(jnp.int32, (t, t), 0)
    col = lax.broadcasted_iota(jnp.int32, (t, t), 1)
    tri_mask = col <= row
    xs = act[:, :B_INNER]
    bb = act[:, B_INNER:B_INNER + 2 * B_STATE].astype(BF16)
    cb_ = act[:, B_INNER + 2 * B_STATE:].astype(BF16)
    dtv = _softplus(dt_raw + dtb)
    a = -jnp.exp(alog) * dtv
    acum = _sel_dot_rhs(tri_mask.astype(BF16), a)
    acum_t = acum.T
    spread = _split2_dot(jnp.concatenate([dtv, jnp.exp(acum), jnp.exp(acum[t - 1:t, :] - acum)], axis=0), expand)
    dt_e, e_all, dec_e = spread[0:t], spread[t:2 * t], spread[2 * t:3 * t]
    xdt = xs * dt_e
    lane_lo = lax.broadcasted_iota(jnp.int32, (BLK, 128), 1) < HEAD_DIM
    ys = []
    for g in range(2):
        cbm = _dot_nt(cb_[:, 128 * g:128 * (g + 1)], bb[:, 128 * g:128 * (g + 1)])
        for pair in range(2):
            xp = xdt[:, 128 * (2 * g + pair):128 * (2 * g + pair + 1)]
            acc = None
            for j in range(2):
                h = 4 * g + 2 * pair + j
                seg = acum[:, h:h + 1] - acum_t[h:h + 1, :]
                lm = jnp.exp(jnp.where(tri_mask, seg, -jnp.inf))
                mh = (cbm * lm).astype(BF16)
                xh = jnp.where(lane_lo if j == 0 else ~lane_lo, xp, 0.0).astype(BF16)
                part = jnp.dot(mh, xh, preferred_element_type=F32)
                acc = part if acc is None else acc + part
            ys.append(acc)
    ydiag = jnp.concatenate(ys, axis=1)

    hst = hs_ref[...]
    hb = hst.astype(BF16)
    yoff = jnp.concatenate([_dot_nt(cb_[:, 128 * g:128 * (g + 1)], hb[256 * g:256 * (g + 1), :]) for g in range(2)],
                           axis=1)
    y = ydiag + yoff * e_all + xs * dskip
    out = _ssd_finish(y, z, nw)

    xw = xdt * dec_e
    last_t = jnp.exp(jnp.broadcast_to(acum_t[:, t - 1:t], (128, 128)))
    hi = last_t.astype(BF16)
    lo = (last_t - hi.astype(F32)).astype(BF16)
    cd = (jnp.dot(expand_t, hi, preferred_element_type=F32)
          + jnp.dot(expand_t, lo, preferred_element_type=F32))
    for g in range(2):
        xw_t = xw[:, 256 * g:256 * (g + 1)].T.astype(BF16)
        dh = jnp.dot(xw_t, bb[:, 128 * g:128 * (g + 1)], preferred_element_type=F32)
        hs_ref[256 * g:256 * (g + 1), :] = hst[256 * g:256 * (g + 1), :] * cd[256 * g:256 * (g + 1), :] + dh
    return out


def _ssd_prompt_body(xbc_ref, z_ref, dt_ref, cw_ref, cb_ref, dtb_ref, alog_ref, dskip_ref, nw_ref,
                     expand_ref, expand_t_ref, mo_ref, hs_ref, xp_ref, before_chunk=None):
    c = pl.program_id(0)
    rows = xbc_ref.shape[0]

    @pl.when(c == 0)
    def _():
        xp_ref[0:TILE, :] = jnp.zeros((TILE, CONV_DIM), F32)
        hs_ref[...] = jnp.zeros_like(hs_ref)

    xp_ref[TILE:, :] = xbc_ref[...]
    for ci in range(rows // BLK):
        r0 = BLK * ci
        if before_chunk is not None:
            before_chunk(ci)
        act = _conv_act(xp_ref, cw_ref, cb_ref, r0)
        mo_ref[r0:r0 + BLK, :] = _ssd_prompt_chunk(
            act, z_ref[r0:r0 + BLK, :], dt_ref[r0:r0 + BLK, :], dtb_ref[...], alog_ref[...], dskip_ref[...],
            nw_ref[...], expand_ref[...], expand_t_ref[...], hs_ref)
    xp_ref[0:TILE, :] = xp_ref[rows:rows + TILE, :]


_N_SSD_IN = 11
_N_SATTN_IN = 8


def _ssd_with_sample_attn_body(*refs, nb, n_alias, w):
    ssd_in = refs[:_N_SSD_IN]
    sattn_in = refs[_N_SSD_IN:_N_SSD_IN + _N_SATTN_IN + n_alias]
    mo_ref, hs_ref, o_ref, vo_ref, xp_ref = refs[_N_SSD_IN + _N_SATTN_IN + n_alias:]
    nchunk = ssd_in[0].shape[0] // BLK

    def sample_part(ci):
        share = range(ci * nb // nchunk, (ci + 1) * nb // nchunk)
        _sample_attn_body(*sattn_in, o_ref, vo_ref, nb=nb, has_sink=False, n_alias=n_alias, w=w,
                          batches=share, write_k=False)

    _ssd_prompt_body(*ssd_in, mo_ref, hs_ref, xp_ref, before_chunk=sample_part)


def _hosted_sample_attn(sample_c, steps, n_host_in, n_host_out):
    q0, q1, kn, vn, cache_k, cache_v, layer, mult_c, mult_n, prev_v = sample_c
    nbatches = q0.shape[0] // TILE
    nb = nbatches // steps
    assert nb * steps == nbatches
    w = cache_k.shape[3]
    srow_spec = lambda wd: pl.BlockSpec((TILE * nb, wd), lambda c: (c, 0))
    cache_spec = pl.BlockSpec((None, nb, KV_WIDTH, w), lambda c: (layer, c, 0, 0))
    in_specs = [srow_spec(128), srow_spec(128), srow_spec(KV_WIDTH), srow_spec(KV_WIDTH), cache_spec, cache_spec,
                _const_spec((4 * TILE, w)), _const_spec((4 * TILE, 128))]
    args = [q0, q1, kn, vn, cache_k, cache_v, mult_c, mult_n]
    assert len(args) == _N_SATTN_IN
    aliases = {}
    if prev_v is not None:
        aliases = {n_host_in + len(args): n_host_out + 1}
        in_specs = in_specs + [pl.BlockSpec(memory_space=pl.ANY)]
        args = args + [prev_v]
    out_specs = [srow_spec(Q_WIDTH), cache_spec]
    out_shape = [jax.ShapeDtypeStruct((nbatches * TILE, Q_WIDTH), BF16), jax.ShapeDtypeStruct(cache_v.shape, F32)]
    return nb, w, in_specs, args, aliases, out_specs, out_shape


def _hosted_cache_shift(shift_c, steps, n_host_in, n_host_out):
    new_rows, cache, layer, prev_out = shift_c
    nbatches = new_rows.shape[0] // TILE
    nb = nbatches // steps
    assert nb * steps == nbatches
    w = cache.shape[3]
    cache_spec = pl.BlockSpec((None, nb, KV_WIDTH, w), lambda c: (layer, c, 0, 0))
    in_specs = [pl.BlockSpec((TILE * nb, KV_WIDTH), lambda c: (c, 0)), cache_spec]
    args = [new_rows, cache]
    aliases = {}
    if prev_out is not None:
        aliases = {n_host_in + len(args): n_host_out}
        in_specs = in_specs + [pl.BlockSpec(memory_space=pl.ANY)]
        args = args + [prev_out]
    return nb, w, in_specs, args, aliases, [cache_spec], [jax.ShapeDtypeStruct(cache.shape, F32)]


def _ssd_prompt(xbc, z, dt, lw, sample_c):
    L = xbc.shape[0]
    rows = SSD_ROWS
    steps = L // rows
    row_spec = lambda wd: pl.BlockSpec((rows, wd), lambda c: (c, 0))
    in_specs = [row_spec(CONV_DIM), row_spec(B_INNER), row_spec(128),
                _const_spec((CONV_K, CONV_DIM)), _const_spec((1, CONV_DIM)), _const_spec((1, 128)),
                _const_spec((1, 128)), _const_spec((1, B_INNER)), _const_spec((1, B_INNER)),
                _const_spec((128, B_INNER)), _const_spec((B_INNER, 128))]
    args = [xbc, z, dt, lw['conv_w'], lw['conv_b'], lw['dt_bias'], lw['a_log'], lw['d_skip'], lw['ssm_norm'],
            lw['expand'], lw['expand_t']]
    assert len(args) == _N_SSD_IN
    nb, w, s_specs, s_args, aliases, s_out_specs, s_out_shape = _hosted_sample_attn(sample_c, steps, _N_SSD_IN, 2)
    return pl.pallas_call(
        functools.partial(_ssd_with_sample_attn_body, nb=nb, n_alias=len(aliases), w=w),
        grid=(steps,),
        in_specs=in_specs + s_specs,
        out_specs=[row_spec(B_INNER), pl.BlockSpec((B_INNER, B_STATE), lambda c: (0, 0))] + s_out_specs,
        out_shape=[jax.ShapeDtypeStruct((L, B_INNER), BF16),
                   jax.ShapeDtypeStruct((B_INNER, B_STATE), F32)] + s_out_shape,
        scratch_shapes=[pltpu.VMEM((rows + TILE, CONV_DIM), F32)],
        input_output_aliases=aliases,
        compiler_params=_params(("arbitrary",)),
        name="ssd_prompt",
    )(*(args + s_args))


def _ssd_sample_body(*refs):
    (xbc_ref, pre_ref, z_ref, dt_ref, h0_ref, cw_ref, cb_ref, dtb_ref, alog_ref, dskip_ref,
     nw_ref, expand_ref, expand_t_ref) = refs[:13]
    mo_ref, hout_ref, xp_ref = refs[-3:]
    t = BLK
    nbt = t // TILE
    rmod = lax.broadcasted_iota(jnp.int32, (t, 1), 0) & (TILE - 1)
    row_ok = rmod >= TOK0
    xp_ref[0:TILE, :] = jnp.zeros((TILE, CONV_DIM), F32)
    xp_ref[TILE:, :] = jnp.where(row_ok, xbc_ref[...], pre_ref[...])
    act = _conv_act(xp_ref, cw_ref, cb_ref)

    row = lax.broadcasted_iota(jnp.int32, (t, t), 0)
    col = lax.broadcasted_iota(jnp.int32, (t, t), 1)
    same = (row // TILE) == (col // TILE)
    tri_mask = (col <= row) & same
    tri_sel = tri_mask.astype(BF16)
    last_sel = (col == (row // TILE) * TILE + (TILE - 1)).astype(BF16)
    expand = expand_ref[...]
    xs, bb, cb_, xdt, acum, acum_t, ydiag = _ssd_intra(
        act, dt_ref[...], dtb_ref[...], alog_ref[...], tri_sel, tri_mask, expand, row_ok)

    yoffs = []
    for b in range(nbt):
        hb = h0_ref[b].astype(BF16)
        yoffs.append(jnp.concatenate(
            [_dot_nt(cb_[TILE * b:TILE * (b + 1), 128 * g:128 * (g + 1)], hb[256 * g:256 * (g + 1), :])
             for g in range(2)], axis=1))
    yoff = jnp.concatenate(yoffs, axis=0)
    e_all = _sel_dot_lhs(jnp.exp(acum), expand)
    y = ydiag + yoff * e_all + xs * dskip_ref[...]
    mo_ref[...] = _ssd_finish(y, z_ref[...], nw_ref[...])

    last = _sel_dot_rhs(last_sel, acum)
    dec_e = _sel_dot_lhs(jnp.exp(last - acum), expand)
    xw = xdt * dec_e
    dcol = _sel_dot_rhs(expand_t_ref[...], jnp.exp(acum_t))
    lane = lax.broadcasted_iota(jnp.int32, (256, t), 1)
    xw_ts = [xw[:, 256 * g:256 * (g + 1)].T for g in range(2)]
    for b in range(nbt):
        in_b = (lane // TILE) == b
        cd = dcol[:, TILE * b + TILE - 1:TILE * b + TILE]
        h0 = h0_ref[b]
        for g in range(2):
            lhs = jnp.where(in_b, xw_ts[g], 0.0).astype(BF16)
            dh = jnp.dot(lhs, bb[:, 128 * g:128 * (g + 1)], preferred_element_type=F32)
            hout_ref[b, 256 * g:256 * (g + 1), :] = (
                h0[256 * g:256 * (g + 1), :] * cd[256 * g:256 * (g + 1), :] + dh)


def _ssd_sample(xbc, prefix, z, dt, state, layer, lw, prev_out):
    rows = xbc.shape[0]
    nbt = BLK // TILE
    row_spec = lambda w: pl.BlockSpec((BLK, w), lambda i: (i, 0))
    st_spec = pl.BlockSpec((None, nbt, B_INNER, B_STATE), lambda i: (layer, i, 0, 0))
    in_specs = [row_spec(CONV_DIM), row_spec(CONV_DIM), row_spec(B_INNER), row_spec(128), st_spec,
                _const_spec((CONV_K, CONV_DIM)), _const_spec((1, CONV_DIM)), _const_spec((1, 128)),
                _const_spec((1, 128)), _const_spec((1, B_INNER)), _const_spec((1, B_INNER)),
                _const_spec((128, B_INNER)), _const_spec((B_INNER, 128))]
    args = [xbc, prefix, z, dt, state, lw['conv_w'], lw['conv_b'], lw['dt_bias'], lw['a_log'], lw['d_skip'],
            lw['ssm_norm'], lw['expand'], lw['expand_t']]
    aliases = {}
    if prev_out is not None:
        aliases = {len(args): 1}
        in_specs.append(pl.BlockSpec(memory_space=pl.ANY))
        args.append(prev_out)
    return pl.pallas_call(
        _ssd_sample_body,
        grid=(rows // BLK,),
        in_specs=in_specs,
        out_specs=[row_spec(B_INNER), st_spec],
        out_shape=[jax.ShapeDtypeStruct((rows, B_INNER), BF16), jax.ShapeDtypeStruct(state.shape, F32)],
        scratch_shapes=[pltpu.VMEM((BLK + TILE, CONV_DIM), F32)],
        input_output_aliases=aliases,
        compiler_params=_params(("parallel",)),
        name="ssd_sample",
    )(*args)


def _out_ffn_body(h_ref, a_ref, m_ref, c_ref, wo_ref, g2_ref, wg_ref, wu_ref, wd_ref, out_ref):
    d = functools.partial(jnp.dot, preferred_element_type=F32)
    h1 = (h_ref[...] + d(a_ref[...], wo_ref[0:256, :]) + d(m_ref[...], wo_ref[256:768, :])
          + d(c_ref[...].astype(BF16), wo_ref[768:1024, :]))
    ms = jnp.mean(h1 * h1, axis=-1, keepdims=True)
    u = (h1 * lax.rsqrt(ms + NORM_EPS) * g2_ref[...]).astype(BF16)
    out_ref[...] = h1
    for c in range(D_FF // FF_CHUNK):
        sl = slice(FF_CHUNK * c, FF_CHUNK * (c + 1))
        act = (_silu(d(u, wg_ref[:, sl])) * d(u, wu_ref[:, sl])).astype(BF16)
        out_ref[...] += d(act, wd_ref[sl, :])


_N_FFN_IN = 9


def _out_ffn_with_cache_shift_body(*refs, nb, w):
    ffn_in = refs[:_N_FFN_IN]
    out_ref, ko_ref = refs[-2:]
    _cache_shift_body(*refs[_N_FFN_IN:-2], ko_ref, nb=nb, w=w)
    _out_ffn_body(*ffn_in, out_ref)


def _out_ffn(h, a_o, m_o, c_o, lw, tm, sample_c=None):
    rows = h.shape[0]
    steps = rows // tm
    row_spec = lambda w: pl.BlockSpec((tm, w), lambda i: (i, 0))
    in_specs = [row_spec(D_MODEL), row_spec(Q_WIDTH), row_spec(B_INNER), row_spec(Q_WIDTH),
                _layer_spec((D_MODEL, D_MODEL), lw['layer']), _const_spec((1, D_MODEL)),
                _layer_spec((D_MODEL, D_FF), lw['layer']), _layer_spec((D_MODEL, D_FF), lw['layer']),
                _layer_spec((D_FF, D_MODEL), lw['layer'])]
    args = [h, a_o, m_o, c_o, lw['w_out'], lw['norm2'], lw['w_gate'], lw['w_up'], lw['w_down']]
    assert len(args) == _N_FFN_IN
    out_spec = row_spec(D_MODEL)
    out_shape = jax.ShapeDtypeStruct((rows, D_MODEL), F32)
    if sample_c is None:
        return pl.pallas_call(
            _out_ffn_body,
            grid=(steps,), in_specs=in_specs, out_specs=out_spec, out_shape=out_shape,
            compiler_params=_params(("parallel",)),
            name="out_ffn",
        )(*args)
    nb, w, s_specs, s_args, aliases, s_out_specs, s_out_shape = _hosted_cache_shift(sample_c, steps, _N_FFN_IN, 1)
    return pl.pallas_call(
        functools.partial(_out_ffn_with_cache_shift_body, nb=nb, w=w),
        grid=(steps,), in_specs=in_specs + s_specs, out_specs=[out_spec] + s_out_specs,
        out_shape=[out_shape] + s_out_shape,
        input_output_aliases=aliases,
        compiler_params=_params(("parallel",)),
        name="out_ffn_host",
    )(*(args + s_args))


def _rope_table(pos):
    half = ROT_DIM // 2
    inv = ROPE_THETA ** (-(jnp.arange(half, dtype=F32) * 2.0 / ROT_DIM))
    ang = pos.astype(F32)[None, :] * inv[:, None]
    cs = jnp.concatenate([jnp.cos(ang), jnp.sin(ang)], axis=0)
    hi = cs.astype(BF16).astype(F32)
    mid = (cs - hi).astype(BF16).astype(F32)
    lo = (cs - hi - mid).astype(BF16).astype(F32)
    return jnp.concatenate([hi, mid, lo], axis=0)


def _rope_selection():
    half = ROT_DIM // 2
    sel = np.zeros((128, 256), np.float32)
    for term in range(3):
        base = term * ROT_DIM
        for lane in range(128):
            f = lane % HEAD_DIM
            if f < half:
                sel[base + f, lane] = 1.0
                sel[base + half + f, 128 + lane] = -1.0
            elif f < ROT_DIM:
                sel[base + f - half, lane] = 1.0
                sel[base + f, 128 + lane] = 1.0
    return sel


_ROPE_SEL = _rope_selection()


def _permute_heads(m, start, axis):
    sl = lambda a, b: lax.slice_in_dim(m, a, b, axis=axis)
    pieces = [sl(0, start)] if start else []
    pieces += [sl(start, start + 64), sl(start + 128, start + 192), sl(start + 64, start + 128),
               sl(start + 192, start + 256), sl(start + 256, m.shape[axis])]
    return jnp.concatenate(pieces, axis=axis)


def _matmul_weights(w_in, w_out, w_gate, w_up, w_down):
    w = _permute_heads(_permute_heads(w_in.astype(BF16), _OFF['aq'], 2), _OFF['cq'], 2)
    w = jnp.pad(w, ((0, 0), (0, 0), (0, N_IN_PAD - N_IN)))
    wo = _permute_heads(_permute_heads(w_out.astype(BF16), 0, 1), 768, 1)
    return dict(w_in=w, w_out=wo, w_gate=w_gate.astype(BF16), w_up=w_up.astype(BF16), w_down=w_down.astype(BF16))


def _layer_weights(l, mm, norm1, a_qn, a_kn, a_sinks, c_qn, c_kn, conv_w, conv_b, dt_bias, a_log, d_skip,
                   ssm_norm, norm2):
    pad8 = lambda v: jnp.pad(v.astype(F32), (0, 128 - B_HEADS))[None, :]
    head_of = np.arange(B_INNER) // 64
    expand = (np.arange(128)[:, None] == head_of[None, :]).astype(np.float32)
    blk = (np.arange(256)[:, None] // 64 == np.arange(256)[None, :] // 64).astype(np.float32) / 64.0
    return dict(
        layer=l, norm1=norm1[l][None, :], w_in=mm['w_in'],
        a_qn=jnp.tile(a_qn[l], 4)[None, :], a_kn=jnp.tile(a_kn[l], 2)[None, :],
        c_qn=jnp.tile(c_qn[l], 4)[None, :], c_kn=jnp.tile(c_kn[l], 2)[None, :],
        bd=jnp.asarray(blk, BF16),
        sink_lanes=jnp.repeat(a_sinks[l].astype(F32)[jnp.asarray([0, 2, 1, 3])], HEAD_DIM).reshape(2, 128),
        sink_rows=jnp.broadcast_to(
            jnp.repeat(a_sinks[l].astype(F32)[jnp.asarray([0, 2, 1, 3])], TILE)[:, None], (4 * TILE, 128)),
        conv_w=conv_w[l], conv_b=conv_b[l][None, :], dt_bias=pad8(dt_bias[l]), a_log=pad8(a_log[l]),
        d_skip=jnp.repeat(d_skip[l].astype(F32), 64)[None, :], ssm_norm=ssm_norm[l][None, :],
        expand=jnp.asarray(expand, BF16), expand_t=jnp.asarray(expand.T, BF16),
        w_out=mm['w_out'], norm2=norm2[l][None, :], w_gate=mm['w_gate'], w_up=mm['w_up'], w_down=mm['w_down'])


def kernel(x_prompt, x_sample, cache_a_k, cache_a_v, cache_c_k, cache_c_v, state_ssm, state_conv, norm1, w_in,
           a_qn, a_kn, a_sinks, c_qn, c_kn, conv_w, conv_b, dt_bias, a_log, d_skip, ssm_norm, w_out, norm2,
           w_gate, w_up, w_down):
    depth = w_in.shape[0]
    batch, seq, _ = x_prompt.shape
    nbatch, dec_seq, _ = x_sample.shape
    assert batch == 1 and dec_seq == TILE - TOK0 and seq % (16 * BLK) == 0 and nbatch % (BLK // TILE) == 0
    past_len = PAST_LEN
    a_buf, c_buf = cache_a_k.shape[2], cache_c_k.shape[2]
    assert a_buf == A_WINDOW and c_buf == C_SPAN

    hp = x_prompt.reshape(seq, D_MODEL)
    hs = jnp.pad(x_sample, ((0, 0), (TOK0, 0), (0, 0))).reshape(nbatch * TILE, D_MODEL)
    tab_p = _rope_table(jnp.arange(seq))
    pos_s = past_len + jnp.maximum(jnp.arange(TILE) - TOK0, 0)
    tab_s = jnp.tile(_rope_table(pos_s), (1, nbatch))
    ma_c, ma_n, mc_c, mc_n = _sample_mult_tables()

    to_fm = lambda c: jnp.transpose(c, (0, 1, 3, 4, 2)).reshape(depth, nbatch, KV_WIDTH, c.shape[2])
    from_fm = lambda c: jnp.transpose(c.reshape(depth, nbatch, 2, HEAD_DIM, c.shape[3]), (0, 1, 4, 2, 3))
    ca_k, ca_v, cc_k, cc_v = to_fm(cache_a_k), to_fm(cache_a_v), to_fm(cache_c_k), to_fm(cache_c_v)
    st = state_ssm.reshape(depth, nbatch, B_INNER, B_STATE)

    mm = _matmul_weights(w_in, w_out, w_gate, w_up, w_down)
    p_out = [[] for _ in range(6)]
    s_conv = []
    new_a = new_ck = new_cv = new_st = None
    tm_p = 512
    tm_s = min(512, nbatch * TILE)
    for l in range(depth):
        lw = _layer_weights(l, mm, norm1, a_qn, a_kn, a_sinks, c_qn, c_kn, conv_w, conv_b, dt_bias, a_log,
                            d_skip, ssm_norm, norm2)
        s_qa0, s_qa1, s_ka, s_va, s_qc0, s_qc1, s_kc, s_vc, s_z, s_xbc, s_dt = _in_proj(hs, lw, tab_s, tm_s)

        qa0, qa1, ka, va, qc0, qc1, kc, vc, z, xbc, dt = _in_proj(hp, lw, tab_p, tm_p)
        a_o = _prompt_attn(qa0, qa1, ka, va, A_PATTERNS, sink_lanes=lw['sink_lanes'], name="attn_a")
        c_o = _prompt_attn(qc0, qc1, kc, vc, C_BANDS, name="attn_c")
        m_o, h_fin, s_c_o, new_cv = _ssd_prompt(
            xbc, z, dt, lw, (s_qc0, s_qc1, s_kc, s_vc, cc_k, cc_v, l, mc_c, mc_n, new_cv))
        hp, new_ck = _out_ffn(hp, a_o, m_o, c_o, lw, FFN_ROWS, (s_kc, cc_k, l, new_ck))
        p_out[0].append(ka[seq - a_buf:].reshape(1, a_buf, 2, HEAD_DIM))
        p_out[1].append(va[seq - a_buf:].reshape(1, a_buf, 2, HEAD_DIM))
        p_out[2].append(kc[seq - c_buf:].reshape(1, c_buf, 2, HEAD_DIM))
        p_out[3].append(vc[seq - c_buf:].reshape(1, c_buf, 2, HEAD_DIM))
        p_out[4].append(h_fin.reshape(1, B_HEADS, 64, B_STATE))
        p_out[5].append(xbc[seq - (CONV_K - 1):].reshape(1, CONV_K - 1, CONV_DIM))

        a_o, *new_a = _sample_attn(s_qa0, s_qa1, s_ka, s_va, ca_k, ca_v, l, ma_c, ma_n, lw['sink_rows'], 8, new_a)
        prefix = jnp.pad(state_conv[l], ((0, 0), (1, TILE - CONV_K), (0, 0))).reshape(nbatch * TILE, CONV_DIM)
        m_o, new_st = _ssd_sample(s_xbc, prefix, s_z, s_dt, st, l, lw, new_st)
        hs = _out_ffn(hs, a_o, m_o, s_c_o, lw, tm_s)
        s_conv.append(s_xbc.reshape(nbatch, TILE, CONV_DIM)[:, TILE - (CONV_K - 1):])

    outs_p = [jnp.stack(t, axis=0) for t in p_out]
    outs_s = [from_fm(new_a[0]), from_fm(new_a[1]), from_fm(new_ck), from_fm(new_cv),
              new_st.reshape(depth, nbatch, B_HEADS, 64, B_STATE), jnp.stack(s_conv, axis=0)]
    y_p = hp.reshape(1, seq, D_MODEL)
    y_s = hs.reshape(nbatch, TILE, D_MODEL)[:, TOK0:]
    return (y_p, y_s, *outs_p, *outs_s)
```

```python
import functools
import math

import numpy as np
import jax
import jax.numpy as jnp
from jax import lax
from jax.experimental import pallas as pl
from jax.experimental.pallas import tpu as pltpu

F32 = jnp.float32
BF16 = jnp.bfloat16

D_MODEL = 1024
HEAD_DIM = 64
ROT_DIM = 16
ROPE_THETA = 500000.0
NORM_EPS = 1e-6
Q_WIDTH = 256
KV_WIDTH = 128
PAST_LEN = 16384
A_WINDOW = 128
C_PATTERNS = ((128, 1), (512, 4), (2048, 16))
C_SPAN = 2048
B_HEADS = 8
B_INNER = 512
B_STATE = 128
CONV_K = 4
CONV_DIM = 1024
D_FF = 2816
N_IN = 2568
N_IN_PAD = 2688
BLK = 128
ATTN_SUPER = 2048
A_PATTERNS = ((1, A_WINDOW - 1),)
C_BANDS = tuple((d, w // d) for w, d in C_PATTERNS)
TOK0 = 4
TILE = 8
FF_CHUNK = 256
SSD_ROWS = 512
FFN_ROWS = 512
VMEM_LIMIT = 56 * 1024 * 1024

_OFF = dict(aq=0, ak=256, av=384, cq=512, ck=768, cv=896, z=1024, xbc=1536, dt=2560, end=N_IN_PAD)
_HEAD_PERM = np.concatenate([np.arange(0, 64), np.arange(128, 192), np.arange(64, 128), np.arange(192, 256)])


def _const_spec(shape):
    nd = len(shape)
    return pl.BlockSpec(shape, lambda *_: (0,) * nd, pipeline_mode=pl.Buffered(1))


def _layer_spec(shape, layer):
    nd = len(shape)
    return pl.BlockSpec((None,) + tuple(shape), lambda *_: (layer,) + (0,) * nd, pipeline_mode=pl.Buffered(1))


def _params(sem):
    return pltpu.CompilerParams(dimension_semantics=sem, vmem_limit_bytes=VMEM_LIMIT)


def _split3(v):
    hi = v.astype(BF16)
    r1 = v - hi.astype(F32)
    mid = r1.astype(BF16)
    lo = (r1 - mid.astype(F32)).astype(BF16)
    return hi, mid, lo


def _sel_dot_rhs(sel, v):
    hi, mid, lo = _split3(v)
    d = functools.partial(jnp.dot, preferred_element_type=F32)
    return d(sel, hi) + d(sel, mid) + d(sel, lo)


def _sel_dot_lhs(v, sel):
    hi, mid, lo = _split3(v)
    d = functools.partial(jnp.dot, preferred_element_type=F32)
    return d(hi, sel) + d(mid, sel) + d(lo, sel)


def _dot_nt(a, b):
    return lax.dot_general(a, b, (((1,), (1,)), ((), ())), preferred_element_type=F32)


def _silu(x):
    return x * jax.nn.sigmoid(x)


def _softplus(x):
    return jnp.maximum(x, 0.0) + jnp.log(1.0 + jnp.exp(-jnp.abs(x)))


def _head_norm(x, gain, bd):
    x2 = x * x
    hi = x2.astype(BF16)
    lo = (x2 - hi.astype(F32)).astype(BF16)
    ms = jnp.dot(hi, bd, preferred_element_type=F32) + jnp.dot(lo, bd, preferred_element_type=F32)
    return x * lax.rsqrt(ms + NORM_EPS) * gain


def _rope(x, cos_t, sin_t):
    w = x.shape[1]
    lane = lax.broadcasted_iota(jnp.int32, x.shape, 1) & (HEAD_DIM - 1)
    partner = jnp.where(lane < ROT_DIM // 2, pltpu.roll(x, w - ROT_DIM // 2, 1), pltpu.roll(x, ROT_DIM // 2, 1))
    return x * cos_t + partner * sin_t


def _in_proj_body(h_ref, g1_ref, w_ref, cs_ref, sel_ref, gqa_ref, gka_ref, gqc_ref, gkc_ref, bd_ref,
                  qa0_ref, qa1_ref, ka_ref, va_ref, qc0_ref, qc1_ref, kc_ref, vc_ref, z_ref, xbc_ref, dt_ref):
    x = h_ref[...]
    tm = x.shape[0]
    ms = jnp.mean(x * x, axis=-1, keepdims=True)
    u = (x * lax.rsqrt(ms + NORM_EPS) * g1_ref[...]).astype(BF16)

    def proj(name, nxt):
        return jnp.dot(u, w_ref[:, _OFF[name]:_OFF[nxt]], preferred_element_type=F32)

    xbc_ref[...] = proj('xbc', 'dt')
    z_ref[...] = proj('z', 'xbc')
    dt_ref[...] = proj('dt', 'end')
    cs_t = jnp.concatenate([cs_ref[...], jnp.zeros((128 - 3 * ROT_DIM, tm), F32)], axis=0).T.astype(BF16)
    tab = jnp.dot(cs_t, sel_ref[...], preferred_element_type=F32)
    unrot = ((lax.broadcasted_iota(jnp.int32, (1, 128), 1) & (HEAD_DIM - 1)) >= ROT_DIM).astype(F32)
    c1, s1 = tab[:, :128] + unrot, tab[:, 128:]
    c2 = jnp.concatenate([c1, c1], axis=1)
    s2 = jnp.concatenate([s1, s1], axis=1)
    bd2 = bd_ref[...]
    bd1 = bd2[:128, :128]
    qa = _rope(_head_norm(proj('aq', 'ak'), gqa_ref[...], bd2), c2, s2) * 0.125
    qa0_ref[...] = qa[:, :128]
    qa1_ref[...] = qa[:, 128:]
    kva = proj('ak', 'cq')
    ka_ref[...] = _rope(_head_norm(kva[:, :128], gka_ref[...], bd1), c1, s1)
    va_ref[...] = kva[:, 128:]
    qc = _rope(_head_norm(proj('cq', 'ck'), gqc_ref[...], bd2), c2, s2) * 0.125
    qc0_ref[...] = qc[:, :128]
    qc1_ref[...] = qc[:, 128:]
    kvc = proj('ck', 'z')
    kc_ref[...] = _rope(_head_norm(kvc[:, :128], gkc_ref[...], bd1), c1, s1)
    vc_ref[...] = kvc[:, 128:]


def _in_proj(h, lw, tab, tm):
    rows = h.shape[0]
    grid = (rows // tm,)
    row_spec = lambda w: pl.BlockSpec((tm, w), lambda i: (i, 0))
    widths = (128, 128, KV_WIDTH, KV_WIDTH, 128, 128, KV_WIDTH, KV_WIDTH, B_INNER, CONV_DIM, 128)
    dtypes = (F32,) * len(widths)
    return pl.pallas_call(
        _in_proj_body,
        grid=grid,
        in_specs=[row_spec(D_MODEL), _const_spec((1, D_MODEL)), _layer_spec((D_MODEL, N_IN_PAD), lw['layer']),
                  pl.BlockSpec((3 * ROT_DIM, tm), lambda i: (0, i)), _const_spec((128, 256)),
                  _const_spec((1, 256)), _const_spec((1, 128)), _const_spec((1, 256)),
                  _const_spec((1, 128)), _const_spec((256, 256))],
        out_specs=[row_spec(w) for w in widths],
        out_shape=[jax.ShapeDtypeStruct((rows, w), dt) for w, dt in zip(widths, dtypes)],
        compiler_params=_params(("parallel",)),
        name="in_proj",
    )(h, lw['norm1'], lw['w_in'], tab, jnp.asarray(_ROPE_SEL, BF16), lw['a_qn'], lw['a_kn'], lw['c_qn'], lw['c_kn'], lw['bd'])


def _attn_body(*refs, patterns, has_sink, sb):
    if has_sink:
        sink_ref, refs = refs[0], refs[1:]
    q0_ref, q1_ref, kp_ref, kc_ref, vp_ref, vc_ref, o_ref, kk, vv, acc_s, m_s, l_s = refs
    q_refs = (q0_ref, q1_ref)
    j = pl.program_id(0)
    kk[0:sb, :] = kp_ref[...]
    kk[sb:2 * sb, :] = kc_ref[...]
    vv[0:sb, :] = vp_ref[...]
    vv[sb:2 * sb, :] = vc_ref[...]
    row4 = lax.broadcasted_iota(jnp.int32, (4 * BLK, BLK), 0) & (BLK - 1)
    col4 = lax.broadcasted_iota(jnp.int32, (4 * BLK, BLK), 1)
    upper4 = col4 > row4
    g_lo = lax.broadcasted_iota(jnp.int32, (BLK, 128), 1) < HEAD_DIM
    ones_cols = jnp.ones((2 * BLK, 128), BF16)
    nblk = sb // BLK

    diag_here = [md == BLK for _, md in patterns]
    boosts = [[] for _ in patterns]
    for p, (d, md) in enumerate(patterns):
        for q in range(p + 1, len(patterns)):
            dq, mdq = patterns[q]
            if diag_here[p] and (md * d) % dq == 0 and (md * d) // dq < min(BLK, mdq + 1):
                boosts[q].append((md * d) // dq)
                diag_here[p] = False

    for pi, (d, max_dist) in enumerate(patterns):
        nsub = nblk // d
        has_diag = diag_here[pi]
        fdist = jnp.where(upper4, row4 + BLK - col4, row4 - col4)
        mult = jnp.ones((4 * BLK, BLK), F32)
        for f in boosts[pi]:
            mult = mult + (fdist == f).astype(F32)
        w_up = jnp.where(upper4, mult, 0.0)
        w_lo = jnp.where(upper4, 0.0, mult)

        def ld(ref, s0, d=d):
            if d == 1:
                return ref[pl.ds(s0, BLK), :]
            return ref[pl.ds(s0, BLK, stride=d), :]

        def block(t, carry, d=d, nsub=nsub, has_diag=has_diag, first=(pi == 0), ld=ld, w_up=w_up, w_lo=w_lo):
            r_ = t // nsub
            n = t - r_ * nsub
            start = r_ + BLK * d * n
            prev_ok = jnp.logical_or(j > 0, n > 0)
            neg = jnp.where(prev_ok, 0.0, -jnp.inf)
            qb = [ld(q_refs[rr], start) for rr in range(2)]
            kprev, kcur = ld(kk, sb + start - BLK * d), ld(kk, sb + start)
            vprev, vcur = ld(vv, sb + start - BLK * d), ld(vv, sb + start)
            qm = jnp.concatenate([jnp.where(g_lo, qb[0], 0.0), jnp.where(g_lo, 0.0, qb[0]),
                                  jnp.where(g_lo, qb[1], 0.0), jnp.where(g_lo, 0.0, qb[1])], axis=0).astype(BF16)
            kcat = jnp.concatenate([kprev, kcur], axis=0).astype(BF16)
            vcat = jnp.concatenate([jnp.concatenate([vprev, vcur], axis=0).astype(BF16), ones_cols], axis=1)
            s2 = _dot_nt(qm, kcat)
            sp, sc = s2[:, :BLK] + neg, s2[:, BLK:]
            s = jnp.where(upper4, sp, sc)
            mb = jnp.max(s, axis=-1, keepdims=True)
            if has_diag:
                sd = jnp.sum(jnp.where(col4 == row4, sp, 0.0), axis=-1, keepdims=True)
                mb = jnp.maximum(mb, sd)
            e = jnp.exp(s - mb)
            ecat = jnp.concatenate([e * w_up, e * w_lo], axis=1).astype(BF16)
            pvl = jnp.dot(ecat, vcat, preferred_element_type=F32)
            pv, lb = pvl[:, :128], pvl[:, 128:]
            if has_diag:
                ed = jnp.exp(sd - mb)
                lb = lb + ed
                pv = pv + ed * jnp.concatenate([vprev] * 4, axis=0)
            for rr in range(2):
                lo, hi = slice(2 * BLK * rr, 2 * BLK * rr + BLK), slice(2 * BLK * rr + BLK, 2 * BLK * (rr + 1))
                o_b = jnp.where(g_lo, pv[lo], pv[hi])
                m_b = jnp.where(g_lo, mb[lo], mb[hi])
                l_b = jnp.where(g_lo, lb[lo], lb[hi])
                rows = pl.ds(start, BLK) if d == 1 else pl.ds(start, BLK, stride=d)
                if first:
                    m_s[rr, rows, :] = m_b
                    l_s[rr, rows, :] = l_b
                    acc_s[rr, rows, :] = o_b
                else:
                    m_old = m_s[rr, rows, :]
                    m_new = jnp.maximum(m_old, m_b)
                    w_old = jnp.exp(m_old - m_new)
                    w_b = jnp.exp(m_b - m_new)
                    m_s[rr, rows, :] = m_new
                    l_s[rr, rows, :] = w_old * l_s[rr, rows, :] + w_b * l_b
                    acc_s[rr, rows, :] = w_old * acc_s[rr, rows, :] + w_b * o_b
            return carry

        lax.fori_loop(0, nblk, block, 0, unroll={1: 16, 4: 8}.get(d, 4))

    chunk = 256
    for rr in range(2):
        for c in range(sb // chunk):
            rows = slice(chunk * c, chunk * (c + 1))
            m_f, l_f, a_f = m_s[rr, rows, :], l_s[rr, rows, :], acc_s[rr, rows, :]
            if has_sink:
                sk = sink_ref[rr:rr + 1, :]
                m2 = jnp.maximum(m_f, sk)
                w = jnp.exp(m_f - m2)
                o = a_f * w / (l_f * w + jnp.exp(sk - m2))
            else:
                o = a_f / l_f
            o_ref[rows, 128 * rr:128 * (rr + 1)] = o.astype(o_ref.dtype)


def _prompt_attn(q0, q1, k, v, patterns, sink_lanes=None, name="attn"):
    L = q0.shape[0]
    sb = ATTN_SUPER
    cur = lambda j: (j, 0)
    prev = lambda j: (jnp.maximum(j - 1, 0), 0)
    blk = lambda im: pl.BlockSpec((sb, 128), im)
    in_specs = [blk(cur), blk(cur), blk(prev), blk(cur), blk(prev), blk(cur)]
    args = [q0, q1, k, k, v, v]
    if sink_lanes is not None:
        in_specs = [_const_spec((2, 128))] + in_specs
        args = [sink_lanes] + args
    return pl.pallas_call(
        functools.partial(_attn_body, patterns=patterns, has_sink=sink_lanes is not None, sb=sb),
        grid=(L // sb,), in_specs=in_specs, out_specs=pl.BlockSpec((sb, Q_WIDTH), cur),
        out_shape=jax.ShapeDtypeStruct((L, Q_WIDTH), BF16),
        scratch_shapes=[pltpu.VMEM((2 * sb, 128), F32), pltpu.VMEM((2 * sb, 128), F32),
                        pltpu.VMEM((2, sb, 128), F32), pltpu.VMEM((2, sb, 128), F32), pltpu.VMEM((2, sb, 128), F32)],
        compiler_params=_params(("parallel",)),
        name=name,
    )(*args)


def _shift_cache(src, new_t, dst, b, w):
    lane = lax.broadcasted_iota(jnp.int32, (128, 128), 1)
    is_new = lane >= 128 - (TILE - TOK0)
    shifted = pltpu.roll(src, w - (TILE - TOK0), 1)
    new_cols = pltpu.roll(new_t, 128 - TILE, 1)
    if w > 128:
        dst[b, :, 0:w - 128] = shifted[:, 0:w - 128]
    dst[b, :, w - 128:w] = jnp.where(is_new, new_cols, shifted[:, w - 128:w])


def _cache_shift_body(*refs, nb, w):
    new_ref, c_ref, o_ref = refs[0], refs[1], refs[-1]
    pad = jnp.zeros((128 - TILE, 128), F32)
    for b in range(nb):
        new_t = jnp.concatenate([new_ref[TILE * b:TILE * (b + 1), :], pad], axis=0).T
        _shift_cache(c_ref[b], new_t, o_ref, b, w)


def _sample_attn_body(*refs, nb, has_sink, n_alias, w, batches=None, write_k=True):
    if has_sink:
        sink_ref, refs = refs[0], refs[1:]
    q0_ref, q1_ref, kn_ref, vn_ref, kc_ref, vc_ref, mc_ref, mn_ref = refs[:8]
    if write_k:
        o_ref, ko_ref, vo_ref = refs[8 + n_alias:]
    else:
        (o_ref, vo_ref), ko_ref = refs[8 + n_alias:], None
    g_lo = lax.broadcasted_iota(jnp.int32, (TILE, 128), 1) < HEAD_DIM
    mult_c = mc_ref[...]
    mult_n = mn_ref[...]
    pad = jnp.zeros((128 - TILE, 128), F32)
    for b in (range(nb) if batches is None else batches):
        parts = []
        for q_ref in (q0_ref, q1_ref):
            qr = q_ref[TILE * b:TILE * (b + 1), :]
            parts += [jnp.where(g_lo, qr, 0.0), jnp.where(g_lo, 0.0, qr)]
        qm = jnp.concatenate(parts, axis=0).astype(BF16)
        kct = kc_ref[b]
        vct = vc_ref[b]
        knp = jnp.concatenate([kn_ref[TILE * b:TILE * (b + 1), :], pad], axis=0)
        vnp = jnp.concatenate([vn_ref[TILE * b:TILE * (b + 1), :], pad], axis=0)
        knt = knp.T
        vnt = vnp.T
        sc = jnp.where(mult_c > 0, jnp.dot(qm, kct.astype(BF16), preferred_element_type=F32), -jnp.inf)
        sn = jnp.where(mult_n > 0, jnp.dot(qm, knt.astype(BF16), preferred_element_type=F32), -jnp.inf)
        m = jnp.maximum(jnp.max(sc, axis=-1, keepdims=True), jnp.max(sn, axis=-1, keepdims=True))
        if has_sink:
            sk = sink_ref[:, :1]
            m = jnp.maximum(m, sk)
        ec = mult_c * jnp.exp(sc - m)
        en = mult_n * jnp.exp(sn - m)
        den = jnp.sum(ec, axis=-1, keepdims=True) + jnp.sum(en, axis=-1, keepdims=True)
        if has_sink:
            den = den + jnp.exp(sk - m)
        o = (_dot_nt(ec.astype(BF16), vct.astype(BF16))
             + jnp.dot(en.astype(BF16), vnp.astype(BF16), preferred_element_type=F32)) / den
        o_ref[TILE * b:TILE * (b + 1), :] = jnp.concatenate(
            [jnp.where(g_lo, o[0:TILE], o[TILE:2 * TILE]),
             jnp.where(g_lo, o[2 * TILE:3 * TILE], o[3 * TILE:4 * TILE])], axis=1).astype(o_ref.dtype)
        if write_k:
            _shift_cache(kct, knt, ko_ref, b, w)
        _shift_cache(vct, vnt, vo_ref, b, w)


def _sample_attn(q0, q1, kn, vn, cache_k, cache_v, layer, mult_c, mult_n, sinks_rows, nb, prev_out):
    rows = q0.shape[0]
    nbatch = rows // TILE
    depth, _, _, w = cache_k.shape
    row_spec = lambda width: pl.BlockSpec((TILE * nb, width), lambda i: (i, 0))
    cache_spec = pl.BlockSpec((None, nb, KV_WIDTH, w), lambda i: (layer, i, 0, 0))
    in_specs = [row_spec(128), row_spec(128), row_spec(KV_WIDTH), row_spec(KV_WIDTH), cache_spec, cache_spec,
                _const_spec((4 * TILE, w)), _const_spec((4 * TILE, 128))]
    args = [q0, q1, kn, vn, cache_k, cache_v, mult_c, mult_n]
    if sinks_rows is not None:
        in_specs = [_const_spec((4 * TILE, 128))] + in_specs
        args = [sinks_rows] + args
    aliases = {}
    if prev_out is not None:
        aliases = {len(args): 1, len(args) + 1: 2}
        in_specs = in_specs + [pl.BlockSpec(memory_space=pl.ANY)] * 2
        args = args + list(prev_out)
    cache_shape = jax.ShapeDtypeStruct(cache_k.shape, F32)
    return pl.pallas_call(
        functools.partial(_sample_attn_body, nb=nb, has_sink=sinks_rows is not None,
                          n_alias=0 if prev_out is None else 2, w=w),
        grid=(nbatch // nb,), in_specs=in_specs, out_specs=[row_spec(Q_WIDTH), cache_spec, cache_spec],
        out_shape=[jax.ShapeDtypeStruct((rows, Q_WIDTH), BF16), cache_shape, cache_shape],
        input_output_aliases=aliases,
        compiler_params=_params(("parallel",)),
        name=f"sample_attn_w{w}",
    )(*args)


def _sample_mult_tables():
    t = np.arange(TILE) - TOK0
    tq = np.maximum(t, 0)[:, None]
    j = np.arange(A_WINDOW)[None, :]
    da = A_WINDOW + tq - j
    ma_c = ((da >= 0) & (da < A_WINDOW)).astype(np.float32)
    tn = (np.arange(128) - TOK0)[None, :]
    dn = tq - tn
    new_ok = (tn >= 0) & (tn < TILE - TOK0) & (dn >= 0)
    ma_n = (new_ok & (dn < A_WINDOW)).astype(np.float32)

    def mult(d):
        out = np.zeros(d.shape, np.float32)
        for w, dil in C_PATTERNS:
            out += ((d >= 0) & (d <= w) & (d % dil == 0)).astype(np.float32)
        return out

    jc = np.arange(C_SPAN)[None, :]
    mc_c = mult(C_SPAN + tq - jc)
    mc_n = np.where(new_ok, mult(dn), 0.0).astype(np.float32)
    tile4 = lambda a: jnp.asarray(np.tile(a, (4, 1)))
    return tile4(ma_c), tile4(ma_n), tile4(mc_c), tile4(mc_n)


def _ssd_intra(act, dt_raw, dtb, alog, tri_sel, tri_mask, expand, row_ok):
    xs = act[:, :B_INNER]
    bb = act[:, B_INNER:B_INNER + 2 * B_STATE].astype(BF16)
    cb_ = act[:, B_INNER + 2 * B_STATE:].astype(BF16)
    dtv = _softplus(dt_raw + dtb)
    if row_ok is not None:
        dtv = jnp.where(row_ok, dtv, 0.0)
    a = -jnp.exp(alog) * dtv
    acum = _sel_dot_rhs(tri_sel, a)
    acum_t = acum.T
    dt_e = _sel_dot_lhs(dtv, expand)
    xdt = xs * dt_e
    lane_lo = lax.broadcasted_iota(jnp.int32, (BLK, 128), 1) < HEAD_DIM
    ys = []
    for g in range(2):
        cbm = _dot_nt(cb_[:, 128 * g:128 * (g + 1)], bb[:, 128 * g:128 * (g + 1)])
        for pair in range(2):
            xp = xdt[:, 128 * (2 * g + pair):128 * (2 * g + pair + 1)]
            acc = None
            for j in range(2):
                h = 4 * g + 2 * pair + j
                seg = acum[:, h:h + 1] - acum_t[h:h + 1, :]
                lm = jnp.exp(jnp.where(tri_mask, seg, -jnp.inf))
                mh = (cbm * lm).astype(BF16)
                xh = jnp.where(lane_lo if j == 0 else ~lane_lo, xp, 0.0).astype(BF16)
                t = jnp.dot(mh, xh, preferred_element_type=F32)
                acc = t if acc is None else acc + t
            ys.append(acc)
    ydiag = jnp.concatenate(ys, axis=1)
    return xs, bb, cb_, xdt, acum, acum_t, ydiag


def _ssd_finish(y, z, norm_w):
    y = y * _silu(z)
    ms = jnp.mean(y * y, axis=-1, keepdims=True)
    return (y * lax.rsqrt(ms + NORM_EPS) * norm_w).astype(BF16)


def _conv_shift_selection():
    sel = np.zeros(((CONV_K - 1) * BLK, 2 * BLK), np.float32)
    for j in range(1, CONV_K):
        for t in range(j, BLK):
            sel[(j - 1) * BLK + t, t - j] = 1.0
            sel[(j - 1) * BLK + t, BLK + t - j] = 1.0
    return sel


def _conv_act(xp_ref, cw_ref, cb_ref, shift_ref, r0=0):
    t = BLK
    x = xp_ref[TILE + r0:TILE + r0 + t, :]
    hi = x.astype(BF16)
    lo = (x - hi.astype(F32)).astype(BF16)
    shifted = jnp.dot(shift_ref[...], jnp.concatenate([hi, lo], axis=0), preferred_element_type=F32)
    row = lax.broadcasted_iota(jnp.int32, (TILE, 1), 0)
    out = cb_ref[...] + x * cw_ref[CONV_K - 1:CONV_K, :]
    for j in range(1, CONV_K):
        sj = shifted[(j - 1) * t:j * t, :]
        head = jnp.where(row < j, xp_ref[TILE + r0 - j:2 * TILE + r0 - j, :], 0.0)
        sj = jnp.concatenate([sj[0:TILE, :] + head, sj[TILE:, :]], axis=0)
        out = out + sj * cw_ref[CONV_K - 1 - j:CONV_K - j, :]
    return _silu(out)


def _split2_dot(v, sel):
    hi = v.astype(BF16)
    lo = (v - hi.astype(F32)).astype(BF16)
    return jnp.dot(hi, sel, preferred_element_type=F32) + jnp.dot(lo, sel, preferred_element_type=F32)


def _ssd_prompt_chunk(act, z, dt_raw, dtb, alog, dskip, nw, expand, expand_t, hst):
    t = BLK
    row = lax.broadcasted_iota(jnp.int32, (t, t), 0)
    col = lax.broadcasted_iota(jnp.int32, (t, t), 1)
    tri_mask = col <= row
    xs = act[:, :B_INNER]
    bb = act[:, B_INNER:B_INNER + 2 * B_STATE].astype(BF16)
    cb_ = act[:, B_INNER + 2 * B_STATE:].astype(BF16)
    dtv = _softplus(dt_raw + dtb)
    a = -jnp.exp(alog) * dtv
    acum = _sel_dot_rhs(tri_mask.astype(BF16), a)
    acum_t = acum.T
    spread = _split2_dot(jnp.concatenate([dtv, jnp.exp(acum), jnp.exp(acum[t - 1:t, :] - acum)], axis=0), expand)
    dt_e, e_all, dec_e = spread[0:t], spread[t:2 * t], spread[2 * t:3 * t]
    xdt = xs * dt_e
    lane_lo = lax.broadcasted_iota(jnp.int32, (BLK, 128), 1) < HEAD_DIM
    ys = []
    for g in range(2):
        cbm = _dot_nt(cb_[:, 128 * g:128 * (g + 1)], bb[:, 128 * g:128 * (g + 1)])
        for pair in range(2):
            xp = xdt[:, 128 * (2 * g + pair):128 * (2 * g + pair + 1)]
            acc = None
            for j in range(2):
                h = 4 * g + 2 * pair + j
                seg = acum[:, h:h + 1] - acum_t[h:h + 1, :]
                lm = jnp.exp(jnp.where(tri_mask, seg, -jnp.inf))
                mh = (cbm * lm).astype(BF16)
                xh = jnp.where(lane_lo if j == 0 else ~lane_lo, xp, 0.0).astype(BF16)
                part = jnp.dot(mh, xh, preferred_element_type=F32)
                acc = part if acc is None else acc + part
            ys.append(acc)
    ydiag = jnp.concatenate(ys, axis=1)

    hb = hst.astype(BF16)
    yoff = jnp.concatenate([_dot_nt(cb_[:, 128 * g:128 * (g + 1)], hb[256 * g:256 * (g + 1), :]) for g in range(2)],
                           axis=1)
    y = ydiag + yoff * e_all + xs * dskip
    out = _ssd_finish(y, z, nw)

    xw = xdt * dec_e
    last_t = jnp.exp(jnp.broadcast_to(acum_t[:, t - 1:t], (128, 128)))
    hi = last_t.astype(BF16)
    lo = (last_t - hi.astype(F32)).astype(BF16)
    cd = (jnp.dot(expand_t, hi, preferred_element_type=F32)
          + jnp.dot(expand_t, lo, preferred_element_type=F32))
    new = []
    for g in range(2):
        xw_t = xw[:, 256 * g:256 * (g + 1)].T.astype(BF16)
        dh = jnp.dot(xw_t, bb[:, 128 * g:128 * (g + 1)], preferred_element_type=F32)
        new.append(hst[256 * g:256 * (g + 1), :] * cd[256 * g:256 * (g + 1), :] + dh)
    return out, jnp.concatenate(new, axis=0)


def _ssd_prompt_body(xbc_ref, z_ref, dt_ref, cw_ref, cb_ref, dtb_ref, alog_ref, dskip_ref, nw_ref,
                     expand_ref, expand_t_ref, shift_ref, mo_ref, hs_ref, xp_ref, before_chunk=None):
    c = pl.program_id(0)
    rows = xbc_ref.shape[0]

    @pl.when(c == 0)
    def _():
        xp_ref[0:TILE, :] = jnp.zeros((TILE, CONV_DIM), F32)
        hs_ref[...] = jnp.zeros_like(hs_ref)

    xp_ref[TILE:, :] = xbc_ref[...]
    hst = hs_ref[...]
    for ci in range(rows // BLK):
        r0 = BLK * ci
        if before_chunk is not None:
            before_chunk(ci)
        act = _conv_act(xp_ref, cw_ref, cb_ref, shift_ref, r0)
        mo_ref[r0:r0 + BLK, :], hst = _ssd_prompt_chunk(
            act, z_ref[r0:r0 + BLK, :], dt_ref[r0:r0 + BLK, :], dtb_ref[...], alog_ref[...], dskip_ref[...],
            nw_ref[...], expand_ref[...], expand_t_ref[...], hst)
    hs_ref[...] = hst
    xp_ref[0:TILE, :] = xp_ref[rows:rows + TILE, :]


_N_SSD_IN = 12
_N_SATTN_IN = 8


def _ssd_with_sample_attn_body(*refs, nb, n_alias, w):
    ssd_in = refs[:_N_SSD_IN]
    sattn_in = refs[_N_SSD_IN:_N_SSD_IN + _N_SATTN_IN + n_alias]
    mo_ref, hs_ref, o_ref, vo_ref, xp_ref = refs[_N_SSD_IN + _N_SATTN_IN + n_alias:]
    nchunk = ssd_in[0].shape[0] // BLK

    def sample_part(ci):
        share = range(ci * nb // nchunk, (ci + 1) * nb // nchunk)
        _sample_attn_body(*sattn_in, o_ref, vo_ref, nb=nb, has_sink=False, n_alias=n_alias, w=w,
                          batches=share, write_k=False)

    _ssd_prompt_body(*ssd_in, mo_ref, hs_ref, xp_ref, before_chunk=sample_part)


def _hosted_sample_attn(sample_c, steps, n_host_in, n_host_out):
    q0, q1, kn, vn, cache_k, cache_v, layer, mult_c, mult_n, prev_v = sample_c
    nbatches = q0.shape[0] // TILE
    nb = nbatches // steps
    assert nb * steps == nbatches
    w = cache_k.shape[3]
    srow_spec = lambda wd: pl.BlockSpec((TILE * nb, wd), lambda c: (c, 0))
    cache_spec = pl.BlockSpec((None, nb, KV_WIDTH, w), lambda c: (layer, c, 0, 0))
    in_specs = [srow_spec(128), srow_spec(128), srow_spec(KV_WIDTH), srow_spec(KV_WIDTH), cache_spec, cache_spec,
                _const_spec((4 * TILE, w)), _const_spec((4 * TILE, 128))]
    args = [q0, q1, kn, vn, cache_k, cache_v, mult_c, mult_n]
    assert len(args) == _N_SATTN_IN
    aliases = {}
    if prev_v is not None:
        aliases = {n_host_in + len(args): n_host_out + 1}
        in_specs = in_specs + [pl.BlockSpec(memory_space=pl.ANY)]
        args = args + [prev_v]
    out_specs = [srow_spec(Q_WIDTH), cache_spec]
    out_shape = [jax.ShapeDtypeStruct((nbatches * TILE, Q_WIDTH), BF16), jax.ShapeDtypeStruct(cache_v.shape, F32)]
    return nb, w, in_specs, args, aliases, out_specs, out_shape


def _hosted_cache_shift(shift_c, steps, n_host_in, n_host_out):
    new_rows, cache, layer, prev_out = shift_c
    nbatches = new_rows.shape[0] // TILE
    nb = nbatches // steps
    assert nb * steps == nbatches
    w = cache.shape[3]
    cache_spec = pl.BlockSpec((None, nb, KV_WIDTH, w), lambda c: (layer, c, 0, 0))
    in_specs = [pl.BlockSpec((TILE * nb, KV_WIDTH), lambda c: (c, 0)), cache_spec]
    args = [new_rows, cache]
    aliases = {}
    if prev_out is not None:
        aliases = {n_host_in + len(args): n_host_out}
        in_specs = in_specs + [pl.BlockSpec(memory_space=pl.ANY)]
        args = args + [prev_out]
    return nb, w, in_specs, args, aliases, [cache_spec], [jax.ShapeDtypeStruct(cache.shape, F32)]


def _ssd_prompt(xbc, z, dt, lw, sample_c):
    L = xbc.shape[0]
    rows = SSD_ROWS
    steps = L // rows
    row_spec = lambda wd: pl.BlockSpec((rows, wd), lambda c: (c, 0))
    in_specs = [row_spec(CONV_DIM), row_spec(B_INNER), row_spec(128),
                _const_spec((CONV_K, CONV_DIM)), _const_spec((1, CONV_DIM)), _const_spec((1, 128)),
                _const_spec((1, 128)), _const_spec((1, B_INNER)), _const_spec((1, B_INNER)),
                _const_spec((128, B_INNER)), _const_spec((B_INNER, 128)),
                _const_spec(((CONV_K - 1) * BLK, 2 * BLK))]
    args = [xbc, z, dt, lw['conv_w'], lw['conv_b'], lw['dt_bias'], lw['a_log'], lw['d_skip'], lw['ssm_norm'],
            lw['expand'], lw['expand_t'], lw['conv_shift']]
    assert len(args) == _N_SSD_IN
    nb, w, s_specs, s_args, aliases, s_out_specs, s_out_shape = _hosted_sample_attn(sample_c, steps, _N_SSD_IN, 2)
    return pl.pallas_call(
        functools.partial(_ssd_with_sample_attn_body, nb=nb, n_alias=len(aliases), w=w),
        grid=(steps,),
        in_specs=in_specs + s_specs,
        out_specs=[row_spec(B_INNER), pl.BlockSpec((B_INNER, B_STATE), lambda c: (0, 0))] + s_out_specs,
        out_shape=[jax.ShapeDtypeStruct((L, B_INNER), BF16),
                   jax.ShapeDtypeStruct((B_INNER, B_STATE), F32)] + s_out_shape,
        scratch_shapes=[pltpu.VMEM((rows + TILE, CONV_DIM), F32)],
        input_output_aliases=aliases,
        compiler_params=_params(("arbitrary",)),
        name="ssd_prompt",
    )(*(args + s_args))


def _ssd_sample_body(*refs):
    (xbc_ref, pre_ref, z_ref, dt_ref, h0_ref, cw_ref, cb_ref, dtb_ref, alog_ref, dskip_ref,
     nw_ref, expand_ref, expand_t_ref, shift_ref) = refs[:14]
    mo_ref, hout_ref, xp_ref = refs[-3:]
    t = BLK
    nbt = t // TILE
    rmod = lax.broadcasted_iota(jnp.int32, (t, 1), 0) & (TILE - 1)
    row_ok = rmod >= TOK0
    xp_ref[0:TILE, :] = jnp.zeros((TILE, CONV_DIM), F32)
    xp_ref[TILE:, :] = jnp.where(row_ok, xbc_ref[...], pre_ref[...])
    act = _conv_act(xp_ref, cw_ref, cb_ref, shift_ref)

    row = lax.broadcasted_iota(jnp.int32, (t, t), 0)
    col = lax.broadcasted_iota(jnp.int32, (t, t), 1)
    same = (row // TILE) == (col // TILE)
    tri_mask = (col <= row) & same
    tri_sel = tri_mask.astype(BF16)
    last_sel = (col == (row // TILE) * TILE + (TILE - 1)).astype(BF16)
    expand = expand_ref[...]
    xs, bb, cb_, xdt, acum, acum_t, ydiag = _ssd_intra(
        act, dt_ref[...], dtb_ref[...], alog_ref[...], tri_sel, tri_mask, expand, row_ok)

    yoffs = []
    for b in range(nbt):
        hb = h0_ref[b].astype(BF16)
        yoffs.append(jnp.concatenate(
            [_dot_nt(cb_[TILE * b:TILE * (b + 1), 128 * g:128 * (g + 1)], hb[256 * g:256 * (g + 1), :])
             for g in range(2)], axis=1))
    yoff = jnp.concatenate(yoffs, axis=0)
    e_all = _sel_dot_lhs(jnp.exp(acum), expand)
    y = ydiag + yoff * e_all + xs * dskip_ref[...]
    mo_ref[...] = _ssd_finish(y, z_ref[...], nw_ref[...])

    last = _sel_dot_rhs(last_sel, acum)
    dec_e = _sel_dot_lhs(jnp.exp(last - acum), expand)
    xw = xdt * dec_e
    dcol = _sel_dot_rhs(expand_t_ref[...], jnp.exp(acum_t))
    lane = lax.broadcasted_iota(jnp.int32, (256, t), 1)
    xw_ts = [xw[:, 256 * g:256 * (g + 1)].T for g in range(2)]
    for b in range(nbt):
        in_b = (lane // TILE) == b
        cd = dcol[:, TILE * b + TILE - 1:TILE * b + TILE]
        h0 = h0_ref[b]
        for g in range(2):
            lhs = jnp.where(in_b, xw_ts[g], 0.0).astype(BF16)
            dh = jnp.dot(lhs, bb[:, 128 * g:128 * (g + 1)], preferred_element_type=F32)
            hout_ref[b, 256 * g:256 * (g + 1), :] = (
                h0[256 * g:256 * (g + 1), :] * cd[256 * g:256 * (g + 1), :] + dh)


def _ssd_sample(xbc, prefix, z, dt, state, layer, lw, prev_out):
    rows = xbc.shape[0]
    nbt = BLK // TILE
    row_spec = lambda w: pl.BlockSpec((BLK, w), lambda i: (i, 0))
    st_spec = pl.BlockSpec((None, nbt, B_INNER, B_STATE), lambda i: (layer, i, 0, 0))
    in_specs = [row_spec(CONV_DIM), row_spec(CONV_DIM), row_spec(B_INNER), row_spec(128), st_spec,
                _const_spec((CONV_K, CONV_DIM)), _const_spec((1, CONV_DIM)), _const_spec((1, 128)),
                _const_spec((1, 128)), _const_spec((1, B_INNER)), _const_spec((1, B_INNER)),
                _const_spec((128, B_INNER)), _const_spec((B_INNER, 128)),
                _const_spec(((CONV_K - 1) * BLK, 2 * BLK))]
    args = [xbc, prefix, z, dt, state, lw['conv_w'], lw['conv_b'], lw['dt_bias'], lw['a_log'], lw['d_skip'],
            lw['ssm_norm'], lw['expand'], lw['expand_t'], lw['conv_shift']]
    aliases = {}
    if prev_out is not None:
        aliases = {len(args): 1}
        in_specs.append(pl.BlockSpec(memory_space=pl.ANY))
        args.append(prev_out)
    return pl.pallas_call(
        _ssd_sample_body,
        grid=(rows // BLK,),
        in_specs=in_specs,
        out_specs=[row_spec(B_INNER), st_spec],
        out_shape=[jax.ShapeDtypeStruct((rows, B_INNER), BF16), jax.ShapeDtypeStruct(state.shape, F32)],
        scratch_shapes=[pltpu.VMEM((BLK + TILE, CONV_DIM), F32)],
        input_output_aliases=aliases,
        compiler_params=_params(("parallel",)),
        name="ssd_sample",
    )(*args)


def _out_ffn_body(h_ref, a_ref, m_ref, c_ref, wo_ref, g2_ref, wg_ref, wu_ref, wd_ref, out_ref):
    d = functools.partial(jnp.dot, preferred_element_type=F32)
    h1 = (h_ref[...] + d(a_ref[...], wo_ref[0:256, :]) + d(m_ref[...], wo_ref[256:768, :])
          + d(c_ref[...].astype(BF16), wo_ref[768:1024, :]))
    ms = jnp.mean(h1 * h1, axis=-1, keepdims=True)
    u = (h1 * lax.rsqrt(ms + NORM_EPS) * g2_ref[...]).astype(BF16)
    out_ref[...] = h1
    for c in range(D_FF // FF_CHUNK):
        sl = slice(FF_CHUNK * c, FF_CHUNK * (c + 1))
        act = (_silu(d(u, wg_ref[:, sl])) * d(u, wu_ref[:, sl])).astype(BF16)
        out_ref[...] += d(act, wd_ref[sl, :])


_N_FFN_IN = 9


def _out_ffn_with_cache_shift_body(*refs, nb, w):
    ffn_in = refs[:_N_FFN_IN]
    out_ref, ko_ref = refs[-2:]
    _cache_shift_body(*refs[_N_FFN_IN:-2], ko_ref, nb=nb, w=w)
    _out_ffn_body(*ffn_in, out_ref)


def _out_ffn(h, a_o, m_o, c_o, lw, tm, sample_c=None):
    rows = h.shape[0]
    steps = rows // tm
    row_spec = lambda w: pl.BlockSpec((tm, w), lambda i: (i, 0))
    in_specs = [row_spec(D_MODEL), row_spec(Q_WIDTH), row_spec(B_INNER), row_spec(Q_WIDTH),
                _layer_spec((D_MODEL, D_MODEL), lw['layer']), _const_spec((1, D_MODEL)),
                _layer_spec((D_MODEL, D_FF), lw['layer']), _layer_spec((D_MODEL, D_FF), lw['layer']),
                _layer_spec((D_FF, D_MODEL), lw['layer'])]
    args = [h, a_o, m_o, c_o, lw['w_out'], lw['norm2'], lw['w_gate'], lw['w_up'], lw['w_down']]
    assert len(args) == _N_FFN_IN
    out_spec = row_spec(D_MODEL)
    out_shape = jax.ShapeDtypeStruct((rows, D_MODEL), F32)
    if sample_c is None:
        return pl.pallas_call(
            _out_ffn_body,
            grid=(steps,), in_specs=in_specs, out_specs=out_spec, out_shape=out_shape,
            compiler_params=_params(("parallel",)),
            name="out_ffn",
        )(*args)
    nb, w, s_specs, s_args, aliases, s_out_specs, s_out_shape = _hosted_cache_shift(sample_c, steps, _N_FFN_IN, 1)
    return pl.pallas_call(
        functools.partial(_out_ffn_with_cache_shift_body, nb=nb, w=w),
        grid=(steps,), in_specs=in_specs + s_specs, out_specs=[out_spec] + s_out_specs,
        out_shape=[out_shape] + s_out_shape,
        input_output_aliases=aliases,
        compiler_params=_params(("parallel",)),
        name="out_ffn_host",
    )(*(args + s_args))


def _rope_table(pos):
    half = ROT_DIM // 2
    inv = ROPE_THETA ** (-(jnp.arange(half, dtype=F32) * 2.0 / ROT_DIM))
    ang = pos.astype(F32)[None, :] * inv[:, None]
    cs = jnp.concatenate([jnp.cos(ang), jnp.sin(ang)], axis=0)
    hi = cs.astype(BF16).astype(F32)
    mid = (cs - hi).astype(BF16).astype(F32)
    lo = (cs - hi - mid).astype(BF16).astype(F32)
    return jnp.concatenate([hi, mid, lo], axis=0)


def _rope_selection():
    half = ROT_DIM // 2
    sel = np.zeros((128, 256), np.float32)
    for term in range(3):
        base = term * ROT_DIM
        for lane in range(128):
            f = lane % HEAD_DIM
            if f < half:
                sel[base + f, lane] = 1.0
                sel[base + half + f, 128 + lane] = -1.0
            elif f < ROT_DIM:
                sel[base + f - half, lane] = 1.0
                sel[base + f, 128 + lane] = 1.0
    return sel


_ROPE_SEL = _rope_selection()


def _permute_heads(m, start, axis):
    sl = lambda a, b: lax.slice_in_dim(m, a, b, axis=axis)
    pieces = [sl(0, start)] if start else []
    pieces += [sl(start, start + 64), sl(start + 128, start + 192), sl(start + 64, start + 128),
               sl(start + 192, start + 256), sl(start + 256, m.shape[axis])]
    return jnp.concatenate(pieces, axis=axis)


def _matmul_weights(w_in, w_out, w_gate, w_up, w_down):
    w = _permute_heads(_permute_heads(w_in.astype(BF16), _OFF['aq'], 2), _OFF['cq'], 2)
    w = jnp.pad(w, ((0, 0), (0, 0), (0, N_IN_PAD - N_IN)))
    wo = _permute_heads(_permute_heads(w_out.astype(BF16), 0, 1), 768, 1)
    return dict(w_in=w, w_out=wo, w_gate=w_gate.astype(BF16), w_up=w_up.astype(BF16), w_down=w_down.astype(BF16))


def _layer_weights(l, mm, norm1, a_qn, a_kn, a_sinks, c_qn, c_kn, conv_w, conv_b, dt_bias, a_log, d_skip,
                   ssm_norm, norm2):
    pad8 = lambda v: jnp.pad(v.astype(F32), (0, 128 - B_HEADS))[None, :]
    head_of = np.arange(B_INNER) // 64
    expand = (np.arange(128)[:, None] == head_of[None, :]).astype(np.float32)
    blk = (np.arange(256)[:, None] // 64 == np.arange(256)[None, :] // 64).astype(np.float32) / 64.0
    return dict(
        layer=l, norm1=norm1[l][None, :], w_in=mm['w_in'],
        a_qn=jnp.tile(a_qn[l], 4)[None, :], a_kn=jnp.tile(a_kn[l], 2)[None, :],
        c_qn=jnp.tile(c_qn[l], 4)[None, :], c_kn=jnp.tile(c_kn[l], 2)[None, :],
        bd=jnp.asarray(blk, BF16),
        sink_lanes=jnp.repeat(a_sinks[l].astype(F32)[jnp.asarray([0, 2, 1, 3])], HEAD_DIM).reshape(2, 128),
        sink_rows=jnp.broadcast_to(
            jnp.repeat(a_sinks[l].astype(F32)[jnp.asarray([0, 2, 1, 3])], TILE)[:, None], (4 * TILE, 128)),
        conv_w=conv_w[l], conv_b=conv_b[l][None, :], dt_bias=pad8(dt_bias[l]), a_log=pad8(a_log[l]),
        d_skip=jnp.repeat(d_skip[l].astype(F32), 64)[None, :], ssm_norm=ssm_norm[l][None, :],
        expand=jnp.asarray(expand, BF16), expand_t=jnp.asarray(expand.T, BF16),
        conv_shift=jnp.asarray(_conv_shift_selection(), BF16),
        w_out=mm['w_out'], norm2=norm2[l][None, :], w_gate=mm['w_gate'], w_up=mm['w_up'], w_down=mm['w_down'])


def kernel(x_prompt, x_sample, cache_a_k, cache_a_v, cache_c_k, cache_c_v, state_ssm, state_conv, norm1, w_in,
           a_qn, a_kn, a_sinks, c_qn, c_kn, conv_w, conv_b, dt_bias, a_log, d_skip, ssm_norm, w_out, norm2,
           w_gate, w_up, w_down):
    depth = w_in.shape[0]
    batch, seq, _ = x_prompt.shape
    nbatch, dec_seq, _ = x_sample.shape
    assert batch == 1 and dec_seq == TILE - TOK0 and seq % (16 * BLK) == 0 and nbatch % (BLK // TILE) == 0
    past_len = PAST_LEN
    a_buf, c_buf = cache_a_k.shape[2], cache_c_k.shape[2]
    assert a_buf == A_WINDOW and c_buf == C_SPAN

    hp = x_prompt.reshape(seq, D_MODEL)
    hs = jnp.pad(x_sample, ((0, 0), (TOK0, 0), (0, 0))).reshape(nbatch * TILE, D_MODEL)
    tab_p = _rope_table(jnp.arange(seq))
    pos_s = past_len + jnp.maximum(jnp.arange(TILE) - TOK0, 0)
    tab_s = jnp.tile(_rope_table(pos_s), (1, nbatch))
    ma_c, ma_n, mc_c, mc_n = _sample_mult_tables()

    to_fm = lambda c: jnp.transpose(c, (0, 1, 3, 4, 2)).reshape(depth, nbatch, KV_WIDTH, c.shape[2])
    from_fm = lambda c: jnp.transpose(c.reshape(depth, nbatch, 2, HEAD_DIM, c.shape[3]), (0, 1, 4, 2, 3))
    ca_k, ca_v, cc_k, cc_v = to_fm(cache_a_k), to_fm(cache_a_v), to_fm(cache_c_k), to_fm(cache_c_v)
    st = state_ssm.reshape(depth, nbatch, B_INNER, B_STATE)

    mm = _matmul_weights(w_in, w_out, w_gate, w_up, w_down)
    p_out = [[] for _ in range(6)]
    s_conv = []
    new_a = new_ck = new_cv = new_st = None
    tm_p = 512
    tm_s = min(512, nbatch * TILE)
    for l in range(depth):
        lw = _layer_weights(l, mm, norm1, a_qn, a_kn, a_sinks, c_qn, c_kn, conv_w, conv_b, dt_bias, a_log,
                            d_skip, ssm_norm, norm2)
        s_qa0, s_qa1, s_ka, s_va, s_qc0, s_qc1, s_kc, s_vc, s_z, s_xbc, s_dt = _in_proj(hs, lw, tab_s, tm_s)

        qa0, qa1, ka, va, qc0, qc1, kc, vc, z, xbc, dt = _in_proj(hp, lw, tab_p, tm_p)
        a_o = _prompt_attn(qa0, qa1, ka, va, A_PATTERNS, sink_lanes=lw['sink_lanes'], name="attn_a")
        c_o = _prompt_attn(qc0, qc1, kc, vc, C_BANDS, name="attn_c")
        m_o, h_fin, s_c_o, new_cv = _ssd_prompt(
            xbc, z, dt, lw, (s_qc0, s_qc1, s_kc, s_vc, cc_k, cc_v, l, mc_c, mc_n, new_cv))
        hp, new_ck = _out_ffn(hp, a_o, m_o, c_o, lw, FFN_ROWS, (s_kc, cc_k, l, new_ck))
        p_out[0].append(ka[seq - a_buf:].reshape(1, a_buf, 2, HEAD_DIM))
        p_out[1].append(va[seq - a_buf:].reshape(1, a_buf, 2, HEAD_DIM))
        p_out[2].append(kc[seq - c_buf:].reshape(1, c_buf, 2, HEAD_DIM))
        p_out[3].append(vc[seq - c_buf:].reshape(1, c_buf, 2, HEAD_DIM))
        p_out[4].append(h_fin.reshape(1, B_HEADS, 64, B_STATE))
        p_out[5].append(xbc[seq - (CONV_K - 1):].reshape(1, CONV_K - 1, CONV_DIM))

        a_o, *new_a = _sample_attn(s_qa0, s_qa1, s_ka, s_va, ca_k, ca_v, l, ma_c, ma_n, lw['sink_rows'], 8, new_a)
        prefix = jnp.pad(state_conv[l], ((0, 0), (1, TILE - CONV_K), (0, 0))).reshape(nbatch * TILE, CONV_DIM)
        m_o, new_st = _ssd_sample(s_xbc, prefix, s_z, s_dt, st, l, lw, new_st)
        hs = _out_ffn(hs, a_o, m_o, s_c_o, lw, tm_s)
        s_conv.append(s_xbc.reshape(nbatch, TILE, CONV_DIM)[:, TILE - (CONV_K - 1):])

    outs_p = [jnp.stack(t, axis=0) for t in p_out]
    outs_s = [from_fm(new_a[0]), from_fm(new_a[1]), from_fm(new_ck), from_fm(new_cv),
              new_st.reshape(depth, nbatch, B_HEADS, 64, B_STATE), jnp.stack(s_conv, axis=0)]
    y_p = hp.reshape(1, seq, D_MODEL)
    y_s = hs.reshape(nbatch, TILE, D_MODEL)[:, TOK0:]
    return (y_p, y_s, *outs_p, *outs_s)
```

```python
import functools
import math

import numpy as np
import jax
import jax.numpy as jnp
from jax import lax
from jax.experimental import pallas as pl
from jax.experimental.pallas import tpu as pltpu

F32 = jnp.float32
BF16 = jnp.bfloat16

D_MODEL = 1024
HEAD_DIM = 64
ROT_DIM = 16
ROPE_THETA = 500000.0
NORM_EPS = 1e-6
Q_WIDTH = 256
KV_WIDTH = 128
PAST_LEN = 16384
A_WINDOW = 128
C_PATTERNS = ((128, 1), (512, 4), (2048, 16))
C_SPAN = 2048
B_HEADS = 8
B_INNER = 512
B_STATE = 128
CONV_K = 4
CONV_DIM = 1024
D_FF = 2816
N_IN = 2568
N_IN_PAD = 2688
BLK = 128
ATTN_SUPER = 2048
A_PATTERNS = ((1, A_WINDOW - 1),)
C_BANDS = tuple((d, w // d) for w, d in C_PATTERNS)
TOK0 = 4
TILE = 8
FF_CHUNK = 256
IN_PROJ_GROUP = 512
SSD_ROWS = 256
FFN_ROWS = 512
VMEM_LIMIT = 56 * 1024 * 1024

_OFF = dict(aq=0, ak=256, av=384, cq=512, ck=768, cv=896, z=1024, xbc=1536, dt=2560, end=N_IN_PAD)
_HEAD_PERM = np.concatenate([np.arange(0, 64), np.arange(128, 192), np.arange(64, 128), np.arange(192, 256)])


def _const_spec(shape):
    nd = len(shape)
    return pl.BlockSpec(shape, lambda *_: (0,) * nd, pipeline_mode=pl.Buffered(1))


def _layer_spec(shape, layer):
    nd = len(shape)
    return pl.BlockSpec((None,) + tuple(shape), lambda *_: (layer,) + (0,) * nd, pipeline_mode=pl.Buffered(1))


def _params(sem):
    return pltpu.CompilerParams(dimension_semantics=sem, vmem_limit_bytes=VMEM_LIMIT)


def _split3(v):
    hi = v.astype(BF16)
    r1 = v - hi.astype(F32)
    mid = r1.astype(BF16)
    lo = (r1 - mid.astype(F32)).astype(BF16)
    return hi, mid, lo


def _sel_dot_rhs(sel, v):
    hi, mid, lo = _split3(v)
    d = functools.partial(jnp.dot, preferred_element_type=F32)
    return d(sel, hi) + d(sel, mid) + d(sel, lo)


def _sel_dot_lhs(v, sel):
    hi, mid, lo = _split3(v)
    d = functools.partial(jnp.dot, preferred_element_type=F32)
    return d(hi, sel) + d(mid, sel) + d(lo, sel)


def _dot_nt(a, b):
    return lax.dot_general(a, b, (((1,), (1,)), ((), ())), preferred_element_type=F32)


def _silu(x):
    return x * jax.nn.sigmoid(x)


def _softplus(x):
    return jnp.maximum(x, 0.0) + jnp.log(1.0 + jnp.exp(-jnp.abs(x)))


def _head_norm(x, gain, bd):
    x2 = x * x
    hi = x2.astype(BF16)
    lo = (x2 - hi.astype(F32)).astype(BF16)
    ms = jnp.dot(hi, bd, preferred_element_type=F32) + jnp.dot(lo, bd, preferred_element_type=F32)
    return x * lax.rsqrt(ms + NORM_EPS) * gain


def _rope(x, cos_t, sin_t):
    w = x.shape[1]
    lane = lax.broadcasted_iota(jnp.int32, x.shape, 1) & (HEAD_DIM - 1)
    partner = jnp.where(lane < ROT_DIM // 2, pltpu.roll(x, w - ROT_DIM // 2, 1), pltpu.roll(x, ROT_DIM // 2, 1))
    return x * cos_t + partner * sin_t


def _in_proj_body(h_ref, g1_ref, w_ref, cs_ref, sel_ref, gqa_ref, gka_ref, gqc_ref, gkc_ref, bd_ref,
                  qa0_ref, qa1_ref, ka_ref, va_ref, qc0_ref, qc1_ref, kc_ref, vc_ref, z_ref, xbc_ref, dt_ref):
    unrot = ((lax.broadcasted_iota(jnp.int32, (1, 128), 1) & (HEAD_DIM - 1)) >= ROT_DIM).astype(F32)
    bd2 = bd_ref[...]
    bd1 = bd2[:128, :128]
    group = min(IN_PROJ_GROUP, h_ref.shape[0])
    for r0 in range(0, h_ref.shape[0], group):
        rows = slice(r0, r0 + group)
        x = h_ref[rows, :]
        ms = jnp.mean(x * x, axis=-1, keepdims=True)
        u = (x * lax.rsqrt(ms + NORM_EPS) * g1_ref[...]).astype(BF16)

        def proj(name, nxt, u=u):
            return jnp.dot(u, w_ref[:, _OFF[name]:_OFF[nxt]], preferred_element_type=F32)

        xbc_ref[rows, :] = proj('xbc', 'dt')
        z_ref[rows, :] = proj('z', 'xbc')
        dt_ref[rows, :] = proj('dt', 'end')
        cs_t = jnp.concatenate([cs_ref[:, rows], jnp.zeros((128 - 3 * ROT_DIM, group), F32)], axis=0).T.astype(BF16)
        tab = jnp.dot(cs_t, sel_ref[...], preferred_element_type=F32)
        c1, s1 = tab[:, :128] + unrot, tab[:, 128:]
        c2 = jnp.concatenate([c1, c1], axis=1)
        s2 = jnp.concatenate([s1, s1], axis=1)
        qa = _rope(_head_norm(proj('aq', 'ak'), gqa_ref[...], bd2), c2, s2) * 0.125
        qa0_ref[rows, :] = qa[:, :128]
        qa1_ref[rows, :] = qa[:, 128:]
        kva = proj('ak', 'cq')
        ka_ref[rows, :] = _rope(_head_norm(kva[:, :128], gka_ref[...], bd1), c1, s1)
        va_ref[rows, :] = kva[:, 128:]
        qc = _rope(_head_norm(proj('cq', 'ck'), gqc_ref[...], bd2), c2, s2) * 0.125
        qc0_ref[rows, :] = qc[:, :128]
        qc1_ref[rows, :] = qc[:, 128:]
        kvc = proj('ck', 'z')
        kc_ref[rows, :] = _rope(_head_norm(kvc[:, :128], gkc_ref[...], bd1), c1, s1)
        vc_ref[rows, :] = kvc[:, 128:]


def _in_proj(h, lw, tab, tm):
    rows = h.shape[0]
    grid = (rows // tm,)
    row_spec = lambda w: pl.BlockSpec((tm, w), lambda i: (i, 0))
    widths = (128, 128, KV_WIDTH, KV_WIDTH, 128, 128, KV_WIDTH, KV_WIDTH, B_INNER, CONV_DIM, 128)
    dtypes = (F32,) * len(widths)
    return pl.pallas_call(
        _in_proj_body,
        grid=grid,
        in_specs=[row_spec(D_MODEL), _const_spec((1, D_MODEL)), _layer_spec((D_MODEL, N_IN_PAD), lw['layer']),
                  pl.BlockSpec((3 * ROT_DIM, tm), lambda i: (0, i)), _const_spec((128, 256)),
                  _const_spec((1, 256)), _const_spec((1, 128)), _const_spec((1, 256)),
                  _const_spec((1, 128)), _const_spec((256, 256))],
        out_specs=[row_spec(w) for w in widths],
        out_shape=[jax.ShapeDtypeStruct((rows, w), dt) for w, dt in zip(widths, dtypes)],
        compiler_params=_params(("parallel",)),
        name="in_proj",
    )(h, lw['norm1'], lw['w_in'], tab, jnp.asarray(_ROPE_SEL, BF16), lw['a_qn'], lw['a_kn'], lw['c_qn'], lw['c_kn'], lw['bd'])


def _attn_body(*refs, patterns, has_sink, sb):
    if has_sink:
        sink_ref, refs = refs[0], refs[1:]
    q0_ref, q1_ref, kp_ref, kc_ref, vp_ref, vc_ref, o_ref, kk, vv, acc_s, m_s, l_s = refs
    q_refs = (q0_ref, q1_ref)
    j = pl.program_id(0)
    kk[0:sb, :] = kp_ref[...]
    kk[sb:2 * sb, :] = kc_ref[...]
    vv[0:sb, :] = vp_ref[...]
    vv[sb:2 * sb, :] = vc_ref[...]
    row4 = lax.broadcasted_iota(jnp.int32, (4 * BLK, BLK), 0) & (BLK - 1)
    col4 = lax.broadcasted_iota(jnp.int32, (4 * BLK, BLK), 1)
    upper4 = col4 > row4
    g_lo = lax.broadcasted_iota(jnp.int32, (BLK, 128), 1) < HEAD_DIM
    ones_cols = jnp.ones((2 * BLK, 128), BF16)
    nblk = sb // BLK

    diag_here = [md == BLK for _, md in patterns]
    boosts = [[] for _ in patterns]
    for p, (d, md) in enumerate(patterns):
        for q in range(p + 1, len(patterns)):
            dq, mdq = patterns[q]
            if diag_here[p] and (md * d) % dq == 0 and (md * d) // dq < min(BLK, mdq + 1):
                boosts[q].append((md * d) // dq)
                diag_here[p] = False

    for pi, (d, max_dist) in enumerate(patterns):
        nsub = nblk // d
        has_diag = diag_here[pi]
        fdist = jnp.where(upper4, row4 + BLK - col4, row4 - col4)
        mult = jnp.ones((4 * BLK, BLK), F32)
        for f in boosts[pi]:
            mult = mult + (fdist == f).astype(F32)
        w_up = jnp.where(upper4, mult, 0.0)
        w_lo = jnp.where(upper4, 0.0, mult)

        def ld(ref, s0, d=d):
            if d == 1:
                return ref[pl.ds(s0, BLK), :]
            return ref[pl.ds(s0, BLK, stride=d), :]

        def block(t, carry, d=d, nsub=nsub, has_diag=has_diag, first=(pi == 0), ld=ld, w_up=w_up, w_lo=w_lo):
            r_ = t // nsub
            n = t - r_ * nsub
            start = r_ + BLK * d * n
            prev_ok = jnp.logical_or(j > 0, n > 0)
            neg = jnp.where(prev_ok, 0.0, -jnp.inf)
            qb = [ld(q_refs[rr], start) for rr in range(2)]
            kprev, kcur = ld(kk, sb + start - BLK * d), ld(kk, sb + start)
            vprev, vcur = ld(vv, sb + start - BLK * d), ld(vv, sb + start)
            qm = jnp.concatenate([jnp.where(g_lo, qb[0], 0.0), jnp.where(g_lo, 0.0, qb[0]),
                                  jnp.where(g_lo, qb[1], 0.0), jnp.where(g_lo, 0.0, qb[1])], axis=0).astype(BF16)
            kcat = jnp.concatenate([kprev, kcur], axis=0).astype(BF16)
            vcat = jnp.concatenate([jnp.concatenate([vprev, vcur], axis=0).astype(BF16), ones_cols], axis=1)
            s2 = _dot_nt(qm, kcat)
            sp, sc = s2[:, :BLK] + neg, s2[:, BLK:]
            s = jnp.where(upper4, sp, sc)
            mb = jnp.max(s, axis=-1, keepdims=True)
            if has_diag:
                sd = jnp.sum(jnp.where(col4 == row4, sp, 0.0), axis=-1, keepdims=True)
                mb = jnp.maximum(mb, sd)
            e = jnp.exp(s - mb)
            ecat = jnp.concatenate([e * w_up, e * w_lo], axis=1).astype(BF16)
            pvl = jnp.dot(ecat, vcat, preferred_element_type=F32)
            pv, lb = pvl[:, :128], pvl[:, 128:]
            if has_diag:
                ed = jnp.exp(sd - mb)
                lb = lb + ed
                pv = pv + ed * jnp.concatenate([vprev] * 4, axis=0)
            for rr in range(2):
                lo, hi = slice(2 * BLK * rr, 2 * BLK * rr + BLK), slice(2 * BLK * rr + BLK, 2 * BLK * (rr + 1))
                o_b = jnp.where(g_lo, pv[lo], pv[hi])
                m_b = jnp.where(g_lo, mb[lo], mb[hi])
                l_b = jnp.where(g_lo, lb[lo], lb[hi])
                rows = pl.ds(start, BLK) if d == 1 else pl.ds(start, BLK, stride=d)
                if first:
                    m_s[rr, rows, :] = m_b
                    l_s[rr, rows, :] = l_b
                    acc_s[rr, rows, :] = o_b
                else:
                    m_old = m_s[rr, rows, :]
                    m_new = jnp.maximum(m_old, m_b)
                    w_old = jnp.exp(m_old - m_new)
                    w_b = jnp.exp(m_b - m_new)
                    m_s[rr, rows, :] = m_new
                    l_s[rr, rows, :] = w_old * l_s[rr, rows, :] + w_b * l_b
                    acc_s[rr, rows, :] = w_old * acc_s[rr, rows, :] + w_b * o_b
            return carry

        lax.fori_loop(0, nblk, block, 0, unroll={1: 16, 4: 8}.get(d, 4))

    chunk = 256
    for rr in range(2):
        for c in range(sb // chunk):
            rows = slice(chunk * c, chunk * (c + 1))
            m_f, l_f, a_f = m_s[rr, rows, :], l_s[rr, rows, :], acc_s[rr, rows, :]
            if has_sink:
                sk = sink_ref[rr:rr + 1, :]
                m2 = jnp.maximum(m_f, sk)
                w = jnp.exp(m_f - m2)
                o = a_f * w / (l_f * w + jnp.exp(sk - m2))
            else:
                o = a_f / l_f
            o_ref[rows, 128 * rr:128 * (rr + 1)] = o.astype(o_ref.dtype)


def _prompt_attn(q0, q1, k, v, patterns, sink_lanes=None, name="attn"):
    L = q0.shape[0]
    sb = ATTN_SUPER
    cur = lambda j: (j, 0)
    prev = lambda j: (jnp.maximum(j - 1, 0), 0)
    blk = lambda im: pl.BlockSpec((sb, 128), im)
    in_specs = [blk(cur), blk(cur), blk(prev), blk(cur), blk(prev), blk(cur)]
    args = [q0, q1, k, k, v, v]
    if sink_lanes is not None:
        in_specs = [_const_spec((2, 128))] + in_specs
        args = [sink_lanes] + args
    return pl.pallas_call(
        functools.partial(_attn_body, patterns=patterns, has_sink=sink_lanes is not None, sb=sb),
        grid=(L // sb,), in_specs=in_specs, out_specs=pl.BlockSpec((sb, Q_WIDTH), cur),
        out_shape=jax.ShapeDtypeStruct((L, Q_WIDTH), BF16),
        scratch_shapes=[pltpu.VMEM((2 * sb, 128), F32), pltpu.VMEM((2 * sb, 128), F32),
                        pltpu.VMEM((2, sb, 128), F32), pltpu.VMEM((2, sb, 128), F32), pltpu.VMEM((2, sb, 128), F32)],
        compiler_params=_params(("parallel",)),
        name=name,
    )(*args)


def _shift_cache(src, new_t, dst, b, w):
    lane = lax.broadcasted_iota(jnp.int32, (128, 128), 1)
    is_new = lane >= 128 - (TILE - TOK0)
    shifted = pltpu.roll(src, w - (TILE - TOK0), 1)
    new_cols = pltpu.roll(new_t, 128 - TILE, 1)
    if w > 128:
        dst[b, :, 0:w - 128] = shifted[:, 0:w - 128]
    dst[b, :, w - 128:w] = jnp.where(is_new, new_cols, shifted[:, w - 128:w])


def _cache_shift_body(*refs, nb, w):
    new_ref, c_ref, o_ref = refs[0], refs[1], refs[-1]
    pad = jnp.zeros((128 - TILE, 128), F32)
    for b in range(nb):
        new_t = jnp.concatenate([new_ref[TILE * b:TILE * (b + 1), :], pad], axis=0).T
        _shift_cache(c_ref[b], new_t, o_ref, b, w)


def _sample_attn_body(*refs, nb, has_sink, n_alias, w, batches=None, write_k=True):
    if has_sink:
        sink_ref, refs = refs[0], refs[1:]
    q0_ref, q1_ref, kn_ref, vn_ref, kc_ref, vc_ref, mc_ref, mn_ref = refs[:8]
    if write_k:
        o_ref, ko_ref, vo_ref = refs[8 + n_alias:]
    else:
        (o_ref, vo_ref), ko_ref = refs[8 + n_alias:], None
    g_lo = lax.broadcasted_iota(jnp.int32, (TILE, 128), 1) < HEAD_DIM
    mult_c = mc_ref[...]
    mult_n = mn_ref[...]
    pad = jnp.zeros((128 - TILE, 128), F32)
    for b in (range(nb) if batches is None else batches):
        parts = []
        for q_ref in (q0_ref, q1_ref):
            qr = q_ref[TILE * b:TILE * (b + 1), :]
            parts += [jnp.where(g_lo, qr, 0.0), jnp.where(g_lo, 0.0, qr)]
        qm = jnp.concatenate(parts, axis=0).astype(BF16)
        kct = kc_ref[b]
        vct = vc_ref[b]
        knp = jnp.concatenate([kn_ref[TILE * b:TILE * (b + 1), :], pad], axis=0)
        vnp = jnp.concatenate([vn_ref[TILE * b:TILE * (b + 1), :], pad], axis=0)
        knt = knp.T
        vnt = vnp.T
        sc = jnp.where(mult_c > 0, jnp.dot(qm, kct.astype(BF16), preferred_element_type=F32), -jnp.inf)
        sn = jnp.where(mult_n > 0, jnp.dot(qm, knt.astype(BF16), preferred_element_type=F32), -jnp.inf)
        m = jnp.maximum(jnp.max(sc, axis=-1, keepdims=True), jnp.max(sn, axis=-1, keepdims=True))
        if has_sink:
            sk = sink_ref[:, :1]
            m = jnp.maximum(m, sk)
        ec = mult_c * jnp.exp(sc - m)
        en = mult_n * jnp.exp(sn - m)
        den = jnp.sum(ec, axis=-1, keepdims=True) + jnp.sum(en, axis=-1, keepdims=True)
        if has_sink:
            den = den + jnp.exp(sk - m)
        o = (_dot_nt(ec.astype(BF16), vct.astype(BF16))
             + jnp.dot(en.astype(BF16), vnp.astype(BF16), preferred_element_type=F32)) / den
        o_ref[TILE * b:TILE * (b + 1), :] = jnp.concatenate(
            [jnp.where(g_lo, o[0:TILE], o[TILE:2 * TILE]),
             jnp.where(g_lo, o[2 * TILE:3 * TILE], o[3 * TILE:4 * TILE])], axis=1).astype(o_ref.dtype)
        if write_k:
            _shift_cache(kct, knt, ko_ref, b, w)
        _shift_cache(vct, vnt, vo_ref, b, w)


def _sample_attn(q0, q1, kn, vn, cache_k, cache_v, layer, mult_c, mult_n, sinks_rows, nb, prev_out):
    rows = q0.shape[0]
    nbatch = rows // TILE
    depth, _, _, w = cache_k.shape
    row_spec = lambda width: pl.BlockSpec((TILE * nb, width), lambda i: (i, 0))
    cache_spec = pl.BlockSpec((None, nb, KV_WIDTH, w), lambda i: (layer, i, 0, 0))
    in_specs = [row_spec(128), row_spec(128), row_spec(KV_WIDTH), row_spec(KV_WIDTH), cache_spec, cache_spec,
                _const_spec((4 * TILE, w)), _const_spec((4 * TILE, 128))]
    args = [q0, q1, kn, vn, cache_k, cache_v, mult_c, mult_n]
    if sinks_rows is not None:
        in_specs = [_const_spec((4 * TILE, 128))] + in_specs
        args = [sinks_rows] + args
    aliases = {}
    if prev_out is not None:
        aliases = {len(args): 1, len(args) + 1: 2}
        in_specs = in_specs + [pl.BlockSpec(memory_space=pl.ANY)] * 2
        args = args + list(prev_out)
    cache_shape = jax.ShapeDtypeStruct(cache_k.shape, F32)
    return pl.pallas_call(
        functools.partial(_sample_attn_body, nb=nb, has_sink=sinks_rows is not None,
                          n_alias=0 if prev_out is None else 2, w=w),
        grid=(nbatch // nb,), in_specs=in_specs, out_specs=[row_spec(Q_WIDTH), cache_spec, cache_spec],
        out_shape=[jax.ShapeDtypeStruct((rows, Q_WIDTH), BF16), cache_shape, cache_shape],
        input_output_aliases=aliases,
        compiler_params=_params(("parallel",)),
        name=f"sample_attn_w{w}",
    )(*args)


def _sample_mult_tables():
    t = np.arange(TILE) - TOK0
    tq = np.maximum(t, 0)[:, None]
    j = np.arange(A_WINDOW)[None, :]
    da = A_WINDOW + tq - j
    ma_c = ((da >= 0) & (da < A_WINDOW)).astype(np.float32)
    tn = (np.arange(128) - TOK0)[None, :]
    dn = tq - tn
    new_ok = (tn >= 0) & (tn < TILE - TOK0) & (dn >= 0)
    ma_n = (new_ok & (dn < A_WINDOW)).astype(np.float32)

    def mult(d):
        out = np.zeros(d.shape, np.float32)
        for w, dil in C_PATTERNS:
            out += ((d >= 0) & (d <= w) & (d % dil == 0)).astype(np.float32)
        return out

    jc = np.arange(C_SPAN)[None, :]
    mc_c = mult(C_SPAN + tq - jc)
    mc_n = np.where(new_ok, mult(dn), 0.0).astype(np.float32)
    tile4 = lambda a: jnp.asarray(np.tile(a, (4, 1)))
    return tile4(ma_c), tile4(ma_n), tile4(mc_c), tile4(mc_n)


def _ssd_intra(act, dt_raw, dtb, alog, tri_sel, tri_mask, expand, row_ok):
    xs = act[:, :B_INNER]
    bb = act[:, B_INNER:B_INNER + 2 * B_STATE].astype(BF16)
    cb_ = act[:, B_INNER + 2 * B_STATE:].astype(BF16)
    dtv = _softplus(dt_raw + dtb)
    if row_ok is not None:
        dtv = jnp.where(row_ok, dtv, 0.0)
    a = -jnp.exp(alog) * dtv
    acum = _sel_dot_rhs(tri_sel, a)
    acum_t = acum.T
    dt_e = _sel_dot_lhs(dtv, expand)
    xdt = xs * dt_e
    lane_lo = lax.broadcasted_iota(jnp.int32, (BLK, 128), 1) < HEAD_DIM
    ys = []
    for g in range(2):
        cbm = _dot_nt(cb_[:, 128 * g:128 * (g + 1)], bb[:, 128 * g:128 * (g + 1)])
        for pair in range(2):
            xp = xdt[:, 128 * (2 * g + pair):128 * (2 * g + pair + 1)]
            acc = None
            for j in range(2):
                h = 4 * g + 2 * pair + j
                seg = acum[:, h:h + 1] - acum_t[h:h + 1, :]
                lm = jnp.exp(jnp.where(tri_mask, seg, -jnp.inf))
                mh = (cbm * lm).astype(BF16)
                xh = jnp.where(lane_lo if j == 0 else ~lane_lo, xp, 0.0).astype(BF16)
                t = jnp.dot(mh, xh, preferred_element_type=F32)
                acc = t if acc is None else acc + t
            ys.append(acc)
    ydiag = jnp.concatenate(ys, axis=1)
    return xs, bb, cb_, xdt, acum, acum_t, ydiag


def _ssd_finish(y, z, norm_w):
    y = y * _silu(z)
    ms = jnp.mean(y * y, axis=-1, keepdims=True)
    return (y * lax.rsqrt(ms + NORM_EPS) * norm_w).astype(BF16)


def _conv_act(xp_ref, cw_ref, cb_ref, r0=0):
    t = BLK
    base = r0 + TILE - (CONV_K - 1)
    out = cb_ref[...] + xp_ref[base:base + t, :] * cw_ref[0:1, :]
    for j in range(1, CONV_K):
        out = out + xp_ref[base + j:base + j + t, :] * cw_ref[j:j + 1, :]
    return _silu(out)


def _split2_dot(v, sel):
    hi = v.astype(BF16)
    lo = (v - hi.astype(F32)).astype(BF16)
    return jnp.dot(hi, sel, preferred_element_type=F32) + jnp.dot(lo, sel, preferred_element_type=F32)


def _ssd_prompt_chunk(act, z, dt_raw, dtb, alog, dskip, nw, expand, expand_t, hst):
    t = BLK
    row = lax.broadcasted_iota(jnp.int32, (t, t), 0)
    col = lax.broadcasted_iota(jnp.int32, (t, t), 1)
    tri_mask = col <= row
    xs = act[:, :B_INNER]
    bb = act[:, B_INNER:B_INNER + 2 * B_STATE].astype(BF16)
    cb_ = act[:, B_INNER + 2 * B_STATE:].astype(BF16)
    dtv = _softplus(dt_raw + dtb)
    a = -jnp.exp(alog) * dtv
    acum = _sel_dot_rhs(tri_mask.astype(BF16), a)
    acum_t = acum.T
    spread = _split2_dot(jnp.concatenate([dtv, jnp.exp(acum), jnp.exp(acum[t - 1:t, :] - acum)], axis=0), expand)
    dt_e, e_all, dec_e = spread[0:t], spread[t:2 * t], spread[2 * t:3 * t]
    xdt = xs * dt_e
    lane_lo = lax.broadcasted_iota(jnp.int32, (BLK, 128), 1) < HEAD_DIM
    ys = []
    for g in range(2):
        cbm = _dot_nt(cb_[:, 128 * g:128 * (g + 1)], bb[:, 128 * g:128 * (g + 1)])
        for pair in range(2):
            xp = xdt[:, 128 * (2 * g + pair):128 * (2 * g + pair + 1)]
            acc = None
            for j in range(2):
                h = 4 * g + 2 * pair + j
                seg = acum[:, h:h + 1] - acum_t[h:h + 1, :]
                lm = jnp.exp(jnp.where(tri_mask, seg, -jnp.inf))
                mh = (cbm * lm).astype(BF16)
                xh = jnp.where(lane_lo if j == 0 else ~lane_lo, xp, 0.0).astype(BF16)
                part = jnp.dot(mh, xh, preferred_element_type=F32)
                acc = part if acc is None else acc + part
            ys.append(acc)
    ydiag = jnp.concatenate(ys, axis=1)

    hb = hst.astype(BF16)
    yoff = jnp.concatenate([_dot_nt(cb_[:, 128 * g:128 * (g + 1)], hb[256 * g:256 * (g + 1), :]) for g in range(2)],
                           axis=1)
    y = ydiag + yoff * e_all + xs * dskip
    out = _ssd_finish(y, z, nw)

    xw = xdt * dec_e
    last_t = jnp.exp(jnp.broadcast_to(acum_t[:, t - 1:t], (128, 128)))
    hi = last_t.astype(BF16)
    lo = (last_t - hi.astype(F32)).astype(BF16)
    cd = (jnp.dot(expand_t, hi, preferred_element_type=F32)
          + jnp.dot(expand_t, lo, preferred_element_type=F32))
    new = []
    for g in range(2):
        xw_t = xw[:, 256 * g:256 * (g + 1)].T.astype(BF16)
        dh = jnp.dot(xw_t, bb[:, 128 * g:128 * (g + 1)], preferred_element_type=F32)
        new.append(hst[256 * g:256 * (g + 1), :] * cd[256 * g:256 * (g + 1), :] + dh)
    return out, jnp.concatenate(new, axis=0)


def _ssd_prompt_body(xbc_ref, z_ref, dt_ref, cw_ref, cb_ref, dtb_ref, alog_ref, dskip_ref, nw_ref,
                     expand_ref, expand_t_ref, mo_ref, hs_ref, xp_ref, before_chunk=None):
    c = pl.program_id(0)
    rows = xbc_ref.shape[0]

    @pl.when(c == 0)
    def _():
        xp_ref[0:TILE, :] = jnp.zeros((TILE, CONV_DIM), F32)
        hs_ref[...] = jnp.zeros_like(hs_ref)

    xp_ref[TILE:, :] = xbc_ref[...]
    hst = hs_ref[...]
    for ci in range(rows // BLK):
        r0 = BLK * ci
        if before_chunk is not None:
            before_chunk(ci)
        act = _conv_act(xp_ref, cw_ref, cb_ref, r0)
        mo_ref[r0:r0 + BLK, :], hst = _ssd_prompt_chunk(
            act, z_ref[r0:r0 + BLK, :], dt_ref[r0:r0 + BLK, :], dtb_ref[...], alog_ref[...], dskip_ref[...],
            nw_ref[...], expand_ref[...], expand_t_ref[...], hst)
    hs_ref[...] = hst
    xp_ref[0:TILE, :] = xp_ref[rows:rows + TILE, :]


_N_SSD_IN = 11
_N_SATTN_IN = 8


def _ssd_with_sample_attn_body(*refs, nb, n_alias, w):
    ssd_in = refs[:_N_SSD_IN]
    sattn_in = refs[_N_SSD_IN:_N_SSD_IN + _N_SATTN_IN + n_alias]
    mo_ref, hs_ref, o_ref, vo_ref, xp_ref = refs[_N_SSD_IN + _N_SATTN_IN + n_alias:]
    nchunk = ssd_in[0].shape[0] // BLK

    def sample_part(ci):
        share = range(ci * nb // nchunk, (ci + 1) * nb // nchunk)
        _sample_attn_body(*sattn_in, o_ref, vo_ref, nb=nb, has_sink=False, n_alias=n_alias, w=w,
                          batches=share, write_k=False)

    _ssd_prompt_body(*ssd_in, mo_ref, hs_ref, xp_ref, before_chunk=sample_part)


def _hosted_sample_attn(sample_c, steps, n_host_in, n_host_out):
    q0, q1, kn, vn, cache_k, cache_v, layer, mult_c, mult_n, prev_v = sample_c
    nbatches = q0.shape[0] // TILE
    nb = nbatches // steps
    assert nb * steps == nbatches
    w = cache_k.shape[3]
    srow_spec = lambda wd: pl.BlockSpec((TILE * nb, wd), lambda c: (c, 0))
    cache_spec = pl.BlockSpec((None, nb, KV_WIDTH, w), lambda c: (layer, c, 0, 0))
    in_specs = [srow_spec(128), srow_spec(128), srow_spec(KV_WIDTH), srow_spec(KV_WIDTH), cache_spec, cache_spec,
                _const_spec((4 * TILE, w)), _const_spec((4 * TILE, 128))]
    args = [q0, q1, kn, vn, cache_k, cache_v, mult_c, mult_n]
    assert len(args) == _N_SATTN_IN
    aliases = {}
    if prev_v is not None:
        aliases = {n_host_in + len(args): n_host_out + 1}
        in_specs = in_specs + [pl.BlockSpec(memory_space=pl.ANY)]
        args = args + [prev_v]
    out_specs = [srow_spec(Q_WIDTH), cache_spec]
    out_shape = [jax.ShapeDtypeStruct((nbatches * TILE, Q_WIDTH), BF16), jax.ShapeDtypeStruct(cache_v.shape, F32)]
    return nb, w, in_specs, args, aliases, out_specs, out_shape


def _hosted_cache_shift(shift_c, steps, n_host_in, n_host_out):
    new_rows, cache, layer, prev_out = shift_c
    nbatches = new_rows.shape[0] // TILE
    nb = nbatches // steps
    assert nb * steps == nbatches
    w = cache.shape[3]
    cache_spec = pl.BlockSpec((None, nb, KV_WIDTH, w), lambda c: (layer, c, 0, 0))
    in_specs = [pl.BlockSpec((TILE * nb, KV_WIDTH), lambda c: (c, 0)), cache_spec]
    args = [new_rows, cache]
    aliases = {}
    if prev_out is not None:
        aliases = {n_host_in + len(args): n_host_out}
        in_specs = in_specs + [pl.BlockSpec(memory_space=pl.ANY)]
        args = args + [prev_out]
    return nb, w, in_specs, args, aliases, [cache_spec], [jax.ShapeDtypeStruct(cache.shape, F32)]


def _ssd_prompt(xbc, z, dt, lw, sample_c):
    L = xbc.shape[0]
    rows = SSD_ROWS
    steps = L // rows
    row_spec = lambda wd: pl.BlockSpec((rows, wd), lambda c: (c, 0))
    in_specs = [row_spec(CONV_DIM), row_spec(B_INNER), row_spec(128),
                _const_spec((CONV_K, CONV_DIM)), _const_spec((1, CONV_DIM)), _const_spec((1, 128)),
                _const_spec((1, 128)), _const_spec((1, B_INNER)), _const_spec((1, B_INNER)),
                _const_spec((128, B_INNER)), _const_spec((B_INNER, 128))]
    args = [xbc, z, dt, lw['conv_w'], lw['conv_b'], lw['dt_bias'], lw['a_log'], lw['d_skip'], lw['ssm_norm'],
            lw['expand'], lw['expand_t']]
    assert len(args) == _N_SSD_IN
    nb, w, s_specs, s_args, aliases, s_out_specs, s_out_shape = _hosted_sample_attn(sample_c, steps, _N_SSD_IN, 2)
    return pl.pallas_call(
        functools.partial(_ssd_with_sample_attn_body, nb=nb, n_alias=len(aliases), w=w),
        grid=(steps,),
        in_specs=in_specs + s_specs,
        out_specs=[row_spec(B_INNER), pl.BlockSpec((B_INNER, B_STATE), lambda c: (0, 0))] + s_out_specs,
        out_shape=[jax.ShapeDtypeStruct((L, B_INNER), BF16),
                   jax.ShapeDtypeStruct((B_INNER, B_STATE), F32)] + s_out_shape,
        scratch_shapes=[pltpu.VMEM((rows + TILE, CONV_DIM), F32)],
        input_output_aliases=aliases,
        compiler_params=_params(("arbitrary",)),
        name="ssd_prompt",
    )(*(args + s_args))


def _ssd_sample_body(*refs):
    (xbc_ref, pre_ref, z_ref, dt_ref, h0_ref, cw_ref, cb_ref, dtb_ref, alog_ref, dskip_ref,
     nw_ref, expand_ref, expand_t_ref) = refs[:13]
    mo_ref, hout_ref, xp_ref = refs[-3:]
    t = BLK
    nbt = t // TILE
    rmod = lax.broadcasted_iota(jnp.int32, (t, 1), 0) & (TILE - 1)
    row_ok = rmod >= TOK0
    xp_ref[0:TILE, :] = jnp.zeros((TILE, CONV_DIM), F32)
    xp_ref[TILE:, :] = jnp.where(row_ok, xbc_ref[...], pre_ref[...])
    act = _conv_act(xp_ref, cw_ref, cb_ref)

    row = lax.broadcasted_iota(jnp.int32, (t, t), 0)
    col = lax.broadcasted_iota(jnp.int32, (t, t), 1)
    same = (row // TILE) == (col // TILE)
    tri_mask = (col <= row) & same
    tri_sel = tri_mask.astype(BF16)
    last_sel = (col == (row // TILE) * TILE + (TILE - 1)).astype(BF16)
    expand = expand_ref[...]
    xs, bb, cb_, xdt, acum, acum_t, ydiag = _ssd_intra(
        act, dt_ref[...], dtb_ref[...], alog_ref[...], tri_sel, tri_mask, expand, row_ok)

    yoffs = []
    for b in range(nbt):
        hb = h0_ref[b].astype(BF16)
        yoffs.append(jnp.concatenate(
            [_dot_nt(cb_[TILE * b:TILE * (b + 1), 128 * g:128 * (g + 1)], hb[256 * g:256 * (g + 1), :])
             for g in range(2)], axis=1))
    yoff = jnp.concatenate(yoffs, axis=0)
    e_all = _sel_dot_lhs(jnp.exp(acum), expand)
    y = ydiag + yoff * e_all + xs * dskip_ref[...]
    mo_ref[...] = _ssd_finish(y, z_ref[...], nw_ref[...])

    last = _sel_dot_rhs(last_sel, acum)
    dec_e = _sel_dot_lhs(jnp.exp(last - acum), expand)
    xw = xdt * dec_e
    dcol = _sel_dot_rhs(expand_t_ref[...], jnp.exp(acum_t))
    lane = lax.broadcasted_iota(jnp.int32, (256, t), 1)
    xw_ts = [xw[:, 256 * g:256 * (g + 1)].T for g in range(2)]
    for b in range(nbt):
        in_b = (lane // TILE) == b
        cd = dcol[:, TILE * b + TILE - 1:TILE * b + TILE]
        h0 = h0_ref[b]
        for g in range(2):
            lhs = jnp.where(in_b, xw_ts[g], 0.0).astype(BF16)
            dh = jnp.dot(lhs, bb[:, 128 * g:128 * (g + 1)], preferred_element_type=F32)
            hout_ref[b, 256 * g:256 * (g + 1), :] = (
                h0[256 * g:256 * (g + 1), :] * cd[256 * g:256 * (g + 1), :] + dh)


def _ssd_sample(xbc, prefix, z, dt, state, layer, lw, prev_out):
    rows = xbc.shape[0]
    nbt = BLK // TILE
    row_spec = lambda w: pl.BlockSpec((BLK, w), lambda i: (i, 0))
    st_spec = pl.BlockSpec((None, nbt, B_INNER, B_STATE), lambda i: (layer, i, 0, 0))
    in_specs = [row_spec(CONV_DIM), row_spec(CONV_DIM), row_spec(B_INNER), row_spec(128), st_spec,
                _const_spec((CONV_K, CONV_DIM)), _const_spec((1, CONV_DIM)), _const_spec((1, 128)),
                _const_spec((1, 128)), _const_spec((1, B_INNER)), _const_spec((1, B_INNER)),
                _const_spec((128, B_INNER)), _const_spec((B_INNER, 128))]
    args = [xbc, prefix, z, dt, state, lw['conv_w'], lw['conv_b'], lw['dt_bias'], lw['a_log'], lw['d_skip'],
            lw['ssm_norm'], lw['expand'], lw['expand_t']]
    aliases = {}
    if prev_out is not None:
        aliases = {len(args): 1}
        in_specs.append(pl.BlockSpec(memory_space=pl.ANY))
        args.append(prev_out)
    return pl.pallas_call(
        _ssd_sample_body,
        grid=(rows // BLK,),
        in_specs=in_specs,
        out_specs=[row_spec(B_INNER), st_spec],
        out_shape=[jax.ShapeDtypeStruct((rows, B_INNER), BF16), jax.ShapeDtypeStruct(state.shape, F32)],
        scratch_shapes=[pltpu.VMEM((BLK + TILE, CONV_DIM), F32)],
        input_output_aliases=aliases,
        compiler_params=_params(("parallel",)),
        name="ssd_sample",
    )(*args)


def _out_ffn_body(h_ref, a_ref, m_ref, c_ref, wo_ref, g2_ref, wg_ref, wu_ref, wd_ref, out_ref):
    d = functools.partial(jnp.dot, preferred_element_type=F32)
    h1 = (h_ref[...] + d(a_ref[...], wo_ref[0:256, :]) + d(m_ref[...], wo_ref[256:768, :])
          + d(c_ref[...].astype(BF16), wo_ref[768:1024, :]))
    ms = jnp.mean(h1 * h1, axis=-1, keepdims=True)
    u = (h1 * lax.rsqrt(ms + NORM_EPS) * g2_ref[...]).astype(BF16)
    out_ref[...] = h1
    for c in range(D_FF // FF_CHUNK):
        sl = slice(FF_CHUNK * c, FF_CHUNK * (c + 1))
        act = (_silu(d(u, wg_ref[:, sl])) * d(u, wu_ref[:, sl])).astype(BF16)
        out_ref[...] += d(act, wd_ref[sl, :])


_N_FFN_IN = 9


def _out_ffn_with_cache_shift_body(*refs, nb, w):
    ffn_in = refs[:_N_FFN_IN]
    out_ref, ko_ref = refs[-2:]
    _cache_shift_body(*refs[_N_FFN_IN:-2], ko_ref, nb=nb, w=w)
    _out_ffn_body(*ffn_in, out_ref)


def _out_ffn(h, a_o, m_o, c_o, lw, tm, sample_c=None):
    rows = h.shape[0]
    steps = rows // tm
    row_spec = lambda w: pl.BlockSpec((tm, w), lambda i: (i, 0))
    in_specs = [row_spec(D_MODEL), row_spec(Q_WIDTH), row_spec(B_INNER), row_spec(Q_WIDTH),
                _layer_spec((D_MODEL, D_MODEL), lw['layer']), _const_spec((1, D_MODEL)),
                _layer_spec((D_MODEL, D_FF), lw['layer']), _layer_spec((D_MODEL, D_FF), lw['layer']),
                _layer_spec((D_FF, D_MODEL), lw['layer'])]
    args = [h, a_o, m_o, c_o, lw['w_out'], lw['norm2'], lw['w_gate'], lw['w_up'], lw['w_down']]
    assert len(args) == _N_FFN_IN
    out_spec = row_spec(D_MODEL)
    out_shape = jax.ShapeDtypeStruct((rows, D_MODEL), F32)
    if sample_c is None:
        return pl.pallas_call(
            _out_ffn_body,
            grid=(steps,), in_specs=in_specs, out_specs=out_spec, out_shape=out_shape,
            compiler_params=_params(("parallel",)),
            name="out_ffn",
        )(*args)
    nb, w, s_specs, s_args, aliases, s_out_specs, s_out_shape = _hosted_cache_shift(sample_c, steps, _N_FFN_IN, 1)
    return pl.pallas_call(
        functools.partial(_out_ffn_with_cache_shift_body, nb=nb, w=w),
        grid=(steps,), in_specs=in_specs + s_specs, out_specs=[out_spec] + s_out_specs,
        out_shape=[out_shape] + s_out_shape,
        input_output_aliases=aliases,
        compiler_params=_params(("parallel",)),
        name="out_ffn_host",
    )(*(args + s_args))


def _rope_table(pos):
    half = ROT_DIM // 2
    inv = ROPE_THETA ** (-(jnp.arange(half, dtype=F32) * 2.0 / ROT_DIM))
    ang = pos.astype(F32)[None, :] * inv[:, None]
    cs = jnp.concatenate([jnp.cos(ang), jnp.sin(ang)], axis=0)
    hi = cs.astype(BF16).astype(F32)
    mid = (cs - hi).astype(BF16).astype(F32)
    lo = (cs - hi - mid).astype(BF16).astype(F32)
    return jnp.concatenate([hi, mid, lo], axis=0)


def _rope_selection():
    half = ROT_DIM // 2
    sel = np.zeros((128, 256), np.float32)
    for term in range(3):
        base = term * ROT_DIM
        for lane in range(128):
            f = lane % HEAD_DIM
            if f < half:
                sel[base + f, lane] = 1.0
                sel[base + half + f, 128 + lane] = -1.0
            elif f < ROT_DIM:
                sel[base + f - half, lane] = 1.0
                sel[base + f, 128 + lane] = 1.0
    return sel


_ROPE_SEL = _rope_selection()


def _permute_heads(m, start, axis):
    sl = lambda a, b: lax.slice_in_dim(m, a, b, axis=axis)
    pieces = [sl(0, start)] if start else []
    pieces += [sl(start, start + 64), sl(start + 128, start + 192), sl(start + 64, start + 128),
               sl(start + 192, start + 256), sl(start + 256, m.shape[axis])]
    return jnp.concatenate(pieces, axis=axis)


def _matmul_weights(w_in, w_out, w_gate, w_up, w_down):
    w = _permute_heads(_permute_heads(w_in.astype(BF16), _OFF['aq'], 2), _OFF['cq'], 2)
    w = jnp.pad(w, ((0, 0), (0, 0), (0, N_IN_PAD - N_IN)))
    wo = _permute_heads(_permute_heads(w_out.astype(BF16), 0, 1), 768, 1)
    return dict(w_in=w, w_out=wo, w_gate=w_gate.astype(BF16), w_up=w_up.astype(BF16), w_down=w_down.astype(BF16))


def _layer_weights(l, mm, norm1, a_qn, a_kn, a_sinks, c_qn, c_kn, conv_w, conv_b, dt_bias, a_log, d_skip,
                   ssm_norm, norm2):
    pad8 = lambda v: jnp.pad(v.astype(F32), (0, 128 - B_HEADS))[None, :]
    head_of = np.arange(B_INNER) // 64
    expand = (np.arange(128)[:, None] == head_of[None, :]).astype(np.float32)
    blk = (np.arange(256)[:, None] // 64 == np.arange(256)[None, :] // 64).astype(np.float32) / 64.0
    return dict(
        layer=l, norm1=norm1[l][None, :], w_in=mm['w_in'],
        a_qn=jnp.tile(a_qn[l], 4)[None, :], a_kn=jnp.tile(a_kn[l], 2)[None, :],
        c_qn=jnp.tile(c_qn[l], 4)[None, :], c_kn=jnp.tile(c_kn[l], 2)[None, :],
        bd=jnp.asarray(blk, BF16),
        sink_lanes=jnp.repeat(a_sinks[l].astype(F32)[jnp.asarray([0, 2, 1, 3])], HEAD_DIM).reshape(2, 128),
        sink_rows=jnp.broadcast_to(
            jnp.repeat(a_sinks[l].astype(F32)[jnp.asarray([0, 2, 1, 3])], TILE)[:, None], (4 * TILE, 128)),
        conv_w=conv_w[l], conv_b=conv_b[l][None, :], dt_bias=pad8(dt_bias[l]), a_log=pad8(a_log[l]),
        d_skip=jnp.repeat(d_skip[l].astype(F32), 64)[None, :], ssm_norm=ssm_norm[l][None, :],
        expand=jnp.asarray(expand, BF16), expand_t=jnp.asarray(expand.T, BF16),
        w_out=mm['w_out'], norm2=norm2[l][None, :], w_gate=mm['w_gate'], w_up=mm['w_up'], w_down=mm['w_down'])


def kernel(x_prompt, x_sample, cache_a_k, cache_a_v, cache_c_k, cache_c_v, state_ssm, state_conv, norm1, w_in,
           a_qn, a_kn, a_sinks, c_qn, c_kn, conv_w, conv_b, dt_bias, a_log, d_skip, ssm_norm, w_out, norm2,
           w_gate, w_up, w_down):
    depth = w_in.shape[0]
    batch, seq, _ = x_prompt.shape
    nbatch, dec_seq, _ = x_sample.shape
    assert batch == 1 and dec_seq == TILE - TOK0 and seq % (16 * BLK) == 0 and nbatch % (BLK // TILE) == 0
    past_len = PAST_LEN
    a_buf, c_buf = cache_a_k.shape[2], cache_c_k.shape[2]
    assert a_buf == A_WINDOW and c_buf == C_SPAN

    hp = x_prompt.reshape(seq, D_MODEL)
    hs = jnp.pad(x_sample, ((0, 0), (TOK0, 0), (0, 0))).reshape(nbatch * TILE, D_MODEL)
    tab_p = _rope_table(jnp.arange(seq))
    pos_s = past_len + jnp.maximum(jnp.arange(TILE) - TOK0, 0)
    tab_s = jnp.tile(_rope_table(pos_s), (1, nbatch))
    ma_c, ma_n, mc_c, mc_n = _sample_mult_tables()

    to_fm = lambda c: jnp.transpose(c, (0, 1, 3, 4, 2)).reshape(depth, nbatch, KV_WIDTH, c.shape[2])
    from_fm = lambda c: jnp.transpose(c.reshape(depth, nbatch, 2, HEAD_DIM, c.shape[3]), (0, 1, 4, 2, 3))
    ca_k, ca_v, cc_k, cc_v = to_fm(cache_a_k), to_fm(cache_a_v), to_fm(cache_c_k), to_fm(cache_c_v)
    st = state_ssm.reshape(depth, nbatch, B_INNER, B_STATE)

    mm = _matmul_weights(w_in, w_out, w_gate, w_up, w_down)
    p_out = [[] for _ in range(6)]
    s_conv = []
    new_a = new_ck = new_cv = new_st = None
    tm_p = 1024
    tm_s = min(512, nbatch * TILE)
    for l in range(depth):
        lw = _layer_weights(l, mm, norm1, a_qn, a_kn, a_sinks, c_qn, c_kn, conv_w, conv_b, dt_bias, a_log,
                            d_skip, ssm_norm, norm2)
        s_qa0, s_qa1, s_ka, s_va, s_qc0, s_qc1, s_kc, s_vc, s_z, s_xbc, s_dt = _in_proj(hs, lw, tab_s, tm_s)

        qa0, qa1, ka, va, qc0, qc1, kc, vc, z, xbc, dt = _in_proj(hp, lw, tab_p, tm_p)
        a_o = _prompt_attn(qa0, qa1, ka, va, A_PATTERNS, sink_lanes=lw['sink_lanes'], name="attn_a")
        c_o = _prompt_attn(qc0, qc1, kc, vc, C_BANDS, name="attn_c")
        m_o, h_fin, s_c_o, new_cv = _ssd_prompt(
            xbc, z, dt, lw, (s_qc0, s_qc1, s_kc, s_vc, cc_k, cc_v, l, mc_c, mc_n, new_cv))
        hp, new_ck = _out_ffn(hp, a_o, m_o, c_o, lw, FFN_ROWS, (s_kc, cc_k, l, new_ck))
        p_out[0].append(ka[seq - a_buf:].reshape(1, a_buf, 2, HEAD_DIM))
        p_out[1].append(va[seq - a_buf:].reshape(1, a_buf, 2, HEAD_DIM))
        p_out[2].append(kc[seq - c_buf:].reshape(1, c_buf, 2, HEAD_DIM))
        p_out[3].append(vc[seq - c_buf:].reshape(1, c_buf, 2, HEAD_DIM))
        p_out[4].append(h_fin.reshape(1, B_HEADS, 64, B_STATE))
        p_out[5].append(xbc[seq - (CONV_K - 1):].reshape(1, CONV_K - 1, CONV_DIM))

        a_o, *new_a = _sample_attn(s_qa0, s_qa1, s_ka, s_va, ca_k, ca_v, l, ma_c, ma_n, lw['sink_rows'], 8, new_a)
        prefix = jnp.pad(state_conv[l], ((0, 0), (1, TILE - CONV_K), (0, 0))).reshape(nbatch * TILE, CONV_DIM)
        m_o, new_st = _ssd_sample(s_xbc, prefix, s_z, s_dt, st, l, lw, new_st)
        hs = _out_ffn(hs, a_o, m_o, s_c_o, lw, tm_s)
        s_conv.append(s_xbc.reshape(nbatch, TILE, CONV_DIM)[:, TILE - (CONV_K - 1):])

    outs_p = [jnp.stack(t, axis=0) for t in p_out]
    outs_s = [from_fm(new_a[0]), from_fm(new_a[1]), from_fm(new_ck), from_fm(new_cv),
              new_st.reshape(depth, nbatch, B_HEADS, 64, B_STATE), jnp.stack(s_conv, axis=0)]
    y_p = hp.reshape(1, seq, D_MODEL)
    y_s = hs.reshape(nbatch, TILE, D_MODEL)[:, TOK0:]
    return (y_p, y_s, *outs_p, *outs_s)
```

```python
import functools
import math

import numpy as np
import jax
import jax.numpy as jnp
from jax import lax
from jax.experimental import pallas as pl
from jax.experimental.pallas import tpu as pltpu

F32 = jnp.float32
BF16 = jnp.bfloat16

D_MODEL = 1024
HEAD_DIM = 64
ROT_DIM = 16
ROPE_THETA = 500000.0
NORM_EPS = 1e-6
Q_WIDTH = 256
KV_WIDTH = 128
PAST_LEN = 16384
A_WINDOW = 128
C_PATTERNS = ((128, 1), (512, 4), (2048, 16))
C_SPAN = 2048
B_HEADS = 8
B_INNER = 512
B_STATE = 128
CONV_K = 4
CONV_DIM = 1024
D_FF = 2816
N_IN = 2568
N_IN_PAD = 2688
BLK = 128
ATTN_SUPER = 2048
A_PATTERNS = ((1, A_WINDOW - 1),)
C_BANDS = tuple((d, w // d) for w, d in C_PATTERNS)
TOK0 = 4
TILE = 8
FF_CHUNK = 256
IN_PROJ_GROUP = 512
SSD_ROWS = 256
FFN_ROWS = 512
VMEM_LIMIT = 56 * 1024 * 1024

_OFF = dict(aq=0, ak=256, av=384, cq=512, ck=768, cv=896, z=1024, xbc=1536, dt=2560, end=N_IN_PAD)
_HEAD_PERM = np.concatenate([np.arange(0, 64), np.arange(128, 192), np.arange(64, 128), np.arange(192, 256)])


def _const_spec(shape):
    nd = len(shape)
    return pl.BlockSpec(shape, lambda *_: (0,) * nd, pipeline_mode=pl.Buffered(1))


def _layer_spec(shape, layer):
    nd = len(shape)
    return pl.BlockSpec((None,) + tuple(shape), lambda *_: (layer,) + (0,) * nd, pipeline_mode=pl.Buffered(1))


def _params(sem):
    return pltpu.CompilerParams(dimension_semantics=sem, vmem_limit_bytes=VMEM_LIMIT)


def _split3(v):
    hi = v.astype(BF16)
    r1 = v - hi.astype(F32)
    mid = r1.astype(BF16)
    lo = (r1 - mid.astype(F32)).astype(BF16)
    return hi, mid, lo


def _sel_dot_rhs(sel, v):
    hi, mid, lo = _split3(v)
    d = functools.partial(jnp.dot, preferred_element_type=F32)
    return d(sel, hi) + d(sel, mid) + d(sel, lo)


def _sel_dot_lhs(v, sel):
    hi, mid, lo = _split3(v)
    d = functools.partial(jnp.dot, preferred_element_type=F32)
    return d(hi, sel) + d(mid, sel) + d(lo, sel)


def _dot_nt(a, b):
    return lax.dot_general(a, b, (((1,), (1,)), ((), ())), preferred_element_type=F32)


def _silu(x):
    return x * jax.nn.sigmoid(x)


def _softplus(x):
    return jnp.maximum(x, 0.0) + jnp.log(1.0 + jnp.exp(-jnp.abs(x)))


def _head_norm(x, gain, bd):
    x2 = x * x
    hi = x2.astype(BF16)
    lo = (x2 - hi.astype(F32)).astype(BF16)
    ms = jnp.dot(hi, bd, preferred_element_type=F32) + jnp.dot(lo, bd, preferred_element_type=F32)
    return x * lax.rsqrt(ms + NORM_EPS) * gain


def _rope(x, cos_t, sin_t):
    w = x.shape[1]
    lane = lax.broadcasted_iota(jnp.int32, x.shape, 1) & (HEAD_DIM - 1)
    partner = jnp.where(lane < ROT_DIM // 2, pltpu.roll(x, w - ROT_DIM // 2, 1), pltpu.roll(x, ROT_DIM // 2, 1))
    return x * cos_t + partner * sin_t


def _in_proj_body(h_ref, g1_ref, w_ref, cs_ref, sel_ref, gqa_ref, gka_ref, gqc_ref, gkc_ref, bd_ref,
                  qa0_ref, qa1_ref, ka_ref, va_ref, qc0_ref, qc1_ref, kc_ref, vc_ref, z_ref, xbc_ref, dt_ref):
    unrot = ((lax.broadcasted_iota(jnp.int32, (1, 128), 1) & (HEAD_DIM - 1)) >= ROT_DIM).astype(F32)
    bd2 = bd_ref[...]
    bd1 = bd2[:128, :128]
    group = min(IN_PROJ_GROUP, h_ref.shape[0])
    for r0 in range(0, h_ref.shape[0], group):
        rows = slice(r0, r0 + group)
        x = h_ref[rows, :]
        ms = jnp.mean(x * x, axis=-1, keepdims=True)
        u = (x * lax.rsqrt(ms + NORM_EPS) * g1_ref[...]).astype(BF16)

        def proj(name, nxt, u=u):
            return jnp.dot(u, w_ref[:, _OFF[name]:_OFF[nxt]], preferred_element_type=F32)

        xbc_ref[rows, :] = proj('xbc', 'dt')
        z_ref[rows, :] = proj('z', 'xbc')
        dt_ref[rows, :] = proj('dt', 'end')
        cs_t = jnp.concatenate([cs_ref[:, rows], jnp.zeros((128 - 3 * ROT_DIM, group), F32)], axis=0).T.astype(BF16)
        tab = jnp.dot(cs_t, sel_ref[...], preferred_element_type=F32)
        c1, s1 = tab[:, :128] + unrot, tab[:, 128:]
        c2 = jnp.concatenate([c1, c1], axis=1)
        s2 = jnp.concatenate([s1, s1], axis=1)
        qa = _rope(_head_norm(proj('aq', 'ak'), gqa_ref[...], bd2), c2, s2) * 0.125
        qa0_ref[rows, :] = qa[:, :128]
        qa1_ref[rows, :] = qa[:, 128:]
        kva = proj('ak', 'cq')
        ka_ref[rows, :] = _rope(_head_norm(kva[:, :128], gka_ref[...], bd1), c1, s1)
        va_ref[rows, :] = kva[:, 128:]
        qc = _rope(_head_norm(proj('cq', 'ck'), gqc_ref[...], bd2), c2, s2) * 0.125
        qc0_ref[rows, :] = qc[:, :128]
        qc1_ref[rows, :] = qc[:, 128:]
        kvc = proj('ck', 'z')
        kc_ref[rows, :] = _rope(_head_norm(kvc[:, :128], gkc_ref[...], bd1), c1, s1)
        vc_ref[rows, :] = kvc[:, 128:]


def _in_proj(h, lw, tab, tm):
    rows = h.shape[0]
    grid = (rows // tm,)
    row_spec = lambda w: pl.BlockSpec((tm, w), lambda i: (i, 0))
    widths = (128, 128, KV_WIDTH, KV_WIDTH, 128, 128, KV_WIDTH, KV_WIDTH, B_INNER, CONV_DIM, 128)
    dtypes = (F32,) * len(widths)
    return pl.pallas_call(
        _in_proj_body,
        grid=grid,
        in_specs=[row_spec(D_MODEL), _const_spec((1, D_MODEL)), _layer_spec((D_MODEL, N_IN_PAD), lw['layer']),
                  pl.BlockSpec((3 * ROT_DIM, tm), lambda i: (0, i)), _const_spec((128, 256)),
                  _const_spec((1, 256)), _const_spec((1, 128)), _const_spec((1, 256)),
                  _const_spec((1, 128)), _const_spec((256, 256))],
        out_specs=[row_spec(w) for w in widths],
        out_shape=[jax.ShapeDtypeStruct((rows, w), dt) for w, dt in zip(widths, dtypes)],
        compiler_params=_params(("parallel",)),
        name="in_proj",
    )(h, lw['norm1'], lw['w_in'], tab, jnp.asarray(_ROPE_SEL, BF16), lw['a_qn'], lw['a_kn'], lw['c_qn'], lw['c_kn'], lw['bd'])


def _attn_body(*refs, patterns, has_sink, sb):
    if has_sink:
        sink_ref, refs = refs[0], refs[1:]
    q0_ref, q1_ref, kp_ref, kc_ref, vp_ref, vc_ref, o_ref, kk, vv, acc_s, m_s, l_s = refs
    q_refs = (q0_ref, q1_ref)
    j = pl.program_id(0)
    kk[0:sb, :] = kp_ref[...]
    kk[sb:2 * sb, :] = kc_ref[...]
    vv[0:sb, :] = vp_ref[...]
    vv[sb:2 * sb, :] = vc_ref[...]
    row4 = lax.broadcasted_iota(jnp.int32, (4 * BLK, BLK), 0) & (BLK - 1)
    col4 = lax.broadcasted_iota(jnp.int32, (4 * BLK, BLK), 1)
    upper4 = col4 > row4
    g_lo = lax.broadcasted_iota(jnp.int32, (BLK, 128), 1) < HEAD_DIM
    ones_cols = jnp.ones((2 * BLK, 128), BF16)
    nblk = sb // BLK

    diag_here = [md == BLK for _, md in patterns]
    boosts = [[] for _ in patterns]
    for p, (d, md) in enumerate(patterns):
        for q in range(p + 1, len(patterns)):
            dq, mdq = patterns[q]
            if diag_here[p] and (md * d) % dq == 0 and (md * d) // dq < min(BLK, mdq + 1):
                boosts[q].append((md * d) // dq)
                diag_here[p] = False

    for pi, (d, max_dist) in enumerate(patterns):
        nsub = nblk // d
        has_diag = diag_here[pi]
        fdist = jnp.where(upper4, row4 + BLK - col4, row4 - col4)
        mult = jnp.ones((4 * BLK, BLK), F32)
        for f in boosts[pi]:
            mult = mult + (fdist == f).astype(F32)
        w_up = jnp.where(upper4, mult, 0.0)
        w_lo = jnp.where(upper4, 0.0, mult)

        def ld(ref, s0, d=d):
            if d == 1:
                return ref[pl.ds(s0, BLK), :]
            return ref[pl.ds(s0, BLK, stride=d), :]

        def block(t, carry, d=d, nsub=nsub, has_diag=has_diag, first=(pi == 0), ld=ld, w_up=w_up, w_lo=w_lo):
            r_ = t // nsub
            n = t - r_ * nsub
            start = r_ + BLK * d * n
            prev_ok = jnp.logical_or(j > 0, n > 0)
            neg = jnp.where(prev_ok, 0.0, -jnp.inf)
            qb = [ld(q_refs[rr], start) for rr in range(2)]
            kprev, kcur = ld(kk, sb + start - BLK * d), ld(kk, sb + start)
            vprev, vcur = ld(vv, sb + start - BLK * d), ld(vv, sb + start)
            qm = jnp.concatenate([jnp.where(g_lo, qb[0], 0.0), jnp.where(g_lo, 0.0, qb[0]),
                                  jnp.where(g_lo, qb[1], 0.0), jnp.where(g_lo, 0.0, qb[1])], axis=0).astype(BF16)
            kcat = jnp.concatenate([kprev, kcur], axis=0).astype(BF16)
            vcat = jnp.concatenate([jnp.concatenate([vprev, vcur], axis=0).astype(BF16), ones_cols], axis=1)
            s2 = _dot_nt(qm, kcat)
            sp, sc = s2[:, :BLK] + neg, s2[:, BLK:]
            s = jnp.where(upper4, sp, sc)
            mb = jnp.max(s, axis=-1, keepdims=True)
            if has_diag:
                sd = jnp.sum(jnp.where(col4 == row4, sp, 0.0), axis=-1, keepdims=True)
                mb = jnp.maximum(mb, sd)
            e = jnp.exp(s - mb)
            ecat = jnp.concatenate([e * w_up, e * w_lo], axis=1).astype(BF16)
            pvl = jnp.dot(ecat, vcat, preferred_element_type=F32)
            pv, lb = pvl[:, :128], pvl[:, 128:]
            if has_diag:
                ed = jnp.exp(sd - mb)
                lb = lb + ed
                pv = pv + ed * jnp.concatenate([vprev] * 4, axis=0)
            for rr in range(2):
                lo, hi = slice(2 * BLK * rr, 2 * BLK * rr + BLK), slice(2 * BLK * rr + BLK, 2 * BLK * (rr + 1))
                o_b = jnp.where(g_lo, pv[lo], pv[hi])
                m_b = jnp.where(g_lo, mb[lo], mb[hi])
                l_b = jnp.where(g_lo, lb[lo], lb[hi])
                rows = pl.ds(start, BLK) if d == 1 else pl.ds(start, BLK, stride=d)
                if first:
                    m_s[rr, rows, :] = m_b
                    l_s[rr, rows, :] = l_b
                    acc_s[rr, rows, :] = o_b
                else:
                    m_old = m_s[rr, rows, :]
                    m_new = jnp.maximum(m_old, m_b)
                    w_old = jnp.exp(m_old - m_new)
                    w_b = jnp.exp(m_b - m_new)
                    m_s[rr, rows, :] = m_new
                    l_s[rr, rows, :] = w_old * l_s[rr, rows, :] + w_b * l_b
                    acc_s[rr, rows, :] = w_old * acc_s[rr, rows, :] + w_b * o_b
            return carry

        lax.fori_loop(0, nblk, block, 0, unroll={1: 16, 4: 16}.get(d, 8))

    chunk = 256
    for rr in range(2):
        for c in range(sb // chunk):
            rows = slice(chunk * c, chunk * (c + 1))
            m_f, l_f, a_f = m_s[rr, rows, :], l_s[rr, rows, :], acc_s[rr, rows, :]
            if has_sink:
                sk = sink_ref[rr:rr + 1, :]
                m2 = jnp.maximum(m_f, sk)
                w = jnp.exp(m_f - m2)
                o = a_f * w / (l_f * w + jnp.exp(sk - m2))
            else:
                o = a_f / l_f
            o_ref[rows, 128 * rr:128 * (rr + 1)] = o.astype(o_ref.dtype)


def _prompt_attn(q0, q1, k, v, patterns, sink_lanes=None, name="attn"):
    L = q0.shape[0]
    sb = ATTN_SUPER
    cur = lambda j: (j, 0)
    prev = lambda j: (jnp.maximum(j - 1, 0), 0)
    blk = lambda im: pl.BlockSpec((sb, 128), im)
    in_specs = [blk(cur), blk(cur), blk(prev), blk(cur), blk(prev), blk(cur)]
    args = [q0, q1, k, k, v, v]
    if sink_lanes is not None:
        in_specs = [_const_spec((2, 128))] + in_specs
        args = [sink_lanes] + args
    return pl.pallas_call(
        functools.partial(_attn_body, patterns=patterns, has_sink=sink_lanes is not None, sb=sb),
        grid=(L // sb,), in_specs=in_specs, out_specs=pl.BlockSpec((sb, Q_WIDTH), cur),
        out_shape=jax.ShapeDtypeStruct((L, Q_WIDTH), BF16),
        scratch_shapes=[pltpu.VMEM((2 * sb, 128), F32), pltpu.VMEM((2 * sb, 128), F32),
                        pltpu.VMEM((2, sb, 128), F32), pltpu.VMEM((2, sb, 128), F32), pltpu.VMEM((2, sb, 128), F32)],
        compiler_params=_params(("parallel",)),
        name=name,
    )(*args)


def _shift_cache(src, new_t, dst, b, w):
    lane = lax.broadcasted_iota(jnp.int32, (128, 128), 1)
    is_new = lane >= 128 - (TILE - TOK0)
    shifted = pltpu.roll(src, w - (TILE - TOK0), 1)
    new_cols = pltpu.roll(new_t, 128 - TILE, 1)
    if w > 128:
        dst[b, :, 0:w - 128] = shifted[:, 0:w - 128]
    dst[b, :, w - 128:w] = jnp.where(is_new, new_cols, shifted[:, w - 128:w])


def _cache_shift_body(*refs, nb, w):
    new_ref, c_ref, o_ref = refs[0], refs[1], refs[-1]
    pad = jnp.zeros((128 - TILE, 128), F32)
    for b in range(nb):
        new_t = jnp.concatenate([new_ref[TILE * b:TILE * (b + 1), :], pad], axis=0).T
        _shift_cache(c_ref[b], new_t, o_ref, b, w)


def _sample_attn_body(*refs, nb, has_sink, n_alias, w, batches=None, write_k=True):
    if has_sink:
        sink_ref, refs = refs[0], refs[1:]
    q0_ref, q1_ref, kn_ref, vn_ref, kc_ref, vc_ref, mc_ref, mn_ref = refs[:8]
    if write_k:
        o_ref, ko_ref, vo_ref = refs[8 + n_alias:]
    else:
        (o_ref, vo_ref), ko_ref = refs[8 + n_alias:], None
    g_lo = lax.broadcasted_iota(jnp.int32, (TILE, 128), 1) < HEAD_DIM
    mult_c = mc_ref[...]
    mult_n = mn_ref[...]
    pad = jnp.zeros((128 - TILE, 128), F32)
    for b in (range(nb) if batches is None else batches):
        parts = []
        for q_ref in (q0_ref, q1_ref):
            qr = q_ref[TILE * b:TILE * (b + 1), :]
            parts += [jnp.where(g_lo, qr, 0.0), jnp.where(g_lo, 0.0, qr)]
        qm = jnp.concatenate(parts, axis=0).astype(BF16)
        kct = kc_ref[b]
        vct = vc_ref[b]
        knp = jnp.concatenate([kn_ref[TILE * b:TILE * (b + 1), :], pad], axis=0)
        vnp = jnp.concatenate([vn_ref[TILE * b:TILE * (b + 1), :], pad], axis=0)
        knt = knp.T
        vnt = vnp.T
        sc = jnp.where(mult_c > 0, jnp.dot(qm, kct.astype(BF16), preferred_element_type=F32), -jnp.inf)
        sn = jnp.where(mult_n > 0, jnp.dot(qm, knt.astype(BF16), preferred_element_type=F32), -jnp.inf)
        m = jnp.maximum(jnp.max(sc, axis=-1, keepdims=True), jnp.max(sn, axis=-1, keepdims=True))
        if has_sink:
            sk = sink_ref[:, :1]
            m = jnp.maximum(m, sk)
        ec = mult_c * jnp.exp(sc - m)
        en = mult_n * jnp.exp(sn - m)
        den = jnp.sum(ec, axis=-1, keepdims=True) + jnp.sum(en, axis=-1, keepdims=True)
        if has_sink:
            den = den + jnp.exp(sk - m)
        o = (_dot_nt(ec.astype(BF16), vct.astype(BF16))
             + jnp.dot(en.astype(BF16), vnp.astype(BF16), preferred_element_type=F32)) / den
        o_ref[TILE * b:TILE * (b + 1), :] = jnp.concatenate(
            [jnp.where(g_lo, o[0:TILE], o[TILE:2 * TILE]),
             jnp.where(g_lo, o[2 * TILE:3 * TILE], o[3 * TILE:4 * TILE])], axis=1).astype(o_ref.dtype)
        if write_k:
            _shift_cache(kct, knt, ko_ref, b, w)
        _shift_cache(vct, vnt, vo_ref, b, w)


def _sample_attn(q0, q1, kn, vn, cache_k, cache_v, layer, mult_c, mult_n, sinks_rows, nb, prev_out):
    rows = q0.shape[0]
    nbatch = rows // TILE
    depth, _, _, w = cache_k.shape
    row_spec = lambda width: pl.BlockSpec((TILE * nb, width), lambda i: (i, 0))
    cache_spec = pl.BlockSpec((None, nb, KV_WIDTH, w), lambda i: (layer, i, 0, 0))
    in_specs = [row_spec(128), row_spec(128), row_spec(KV_WIDTH), row_spec(KV_WIDTH), cache_spec, cache_spec,
                _const_spec((4 * TILE, w)), _const_spec((4 * TILE, 128))]
    args = [q0, q1, kn, vn, cache_k, cache_v, mult_c, mult_n]
    if sinks_rows is not None:
        in_specs = [_const_spec((4 * TILE, 128))] + in_specs
        args = [sinks_rows] + args
    aliases = {}
    if prev_out is not None:
        aliases = {len(args): 1, len(args) + 1: 2}
        in_specs = in_specs + [pl.BlockSpec(memory_space=pl.ANY)] * 2
        args = args + list(prev_out)
    cache_shape = jax.ShapeDtypeStruct(cache_k.shape, F32)
    return pl.pallas_call(
        functools.partial(_sample_attn_body, nb=nb, has_sink=sinks_rows is not None,
                          n_alias=0 if prev_out is None else 2, w=w),
        grid=(nbatch // nb,), in_specs=in_specs, out_specs=[row_spec(Q_WIDTH), cache_spec, cache_spec],
        out_shape=[jax.ShapeDtypeStruct((rows, Q_WIDTH), BF16), cache_shape, cache_shape],
        input_output_aliases=aliases,
        compiler_params=_params(("parallel",)),
        name=f"sample_attn_w{w}",
    )(*args)


def _sample_mult_tables():
    t = np.arange(TILE) - TOK0
    tq = np.maximum(t, 0)[:, None]
    j = np.arange(A_WINDOW)[None, :]
    da = A_WINDOW + tq - j
    ma_c = ((da >= 0) & (da < A_WINDOW)).astype(np.float32)
    tn = (np.arange(128) - TOK0)[None, :]
    dn = tq - tn
    new_ok = (tn >= 0) & (tn < TILE - TOK0) & (dn >= 0)
    ma_n = (new_ok & (dn < A_WINDOW)).astype(np.float32)

    def mult(d):
        out = np.zeros(d.shape, np.float32)
        for w, dil in C_PATTERNS:
            out += ((d >= 0) & (d <= w) & (d % dil == 0)).astype(np.float32)
        return out

    jc = np.arange(C_SPAN)[None, :]
    mc_c = mult(C_SPAN + tq - jc)
    mc_n = np.where(new_ok, mult(dn), 0.0).astype(np.float32)
    tile4 = lambda a: jnp.asarray(np.tile(a, (4, 1)))
    return tile4(ma_c), tile4(ma_n), tile4(mc_c), tile4(mc_n)


def _ssd_intra(act, dt_raw, dtb, alog, tri_sel, tri_mask, expand, row_ok):
    xs = act[:, :B_INNER]
    bb = act[:, B_INNER:B_INNER + 2 * B_STATE].astype(BF16)
    cb_ = act[:, B_INNER + 2 * B_STATE:].astype(BF16)
    dtv = _softplus(dt_raw + dtb)
    if row_ok is not None:
        dtv = jnp.where(row_ok, dtv, 0.0)
    a = -jnp.exp(alog) * dtv
    acum = _sel_dot_rhs(tri_sel, a)
    acum_t = acum.T
    dt_e = _sel_dot_lhs(dtv, expand)
    xdt = xs * dt_e
    lane_lo = lax.broadcasted_iota(jnp.int32, (BLK, 128), 1) < HEAD_DIM
    ys = []
    for g in range(2):
        cbm = _dot_nt(cb_[:, 128 * g:128 * (g + 1)], bb[:, 128 * g:128 * (g + 1)])
        for pair in range(2):
            xp = xdt[:, 128 * (2 * g + pair):128 * (2 * g + pair + 1)]
            acc = None
            for j in range(2):
                h = 4 * g + 2 * pair + j
                seg = acum[:, h:h + 1] - acum_t[h:h + 1, :]
                lm = jnp.exp(jnp.where(tri_mask, seg, -jnp.inf))
                mh = (cbm * lm).astype(BF16)
                xh = jnp.where(lane_lo if j == 0 else ~lane_lo, xp, 0.0).astype(BF16)
                t = jnp.dot(mh, xh, preferred_element_type=F32)
                acc = t if acc is None else acc + t
            ys.append(acc)
    ydiag = jnp.concatenate(ys, axis=1)
    return xs, bb, cb_, xdt, acum, acum_t, ydiag


def _ssd_finish(y, z, norm_w):
    y = y * _silu(z)
    ms = jnp.mean(y * y, axis=-1, keepdims=True)
    return (y * lax.rsqrt(ms + NORM_EPS) * norm_w).astype(BF16)


def _conv_act(xp_ref, cw_ref, cb_ref, r0=0):
    t = BLK
    base = r0 + TILE - (CONV_K - 1)
    out = cb_ref[...] + xp_ref[base:base + t, :] * cw_ref[0:1, :]
    for j in range(1, CONV_K):
        out = out + xp_ref[base + j:base + j + t, :] * cw_ref[j:j + 1, :]
    return _silu(out)


def _split2_dot(v, sel):
    hi = v.astype(BF16)
    lo = (v - hi.astype(F32)).astype(BF16)
    return jnp.dot(hi, sel, preferred_element_type=F32) + jnp.dot(lo, sel, preferred_element_type=F32)


def _ssd_prompt_chunk(act, z, dt_raw, dtb, alog, dskip, nw, expand, expand_t, hst):
    t = BLK
    row = lax.broadcasted_iota(jnp.int32, (t, t), 0)
    col = lax.broadcasted_iota(jnp.int32, (t, t), 1)
    tri_mask = col <= row
    xs = act[:, :B_INNER]
    bb = act[:, B_INNER:B_INNER + 2 * B_STATE].astype(BF16)
    cb_ = act[:, B_INNER + 2 * B_STATE:].astype(BF16)
    dtv = _softplus(dt_raw + dtb)
    a = -jnp.exp(alog) * dtv
    acum = _sel_dot_rhs(tri_mask.astype(BF16), a)
    acum_t = acum.T
    spread = _split2_dot(jnp.concatenate([dtv, jnp.exp(acum), jnp.exp(acum[t - 1:t, :] - acum)], axis=0), expand)
    dt_e, e_all, dec_e = spread[0:t], spread[t:2 * t], spread[2 * t:3 * t]
    xdt = xs * dt_e
    lane_lo = lax.broadcasted_iota(jnp.int32, (BLK, 128), 1) < HEAD_DIM
    ys = []
    for g in range(2):
        cbm = _dot_nt(cb_[:, 128 * g:128 * (g + 1)], bb[:, 128 * g:128 * (g + 1)])
        for pair in range(2):
            xp = xdt[:, 128 * (2 * g + pair):128 * (2 * g + pair + 1)]
            acc = None
            for j in range(2):
                h = 4 * g + 2 * pair + j
                seg = acum[:, h:h + 1] - acum_t[h:h + 1, :]
                lm = jnp.exp(jnp.where(tri_mask, seg, -jnp.inf))
                mh = (cbm * lm).astype(BF16)
                xh = jnp.where(lane_lo if j == 0 else ~lane_lo, xp, 0.0).astype(BF16)
                part = jnp.dot(mh, xh, preferred_element_type=F32)
                acc = part if acc is None else acc + part
            ys.append(acc)
    ydiag = jnp.concatenate(ys, axis=1)

    hb = hst.astype(BF16)
    yoff = jnp.concatenate([_dot_nt(cb_[:, 128 * g:128 * (g + 1)], hb[256 * g:256 * (g + 1), :]) for g in range(2)],
                           axis=1)
    y = ydiag + yoff * e_all + xs * dskip
    out = _ssd_finish(y, z, nw)

    xw = xdt * dec_e
    last_t = jnp.exp(jnp.broadcast_to(acum_t[:, t - 1:t], (128, 128)))
    hi = last_t.astype(BF16)
    lo = (last_t - hi.astype(F32)).astype(BF16)
    cd = (jnp.dot(expand_t, hi, preferred_element_type=F32)
          + jnp.dot(expand_t, lo, preferred_element_type=F32))
    new = []
    for g in range(2):
        xw_t = xw[:, 256 * g:256 * (g + 1)].T.astype(BF16)
        dh = jnp.dot(xw_t, bb[:, 128 * g:128 * (g + 1)], preferred_element_type=F32)
        new.append(hst[256 * g:256 * (g + 1), :] * cd[256 * g:256 * (g + 1), :] + dh)
    return out, jnp.concatenate(new, axis=0)


def _ssd_prompt_body(xbc_ref, z_ref, dt_ref, cw_ref, cb_ref, dtb_ref, alog_ref, dskip_ref, nw_ref,
                     expand_ref, expand_t_ref, mo_ref, hs_ref, xp_ref, before_chunk=None):
    c = pl.program_id(0)
    rows = xbc_ref.shape[0]

    @pl.when(c == 0)
    def _():
        xp_ref[0:TILE, :] = jnp.zeros((TILE, CONV_DIM), F32)
        hs_ref[...] = jnp.zeros_like(hs_ref)

    xp_ref[TILE:, :] = xbc_ref[...]
    hst = hs_ref[...]
    for ci in range(rows // BLK):
        r0 = BLK * ci
        if before_chunk is not None:
            before_chunk(ci)
        act = _conv_act(xp_ref, cw_ref, cb_ref, r0)
        mo_ref[r0:r0 + BLK, :], hst = _ssd_prompt_chunk(
            act, z_ref[r0:r0 + BLK, :], dt_ref[r0:r0 + BLK, :], dtb_ref[...], alog_ref[...], dskip_ref[...],
            nw_ref[...], expand_ref[...], expand_t_ref[...], hst)
    hs_ref[...] = hst
    xp_ref[0:TILE, :] = xp_ref[rows:rows + TILE, :]


_N_SSD_IN = 11
_N_SATTN_IN = 8


def _ssd_with_sample_attn_body(*refs, nb, n_alias, w):
    ssd_in = refs[:_N_SSD_IN]
    sattn_in = refs[_N_SSD_IN:_N_SSD_IN + _N_SATTN_IN + n_alias]
    mo_ref, hs_ref, o_ref, vo_ref, xp_ref = refs[_N_SSD_IN + _N_SATTN_IN + n_alias:]
    nchunk = ssd_in[0].shape[0] // BLK

    def sample_part(ci):
        share = range(ci * nb // nchunk, (ci + 1) * nb // nchunk)
        _sample_attn_body(*sattn_in, o_ref, vo_ref, nb=nb, has_sink=False, n_alias=n_alias, w=w,
                          batches=share, write_k=False)

    _ssd_prompt_body(*ssd_in, mo_ref, hs_ref, xp_ref, before_chunk=sample_part)


def _hosted_sample_attn(sample_c, steps, n_host_in, n_host_out):
    q0, q1, kn, vn, cache_k, cache_v, layer, mult_c, mult_n, prev_v = sample_c
    nbatches = q0.shape[0] // TILE
    nb = nbatches // steps
    assert nb * steps == nbatches
    w = cache_k.shape[3]
    srow_spec = lambda wd: pl.BlockSpec((TILE * nb, wd), lambda c: (c, 0))
    cache_spec = pl.BlockSpec((None, nb, KV_WIDTH, w), lambda c: (layer, c, 0, 0))
    in_specs = [srow_spec(128), srow_spec(128), srow_spec(KV_WIDTH), srow_spec(KV_WIDTH), cache_spec, cache_spec,
                _const_spec((4 * TILE, w)), _const_spec((4 * TILE, 128))]
    args = [q0, q1, kn, vn, cache_k, cache_v, mult_c, mult_n]
    assert len(args) == _N_SATTN_IN
    aliases = {}
    if prev_v is not None:
        aliases = {n_host_in + len(args): n_host_out + 1}
        in_specs = in_specs + [pl.BlockSpec(memory_space=pl.ANY)]
        args = args + [prev_v]
    out_specs = [srow_spec(Q_WIDTH), cache_spec]
    out_shape = [jax.ShapeDtypeStruct((nbatches * TILE, Q_WIDTH), BF16), jax.ShapeDtypeStruct(cache_v.shape, F32)]
    return nb, w, in_specs, args, aliases, out_specs, out_shape


def _hosted_cache_shift(shift_c, steps, n_host_in, n_host_out):
    new_rows, cache, layer, prev_out = shift_c
    nbatches = new_rows.shape[0] // TILE
    nb = nbatches // steps
    assert nb * steps == nbatches
    w = cache.shape[3]
    cache_spec = pl.BlockSpec((None, nb, KV_WIDTH, w), lambda c: (layer, c, 0, 0))
    in_specs = [pl.BlockSpec((TILE * nb, KV_WIDTH), lambda c: (c, 0)), cache_spec]
    args = [new_rows, cache]
    aliases = {}
    if prev_out is not None:
        aliases = {n_host_in + len(args): n_host_out}
        in_specs = in_specs + [pl.BlockSpec(memory_space=pl.ANY)]
        args = args + [prev_out]
    return nb, w, in_specs, args, aliases, [cache_spec], [jax.ShapeDtypeStruct(cache.shape, F32)]


def _ssd_prompt(xbc, z, dt, lw, sample_c):
    L = xbc.shape[0]
    rows = SSD_ROWS
    steps = L // rows
    row_spec = lambda wd: pl.BlockSpec((rows, wd), lambda c: (c, 0))
    in_specs = [row_spec(CONV_DIM), row_spec(B_INNER), row_spec(128),
                _const_spec((CONV_K, CONV_DIM)), _const_spec((1, CONV_DIM)), _const_spec((1, 128)),
                _const_spec((1, 128)), _const_spec((1, B_INNER)), _const_spec((1, B_INNER)),
                _const_spec((128, B_INNER)), _const_spec((B_INNER, 128))]
    args = [xbc, z, dt, lw['conv_w'], lw['conv_b'], lw['dt_bias'], lw['a_log'], lw['d_skip'], lw['ssm_norm'],
            lw['expand'], lw['expand_t']]
    assert len(args) == _N_SSD_IN
    nb, w, s_specs, s_args, aliases, s_out_specs, s_out_shape = _hosted_sample_attn(sample_c, steps, _N_SSD_IN, 2)
    return pl.pallas_call(
        functools.partial(_ssd_with_sample_attn_body, nb=nb, n_alias=len(aliases), w=w),
        grid=(steps,),
        in_specs=in_specs + s_specs,
        out_specs=[row_spec(B_INNER), pl.BlockSpec((B_INNER, B_STATE), lambda c: (0, 0))] + s_out_specs,
        out_shape=[jax.ShapeDtypeStruct((L, B_INNER), BF16),
                   jax.ShapeDtypeStruct((B_INNER, B_STATE), F32)] + s_out_shape,
        scratch_shapes=[pltpu.VMEM((rows + TILE, CONV_DIM), F32)],
        input_output_aliases=aliases,
        compiler_params=_params(("arbitrary",)),
        name="ssd_prompt",
    )(*(args + s_args))


def _ssd_sample_body(*refs):
    (xbc_ref, pre_ref, z_ref, dt_ref, h0_ref, cw_ref, cb_ref, dtb_ref, alog_ref, dskip_ref,
     nw_ref, expand_ref, expand_t_ref) = refs[:13]
    mo_ref, hout_ref, xp_ref = refs[-3:]
    t = BLK
    nbt = t // TILE
    rmod = lax.broadcasted_iota(jnp.int32, (t, 1), 0) & (TILE - 1)
    row_ok = rmod >= TOK0
    xp_ref[0:TILE, :] = jnp.zeros((TILE, CONV_DIM), F32)
    xp_ref[TILE:, :] = jnp.where(row_ok, xbc_ref[...], pre_ref[...])
    act = _conv_act(xp_ref, cw_ref, cb_ref)

    row = lax.broadcasted_iota(jnp.int32, (t, t), 0)
    col = lax.broadcasted_iota(jnp.int32, (t, t), 1)
    same = (row // TILE) == (col // TILE)
    tri_mask = (col <= row) & same
    tri_sel = tri_mask.astype(BF16)
    last_sel = (col == (row // TILE) * TILE + (TILE - 1)).astype(BF16)
    expand = expand_ref[...]
    xs, bb, cb_, xdt, acum, acum_t, ydiag = _ssd_intra(
        act, dt_ref[...], dtb_ref[...], alog_ref[...], tri_sel, tri_mask, expand, row_ok)

    yoffs = []
    for b in range(nbt):
        hb = h0_ref[b].astype(BF16)
        yoffs.append(jnp.concatenate(
            [_dot_nt(cb_[TILE * b:TILE * (b + 1), 128 * g:128 * (g + 1)], hb[256 * g:256 * (g + 1), :])
             for g in range(2)], axis=1))
    yoff = jnp.concatenate(yoffs, axis=0)
    e_all = _sel_dot_lhs(jnp.exp(acum), expand)
    y = ydiag + yoff * e_all + xs * dskip_ref[...]
    mo_ref[...] = _ssd_finish(y, z_ref[...], nw_ref[...])

    last = _sel_dot_rhs(last_sel, acum)
    dec_e = _sel_dot_lhs(jnp.exp(last - acum), expand)
    xw = xdt * dec_e
    dcol = _sel_dot_rhs(expand_t_ref[...], jnp.exp(acum_t))
    lane = lax.broadcasted_iota(jnp.int32, (256, t), 1)
    xw_ts = [xw[:, 256 * g:256 * (g + 1)].T for g in range(2)]
    for b in range(nbt):
        in_b = (lane // TILE) == b
        cd = dcol[:, TILE * b + TILE - 1:TILE * b + TILE]
        h0 = h0_ref[b]
        for g in range(2):
            lhs = jnp.where(in_b, xw_ts[g], 0.0).astype(BF16)
            dh = jnp.dot(lhs, bb[:, 128 * g:128 * (g + 1)], preferred_element_type=F32)
            hout_ref[b, 256 * g:256 * (g + 1), :] = (
                h0[256 * g:256 * (g + 1), :] * cd[256 * g:256 * (g + 1), :] + dh)


def _ssd_sample(xbc, prefix, z, dt, state, layer, lw, prev_out):
    rows = xbc.shape[0]
    nbt = BLK // TILE
    row_spec = lambda w: pl.BlockSpec((BLK, w), lambda i: (i, 0))
    st_spec = pl.BlockSpec((None, nbt, B_INNER, B_STATE), lambda i: (layer, i, 0, 0))
    in_specs = [row_spec(CONV_DIM), row_spec(CONV_DIM), row_spec(B_INNER), row_spec(128), st_spec,
                _const_spec((CONV_K, CONV_DIM)), _const_spec((1, CONV_DIM)), _const_spec((1, 128)),
                _const_spec((1, 128)), _const_spec((1, B_INNER)), _const_spec((1, B_INNER)),
                _const_spec((128, B_INNER)), _const_spec((B_INNER, 128))]
    args = [xbc, prefix, z, dt, state, lw['conv_w'], lw['conv_b'], lw['dt_bias'], lw['a_log'], lw['d_skip'],
            lw['ssm_norm'], lw['expand'], lw['expand_t']]
    aliases = {}
    if prev_out is not None:
        aliases = {len(args): 1}
        in_specs.append(pl.BlockSpec(memory_space=pl.ANY))
        args.append(prev_out)
    return pl.pallas_call(
        _ssd_sample_body,
        grid=(rows // BLK,),
        in_specs=in_specs,
        out_specs=[row_spec(B_INNER), st_spec],
        out_shape=[jax.ShapeDtypeStruct((rows, B_INNER), BF16), jax.ShapeDtypeStruct(state.shape, F32)],
        scratch_shapes=[pltpu.VMEM((BLK + TILE, CONV_DIM), F32)],
        input_output_aliases=aliases,
        compiler_params=_params(("parallel",)),
        name="ssd_sample",
    )(*args)


def _out_ffn_body(h_ref, a_ref, m_ref, c_ref, wo_ref, g2_ref, wg_ref, wu_ref, wd_ref, out_ref):
    d = functools.partial(jnp.dot, preferred_element_type=F32)
    h1 = (h_ref[...] + d(a_ref[...], wo_ref[0:256, :]) + d(m_ref[...], wo_ref[256:768, :])
          + d(c_ref[...].astype(BF16), wo_ref[768:1024, :]))
    ms = jnp.mean(h1 * h1, axis=-1, keepdims=True)
    u = (h1 * lax.rsqrt(ms + NORM_EPS) * g2_ref[...]).astype(BF16)
    out_ref[...] = h1
    for c in range(D_FF // FF_CHUNK):
        sl = slice(FF_CHUNK * c, FF_CHUNK * (c + 1))
        act = (_silu(d(u, wg_ref[:, sl])) * d(u, wu_ref[:, sl])).astype(BF16)
        out_ref[...] += d(act, wd_ref[sl, :])


_N_FFN_IN = 9


def _out_ffn_with_cache_shift_body(*refs, nb, w):
    ffn_in = refs[:_N_FFN_IN]
    out_ref, ko_ref = refs[-2:]
    _cache_shift_body(*refs[_N_FFN_IN:-2], ko_ref, nb=nb, w=w)
    _out_ffn_body(*ffn_in, out_ref)


def _out_ffn(h, a_o, m_o, c_o, lw, tm, sample_c=None):
    rows = h.shape[0]
    steps = rows // tm
    row_spec = lambda w: pl.BlockSpec((tm, w), lambda i: (i, 0))
    in_specs = [row_spec(D_MODEL), row_spec(Q_WIDTH), row_spec(B_INNER), row_spec(Q_WIDTH),
                _layer_spec((D_MODEL, D_MODEL), lw['layer']), _const_spec((1, D_MODEL)),
                _layer_spec((D_MODEL, D_FF), lw['layer']), _layer_spec((D_MODEL, D_FF), lw['layer']),
                _layer_spec((D_FF, D_MODEL), lw['layer'])]
    args = [h, a_o, m_o, c_o, lw['w_out'], lw['norm2'], lw['w_gate'], lw['w_up'], lw['w_down']]
    assert len(args) == _N_FFN_IN
    out_spec = row_spec(D_MODEL)
    out_shape = jax.ShapeDtypeStruct((rows, D_MODEL), F32)
    if sample_c is None:
        return pl.pallas_call(
            _out_ffn_body,
            grid=(steps,), in_specs=in_specs, out_specs=out_spec, out_shape=out_shape,
            compiler_params=_params(("parallel",)),
            name="out_ffn",
        )(*args)
    nb, w, s_specs, s_args, aliases, s_out_specs, s_out_shape = _hosted_cache_shift(sample_c, steps, _N_FFN_IN, 1)
    return pl.pallas_call(
        functools.partial(_out_ffn_with_cache_shift_body, nb=nb, w=w),
        grid=(steps,), in_specs=in_specs + s_specs, out_specs=[out_spec] + s_out_specs,
        out_shape=[out_shape] + s_out_shape,
        input_output_aliases=aliases,
        compiler_params=_params(("parallel",)),
        name="out_ffn_host",
    )(*(args + s_args))


def _rope_table(pos):
    half = ROT_DIM // 2
    inv = ROPE_THETA ** (-(jnp.arange(half, dtype=F32) * 2.0 / ROT_DIM))
    ang = pos.astype(F32)[None, :] * inv[:, None]
    cs = jnp.concatenate([jnp.cos(ang), jnp.sin(ang)], axis=0)
    hi = cs.astype(BF16).astype(F32)
    mid = (cs - hi).astype(BF16).astype(F32)
    lo = (cs - hi - mid).astype(BF16).astype(F32)
    return jnp.concatenate([hi, mid, lo], axis=0)


def _rope_selection():
    half = ROT_DIM // 2
    sel = np.zeros((128, 256), np.float32)
    for term in range(3):
        base = term * ROT_DIM
        for lane in range(128):
            f = lane % HEAD_DIM
            if f < half:
                sel[base + f, lane] = 1.0
                sel[base + half + f, 128 + lane] = -1.0
            elif f < ROT_DIM:
                sel[base + f - half, lane] = 1.0
                sel[base + f, 128 + lane] = 1.0
    return sel


_ROPE_SEL = _rope_selection()


def _permute_heads(m, start, axis):
    sl = lambda a, b: lax.slice_in_dim(m, a, b, axis=axis)
    pieces = [sl(0, start)] if start else []
    pieces += [sl(start, start + 64), sl(start + 128, start + 192), sl(start + 64, start + 128),
               sl(start + 192, start + 256), sl(start + 256, m.shape[axis])]
    return jnp.concatenate(pieces, axis=axis)


def _matmul_weights(w_in, w_out, w_gate, w_up, w_down):
    w = _permute_heads(_permute_heads(w_in.astype(BF16), _OFF['aq'], 2), _OFF['cq'], 2)
    w = jnp.pad(w, ((0, 0), (0, 0), (0, N_IN_PAD - N_IN)))
    wo = _permute_heads(_permute_heads(w_out.astype(BF16), 0, 1), 768, 1)
    return dict(w_in=w, w_out=wo, w_gate=w_gate.astype(BF16), w_up=w_up.astype(BF16), w_down=w_down.astype(BF16))


def _layer_weights(l, mm, norm1, a_qn, a_kn, a_sinks, c_qn, c_kn, conv_w, conv_b, dt_bias, a_log, d_skip,
                   ssm_norm, norm2):
    pad8 = lambda v: jnp.pad(v.astype(F32), (0, 128 - B_HEADS))[None, :]
    head_of = np.arange(B_INNER) // 64
    expand = (np.arange(128)[:, None] == head_of[None, :]).astype(np.float32)
    blk = (np.arange(256)[:, None] // 64 == np.arange(256)[None, :] // 64).astype(np.float32) / 64.0
    return dict(
        layer=l, norm1=norm1[l][None, :], w_in=mm['w_in'],
        a_qn=jnp.tile(a_qn[l], 4)[None, :], a_kn=jnp.tile(a_kn[l], 2)[None, :],
        c_qn=jnp.tile(c_qn[l], 4)[None, :], c_kn=jnp.tile(c_kn[l], 2)[None, :],
        bd=jnp.asarray(blk, BF16),
        sink_lanes=jnp.repeat(a_sinks[l].astype(F32)[jnp.asarray([0, 2, 1, 3])], HEAD_DIM).reshape(2, 128),
        sink_rows=jnp.broadcast_to(
            jnp.repeat(a_sinks[l].astype(F32)[jnp.asarray([0, 2, 1, 3])], TILE)[:, None], (4 * TILE, 128)),
        conv_w=conv_w[l], conv_b=conv_b[l][None, :], dt_bias=pad8(dt_bias[l]), a_log=pad8(a_log[l]),
        d_skip=jnp.repeat(d_skip[l].astype(F32), 64)[None, :], ssm_norm=ssm_norm[l][None, :],
        expand=jnp.asarray(expand, BF16), expand_t=jnp.asarray(expand.T, BF16),
        w_out=mm['w_out'], norm2=norm2[l][None, :], w_gate=mm['w_gate'], w_up=mm['w_up'], w_down=mm['w_down'])


def kernel(x_prompt, x_sample, cache_a_k, cache_a_v, cache_c_k, cache_c_v, state_ssm, state_conv, norm1, w_in,
           a_qn, a_kn, a_sinks, c_qn, c_kn, conv_w, conv_b, dt_bias, a_log, d_skip, ssm_norm, w_out, norm2,
           w_gate, w_up, w_down):
    depth = w_in.shape[0]
    batch, seq, _ = x_prompt.shape
    nbatch, dec_seq, _ = x_sample.shape
    assert batch == 1 and dec_seq == TILE - TOK0 and seq % (16 * BLK) == 0 and nbatch % (BLK // TILE) == 0
    past_len = PAST_LEN
    a_buf, c_buf = cache_a_k.shape[2], cache_c_k.shape[2]
    assert a_buf == A_WINDOW and c_buf == C_SPAN

    hp = x_prompt.reshape(seq, D_MODEL)
    hs = jnp.pad(x_sample, ((0, 0), (TOK0, 0), (0, 0))).reshape(nbatch * TILE, D_MODEL)
    tab_p = _rope_table(jnp.arange(seq))
    pos_s = past_len + jnp.maximum(jnp.arange(TILE) - TOK0, 0)
    tab_s = jnp.tile(_rope_table(pos_s), (1, nbatch))
    ma_c, ma_n, mc_c, mc_n = _sample_mult_tables()

    to_fm = lambda c: jnp.transpose(c, (0, 1, 3, 4, 2)).reshape(depth, nbatch, KV_WIDTH, c.shape[2])
    from_fm = lambda c: jnp.transpose(c.reshape(depth, nbatch, 2, HEAD_DIM, c.shape[3]), (0, 1, 4, 2, 3))
    ca_k, ca_v, cc_k, cc_v = to_fm(cache_a_k), to_fm(cache_a_v), to_fm(cache_c_k), to_fm(cache_c_v)
    st = state_ssm.reshape(depth, nbatch, B_INNER, B_STATE)

    mm = _matmul_weights(w_in, w_out, w_gate, w_up, w_down)
    p_out = [[] for _ in range(6)]
    s_conv = []
    new_a = new_ck = new_cv = new_st = None
    tm_p = 1024
    tm_s = min(512, nbatch * TILE)
    for l in range(depth):
        lw = _layer_weights(l, mm, norm1, a_qn, a_kn, a_sinks, c_qn, c_kn, conv_w, conv_b, dt_bias, a_log,
                            d_skip, ssm_norm, norm2)
        s_qa0, s_qa1, s_ka, s_va, s_qc0, s_qc1, s_kc, s_vc, s_z, s_xbc, s_dt = _in_proj(hs, lw, tab_s, tm_s)

        qa0, qa1, ka, va, qc0, qc1, kc, vc, z, xbc, dt = _in_proj(hp, lw, tab_p, tm_p)
        a_o = _prompt_attn(qa0, qa1, ka, va, A_PATTERNS, sink_lanes=lw['sink_lanes'], name="attn_a")
        c_o = _prompt_attn(qc0, qc1, kc, vc, C_BANDS, name="attn_c")
        m_o, h_fin, s_c_o, new_cv = _ssd_prompt(
            xbc, z, dt, lw, (s_qc0, s_qc1, s_kc, s_vc, cc_k, cc_v, l, mc_c, mc_n, new_cv))
        hp, new_ck = _out_ffn(hp, a_o, m_o, c_o, lw, FFN_ROWS, (s_kc, cc_k, l, new_ck))
        p_out[0].append(ka[seq - a_buf:].reshape(1, a_buf, 2, HEAD_DIM))
        p_out[1].append(va[seq - a_buf:].reshape(1, a_buf, 2, HEAD_DIM))
        p_out[2].append(kc[seq - c_buf:].reshape(1, c_buf, 2, HEAD_DIM))
        p_out[3].append(vc[seq - c_buf:].reshape(1, c_buf, 2, HEAD_DIM))
        p_out[4].append(h_fin.reshape(1, B_HEADS, 64, B_STATE))
        p_out[5].append(xbc[seq - (CONV_K - 1):].reshape(1, CONV_K - 1, CONV_DIM))

        a_o, *new_a = _sample_attn(s_qa0, s_qa1, s_ka, s_va, ca_k, ca_v, l, ma_c, ma_n, lw['sink_rows'], 8, new_a)
        prefix = jnp.pad(state_conv[l], ((0, 0), (1, TILE - CONV_K), (0, 0))).reshape(nbatch * TILE, CONV_DIM)
        m_o, new_st = _ssd_sample(s_xbc, prefix, s_z, s_dt, st, l, lw, new_st)
        hs = _out_ffn(hs, a_o, m_o, s_c_o, lw, tm_s)
        s_conv.append(s_xbc.reshape(nbatch, TILE, CONV_DIM)[:, TILE - (CONV_K - 1):])

    outs_p = [jnp.stack(t, axis=0) for t in p_out]
    outs_s = [from_fm(new_a[0]), from_fm(new_a[1]), from_fm(new_ck), from_fm(new_cv),
              new_st.reshape(depth, nbatch, B_HEADS, 64, B_STATE), jnp.stack(s_conv, axis=0)]
    y_p = hp.reshape(1, seq, D_MODEL)
    y_s = hs.reshape(nbatch, TILE, D_MODEL)[:, TOK0:]
    return (y_p, y_s, *outs_p, *outs_s)
```

```python
import functools

import numpy as np
import jax
import jax.numpy as jnp
from jax import lax
from jax.experimental import pallas as pl
from jax.experimental.pallas import tpu as pltpu

F32 = jnp.float32
BF16 = jnp.bfloat16

D_MODEL = 1024
HEAD_DIM = 64
ROT_DIM = 16
ROPE_THETA = 500000.0
NORM_EPS = 1e-6
Q_WIDTH = 256
KV_WIDTH = 128
PAST_LEN = 16384
A_WINDOW = 128
C_PATTERNS = ((128, 1), (512, 4), (2048, 16))
C_SPAN = 2048
B_HEADS = 8
B_INNER = 512
B_STATE = 128
CONV_K = 4
CONV_DIM = 1024
D_FF = 2816
N_IN = 2568
N_IN_PAD = 2688
BLK = 128
ATTN_SUPER = 2048
A_PATTERNS = ((1, A_WINDOW - 1),)
C_BANDS = tuple((d, w // d) for w, d in C_PATTERNS)
TOK0 = 4
TILE = 8
FF_CHUNK = 256
IN_PROJ_GROUP = 512
SSD_ROWS = 256
FFN_ROWS = 512
VMEM_LIMIT = 56 * 1024 * 1024

_OFF = dict(aq=0, ak=256, av=384, cq=512, ck=768, cv=896, z=1024, xbc=1536, dt=2560, end=N_IN_PAD)


def _const_spec(shape):
    nd = len(shape)
    return pl.BlockSpec(shape, lambda *_: (0,) * nd, pipeline_mode=pl.Buffered(1))


def _layer_spec(shape, layer):
    nd = len(shape)
    return pl.BlockSpec((None,) + tuple(shape), lambda *_: (layer,) + (0,) * nd, pipeline_mode=pl.Buffered(1))


def _params(sem):
    return pltpu.CompilerParams(dimension_semantics=sem, vmem_limit_bytes=VMEM_LIMIT)


def _split3(v):
    hi = v.astype(BF16)
    r1 = v - hi.astype(F32)
    mid = r1.astype(BF16)
    lo = (r1 - mid.astype(F32)).astype(BF16)
    return hi, mid, lo


def _sel_dot_rhs(sel, v):
    hi, mid, lo = _split3(v)
    d = functools.partial(jnp.dot, preferred_element_type=F32)
    return d(sel, hi) + d(sel, mid) + d(sel, lo)


def _sel_dot_lhs(v, sel):
    hi, mid, lo = _split3(v)
    d = functools.partial(jnp.dot, preferred_element_type=F32)
    return d(hi, sel) + d(mid, sel) + d(lo, sel)


def _dot_nt(a, b):
    return lax.dot_general(a, b, (((1,), (1,)), ((), ())), preferred_element_type=F32)


def _silu(x):
    return x * jax.nn.sigmoid(x)


def _softplus(x):
    return jnp.maximum(x, 0.0) + jnp.log(1.0 + jnp.exp(-jnp.abs(x)))


def _head_norm(x, gain, bd):
    x2 = x * x
    hi = x2.astype(BF16)
    lo = (x2 - hi.astype(F32)).astype(BF16)
    ms = jnp.dot(hi, bd, preferred_element_type=F32) + jnp.dot(lo, bd, preferred_element_type=F32)
    return x * lax.rsqrt(ms + NORM_EPS) * gain


def _rope(x, cos_t, sin_t):
    w = x.shape[1]
    lane = lax.broadcasted_iota(jnp.int32, x.shape, 1) & (HEAD_DIM - 1)
    partner = jnp.where(lane < ROT_DIM // 2, pltpu.roll(x, w - ROT_DIM // 2, 1), pltpu.roll(x, ROT_DIM // 2, 1))
    return x * cos_t + partner * sin_t


def _in_proj_body(h_ref, g1_ref, w_ref, cs_ref, sel_ref, gqa_ref, gka_ref, gqc_ref, gkc_ref, bd_ref,
                  qa0_ref, qa1_ref, ka_ref, va_ref, qc0_ref, qc1_ref, kc_ref, vc_ref, z_ref, xbc_ref, dt_ref):
    unrot = ((lax.broadcasted_iota(jnp.int32, (1, 128), 1) & (HEAD_DIM - 1)) >= ROT_DIM).astype(F32)
    bd2 = bd_ref[...]
    bd1 = bd2[:128, :128]
    group = min(IN_PROJ_GROUP, h_ref.shape[0])
    for r0 in range(0, h_ref.shape[0], group):
        rows = slice(r0, r0 + group)
        x = h_ref[rows, :]
        ms = jnp.mean(x * x, axis=-1, keepdims=True)
        u = (x * lax.rsqrt(ms + NORM_EPS) * g1_ref[...]).astype(BF16)

        def proj(name, nxt, u=u):
            return jnp.dot(u, w_ref[:, _OFF[name]:_OFF[nxt]], preferred_element_type=F32)

        xbc_ref[rows, :] = proj('xbc', 'dt')
        z_ref[rows, :] = proj('z', 'xbc')
        dt_ref[rows, :] = proj('dt', 'end')
        cs_t = jnp.concatenate([cs_ref[:, rows], jnp.zeros((128 - 3 * ROT_DIM, group), F32)], axis=0).T.astype(BF16)
        tab = jnp.dot(cs_t, sel_ref[...], preferred_element_type=F32)
        c1, s1 = tab[:, :128] + unrot, tab[:, 128:]
        c2 = jnp.concatenate([c1, c1], axis=1)
        s2 = jnp.concatenate([s1, s1], axis=1)
        qa = _rope(_head_norm(proj('aq', 'ak'), gqa_ref[...], bd2), c2, s2) * 0.125
        qa0_ref[rows, :] = qa[:, :128]
        qa1_ref[rows, :] = qa[:, 128:]
        kva = proj('ak', 'cq')
        ka_ref[rows, :] = _rope(_head_norm(kva[:, :128], gka_ref[...], bd1), c1, s1)
        va_ref[rows, :] = kva[:, 128:]
        qc = _rope(_head_norm(proj('cq', 'ck'), gqc_ref[...], bd2), c2, s2) * 0.125
        qc0_ref[rows, :] = qc[:, :128]
        qc1_ref[rows, :] = qc[:, 128:]
        kvc = proj('ck', 'z')
        kc_ref[rows, :] = _rope(_head_norm(kvc[:, :128], gkc_ref[...], bd1), c1, s1)
        vc_ref[rows, :] = kvc[:, 128:]


def _in_proj(h, lw, tab, tm):
    rows = h.shape[0]
    grid = (rows // tm,)
    row_spec = lambda w: pl.BlockSpec((tm, w), lambda i: (i, 0))
    widths = (128, 128, KV_WIDTH, KV_WIDTH, 128, 128, KV_WIDTH, KV_WIDTH, B_INNER, CONV_DIM, 128)
    dtypes = (F32,) * len(widths)
    return pl.pallas_call(
        _in_proj_body,
        grid=grid,
        in_specs=[row_spec(D_MODEL), _const_spec((1, D_MODEL)), _layer_spec((D_MODEL, N_IN_PAD), lw['layer']),
                  pl.BlockSpec((3 * ROT_DIM, tm), lambda i: (0, i)), _const_spec((128, 256)),
                  _const_spec((1, 256)), _const_spec((1, 128)), _const_spec((1, 256)),
                  _const_spec((1, 128)), _const_spec((256, 256))],
        out_specs=[row_spec(w) for w in widths],
        out_shape=[jax.ShapeDtypeStruct((rows, w), dt) for w, dt in zip(widths, dtypes)],
        compiler_params=_params(("parallel",)),
        name="in_proj",
    )(h, lw['norm1'], lw['w_in'], tab, jnp.asarray(_ROPE_SEL, BF16), lw['a_qn'], lw['a_kn'], lw['c_qn'], lw['c_kn'], lw['bd'])


def _attn_body(*refs, patterns, has_sink, sb):
    if has_sink:
        sink_ref, refs = refs[0], refs[1:]
    q0_ref, q1_ref, kp_ref, kc_ref, vp_ref, vc_ref, o_ref, kk, vv, acc_s, m_s, l_s = refs
    q_refs = (q0_ref, q1_ref)
    j = pl.program_id(0)
    kk[0:sb, :] = kp_ref[...]
    kk[sb:2 * sb, :] = kc_ref[...]
    vv[0:sb, :] = vp_ref[...]
    vv[sb:2 * sb, :] = vc_ref[...]
    row4 = lax.broadcasted_iota(jnp.int32, (4 * BLK, BLK), 0) & (BLK - 1)
    col4 = lax.broadcasted_iota(jnp.int32, (4 * BLK, BLK), 1)
    upper4 = col4 > row4
    g_lo = lax.broadcasted_iota(jnp.int32, (BLK, 128), 1) < HEAD_DIM
    ones_cols = jnp.ones((2 * BLK, 128), BF16)
    nblk = sb // BLK

    diag_here = [md == BLK for _, md in patterns]
    boosts = [[] for _ in patterns]
    for p, (d, md) in enumerate(patterns):
        for q in range(p + 1, len(patterns)):
            dq, mdq = patterns[q]
            if diag_here[p] and (md * d) % dq == 0 and (md * d) // dq < min(BLK, mdq + 1):
                boosts[q].append((md * d) // dq)
                diag_here[p] = False

    for pi, (d, max_dist) in enumerate(patterns):
        nsub = nblk // d
        has_diag = diag_here[pi]
        fdist = jnp.where(upper4, row4 + BLK - col4, row4 - col4)
        mult = jnp.ones((4 * BLK, BLK), F32)
        for f in boosts[pi]:
            mult = mult + (fdist == f).astype(F32)
        w_up = jnp.where(upper4, mult, 0.0)
        w_lo = jnp.where(upper4, 0.0, mult)

        def ld(ref, s0, d=d):
            if d == 1:
                return ref[pl.ds(s0, BLK), :]
            return ref[pl.ds(s0, BLK, stride=d), :]

        def block(t, carry, d=d, nsub=nsub, has_diag=has_diag, first=(pi == 0), ld=ld, w_up=w_up, w_lo=w_lo):
            r_ = t // nsub
            n = t - r_ * nsub
            start = r_ + BLK * d * n
            prev_ok = jnp.logical_or(j > 0, n > 0)
            neg = jnp.where(prev_ok, 0.0, -jnp.inf)
            qb = [ld(q_refs[rr], start) for rr in range(2)]
            kprev, kcur = ld(kk, sb + start - BLK * d), ld(kk, sb + start)
            vprev, vcur = ld(vv, sb + start - BLK * d), ld(vv, sb + start)
            qm = jnp.concatenate([jnp.where(g_lo, qb[0], 0.0), jnp.where(g_lo, 0.0, qb[0]),
                                  jnp.where(g_lo, qb[1], 0.0), jnp.where(g_lo, 0.0, qb[1])], axis=0).astype(BF16)
            kcat = jnp.concatenate([kprev, kcur], axis=0).astype(BF16)
            vcat = jnp.concatenate([jnp.concatenate([vprev, vcur], axis=0).astype(BF16), ones_cols], axis=1)
            s2 = _dot_nt(qm, kcat)
            sp, sc = s2[:, :BLK] + neg, s2[:, BLK:]
            s = jnp.where(upper4, sp, sc)
            mb = jnp.max(s, axis=-1, keepdims=True)
            if has_diag:
                sd = jnp.sum(jnp.where(col4 == row4, sp, 0.0), axis=-1, keepdims=True)
                mb = jnp.maximum(mb, sd)
            e = jnp.exp(s - mb)
            ecat = jnp.concatenate([e * w_up, e * w_lo], axis=1).astype(BF16)
            pvl = jnp.dot(ecat, vcat, preferred_element_type=F32)
            pv, lb = pvl[:, :128], pvl[:, 128:]
            if has_diag:
                ed = jnp.exp(sd - mb)
                lb = lb + ed
                pv = pv + ed * jnp.concatenate([vprev] * 4, axis=0)
            for rr in range(2):
                lo, hi = slice(2 * BLK * rr, 2 * BLK * rr + BLK), slice(2 * BLK * rr + BLK, 2 * BLK * (rr + 1))
                o_b = jnp.where(g_lo, pv[lo], pv[hi])
                m_b = jnp.where(g_lo, mb[lo], mb[hi])
                l_b = jnp.where(g_lo, lb[lo], lb[hi])
                rows = pl.ds(start, BLK) if d == 1 else pl.ds(start, BLK, stride=d)
                if first:
                    m_s[rr, rows, :] = m_b
                    l_s[rr, rows, :] = l_b
                    acc_s[rr, rows, :] = o_b
                else:
                    m_old = m_s[rr, rows, :]
                    m_new = jnp.maximum(m_old, m_b)
                    w_old = jnp.exp(m_old - m_new)
                    w_b = jnp.exp(m_b - m_new)
                    m_s[rr, rows, :] = m_new
                    l_s[rr, rows, :] = w_old * l_s[rr, rows, :] + w_b * l_b
                    acc_s[rr, rows, :] = w_old * acc_s[rr, rows, :] + w_b * o_b
            return carry

        lax.fori_loop(0, nblk, block, 0, unroll={1: 16, 4: 16}.get(d, 8))

    chunk = 256
    for rr in range(2):
        for c in range(sb // chunk):
            rows = slice(chunk * c, chunk * (c + 1))
            m_f, l_f, a_f = m_s[rr, rows, :], l_s[rr, rows, :], acc_s[rr, rows, :]
            if has_sink:
                sk = sink_ref[rr:rr + 1, :]
                m2 = jnp.maximum(m_f, sk)
                w = jnp.exp(m_f - m2)
                o = a_f * w / (l_f * w + jnp.exp(sk - m2))
            else:
                o = a_f / l_f
            o_ref[rows, 128 * rr:128 * (rr + 1)] = o.astype(o_ref.dtype)


def _prompt_attn(q0, q1, k, v, patterns, sink_lanes=None, name="attn"):
    L = q0.shape[0]
    sb = ATTN_SUPER
    cur = lambda j: (j, 0)
    prev = lambda j: (jnp.maximum(j - 1, 0), 0)
    blk = lambda im: pl.BlockSpec((sb, 128), im)
    in_specs = [blk(cur), blk(cur), blk(prev), blk(cur), blk(prev), blk(cur)]
    args = [q0, q1, k, k, v, v]
    if sink_lanes is not None:
        in_specs = [_const_spec((2, 128))] + in_specs
        args = [sink_lanes] + args
    return pl.pallas_call(
        functools.partial(_attn_body, patterns=patterns, has_sink=sink_lanes is not None, sb=sb),
        grid=(L // sb,), in_specs=in_specs, out_specs=pl.BlockSpec((sb, Q_WIDTH), cur),
        out_shape=jax.ShapeDtypeStruct((L, Q_WIDTH), BF16),
        scratch_shapes=[pltpu.VMEM((2 * sb, 128), F32), pltpu.VMEM((2 * sb, 128), F32),
                        pltpu.VMEM((2, sb, 128), F32), pltpu.VMEM((2, sb, 128), F32), pltpu.VMEM((2, sb, 128), F32)],
        compiler_params=_params(("parallel",)),
        name=name,
    )(*args)


def _shift_cache(src, new_t, dst, b, w):
    lane = lax.broadcasted_iota(jnp.int32, (128, 128), 1)
    is_new = lane >= 128 - (TILE - TOK0)
    shifted = pltpu.roll(src, w - (TILE - TOK0), 1)
    new_cols = pltpu.roll(new_t, 128 - TILE, 1)
    if w > 128:
        dst[b, :, 0:w - 128] = shifted[:, 0:w - 128]
    dst[b, :, w - 128:w] = jnp.where(is_new, new_cols, shifted[:, w - 128:w])


def _cache_shift_body(*refs, nb, w):
    new_ref, c_ref, o_ref = refs[0], refs[1], refs[-1]
    pad = jnp.zeros((128 - TILE, 128), F32)
    for b in range(nb):
        new_t = jnp.concatenate([new_ref[TILE * b:TILE * (b + 1), :], pad], axis=0).T
        _shift_cache(c_ref[b], new_t, o_ref, b, w)


def _sample_attn_body(*refs, nb, has_sink, n_alias, w, batches=None, write_k=True):
    if has_sink:
        sink_ref, refs = refs[0], refs[1:]
    q0_ref, q1_ref, kn_ref, vn_ref, kc_ref, vc_ref, mc_ref, mn_ref = refs[:8]
    if write_k:
        o_ref, ko_ref, vo_ref = refs[8 + n_alias:]
    else:
        (o_ref, vo_ref), ko_ref = refs[8 + n_alias:], None
    g_lo = lax.broadcasted_iota(jnp.int32, (TILE, 128), 1) < HEAD_DIM
    mult_c = mc_ref[...]
    mult_n = mn_ref[...]
    pad = jnp.zeros((128 - TILE, 128), F32)
    for b in (range(nb) if batches is None else batches):
        parts = []
        for q_ref in (q0_ref, q1_ref):
            qr = q_ref[TILE * b:TILE * (b + 1), :]
            parts += [jnp.where(g_lo, qr, 0.0), jnp.where(g_lo, 0.0, qr)]
        qm = jnp.concatenate(parts, axis=0).astype(BF16)
        kct = kc_ref[b]
        vct = vc_ref[b]
        knp = jnp.concatenate([kn_ref[TILE * b:TILE * (b + 1), :], pad], axis=0)
        vnp = jnp.concatenate([vn_ref[TILE * b:TILE * (b + 1), :], pad], axis=0)
        knt = knp.T
        vnt = vnp.T
        sc = jnp.where(mult_c > 0, jnp.dot(qm, kct.astype(BF16), preferred_element_type=F32), -jnp.inf)
        sn = jnp.where(mult_n > 0, jnp.dot(qm, knt.astype(BF16), preferred_element_type=F32), -jnp.inf)
        m = jnp.maximum(jnp.max(sc, axis=-1, keepdims=True), jnp.max(sn, axis=-1, keepdims=True))
        if has_sink:
            sk = sink_ref[:, :1]
            m = jnp.maximum(m, sk)
        ec = mult_c * jnp.exp(sc - m)
        en = mult_n * jnp.exp(sn - m)
        den = jnp.sum(ec, axis=-1, keepdims=True) + jnp.sum(en, axis=-1, keepdims=True)
        if has_sink:
            den = den + jnp.exp(sk - m)
        o = (_dot_nt(ec.astype(BF16), vct.astype(BF16))
             + jnp.dot(en.astype(BF16), vnp.astype(BF16), preferred_element_type=F32)) / den
        o_ref[TILE * b:TILE * (b + 1), :] = jnp.concatenate(
            [jnp.where(g_lo, o[0:TILE], o[TILE:2 * TILE]),
             jnp.where(g_lo, o[2 * TILE:3 * TILE], o[3 * TILE:4 * TILE])], axis=1).astype(o_ref.dtype)
        if write_k:
            _shift_cache(kct, knt, ko_ref, b, w)
        _shift_cache(vct, vnt, vo_ref, b, w)


def _sample_attn(q0, q1, kn, vn, cache_k, cache_v, layer, mult_c, mult_n, sinks_rows, nb, prev_out):
    rows = q0.shape[0]
    nbatch = rows // TILE
    depth, _, _, w = cache_k.shape
    row_spec = lambda width: pl.BlockSpec((TILE * nb, width), lambda i: (i, 0))
    cache_spec = pl.BlockSpec((None, nb, KV_WIDTH, w), lambda i: (layer, i, 0, 0))
    in_specs = [row_spec(128), row_spec(128), row_spec(KV_WIDTH), row_spec(KV_WIDTH), cache_spec, cache_spec,
                _const_spec((4 * TILE, w)), _const_spec((4 * TILE, 128))]
    args = [q0, q1, kn, vn, cache_k, cache_v, mult_c, mult_n]
    if sinks_rows is not None:
        in_specs = [_const_spec((4 * TILE, 128))] + in_specs
        args = [sinks_rows] + args
    aliases = {}
    if prev_out is not None:
        aliases = {len(args): 1, len(args) + 1: 2}
        in_specs = in_specs + [pl.BlockSpec(memory_space=pl.ANY)] * 2
        args = args + list(prev_out)
    cache_shape = jax.ShapeDtypeStruct(cache_k.shape, F32)
    return pl.pallas_call(
        functools.partial(_sample_attn_body, nb=nb, has_sink=sinks_rows is not None,
                          n_alias=0 if prev_out is None else 2, w=w),
        grid=(nbatch // nb,), in_specs=in_specs, out_specs=[row_spec(Q_WIDTH), cache_spec, cache_spec],
        out_shape=[jax.ShapeDtypeStruct((rows, Q_WIDTH), BF16), cache_shape, cache_shape],
        input_output_aliases=aliases,
        compiler_params=_params(("parallel",)),
        name=f"sample_attn_w{w}",
    )(*args)


def _sample_mult_tables():
    t = np.arange(TILE) - TOK0
    tq = np.maximum(t, 0)[:, None]
    j = np.arange(A_WINDOW)[None, :]
    da = A_WINDOW + tq - j
    ma_c = ((da >= 0) & (da < A_WINDOW)).astype(np.float32)
    tn = (np.arange(128) - TOK0)[None, :]
    dn = tq - tn
    new_ok = (tn >= 0) & (tn < TILE - TOK0) & (dn >= 0)
    ma_n = (new_ok & (dn < A_WINDOW)).astype(np.float32)

    def mult(d):
        out = np.zeros(d.shape, np.float32)
        for w, dil in C_PATTERNS:
            out += ((d >= 0) & (d <= w) & (d % dil == 0)).astype(np.float32)
        return out

    jc = np.arange(C_SPAN)[None, :]
    mc_c = mult(C_SPAN + tq - jc)
    mc_n = np.where(new_ok, mult(dn), 0.0).astype(np.float32)
    tile4 = lambda a: jnp.asarray(np.tile(a, (4, 1)))
    return tile4(ma_c), tile4(ma_n), tile4(mc_c), tile4(mc_n)


def _ssd_intra(act, dt_raw, dtb, alog, tri_sel, tri_mask, expand, row_ok):
    xs = act[:, :B_INNER]
    bb = act[:, B_INNER:B_INNER + 2 * B_STATE].astype(BF16)
    cb_ = act[:, B_INNER + 2 * B_STATE:].astype(BF16)
    dtv = _softplus(dt_raw + dtb)
    if row_ok is not None:
        dtv = jnp.where(row_ok, dtv, 0.0)
    a = -jnp.exp(alog) * dtv
    acum = _sel_dot_rhs(tri_sel, a)
    acum_t = acum.T
    dt_e = _sel_dot_lhs(dtv, expand)
    xdt = xs * dt_e
    lane_lo = lax.broadcasted_iota(jnp.int32, (BLK, 128), 1) < HEAD_DIM
    ys = []
    for g in range(2):
        cbm = _dot_nt(cb_[:, 128 * g:128 * (g + 1)], bb[:, 128 * g:128 * (g + 1)])
        for pair in range(2):
            xp = xdt[:, 128 * (2 * g + pair):128 * (2 * g + pair + 1)]
            acc = None
            for j in range(2):
                h = 4 * g + 2 * pair + j
                seg = acum[:, h:h + 1] - acum_t[h:h + 1, :]
                lm = jnp.exp(jnp.where(tri_mask, seg, -jnp.inf))
                mh = (cbm * lm).astype(BF16)
                xh = jnp.where(lane_lo if j == 0 else ~lane_lo, xp, 0.0).astype(BF16)
                t = jnp.dot(mh, xh, preferred_element_type=F32)
                acc = t if acc is None else acc + t
            ys.append(acc)
    ydiag = jnp.concatenate(ys, axis=1)
    return xs, bb, cb_, xdt, acum, acum_t, ydiag


def _ssd_finish(y, z, norm_w):
    y = y * _silu(z)
    ms = jnp.mean(y * y, axis=-1, keepdims=True)
    return (y * lax.rsqrt(ms + NORM_EPS) * norm_w).astype(BF16)


def _conv_act(xp_ref, cw_ref, cb_ref, r0=0):
    t = BLK
    base = r0 + TILE - (CONV_K - 1)
    out = cb_ref[...] + xp_ref[base:base + t, :] * cw_ref[0:1, :]
    for j in range(1, CONV_K):
        out = out + xp_ref[base + j:base + j + t, :] * cw_ref[j:j + 1, :]
    return _silu(out)


def _split2_dot(v, sel):
    hi = v.astype(BF16)
    lo = (v - hi.astype(F32)).astype(BF16)
    return jnp.dot(hi, sel, preferred_element_type=F32) + jnp.dot(lo, sel, preferred_element_type=F32)


def _ssd_prompt_chunk(act, z, dt_raw, dtb, alog, dskip, nw, expand, expand_t, hst):
    t = BLK
    row = lax.broadcasted_iota(jnp.int32, (t, t), 0)
    col = lax.broadcasted_iota(jnp.int32, (t, t), 1)
    tri_mask = col <= row
    xs = act[:, :B_INNER]
    bb = act[:, B_INNER:B_INNER + 2 * B_STATE].astype(BF16)
    cb_ = act[:, B_INNER + 2 * B_STATE:].astype(BF16)
    dtv = _softplus(dt_raw + dtb)
    a = -jnp.exp(alog) * dtv
    acum = _sel_dot_rhs(tri_mask.astype(BF16), a)
    acum_t = acum.T
    spread = _split2_dot(jnp.concatenate([dtv, jnp.exp(acum), jnp.exp(acum[t - 1:t, :] - acum)], axis=0), expand)
    dt_e, e_all, dec_e = spread[0:t], spread[t:2 * t], spread[2 * t:3 * t]
    xdt = xs * dt_e
    lane_lo = lax.broadcasted_iota(jnp.int32, (BLK, 128), 1) < HEAD_DIM
    ys = []
    for g in range(2):
        cbm = _dot_nt(cb_[:, 128 * g:128 * (g + 1)], bb[:, 128 * g:128 * (g + 1)])
        for pair in range(2):
            xp = xdt[:, 128 * (2 * g + pair):128 * (2 * g + pair + 1)]
            acc = None
            for j in range(2):
                h = 4 * g + 2 * pair + j
                seg = acum[:, h:h + 1] - acum_t[h:h + 1, :]
                lm = jnp.exp(jnp.where(tri_mask, seg, -jnp.inf))
                mh = (cbm * lm).astype(BF16)
                xh = jnp.where(lane_lo if j == 0 else ~lane_lo, xp, 0.0).astype(BF16)
                part = jnp.dot(mh, xh, preferred_element_type=F32)
                acc = part if acc is None else acc + part
            ys.append(acc)
    ydiag = jnp.concatenate(ys, axis=1)

    hb = hst.astype(BF16)
    yoff = jnp.concatenate([_dot_nt(cb_[:, 128 * g:128 * (g + 1)], hb[256 * g:256 * (g + 1), :]) for g in range(2)],
                           axis=1)
    y = ydiag + yoff * e_all + xs * dskip
    out = _ssd_finish(y, z, nw)

    xw = xdt * dec_e
    last_t = jnp.exp(jnp.broadcast_to(acum_t[:, t - 1:t], (128, 128)))
    hi = last_t.astype(BF16)
    lo = (last_t - hi.astype(F32)).astype(BF16)
    cd = (jnp.dot(expand_t, hi, preferred_element_type=F32)
          + jnp.dot(expand_t, lo, preferred_element_type=F32))
    new = []
    for g in range(2):
        xw_t = xw[:, 256 * g:256 * (g + 1)].T.astype(BF16)
        dh = jnp.dot(xw_t, bb[:, 128 * g:128 * (g + 1)], preferred_element_type=F32)
        new.append(hst[256 * g:256 * (g + 1), :] * cd[256 * g:256 * (g + 1), :] + dh)
    return out, jnp.concatenate(new, axis=0)


def _ssd_prompt_body(xbc_ref, z_ref, dt_ref, cw_ref, cb_ref, dtb_ref, alog_ref, dskip_ref, nw_ref,
                     expand_ref, expand_t_ref, mo_ref, hs_ref, xp_ref, before_chunk=None):
    c = pl.program_id(0)
    rows = xbc_ref.shape[0]

    @pl.when(c == 0)
    def _():
        xp_ref[0:TILE, :] = jnp.zeros((TILE, CONV_DIM), F32)
        hs_ref[...] = jnp.zeros_like(hs_ref)

    xp_ref[TILE:, :] = xbc_ref[...]
    hst = hs_ref[...]
    for ci in range(rows // BLK):
        r0 = BLK * ci
        if before_chunk is not None:
            before_chunk(ci)
        act = _conv_act(xp_ref, cw_ref, cb_ref, r0)
        mo_ref[r0:r0 + BLK, :], hst = _ssd_prompt_chunk(
            act, z_ref[r0:r0 + BLK, :], dt_ref[r0:r0 + BLK, :], dtb_ref[...], alog_ref[...], dskip_ref[...],
            nw_ref[...], expand_ref[...], expand_t_ref[...], hst)
    hs_ref[...] = hst
    xp_ref[0:TILE, :] = xp_ref[rows:rows + TILE, :]


_N_SSD_IN = 11
_N_SATTN_IN = 8


def _ssd_with_sample_attn_body(*refs, nb, n_alias, w):
    ssd_in = refs[:_N_SSD_IN]
    sattn_in = refs[_N_SSD_IN:_N_SSD_IN + _N_SATTN_IN + n_alias]
    mo_ref, hs_ref, o_ref, vo_ref, xp_ref = refs[_N_SSD_IN + _N_SATTN_IN + n_alias:]
    nchunk = ssd_in[0].shape[0] // BLK

    def sample_part(ci):
        share = range(ci * nb // nchunk, (ci + 1) * nb // nchunk)
        _sample_attn_body(*sattn_in, o_ref, vo_ref, nb=nb, has_sink=False, n_alias=n_alias, w=w,
                          batches=share, write_k=False)

    _ssd_prompt_body(*ssd_in, mo_ref, hs_ref, xp_ref, before_chunk=sample_part)


def _hosted_sample_attn(sample_c, steps, n_host_in, n_host_out):
    q0, q1, kn, vn, cache_k, cache_v, layer, mult_c, mult_n, prev_v = sample_c
    nbatches = q0.shape[0] // TILE
    nb = nbatches // steps
    assert nb * steps == nbatches
    w = cache_k.shape[3]
    srow_spec = lambda wd: pl.BlockSpec((TILE * nb, wd), lambda c: (c, 0))
    cache_spec = pl.BlockSpec((None, nb, KV_WIDTH, w), lambda c: (layer, c, 0, 0))
    in_specs = [srow_spec(128), srow_spec(128), srow_spec(KV_WIDTH), srow_spec(KV_WIDTH), cache_spec, cache_spec,
                _const_spec((4 * TILE, w)), _const_spec((4 * TILE, 128))]
    args = [q0, q1, kn, vn, cache_k, cache_v, mult_c, mult_n]
    assert len(args) == _N_SATTN_IN
    aliases = {}
    if prev_v is not None:
        aliases = {n_host_in + len(args): n_host_out + 1}
        in_specs = in_specs + [pl.BlockSpec(memory_space=pl.ANY)]
        args = args + [prev_v]
    out_specs = [srow_spec(Q_WIDTH), cache_spec]
    out_shape = [jax.ShapeDtypeStruct((nbatches * TILE, Q_WIDTH), BF16), jax.ShapeDtypeStruct(cache_v.shape, F32)]
    return nb, w, in_specs, args, aliases, out_specs, out_shape


def _hosted_cache_shift(shift_c, steps, n_host_in, n_host_out):
    new_rows, cache, layer, prev_out = shift_c
    nbatches = new_rows.shape[0] // TILE
    nb = nbatches // steps
    assert nb * steps == nbatches
    w = cache.shape[3]
    cache_spec = pl.BlockSpec((None, nb, KV_WIDTH, w), lambda c: (layer, c, 0, 0))
    in_specs = [pl.BlockSpec((TILE * nb, KV_WIDTH), lambda c: (c, 0)), cache_spec]
    args = [new_rows, cache]
    aliases = {}
    if prev_out is not None:
        aliases = {n_host_in + len(args): n_host_out}
        in_specs = in_specs + [pl.BlockSpec(memory_space=pl.ANY)]
        args = args + [prev_out]
    return nb, w, in_specs, args, aliases, [cache_spec], [jax.ShapeDtypeStruct(cache.shape, F32)]


def _ssd_prompt(xbc, z, dt, lw, sample_c):
    L = xbc.shape[0]
    rows = SSD_ROWS
    steps = L // rows
    row_spec = lambda wd: pl.BlockSpec((rows, wd), lambda c: (c, 0))
    in_specs = [row_spec(CONV_DIM), row_spec(B_INNER), row_spec(128),
                _const_spec((CONV_K, CONV_DIM)), _const_spec((1, CONV_DIM)), _const_spec((1, 128)),
                _const_spec((1, 128)), _const_spec((1, B_INNER)), _const_spec((1, B_INNER)),
                _const_spec((128, B_INNER)), _const_spec((B_INNER, 128))]
    args = [xbc, z, dt, lw['conv_w'], lw['conv_b'], lw['dt_bias'], lw['a_log'], lw['d_skip'], lw['ssm_norm'],
            lw['expand'], lw['expand_t']]
    assert len(args) == _N_SSD_IN
    nb, w, s_specs, s_args, aliases, s_out_specs, s_out_shape = _hosted_sample_attn(sample_c, steps, _N_SSD_IN, 2)
    return pl.pallas_call(
        functools.partial(_ssd_with_sample_attn_body, nb=nb, n_alias=len(aliases), w=w),
        grid=(steps,),
        in_specs=in_specs + s_specs,
        out_specs=[row_spec(B_INNER), pl.BlockSpec((B_INNER, B_STATE), lambda c: (0, 0))] + s_out_specs,
        out_shape=[jax.ShapeDtypeStruct((L, B_INNER), BF16),
                   jax.ShapeDtypeStruct((B_INNER, B_STATE), F32)] + s_out_shape,
        scratch_shapes=[pltpu.VMEM((rows + TILE, CONV_DIM), F32)],
        input_output_aliases=aliases,
        compiler_params=_params(("arbitrary",)),
        name="ssd_prompt",
    )(*(args + s_args))


def _ssd_sample_body(*refs):
    (xbc_ref, pre_ref, z_ref, dt_ref, h0_ref, cw_ref, cb_ref, dtb_ref, alog_ref, dskip_ref,
     nw_ref, expand_ref, expand_t_ref) = refs[:13]
    mo_ref, hout_ref, xp_ref = refs[-3:]
    t = BLK
    nbt = t // TILE
    rmod = lax.broadcasted_iota(jnp.int32, (t, 1), 0) & (TILE - 1)
    row_ok = rmod >= TOK0
    xp_ref[0:TILE, :] = jnp.zeros((TILE, CONV_DIM), F32)
    xp_ref[TILE:, :] = jnp.where(row_ok, xbc_ref[...], pre_ref[...])
    act = _conv_act(xp_ref, cw_ref, cb_ref)

    row = lax.broadcasted_iota(jnp.int32, (t, t), 0)
    col = lax.broadcasted_iota(jnp.int32, (t, t), 1)
    same = (row // TILE) == (col // TILE)
    tri_mask = (col <= row) & same
    tri_sel = tri_mask.astype(BF16)
    last_sel = (col == (row // TILE) * TILE + (TILE - 1)).astype(BF16)
    expand = expand_ref[...]
    xs, bb, cb_, xdt, acum, acum_t, ydiag = _ssd_intra(
        act, dt_ref[...], dtb_ref[...], alog_ref[...], tri_sel, tri_mask, expand, row_ok)

    yoffs = []
    for b in range(nbt):
        hb = h0_ref[b].astype(BF16)
        yoffs.append(jnp.concatenate(
            [_dot_nt(cb_[TILE * b:TILE * (b + 1), 128 * g:128 * (g + 1)], hb[256 * g:256 * (g + 1), :])
             for g in range(2)], axis=1))
    yoff = jnp.concatenate(yoffs, axis=0)
    e_all = _sel_dot_lhs(jnp.exp(acum), expand)
    y = ydiag + yoff * e_all + xs * dskip_ref[...]
    mo_ref[...] = _ssd_finish(y, z_ref[...], nw_ref[...])

    last = _sel_dot_rhs(last_sel, acum)
    dec_e = _sel_dot_lhs(jnp.exp(last - acum), expand)
    xw = xdt * dec_e
    dcol = _sel_dot_rhs(expand_t_ref[...], jnp.exp(acum_t))
    lane = lax.broadcasted_iota(jnp.int32, (256, t), 1)
    xw_ts = [xw[:, 256 * g:256 * (g + 1)].T for g in range(2)]
    for b in range(nbt):
        in_b = (lane // TILE) == b
        cd = dcol[:, TILE * b + TILE - 1:TILE * b + TILE]
        h0 = h0_ref[b]
        for g in range(2):
            lhs = jnp.where(in_b, xw_ts[g], 0.0).astype(BF16)
            dh = jnp.dot(lhs, bb[:, 128 * g:128 * (g + 1)], preferred_element_type=F32)
            hout_ref[b, 256 * g:256 * (g + 1), :] = (
                h0[256 * g:256 * (g + 1), :] * cd[256 * g:256 * (g + 1), :] + dh)


def _ssd_sample(xbc, prefix, z, dt, state, layer, lw, prev_out):
    rows = xbc.shape[0]
    nbt = BLK // TILE
    row_spec = lambda w: pl.BlockSpec((BLK, w), lambda i: (i, 0))
    st_spec = pl.BlockSpec((None, nbt, B_INNER, B_STATE), lambda i: (layer, i, 0, 0))
    in_specs = [row_spec(CONV_DIM), row_spec(CONV_DIM), row_spec(B_INNER), row_spec(128), st_spec,
                _const_spec((CONV_K, CONV_DIM)), _const_spec((1, CONV_DIM)), _const_spec((1, 128)),
                _const_spec((1, 128)), _const_spec((1, B_INNER)), _const_spec((1, B_INNER)),
                _const_spec((128, B_INNER)), _const_spec((B_INNER, 128))]
    args = [xbc, prefix, z, dt, state, lw['conv_w'], lw['conv_b'], lw['dt_bias'], lw['a_log'], lw['d_skip'],
            lw['ssm_norm'], lw['expand'], lw['expand_t']]
    aliases = {}
    if prev_out is not None:
        aliases = {len(args): 1}
        in_specs.append(pl.BlockSpec(memory_space=pl.ANY))
        args.append(prev_out)
    return pl.pallas_call(
        _ssd_sample_body,
        grid=(rows // BLK,),
        in_specs=in_specs,
        out_specs=[row_spec(B_INNER), st_spec],
        out_shape=[jax.ShapeDtypeStruct((rows, B_INNER), BF16), jax.ShapeDtypeStruct(state.shape, F32)],
        scratch_shapes=[pltpu.VMEM((BLK + TILE, CONV_DIM), F32)],
        input_output_aliases=aliases,
        compiler_params=_params(("parallel",)),
        name="ssd_sample",
    )(*args)


def _out_ffn_body(h_ref, a_ref, m_ref, c_ref, wo_ref, g2_ref, wg_ref, wu_ref, wd_ref, out_ref):
    d = functools.partial(jnp.dot, preferred_element_type=F32)
    h1 = (h_ref[...] + d(a_ref[...], wo_ref[0:256, :]) + d(m_ref[...], wo_ref[256:768, :])
          + d(c_ref[...].astype(BF16), wo_ref[768:1024, :]))
    ms = jnp.mean(h1 * h1, axis=-1, keepdims=True)
    u = (h1 * lax.rsqrt(ms + NORM_EPS) * g2_ref[...]).astype(BF16)
    out_ref[...] = h1
    for c in range(D_FF // FF_CHUNK):
        sl = slice(FF_CHUNK * c, FF_CHUNK * (c + 1))
        act = (_silu(d(u, wg_ref[:, sl])) * d(u, wu_ref[:, sl])).astype(BF16)
        out_ref[...] += d(act, wd_ref[sl, :])


_N_FFN_IN = 9


def _out_ffn_with_cache_shift_body(*refs, nb, w):
    ffn_in = refs[:_N_FFN_IN]
    out_ref, ko_ref = refs[-2:]
    _cache_shift_body(*refs[_N_FFN_IN:-2], ko_ref, nb=nb, w=w)
    _out_ffn_body(*ffn_in, out_ref)


def _out_ffn(h, a_o, m_o, c_o, lw, tm, sample_c=None):
    rows = h.shape[0]
    steps = rows // tm
    row_spec = lambda w: pl.BlockSpec((tm, w), lambda i: (i, 0))
    in_specs = [row_spec(D_MODEL), row_spec(Q_WIDTH), row_spec(B_INNER), row_spec(Q_WIDTH),
                _layer_spec((D_MODEL, D_MODEL), lw['layer']), _const_spec((1, D_MODEL)),
                _layer_spec((D_MODEL, D_FF), lw['layer']), _layer_spec((D_MODEL, D_FF), lw['layer']),
                _layer_spec((D_FF, D_MODEL), lw['layer'])]
    args = [h, a_o, m_o, c_o, lw['w_out'], lw['norm2'], lw['w_gate'], lw['w_up'], lw['w_down']]
    assert len(args) == _N_FFN_IN
    out_spec = row_spec(D_MODEL)
    out_shape = jax.ShapeDtypeStruct((rows, D_MODEL), F32)
    if sample_c is None:
        return pl.pallas_call(
            _out_ffn_body,
            grid=(steps,), in_specs=in_specs, out_specs=out_spec, out_shape=out_shape,
            compiler_params=_params(("parallel",)),
            name="out_ffn",
        )(*args)
    nb, w, s_specs, s_args, aliases, s_out_specs, s_out_shape = _hosted_cache_shift(sample_c, steps, _N_FFN_IN, 1)
    return pl.pallas_call(
        functools.partial(_out_ffn_with_cache_shift_body, nb=nb, w=w),
        grid=(steps,), in_specs=in_specs + s_specs, out_specs=[out_spec] + s_out_specs,
        out_shape=[out_shape] + s_out_shape,
        input_output_aliases=aliases,
        compiler_params=_params(("parallel",)),
        name="out_ffn_host",
    )(*(args + s_args))


def _rope_table(pos):
    half = ROT_DIM // 2
    inv = ROPE_THETA ** (-(jnp.arange(half, dtype=F32) * 2.0 / ROT_DIM))
    ang = pos.astype(F32)[None, :] * inv[:, None]
    cs = jnp.concatenate([jnp.cos(ang), jnp.sin(ang)], axis=0)
    hi = cs.astype(BF16).astype(F32)
    mid = (cs - hi).astype(BF16).astype(F32)
    lo = (cs - hi - mid).astype(BF16).astype(F32)
    return jnp.concatenate([hi, mid, lo], axis=0)


def _rope_selection():
    half = ROT_DIM // 2
    sel = np.zeros((128, 256), np.float32)
    for term in range(3):
        base = term * ROT_DIM
        for lane in range(128):
            f = lane % HEAD_DIM
            if f < half:
                sel[base + f, lane] = 1.0
                sel[base + half + f, 128 + lane] = -1.0
            elif f < ROT_DIM:
                sel[base + f - half, lane] = 1.0
                sel[base + f, 128 + lane] = 1.0
    return sel


_ROPE_SEL = _rope_selection()


def _permute_heads(m, start, axis):
    sl = lambda a, b: lax.slice_in_dim(m, a, b, axis=axis)
    pieces = [sl(0, start)] if start else []
    pieces += [sl(start, start + 64), sl(start + 128, start + 192), sl(start + 64, start + 128),
               sl(start + 192, start + 256), sl(start + 256, m.shape[axis])]
    return jnp.concatenate(pieces, axis=axis)


def _matmul_weights(w_in, w_out, w_gate, w_up, w_down):
    w = _permute_heads(_permute_heads(w_in.astype(BF16), _OFF['aq'], 2), _OFF['cq'], 2)
    w = jnp.pad(w, ((0, 0), (0, 0), (0, N_IN_PAD - N_IN)))
    wo = _permute_heads(_permute_heads(w_out.astype(BF16), 0, 1), 768, 1)
    return dict(w_in=w, w_out=wo, w_gate=w_gate.astype(BF16), w_up=w_up.astype(BF16), w_down=w_down.astype(BF16))


def _layer_weights(l, mm, norm1, a_qn, a_kn, a_sinks, c_qn, c_kn, conv_w, conv_b, dt_bias, a_log, d_skip,
                   ssm_norm, norm2):
    pad8 = lambda v: jnp.pad(v.astype(F32), (0, 128 - B_HEADS))[None, :]
    head_of = np.arange(B_INNER) // 64
    expand = (np.arange(128)[:, None] == head_of[None, :]).astype(np.float32)
    blk = (np.arange(256)[:, None] // 64 == np.arange(256)[None, :] // 64).astype(np.float32) / 64.0
    return dict(
        layer=l, norm1=norm1[l][None, :], w_in=mm['w_in'],
        a_qn=jnp.tile(a_qn[l], 4)[None, :], a_kn=jnp.tile(a_kn[l], 2)[None, :],
        c_qn=jnp.tile(c_qn[l], 4)[None, :], c_kn=jnp.tile(c_kn[l], 2)[None, :],
        bd=jnp.asarray(blk, BF16),
        sink_lanes=jnp.repeat(a_sinks[l].astype(F32)[jnp.asarray([0, 2, 1, 3])], HEAD_DIM).reshape(2, 128),
        sink_rows=jnp.broadcast_to(
            jnp.repeat(a_sinks[l].astype(F32)[jnp.asarray([0, 2, 1, 3])], TILE)[:, None], (4 * TILE, 128)),
        conv_w=conv_w[l], conv_b=conv_b[l][None, :], dt_bias=pad8(dt_bias[l]), a_log=pad8(a_log[l]),
        d_skip=jnp.repeat(d_skip[l].astype(F32), 64)[None, :], ssm_norm=ssm_norm[l][None, :],
        expand=jnp.asarray(expand, BF16), expand_t=jnp.asarray(expand.T, BF16),
        w_out=mm['w_out'], norm2=norm2[l][None, :], w_gate=mm['w_gate'], w_up=mm['w_up'], w_down=mm['w_down'])


def kernel(x_prompt, x_sample, cache_a_k, cache_a_v, cache_c_k, cache_c_v, state_ssm, state_conv, norm1, w_in,
           a_qn, a_kn, a_sinks, c_qn, c_kn, conv_w, conv_b, dt_bias, a_log, d_skip, ssm_norm, w_out, norm2,
           w_gate, w_up, w_down):
    depth = w_in.shape[0]
    batch, seq, _ = x_prompt.shape
    nbatch, dec_seq, _ = x_sample.shape
    assert batch == 1 and dec_seq == TILE - TOK0 and seq % (16 * BLK) == 0 and nbatch % (BLK // TILE) == 0
    past_len = PAST_LEN
    a_buf, c_buf = cache_a_k.shape[2], cache_c_k.shape[2]
    assert a_buf == A_WINDOW and c_buf == C_SPAN

    hp = x_prompt.reshape(seq, D_MODEL)
    hs = jnp.pad(x_sample, ((0, 0), (TOK0, 0), (0, 0))).reshape(nbatch * TILE, D_MODEL)
    tab_p = _rope_table(jnp.arange(seq))
    pos_s = past_len + jnp.maximum(jnp.arange(TILE) - TOK0, 0)
    tab_s = jnp.tile(_rope_table(pos_s), (1, nbatch))
    ma_c, ma_n, mc_c, mc_n = _sample_mult_tables()

    to_fm = lambda c: jnp.transpose(c, (0, 1, 3, 4, 2)).reshape(depth, nbatch, KV_WIDTH, c.shape[2])
    from_fm = lambda c: jnp.transpose(c.reshape(depth, nbatch, 2, HEAD_DIM, c.shape[3]), (0, 1, 4, 2, 3))
    ca_k, ca_v, cc_k, cc_v = to_fm(cache_a_k), to_fm(cache_a_v), to_fm(cache_c_k), to_fm(cache_c_v)
    st = state_ssm.reshape(depth, nbatch, B_INNER, B_STATE)

    mm = _matmul_weights(w_in, w_out, w_gate, w_up, w_down)
    p_out = [[] for _ in range(6)]
    s_conv = []
    new_a = new_ck = new_cv = new_st = None
    tm_p = 1024
    tm_s = min(512, nbatch * TILE)
    for l in range(depth):
        lw = _layer_weights(l, mm, norm1, a_qn, a_kn, a_sinks, c_qn, c_kn, conv_w, conv_b, dt_bias, a_log,
                            d_skip, ssm_norm, norm2)
        s_qa0, s_qa1, s_ka, s_va, s_qc0, s_qc1, s_kc, s_vc, s_z, s_xbc, s_dt = _in_proj(hs, lw, tab_s, tm_s)

        qa0, qa1, ka, va, qc0, qc1, kc, vc, z, xbc, dt = _in_proj(hp, lw, tab_p, tm_p)
        a_o = _prompt_attn(qa0, qa1, ka, va, A_PATTERNS, sink_lanes=lw['sink_lanes'], name="attn_a")
        c_o = _prompt_attn(qc0, qc1, kc, vc, C_BANDS, name="attn_c")
        m_o, h_fin, s_c_o, new_cv = _ssd_prompt(
            xbc, z, dt, lw, (s_qc0, s_qc1, s_kc, s_vc, cc_k, cc_v, l, mc_c, mc_n, new_cv))
        hp, new_ck = _out_ffn(hp, a_o, m_o, c_o, lw, FFN_ROWS, (s_kc, cc_k, l, new_ck))
        p_out[0].append(ka[seq - a_buf:].reshape(1, a_buf, 2, HEAD_DIM))
        p_out[1].append(va[seq - a_buf:].reshape(1, a_buf, 2, HEAD_DIM))
        p_out[2].append(kc[seq - c_buf:].reshape(1, c_buf, 2, HEAD_DIM))
        p_out[3].append(vc[seq - c_buf:].reshape(1, c_buf, 2, HEAD_DIM))
        p_out[4].append(h_fin.reshape(1, B_HEADS, 64, B_STATE))
        p_out[5].append(xbc[seq - (CONV_K - 1):].reshape(1, CONV_K - 1, CONV_DIM))

        a_o, *new_a = _sample_attn(s_qa0, s_qa1, s_ka, s_va, ca_k, ca_v, l, ma_c, ma_n, lw['sink_rows'], 8, new_a)
        prefix = jnp.pad(state_conv[l], ((0, 0), (1, TILE - CONV_K), (0, 0))).reshape(nbatch * TILE, CONV_DIM)
        m_o, new_st = _ssd_sample(s_xbc, prefix, s_z, s_dt, st, l, lw, new_st)
        hs = _out_ffn(hs, a_o, m_o, s_c_o, lw, tm_s)
        s_conv.append(s_xbc.reshape(nbatch, TILE, CONV_DIM)[:, TILE - (CONV_K - 1):])

    outs_p = [jnp.stack(t, axis=0) for t in p_out]
    outs_s = [from_fm(new_a[0]), from_fm(new_a[1]), from_fm(new_ck), from_fm(new_cv),
              new_st.reshape(depth, nbatch, B_HEADS, 64, B_STATE), jnp.stack(s_conv, axis=0)]
    y_p = hp.reshape(1, seq, D_MODEL)
    y_s = hs.reshape(nbatch, TILE, D_MODEL)[:, TOK0:]
    return (y_p, y_s, *outs_p, *outs_s)
```
